```python
import math
import jax, jax.numpy as jnp
from jax import lax
import numpy as np

D_MODEL = 2048
BATCH = 4
SEQ = 4096
DEPTH = 2

GRID_W = 64
CTX_LEN = 256
HEAD_DIM = 128
N_Q_HEADS = 8
N_KV_HEADS = 2
GQA_GROUP = N_Q_HEADS // N_KV_HEADS
ATTN_WIDTH = N_Q_HEADS * HEAD_DIM
HYENA_WIDTH = D_MODEL - ATTN_WIDTH
HYENA_ORDER = 2
SHORT_CONV = 3
FILTER_EMB = 33
FILTER_HIDDEN = 64
DECAY_TARGET = 1e-2
FAST_DECAY_PCT = 0.3
SLOW_DECAY_PCT = 1.5
MAX_DECAY = math.log(DECAY_TARGET) / FAST_DECAY_PCT
MIN_DECAY = math.log(DECAY_TARGET) / SLOW_DECAY_PCT
WINDOW = 128
BLOCK = 128
ROPE_BASE = 10000.0
D_FF = 4 * D_MODEL
N_MOD = 6
EPS = 1e-6
NEG_INF = -1e30
Q_END = ATTN_WIDTH
K_END = Q_END + N_KV_HEADS * HEAD_DIM
V_END = K_END + N_KV_HEADS * HEAD_DIM
IN_COLS = V_END + (HYENA_ORDER + 1) * HYENA_WIDTH

kernel_name = 'hybrid_hyena_window_gqa_prefix_dit'


def _rmsnorm(x, g):
    xf = x.astype(jnp.float32)
    y = xf * lax.rsqrt(jnp.mean(xf * xf, axis=-1, keepdims=True) + EPS)
    return (y * g.astype(jnp.float32)).astype(x.dtype)


def _modulate(h, shift, scale):
    return h * (1.0 + scale) + shift


def _short_conv(u, w, b):
    l = u.shape[1]
    pad = SHORT_CONV // 2
    up = jnp.pad(u, ((0, 0), (pad, SHORT_CONV - 1 - pad), (0, 0)))
    return sum(up[:, i:i + l] * w[i] for i in range(SHORT_CONV)) + b


def _axial_rope_tables(l):
    rows = l // GRID_W
    row = jnp.repeat(jnp.arange(rows), GRID_W).astype(jnp.float32)
    col = jnp.tile(jnp.arange(GRID_W), rows).astype(jnp.float32)
    n_freq = HEAD_DIM // 4
    inv = ROPE_BASE ** (-jnp.arange(n_freq, dtype=jnp.float32) / n_freq)
    ang_r = row[:, None] * inv
    ang_c = col[:, None] * inv
    return (jnp.cos(ang_r), jnp.sin(ang_r), jnp.cos(ang_c), jnp.sin(ang_c))


def _rotate(x, cos, sin):
    x1, x2 = jnp.split(x, 2, axis=-1)
    cos = cos[None, :, None, :]
    sin = sin[None, :, None, :]
    return jnp.concatenate([x1 * cos - x2 * sin, x2 * cos + x1 * sin], axis=-1)


def _apply_axial_rope(x, tables):
    cr, sr, cc, sc = tables
    half = HEAD_DIM // 2
    xf = x.astype(jnp.float32)
    out = jnp.concatenate([_rotate(xf[..., :half], cr, sr), _rotate(xf[..., half:], cc, sc)], axis=-1)
    return out.astype(x.dtype)


def _split_proj(p):
    lead = p.shape[:2]
    q = p[..., :Q_END].reshape(lead + (N_Q_HEADS, HEAD_DIM))
    k = p[..., Q_END:K_END].reshape(lead + (N_KV_HEADS, HEAD_DIM))
    v = p[..., K_END:V_END].reshape(lead + (N_KV_HEADS, HEAD_DIM))
    u = p[..., V_END:]
    return q, k, v, u


def _window_attention(q, k, v, k_ctx, v_ctx, sink):
    b, l = q.shape[:2]
    nb = l // BLOCK
    qb = q.reshape(b, nb, BLOCK, N_KV_HEADS, GQA_GROUP, HEAD_DIM)

    def band(t):
        tb = jnp.pad(t.reshape(b, nb, BLOCK, N_KV_HEADS, HEAD_DIM), ((0, 0), (1, 1), (0, 0), (0, 0), (0, 0)))
        return jnp.concatenate([tb[:, :-2], tb[:, 1:-1], tb[:, 2:]], axis=2)

    kb, vb = band(k), band(v)
    scale = HEAD_DIM ** -0.5
    s_loc = jnp.einsum('bnqkgd,bnskd->bnkgqs', qb, kb).astype(jnp.float32) * scale
    s_ctx = jnp.einsum('bnqkgd,bckd->bnkgqc', qb, k_ctx).astype(jnp.float32) * scale
    qi = jnp.arange(BLOCK)[:, None]
    sj = jnp.arange(3 * BLOCK)[None, :] - BLOCK
    near = jnp.abs(sj - qi) <= WINDOW
    key_pos = jnp.arange(nb)[:, None] * BLOCK + sj
    inside = (key_pos >= 0) & (key_pos < l)
    mask = near[None] & inside[:, None, :]
    s_loc = jnp.where(mask[None, :, None, None], s_loc, NEG_INF)
    s_sink = jnp.broadcast_to(
        sink.astype(jnp.float32).reshape(N_KV_HEADS, GQA_GROUP)[None, None, :, :, None, None],
        s_ctx.shape[:-1] + (1,))
    n_ctx = k_ctx.shape[1]
    p = jax.nn.softmax(jnp.concatenate([s_sink, s_ctx, s_loc], axis=-1), axis=-1).astype(v.dtype)
    p_ctx = p[..., 1:1 + n_ctx]
    p_loc = p[..., 1 + n_ctx:]
    o = (jnp.einsum('bnkgqc,bckd->bnqkgd', p_ctx, v_ctx)
         + jnp.einsum('bnkgqs,bnskd->bnqkgd', p_loc, vb))
    return o.reshape(b, l, ATTN_WIDTH)


def _context_attention(q, k, v, sink):
    b, n = q.shape[:2]
    qg = q.reshape(b, n, N_KV_HEADS, GQA_GROUP, HEAD_DIM)
    s = jnp.einsum('bqkgd,bskd->bkgqs', qg, k).astype(jnp.float32) * (HEAD_DIM ** -0.5)
    s_sink = jnp.broadcast_to(
        sink.astype(jnp.float32).reshape(N_KV_HEADS, GQA_GROUP)[None, :, :, None, None],
        s.shape[:-1] + (1,))
    p = jax.nn.softmax(jnp.concatenate([s_sink, s], axis=-1), axis=-1).astype(v.dtype)
    o = jnp.einsum('bkgqs,bskd->bqkgd', p[..., 1:], v)
    return o.reshape(b, n, ATTN_WIDTH)


def _hyena_filter_fft(l, w1, b1, w2, b2, w3, b3, freq, w_fout):
    t = jnp.linspace(0.0, 1.0, l, dtype=jnp.float32)[:, None]
    bands = (FILTER_EMB - 1) // 2
    f = jnp.linspace(1e-4, bands - 1, bands, dtype=jnp.float32)[None, :]
    w = 2.0 * math.pi * jnp.arange(l, dtype=jnp.float32)[:, None] / l
    z = jnp.concatenate([t, jnp.cos(f * w), -jnp.sin(f * w)], axis=-1)
    h = jnp.sin(freq * (z @ w1 + b1))
    h = jnp.sin(freq * (h @ w2 + b2))
    h = jnp.sin(freq * (h @ w3 + b3))
    h = (h @ w_fout).astype(jnp.float32).reshape(l, HYENA_ORDER, 2, HYENA_WIDTH)
    deltas = jnp.linspace(MIN_DECAY, MAX_DECAY, HYENA_WIDTH, dtype=jnp.float32)
    decay = jnp.exp(-t[:, :, None] * jnp.abs(deltas))
    h = h * decay[:, :, None, :]
    h_fwd, h_bwd = h[:, :, 0], h[:, :, 1]
    filt = jnp.concatenate([h_fwd, jnp.zeros((1, HYENA_ORDER, HYENA_WIDTH), jnp.float32), h_bwd[1:][::-1]], axis=0)
    filt = filt / jnp.sum(jnp.abs(filt), axis=0, keepdims=True)
    return jnp.fft.rfft(filt, n=2 * l, axis=0)


def _fftconv(z, filt_f):
    l = z.shape[1]
    zf = jnp.fft.rfft(z.astype(jnp.float32), n=2 * l, axis=1)
    return jnp.fft.irfft(zf * filt_f, n=2 * l, axis=1)[:, :l]


def _hyena(u, conv_w, conv_b, filt, skip):
    l = u.shape[1]
    parts = jnp.split(_short_conv(u, conv_w, conv_b), HYENA_ORDER + 1, axis=-1)
    filt_f = _hyena_filter_fft(l, *filt)
    z = parts[0]
    for o in range(HYENA_ORDER):
        z = parts[o + 1] * (_fftconv(z, filt_f[:, o]).astype(z.dtype) + skip[o] * z)
    return z


def _sq_relu_mlp(h, w1, w2):
    return jnp.square(jax.nn.relu(h @ w1)) @ w2


def _layer(x, xc, c, c_ctx, w_ada, b_ada, norm_mix, norm_mlp, w_in, q_norm, k_norm, attn_sink,
           conv_w, conv_b, filt, hyena_skip, w_out, w_mlp_in, w_mlp_out, rope, update_ctx):
    mod = jax.nn.silu(c) @ w_ada + b_ada
    sh1, sc1, g1, sh2, sc2, g2 = jnp.split(mod[:, None, :], N_MOD, axis=-1)
    modc = jax.nn.silu(c_ctx) @ w_ada + b_ada
    csh1, csc1, cg1, csh2, csc2, cg2 = jnp.split(modc[None, None, :], N_MOD, axis=-1)

    h = _modulate(_rmsnorm(x, norm_mix), sh1, sc1)
    hc = _modulate(_rmsnorm(xc, norm_mix), csh1, csc1)

    q, k, v, u = _split_proj(h @ w_in)
    q = _apply_axial_rope(_rmsnorm(q, q_norm), rope)
    k = _apply_axial_rope(_rmsnorm(k, k_norm), rope)

    if update_ctx:
        qc, kc, vc, uc = _split_proj(hc @ w_in)
    else:
        pc = hc @ w_in[:, Q_END:V_END]
        lead = pc.shape[:2]
        kc = pc[..., :K_END - Q_END].reshape(lead + (N_KV_HEADS, HEAD_DIM))
        vc = pc[..., K_END - Q_END:].reshape(lead + (N_KV_HEADS, HEAD_DIM))
    kc = _rmsnorm(kc, k_norm)

    attn = _window_attention(q, k, v, kc, vc, attn_sink)
    hy = _hyena(u, conv_w, conv_b, filt, hyena_skip)
    x = x + g1 * (jnp.concatenate([attn, hy], axis=-1) @ w_out)
    x = x + g2 * _sq_relu_mlp(_modulate(_rmsnorm(x, norm_mlp), sh2, sc2), w_mlp_in, w_mlp_out)

    if update_ctx:
        qc = _rmsnorm(qc, q_norm)
        attn_c = _context_attention(qc, kc, vc, attn_sink)
        hy_c = _hyena(uc, conv_w, conv_b, filt, hyena_skip)
        xc = xc + cg1 * (jnp.concatenate([attn_c, hy_c], axis=-1) @ w_out)
        xc = xc + cg2 * _sq_relu_mlp(_modulate(_rmsnorm(xc, norm_mlp), csh2, csc2), w_mlp_in, w_mlp_out)
    return x, xc


def setup_inputs(seed: int = 0) -> dict:
    key = jax.random.key(seed)
    ks = jax.random.split(key, 26)

    def nrm(i, shape, scale=1.0):
        return scale * jax.random.normal(ks[i], shape, jnp.float32)

    hw3 = (HYENA_ORDER + 1) * HYENA_WIDTH
    return {
        'x': nrm(0, (BATCH, SEQ, D_MODEL)),
        'c': nrm(1, (BATCH, D_MODEL)),
        'ctx': nrm(2, (BATCH, CTX_LEN, D_MODEL)),
        'c_ctx': nrm(3, (D_MODEL,)),
        'w_ada': nrm(4, (DEPTH, D_MODEL, N_MOD * D_MODEL), D_MODEL ** -0.5),
        'b_ada': nrm(5, (DEPTH, N_MOD * D_MODEL), 0.02),
        'norm_mix': 1.0 + nrm(6, (DEPTH, D_MODEL), 0.02),
        'norm_mlp': 1.0 + nrm(7, (DEPTH, D_MODEL), 0.02),
        'w_in': nrm(8, (DEPTH, D_MODEL, IN_COLS), D_MODEL ** -0.5),
        'q_norm': 1.0 + nrm(9, (DEPTH, HEAD_DIM), 0.02),
        'k_norm': 1.0 + nrm(10, (DEPTH, HEAD_DIM), 0.02),
        'attn_sink': nrm(11, (DEPTH, N_Q_HEADS), 0.5),
        'conv_w': nrm(12, (DEPTH, SHORT_CONV, hw3), SHORT_CONV ** -0.5),
        'conv_b': nrm(13, (DEPTH, hw3), 0.02),
        'filt_w1': nrm(14, (DEPTH, FILTER_EMB, FILTER_HIDDEN), FILTER_EMB ** -0.5),
        'filt_b1': nrm(15, (DEPTH, FILTER_HIDDEN), 0.1),
        'filt_w2': nrm(16, (DEPTH, FILTER_HIDDEN, FILTER_HIDDEN), FILTER_HIDDEN ** -0.5),
        'filt_b2': nrm(17, (DEPTH, FILTER_HIDDEN), 0.1),
        'filt_w3': nrm(18, (DEPTH, FILTER_HIDDEN, FILTER_HIDDEN), FILTER_HIDDEN ** -0.5),
        'filt_b3': nrm(19, (DEPTH, FILTER_HIDDEN), 0.1),
        'filt_freq': 1.0 + nrm(20, (DEPTH, FILTER_HIDDEN), 0.02),
        'filt_w_out': nrm(21, (DEPTH, FILTER_HIDDEN, HYENA_ORDER * 2 * HYENA_WIDTH), FILTER_HIDDEN ** -0.5),
        'hyena_skip': nrm(22, (DEPTH, HYENA_ORDER, HYENA_WIDTH)),
        'w_out': nrm(23, (DEPTH, D_MODEL, D_MODEL), D_MODEL ** -0.5),
        'w_mlp_in': nrm(24, (DEPTH, D_MODEL, D_FF), D_MODEL ** -0.5),
        'w_mlp_out': nrm(25, (DEPTH, D_FF, D_MODEL), D_FF ** -0.5),
    }


def reference(x, c, ctx, c_ctx, w_ada, b_ada, norm_mix, norm_mlp, w_in, q_norm, k_norm, attn_sink,
              conv_w, conv_b, filt_w1, filt_b1, filt_w2, filt_b2, filt_w3, filt_b3, filt_freq,
              filt_w_out, hyena_skip, w_out, w_mlp_in, w_mlp_out):
    rope = _axial_rope_tables(x.shape[1])
    xc = ctx
    for i in range(DEPTH):
        filt = (filt_w1[i], filt_b1[i], filt_w2[i], filt_b2[i], filt_w3[i], filt_b3[i],
                filt_freq[i], filt_w_out[i])
        x, xc = _layer(x, xc, c, c_ctx, w_ada[i], b_ada[i], norm_mix[i], norm_mlp[i], w_in[i],
                       q_norm[i], k_norm[i], attn_sink[i], conv_w[i], conv_b[i], filt,
                       hyena_skip[i], w_out[i], w_mlp_in[i], w_mlp_out[i], rope,
                       update_ctx=(i < DEPTH - 1))
    return x
```

```python
import functools
import math

import jax
import jax.numpy as jnp
from jax import lax
from jax.experimental import pallas as pl
from jax.experimental.pallas import tpu as pltpu

D_MODEL = 2048
BATCH = 4
SEQ = 4096
DEPTH = 2
GRID_W = 64
CTX_LEN = 256
HEAD_DIM = 128
N_Q_HEADS = 8
N_KV_HEADS = 2
GQA_GROUP = N_Q_HEADS // N_KV_HEADS
ATTN_WIDTH = N_Q_HEADS * HEAD_DIM
HYENA_WIDTH = D_MODEL - ATTN_WIDTH
HYENA_ORDER = 2
SHORT_CONV = 3
FILTER_EMB = 33
DECAY_TARGET = 1e-2
FAST_DECAY_PCT = 0.3
SLOW_DECAY_PCT = 1.5
MAX_DECAY = math.log(DECAY_TARGET) / FAST_DECAY_PCT
MIN_DECAY = math.log(DECAY_TARGET) / SLOW_DECAY_PCT
WINDOW = 128
BLOCK = 128
ROPE_BASE = 10000.0
D_FF = 4 * D_MODEL
N_MOD = 6
EPS = 1e-6
NEG_INF = -1e30
Q_END = ATTN_WIDTH
K_END = Q_END + N_KV_HEADS * HEAD_DIM
V_END = K_END + N_KV_HEADS * HEAD_DIM
IN_COLS = V_END + (HYENA_ORDER + 1) * HYENA_WIDTH

MOD_ROWS = 8
CTX_ROW = BATCH
KV_COLS = 2 * N_KV_HEADS * HEAD_DIM

F32 = jnp.float32
BF16 = jnp.bfloat16
MIB = 1024 * 1024


def _cparams(sem, vmem_mib):
    return pltpu.CompilerParams(dimension_semantics=sem, vmem_limit_bytes=vmem_mib * MIB)


def _mod_kernel(c_ref, w_ref, b_ref, o_ref):
    c = c_ref[...]
    s = (c * jax.nn.sigmoid(c)).astype(BF16)
    o_ref[...] = jnp.dot(s, w_ref[...].astype(BF16), preferred_element_type=F32) + b_ref[...]


def _modulation(cvec, w_ada, b_ada):
    tn = 1024
    n_cols = N_MOD * D_MODEL
    return pl.pallas_call(
        _mod_kernel,
        grid=(DEPTH, n_cols // tn),
        in_specs=[
            pl.BlockSpec((MOD_ROWS, D_MODEL), lambda l, j: (0, 0)),
            pl.BlockSpec((None, D_MODEL, tn), lambda l, j: (l, 0, j)),
            pl.BlockSpec((None, 1, tn), lambda l, j: (l, 0, j)),
        ],
        out_specs=pl.BlockSpec((None, MOD_ROWS, tn), lambda l, j: (l, 0, j)),
        out_shape=jax.ShapeDtypeStruct((DEPTH, MOD_ROWS, n_cols), F32),
        compiler_params=_cparams(("arbitrary", "arbitrary"), 40),
        name="adaln_mod",
    )(cvec, w_ada, b_ada.reshape(DEPTH, 1, n_cols))


IN_TN = 512
Q_TILES = Q_END // IN_TN
KV_TILE = Q_TILES
HEADS_PER_TILE = IN_TN // HEAD_DIM


def _rms_modulate(xf, nw, shift, scale):
    ms = jnp.mean(xf * xf, axis=-1, keepdims=True)
    y = xf * lax.rsqrt(ms + EPS) * nw
    return y * (1.0 + scale) + shift


def _head_norm_rope(a, g, cos, sin):
    ms = jnp.mean(a * a, axis=-1, keepdims=True)
    y = a * lax.rsqrt(ms + EPS) * g
    if cos is None:
        return y
    lane = lax.broadcasted_iota(jnp.int32, y.shape, 1)
    partner = jnp.where((lane % 64) < 32, pltpu.roll(y, 96, 1), pltpu.roll(y, 32, 1))
    return y * cos + partner * sin


def _inproj_kernel(x_ref, m_ref, nw_ref, w_ref, qn_ref, kn_ref, cos_ref, sin_ref, o_ref, h_ref,
                   *, n_off, n_tiles, use_rope):
    n = pl.program_id(2)

    @pl.when(n == 0)
    def _():
        h = _rms_modulate(x_ref[...], nw_ref[...], m_ref[0:1, :], m_ref[1:2, :])
        h_ref[...] = h.astype(BF16)

    acc = jnp.dot(h_ref[...], w_ref[...], preferred_element_type=F32)
    n_abs = n + n_off
    cos = cos_ref[...] if use_rope else None
    sin = sin_ref[...] if use_rope else None
    tiles = range(n_off, n_off + n_tiles)

    def heads(gains):
        for c, g in enumerate(gains):
            sl = slice(c * HEAD_DIM, (c + 1) * HEAD_DIM)
            if g is None:
                o_ref[:, sl] = acc[:, sl].astype(BF16)
            else:
                o_ref[:, sl] = _head_norm_rope(acc[:, sl], g, cos, sin).astype(BF16)

    if any(t < Q_TILES for t in tiles):
        @pl.when(n_abs < Q_TILES)
        def _():
            heads([qn_ref[...]] * HEADS_PER_TILE)

    if any(t == KV_TILE for t in tiles):
        @pl.when(n_abs == KV_TILE)
        def _():
            heads([kn_ref[...]] * N_KV_HEADS + [None] * N_KV_HEADS)

    if any(t > KV_TILE for t in tiles):
        @pl.when(n_abs > KV_TILE)
        def _():
            o_ref[...] = acc.astype(BF16)


def _in_projection(x, mod, mod_row, nw, w_bf, qn, kn, cos, sin, *, tm, n_off, n_tiles, use_rope):
    b, t, _ = x.shape
    row = (lambda bb: bb) if mod_row is None else (lambda bb: mod_row)
    kern = functools.partial(_inproj_kernel, n_off=n_off, n_tiles=n_tiles, use_rope=use_rope)
    return pl.pallas_call(
        kern,
        grid=(b, t // tm, n_tiles),
        in_specs=[
            pl.BlockSpec((None, tm, D_MODEL), lambda bb, i, n: (bb, i, 0)),
            pl.BlockSpec((None, N_MOD, D_MODEL), lambda bb, i, n: (row(bb), 0, 0)),
            pl.BlockSpec((1, D_MODEL), lambda bb, i, n: (0, 0)),
            pl.BlockSpec((D_MODEL, IN_TN), lambda bb, i, n: (0, n + n_off)),
            pl.BlockSpec((1, HEAD_DIM), lambda bb, i, n: (0, 0)),
            pl.BlockSpec((1, HEAD_DIM), lambda bb, i, n: (0, 0)),
            pl.BlockSpec((tm, HEAD_DIM), lambda bb, i, n: (i, 0)),
            pl.BlockSpec((tm, HEAD_DIM), lambda bb, i, n: (i, 0)),
        ],
        out_specs=pl.BlockSpec((None, tm, IN_TN), lambda bb, i, n: (bb, i, n)),
        out_shape=jax.ShapeDtypeStruct((b, t, n_tiles * IN_TN), BF16),
        scratch_shapes=[pltpu.VMEM((tm, D_MODEL), BF16)],
        compiler_params=_cparams(("arbitrary", "arbitrary", "arbitrary"), 48),
        name="in_proj",
    )(x, mod, nw, w_bf, qn, kn, cos, sin)


def _sink_column(sink_ref, kv, rows_per_head):
    rows = GQA_GROUP * rows_per_head
    head = lax.broadcasted_iota(jnp.int32, (rows, 1), 0) // rows_per_head
    col = jnp.full((rows, 1), sink_ref[kv * GQA_GROUP], F32)
    for g in range(1, GQA_GROUP):
        col = jnp.where(head == g, sink_ref[kv * GQA_GROUP + g], col)
    return col


def _stack_heads(q_ref, kv):
    return jnp.concatenate(
        [q_ref[:, (kv * GQA_GROUP + g) * HEAD_DIM:(kv * GQA_GROUP + g + 1) * HEAD_DIM]
         for g in range(GQA_GROUP)], axis=0)


def _qk(q, k):
    return lax.dot_general(q, k, (((1,), (1,)), ((), ())), preferred_element_type=F32) * (HEAD_DIM ** -0.5)


def _window_attn_kernel(sink_ref, q_ref, kp_ref, kc_ref, kn_ref, vp_ref, vc_ref, vn_ref, kx_ref, vx_ref, o_ref,
                        *, seq):
    i = pl.program_id(1)
    rows = GQA_GROUP * BLOCK
    qi = lax.broadcasted_iota(jnp.int32, (rows, 3 * BLOCK), 0) % BLOCK
    sj = lax.broadcasted_iota(jnp.int32, (rows, 3 * BLOCK), 1) - BLOCK
    key_pos = i * BLOCK + sj
    mask = (jnp.abs(sj - qi) <= WINDOW) & (key_pos >= 0) & (key_pos < seq)
    for kv in range(N_KV_HEADS):
        ks = slice(kv * HEAD_DIM, (kv + 1) * HEAD_DIM)
        k_loc = jnp.concatenate([kp_ref[:, ks], kc_ref[:, ks], kn_ref[:, ks]], axis=0)
        v_loc = jnp.concatenate([vp_ref[:, ks], vc_ref[:, ks], vn_ref[:, ks]], axis=0)
        q4 = _stack_heads(q_ref, kv)
        s_loc = jnp.where(mask, _qk(q4, k_loc), NEG_INF)
        s_ctx = _qk(q4, kx_ref[:, ks])
        s_sink = _sink_column(sink_ref, kv, BLOCK)
        m = jnp.maximum(jnp.maximum(jnp.max(s_loc, axis=-1, keepdims=True),
                                    jnp.max(s_ctx, axis=-1, keepdims=True)), s_sink)
        e_loc = jnp.exp(s_loc - m)
        e_ctx = jnp.exp(s_ctx - m)
        denom = (jnp.exp(s_sink - m) + jnp.sum(e_ctx, axis=-1, keepdims=True)
                 + jnp.sum(e_loc, axis=-1, keepdims=True))
        pv = (jnp.dot(e_ctx.astype(BF16), vx_ref[:, ks], preferred_element_type=F32)
              + jnp.dot(e_loc.astype(BF16), v_loc, preferred_element_type=F32))
        o = pv / denom
        for g in range(GQA_GROUP):
            h = kv * GQA_GROUP + g
            o_ref[:, h * HEAD_DIM:(h + 1) * HEAD_DIM] = o[g * BLOCK:(g + 1) * BLOCK].astype(BF16)


def _window_attention(p, pc, kx_blk, vx_blk, sink):
    b, seq, _ = p.shape
    nb = seq // BLOCK
    kvw = N_KV_HEADS * HEAD_DIM
    k_blk, v_blk = Q_END // kvw, K_END // kvw
    prev = lambda bb, i: (bb, jnp.maximum(i - 1, 0))
    nxt = lambda bb, i: (bb, jnp.minimum(i + 1, nb - 1))
    cur = lambda bb, i: (bb, i)

    def band(which, blk):
        return pl.BlockSpec((None, BLOCK, kvw), lambda bb, i: which(bb, i) + (blk,))

    return pl.pallas_call(
        functools.partial(_window_attn_kernel, seq=seq),
        grid=(b, nb),
        in_specs=[
            pl.BlockSpec(memory_space=pltpu.SMEM),
            pl.BlockSpec((None, BLOCK, ATTN_WIDTH), lambda bb, i: (bb, i, 0)),
            band(prev, k_blk), band(cur, k_blk), band(nxt, k_blk),
            band(prev, v_blk), band(cur, v_blk), band(nxt, v_blk),
            pl.BlockSpec((None, CTX_LEN, kvw), lambda bb, i: (bb, 0, kx_blk)),
            pl.BlockSpec((None, CTX_LEN, kvw), lambda bb, i: (bb, 0, vx_blk)),
        ],
        out_specs=pl.BlockSpec((None, BLOCK, ATTN_WIDTH), lambda bb, i: (bb, i, 0)),
        out_shape=jax.ShapeDtypeStruct((b, seq, ATTN_WIDTH), BF16),
        compiler_params=_cparams(("arbitrary", "arbitrary"), 32),
        name="window_attn",
    )(sink, p, p, p, p, p, p, p, pc, pc)


def _ctx_attn_kernel(sink_ref, q_ref, k_ref, v_ref, o_ref):
    n = q_ref.shape[0]
    for kv in range(N_KV_HEADS):
        ks = slice(kv * HEAD_DIM, (kv + 1) * HEAD_DIM)
        q4 = _stack_heads(q_ref, kv)
        s = _qk(q4, k_ref[:, ks])
        s_sink = _sink_column(sink_ref, kv, n)
        m = jnp.maximum(jnp.max(s, axis=-1, keepdims=True), s_sink)
        e = jnp.exp(s - m)
        denom = jnp.exp(s_sink - m) + jnp.sum(e, axis=-1, keepdims=True)
        o = jnp.dot(e.astype(BF16), v_ref[:, ks], preferred_element_type=F32) / denom
        for g in range(GQA_GROUP):
            h = kv * GQA_GROUP + g
            o_ref[:, h * HEAD_DIM:(h + 1) * HEAD_DIM] = o[g * n:(g + 1) * n].astype(BF16)


def _context_attention(pc, sink):
    b, n, _ = pc.shape
    kvw = N_KV_HEADS * HEAD_DIM
    return pl.pallas_call(
        _ctx_attn_kernel,
        grid=(b,),
        in_specs=[
            pl.BlockSpec(memory_space=pltpu.SMEM),
            pl.BlockSpec((None, n, ATTN_WIDTH), lambda bb: (bb, 0, 0)),
            pl.BlockSpec((None, n, kvw), lambda bb: (bb, 0, Q_END // kvw)),
            pl.BlockSpec((None, n, kvw), lambda bb: (bb, 0, K_END // kvw)),
        ],
        out_specs=pl.BlockSpec((None, n, ATTN_WIDTH), lambda bb: (bb, 0, 0)),
        out_shape=jax.ShapeDtypeStruct((b, n, ATTN_WIDTH), BF16),
        compiler_params=_cparams(("arbitrary",), 32),
        name="ctx_attn",
    )(sink, pc, pc, pc)


def _outproj_kernel(a_ref, y_ref, x_ref, m_ref, wa_ref, wy_ref, o_ref):
    mix = (jnp.dot(a_ref[...], wa_ref[...], preferred_element_type=F32)
           + jnp.dot(y_ref[...], wy_ref[...], preferred_element_type=F32))
    o_ref[...] = x_ref[...] + m_ref[2:3, :] * mix


def _out_projection(attn, hy, x, mod, mod_row, w_bf, *, tm):
    b, t, _ = x.shape
    row = (lambda bb: bb) if mod_row is None else (lambda bb: mod_row)
    return pl.pallas_call(
        _outproj_kernel,
        grid=(b, t // tm),
        in_specs=[
            pl.BlockSpec((None, tm, ATTN_WIDTH), lambda bb, i: (bb, i, 0)),
            pl.BlockSpec((None, tm, HYENA_WIDTH), lambda bb, i: (bb, i, 0)),
            pl.BlockSpec((None, tm, D_MODEL), lambda bb, i: (bb, i, 0)),
            pl.BlockSpec((None, N_MOD, D_MODEL), lambda bb, i: (row(bb), 0, 0)),
            pl.BlockSpec((ATTN_WIDTH, D_MODEL), lambda bb, i: (0, 0)),
            pl.BlockSpec((HYENA_WIDTH, D_MODEL), lambda bb, i: (1, 0)),
        ],
        out_specs=pl.BlockSpec((None, tm, D_MODEL), lambda bb, i: (bb, i, 0)),
        out_shape=jax.ShapeDtypeStruct(x.shape, F32),
        compiler_params=_cparams(("arbitrary", "arbitrary"), 48),
        name="out_proj",
    )(attn, hy, x, mod, w_bf, w_bf)


MLP_OUT_CHUNK = 512


def _mlp_kernel(x_ref, m_ref, nw_ref, w1_ref, w2_ref, o_ref, h_ref, *, n_f):
    f = pl.program_id(2)

    @pl.when(f == 0)
    def _():
        h = _rms_modulate(x_ref[...], nw_ref[...], m_ref[3:4, :], m_ref[4:5, :])
        h_ref[...] = h.astype(BF16)
        o_ref[...] = jnp.zeros_like(o_ref)

    a = jnp.maximum(jnp.dot(h_ref[...], w1_ref[...], preferred_element_type=F32), 0.0)
    a = (a * a).astype(BF16)
    for n0 in range(0, D_MODEL, MLP_OUT_CHUNK):
        sl = slice(n0, n0 + MLP_OUT_CHUNK)
        o_ref[:, sl] += jnp.dot(a, w2_ref[:, sl], preferred_element_type=F32)

    @pl.when(f == n_f - 1)
    def _():
        o_ref[...] = x_ref[...] + m_ref[5:6, :] * o_ref[...]


def _mlp(x, mod, mod_row, nw, w1_bf, w2_bf, *, tm, tf):
    b, t, _ = x.shape
    n_f = D_FF // tf
    row = (lambda bb: bb) if mod_row is None else (lambda bb: mod_row)
    return pl.pallas_call(
        functools.partial(_mlp_kernel, n_f=n_f),
        grid=(b, t // tm, n_f),
        in_specs=[
            pl.BlockSpec((None, tm, D_MODEL), lambda bb, i, f: (bb, i, 0)),
            pl.BlockSpec((None, N_MOD, D_MODEL), lambda bb, i, f: (row(bb), 0, 0)),
            pl.BlockSpec((1, D_MODEL), lambda bb, i, f: (0, 0)),
            pl.BlockSpec((D_MODEL, tf), lambda bb, i, f: (0, f)),
            pl.BlockSpec((tf, D_MODEL), lambda bb, i, f: (f, 0)),
        ],
        out_specs=pl.BlockSpec((None, tm, D_MODEL), lambda bb, i, f: (bb, i, 0)),
        out_shape=jax.ShapeDtypeStruct(x.shape, F32),
        scratch_shapes=[pltpu.VMEM((tm, D_MODEL), BF16)],
        compiler_params=_cparams(("arbitrary", "arbitrary", "arbitrary"), 56),
        name="mlp",
    )(x, mod, nw, w1_bf, w2_bf)


def _short_conv(u, w, b):
    l = u.shape[1]
    pad = SHORT_CONV // 2
    up = jnp.pad(u, ((0, 0), (pad, SHORT_CONV - 1 - pad), (0, 0)))
    return sum(up[:, i:i + l] * w[i] for i in range(SHORT_CONV)) + b


def _hyena_filter_fft(l, w1, b1, w2, b2, w3, b3, freq, w_fout):
    t = jnp.linspace(0.0, 1.0, l, dtype=F32)[:, None]
    bands = (FILTER_EMB - 1) // 2
    f = jnp.linspace(1e-4, bands - 1, bands, dtype=F32)[None, :]
    w = 2.0 * math.pi * jnp.arange(l, dtype=F32)[:, None] / l
    z = jnp.concatenate([t, jnp.cos(f * w), -jnp.sin(f * w)], axis=-1)
    h = jnp.sin(freq * (z @ w1 + b1))
    h = jnp.sin(freq * (h @ w2 + b2))
    h = jnp.sin(freq * (h @ w3 + b3))
    h = (h @ w_fout).astype(F32).reshape(l, HYENA_ORDER, 2, HYENA_WIDTH)
    deltas = jnp.linspace(MIN_DECAY, MAX_DECAY, HYENA_WIDTH, dtype=F32)
    decay = jnp.exp(-t[:, :, None] * jnp.abs(deltas))
    h = h * decay[:, :, None, :]
    h_fwd, h_bwd = h[:, :, 0], h[:, :, 1]
    filt = jnp.concatenate([h_fwd, jnp.zeros((1, HYENA_ORDER, HYENA_WIDTH), F32), h_bwd[1:][::-1]], axis=0)
    filt = filt / jnp.sum(jnp.abs(filt), axis=0, keepdims=True)
    return jnp.fft.rfft(filt, n=2 * l, axis=0)


def _fftconv(z, filt_f):
    l = z.shape[1]
    zf = jnp.fft.rfft(z.astype(F32), n=2 * l, axis=1)
    return jnp.fft.irfft(zf * filt_f, n=2 * l, axis=1)[:, :l]


def _hyena(u, conv_w, conv_b, filt, skip):
    l = u.shape[1]
    parts = jnp.split(_short_conv(u, conv_w, conv_b), HYENA_ORDER + 1, axis=-1)
    filt_f = _hyena_filter_fft(l, *filt)
    z = parts[0]
    for o in range(HYENA_ORDER):
        z = parts[o + 1] * (_fftconv(z, filt_f[:, o]) + skip[o] * z)
    return z


def _rope_tables(l):
    rows = l // GRID_W
    row = jnp.repeat(jnp.arange(rows), GRID_W).astype(F32)
    col = jnp.tile(jnp.arange(GRID_W), rows).astype(F32)
    n_freq = HEAD_DIM // 4
    inv = ROPE_BASE ** (-jnp.arange(n_freq, dtype=F32) / n_freq)
    ang_r = row[:, None] * inv
    ang_c = col[:, None] * inv
    cos = jnp.concatenate([jnp.cos(ang_r), jnp.cos(ang_r), jnp.cos(ang_c), jnp.cos(ang_c)], axis=-1)
    sin = jnp.concatenate([-jnp.sin(ang_r), jnp.sin(ang_r), -jnp.sin(ang_c), jnp.sin(ang_c)], axis=-1)
    return cos, sin


def kernel(x, c, ctx, c_ctx, w_ada, b_ada, norm_mix, norm_mlp, w_in, q_norm, k_norm, attn_sink, conv_w, conv_b, filt_w1, filt_b1, filt_w2, filt_b2, filt_w3, filt_b3, filt_freq, filt_w_out, hyena_skip, w_out, w_mlp_in, w_mlp_out):
    cos, sin = _rope_tables(SEQ)
    cvec = jnp.zeros((MOD_ROWS, D_MODEL), F32).at[:BATCH].set(c).at[CTX_ROW].set(c_ctx)
    mod_all = _modulation(cvec, w_ada, b_ada).reshape(DEPTH, MOD_ROWS, N_MOD, D_MODEL)
    xc = ctx
    for i in range(DEPTH):
        update_ctx = i < DEPTH - 1
        mod = mod_all[i]
        w_in_bf = w_in[i].astype(BF16)
        w_out_bf = w_out[i].astype(BF16)
        w1_bf = w_mlp_in[i].astype(BF16)
        w2_bf = w_mlp_out[i].astype(BF16)
        nmix = norm_mix[i].reshape(1, D_MODEL)
        nmlp = norm_mlp[i].reshape(1, D_MODEL)
        qn = q_norm[i].reshape(1, HEAD_DIM)
        kn = k_norm[i].reshape(1, HEAD_DIM)
        filt = (filt_w1[i], filt_b1[i], filt_w2[i], filt_b2[i], filt_w3[i], filt_b3[i],
                filt_freq[i], filt_w_out[i])

        p = _in_projection(x, mod, None, nmix, w_in_bf, qn, kn, cos, sin,
                           tm=1024, n_off=0, n_tiles=IN_COLS // IN_TN, use_rope=True)
        if update_ctx:
            pc = _in_projection(xc, mod, CTX_ROW, nmix, w_in_bf, qn, kn, cos, sin,
                                tm=CTX_LEN, n_off=0, n_tiles=IN_COLS // IN_TN, use_rope=False)
            kx_blk, vx_blk = Q_END // KV_COLS * 2, Q_END // KV_COLS * 2 + 1
        else:
            pc = _in_projection(xc, mod, CTX_ROW, nmix, w_in_bf, qn, kn, cos, sin,
                                tm=CTX_LEN, n_off=KV_TILE, n_tiles=1, use_rope=False)
            kx_blk, vx_blk = 0, 1

        attn = _window_attention(p, pc, kx_blk, vx_blk, attn_sink[i])
        hy = _hyena(p[:, :, V_END:].astype(F32), conv_w[i], conv_b[i], filt, hyena_skip[i]).astype(BF16)
        x = _out_projection(attn, hy, x, mod, None, w_out_bf, tm=512)
        x = _mlp(x, mod, None, nmlp, w1_bf, w2_bf, tm=1024, tf=512)

        if update_ctx:
            attn_c = _context_attention(pc, attn_sink[i])
            hy_c = _hyena(pc[:, :, V_END:].astype(F32), conv_w[i], conv_b[i], filt, hyena_skip[i]).astype(BF16)
            xc = _out_projection(attn_c, hy_c, xc, mod, CTX_ROW, w_out_bf, tm=CTX_LEN)
            xc = _mlp(xc, mod, CTX_ROW, nmlp, w1_bf, w2_bf, tm=CTX_LEN, tf=512)
    return x
```

```python
import cmath
import functools
import math

import jax
import jax.numpy as jnp
import numpy as np
from jax import lax
from jax.experimental import pallas as pl
from jax.experimental.pallas import tpu as pltpu

D_MODEL = 2048
BATCH = 4
SEQ = 4096
DEPTH = 2
GRID_W = 64
CTX_LEN = 256
HEAD_DIM = 128
N_Q_HEADS = 8
N_KV_HEADS = 2
GQA_GROUP = N_Q_HEADS // N_KV_HEADS
ATTN_WIDTH = N_Q_HEADS * HEAD_DIM
HYENA_WIDTH = D_MODEL - ATTN_WIDTH
HYENA_ORDER = 2
SHORT_CONV = 3
FILTER_EMB = 33
DECAY_TARGET = 1e-2
FAST_DECAY_PCT = 0.3
SLOW_DECAY_PCT = 1.5
MAX_DECAY = math.log(DECAY_TARGET) / FAST_DECAY_PCT
MIN_DECAY = math.log(DECAY_TARGET) / SLOW_DECAY_PCT
WINDOW = 128
BLOCK = 128
ROPE_BASE = 10000.0
D_FF = 4 * D_MODEL
N_MOD = 6
EPS = 1e-6
NEG_INF = -1e30
Q_END = ATTN_WIDTH
K_END = Q_END + N_KV_HEADS * HEAD_DIM
V_END = K_END + N_KV_HEADS * HEAD_DIM
IN_COLS = V_END + (HYENA_ORDER + 1) * HYENA_WIDTH

MOD_ROWS = 8
CTX_ROW = BATCH
KV_COLS = 2 * N_KV_HEADS * HEAD_DIM

F32 = jnp.float32
BF16 = jnp.bfloat16
MIB = 1024 * 1024


def _cparams(sem, vmem_mib):
    return pltpu.CompilerParams(dimension_semantics=sem, vmem_limit_bytes=vmem_mib * MIB)


def _mod_kernel(c_ref, w_ref, b_ref, o_ref):
    c = c_ref[...]
    s = (c * jax.nn.sigmoid(c)).astype(BF16)
    o_ref[...] = jnp.dot(s, w_ref[...].astype(BF16), preferred_element_type=F32) + b_ref[...]


def _modulation(cvec, w_ada, b_ada):
    tn = 1024
    n_cols = N_MOD * D_MODEL
    return pl.pallas_call(
        _mod_kernel,
        grid=(DEPTH, n_cols // tn),
        in_specs=[
            pl.BlockSpec((MOD_ROWS, D_MODEL), lambda l, j: (0, 0)),
            pl.BlockSpec((None, D_MODEL, tn), lambda l, j: (l, 0, j)),
            pl.BlockSpec((None, 1, tn), lambda l, j: (l, 0, j)),
        ],
        out_specs=pl.BlockSpec((None, MOD_ROWS, tn), lambda l, j: (l, 0, j)),
        out_shape=jax.ShapeDtypeStruct((DEPTH, MOD_ROWS, n_cols), F32),
        compiler_params=_cparams(("arbitrary", "arbitrary"), 40),
        name="adaln_mod",
    )(cvec, w_ada, b_ada.reshape(DEPTH, 1, n_cols))


IN_TN = 512
Q_TILES = Q_END // IN_TN
KV_TILE = Q_TILES
HEADS_PER_TILE = IN_TN // HEAD_DIM


def _rms_modulate(xf, nw, shift, scale):
    ms = jnp.mean(xf * xf, axis=-1, keepdims=True)
    y = xf * lax.rsqrt(ms + EPS) * nw
    return y * (1.0 + scale) + shift


def _head_norm_rope(a, g, cos, sin):
    ms = jnp.mean(a * a, axis=-1, keepdims=True)
    y = a * lax.rsqrt(ms + EPS) * g
    if cos is None:
        return y
    lane = lax.broadcasted_iota(jnp.int32, y.shape, 1)
    partner = jnp.where((lane % 64) < 32, pltpu.roll(y, 96, 1), pltpu.roll(y, 32, 1))
    return y * cos + partner * sin


def _inproj_kernel(x_ref, m_ref, nw_ref, w_ref, qn_ref, kn_ref, cos_ref, sin_ref, o_ref, h_ref,
                   *, n_off, n_tiles, use_rope):
    n = pl.program_id(2)

    @pl.when(n == 0)
    def _():
        h = _rms_modulate(x_ref[...], nw_ref[...], m_ref[0:1, :], m_ref[1:2, :])
        h_ref[...] = h.astype(BF16)

    acc = jnp.dot(h_ref[...], w_ref[...], preferred_element_type=F32)
    n_abs = n + n_off
    cos = cos_ref[...] if use_rope else None
    sin = sin_ref[...] if use_rope else None
    tiles = range(n_off, n_off + n_tiles)

    def heads(gains):
        for c, g in enumerate(gains):
            sl = slice(c * HEAD_DIM, (c + 1) * HEAD_DIM)
            if g is None:
                o_ref[:, sl] = acc[:, sl].astype(BF16)
            else:
                o_ref[:, sl] = _head_norm_rope(acc[:, sl], g, cos, sin).astype(BF16)

    if any(t < Q_TILES for t in tiles):
        @pl.when(n_abs < Q_TILES)
        def _():
            heads([qn_ref[...]] * HEADS_PER_TILE)

    if any(t == KV_TILE for t in tiles):
        @pl.when(n_abs == KV_TILE)
        def _():
            heads([kn_ref[...]] * N_KV_HEADS + [None] * N_KV_HEADS)

    if any(t > KV_TILE for t in tiles):
        @pl.when(n_abs > KV_TILE)
        def _():
            o_ref[...] = acc.astype(BF16)


def _in_projection(x, mod, mod_row, nw, w_bf, qn, kn, cos, sin, *, tm, n_off, n_tiles, use_rope):
    b, t, _ = x.shape
    row = (lambda bb: bb) if mod_row is None else (lambda bb: mod_row)
    kern = functools.partial(_inproj_kernel, n_off=n_off, n_tiles=n_tiles, use_rope=use_rope)
    return pl.pallas_call(
        kern,
        grid=(b, t // tm, n_tiles),
        in_specs=[
            pl.BlockSpec((None, tm, D_MODEL), lambda bb, i, n: (bb, i, 0)),
            pl.BlockSpec((None, N_MOD, D_MODEL), lambda bb, i, n: (row(bb), 0, 0)),
            pl.BlockSpec((1, D_MODEL), lambda bb, i, n: (0, 0)),
            pl.BlockSpec((D_MODEL, IN_TN), lambda bb, i, n: (0, n + n_off)),
            pl.BlockSpec((1, HEAD_DIM), lambda bb, i, n: (0, 0)),
            pl.BlockSpec((1, HEAD_DIM), lambda bb, i, n: (0, 0)),
            pl.BlockSpec((tm, HEAD_DIM), lambda bb, i, n: (i, 0)),
            pl.BlockSpec((tm, HEAD_DIM), lambda bb, i, n: (i, 0)),
        ],
        out_specs=pl.BlockSpec((None, tm, IN_TN), lambda bb, i, n: (bb, i, n)),
        out_shape=jax.ShapeDtypeStruct((b, t, n_tiles * IN_TN), BF16),
        scratch_shapes=[pltpu.VMEM((tm, D_MODEL), BF16)],
        compiler_params=_cparams(("arbitrary", "arbitrary", "arbitrary"), 48),
        name="in_proj",
    )(x, mod, nw, w_bf, qn, kn, cos, sin)


def _sink_column(sink_ref, kv, rows_per_head):
    rows = GQA_GROUP * rows_per_head
    head = lax.broadcasted_iota(jnp.int32, (rows, 1), 0) // rows_per_head
    col = jnp.full((rows, 1), sink_ref[kv * GQA_GROUP], F32)
    for g in range(1, GQA_GROUP):
        col = jnp.where(head == g, sink_ref[kv * GQA_GROUP + g], col)
    return col


def _stack_heads(q_ref, kv):
    return jnp.concatenate(
        [q_ref[:, (kv * GQA_GROUP + g) * HEAD_DIM:(kv * GQA_GROUP + g + 1) * HEAD_DIM]
         for g in range(GQA_GROUP)], axis=0)


def _qk(q, k):
    return lax.dot_general(q, k, (((1,), (1,)), ((), ())), preferred_element_type=F32) * (HEAD_DIM ** -0.5)


def _window_attn_kernel(sink_ref, q_ref, kp_ref, kc_ref, kn_ref, vp_ref, vc_ref, vn_ref, kx_ref, vx_ref, o_ref,
                        *, seq):
    i = pl.program_id(1)
    rows = GQA_GROUP * BLOCK
    qi = lax.broadcasted_iota(jnp.int32, (rows, 3 * BLOCK), 0) % BLOCK
    sj = lax.broadcasted_iota(jnp.int32, (rows, 3 * BLOCK), 1) - BLOCK
    key_pos = i * BLOCK + sj
    mask = (jnp.abs(sj - qi) <= WINDOW) & (key_pos >= 0) & (key_pos < seq)
    for kv in range(N_KV_HEADS):
        ks = slice(kv * HEAD_DIM, (kv + 1) * HEAD_DIM)
        k_loc = jnp.concatenate([kp_ref[:, ks], kc_ref[:, ks], kn_ref[:, ks]], axis=0)
        v_loc = jnp.concatenate([vp_ref[:, ks], vc_ref[:, ks], vn_ref[:, ks]], axis=0)
        q4 = _stack_heads(q_ref, kv)
        s_loc = jnp.where(mask, _qk(q4, k_loc), NEG_INF)
        s_ctx = _qk(q4, kx_ref[:, ks])
        s_sink = _sink_column(sink_ref, kv, BLOCK)
        m = jnp.maximum(jnp.maximum(jnp.max(s_loc, axis=-1, keepdims=True),
                                    jnp.max(s_ctx, axis=-1, keepdims=True)), s_sink)
        e_loc = jnp.exp(s_loc - m)
        e_ctx = jnp.exp(s_ctx - m)
        denom = (jnp.exp(s_sink - m) + jnp.sum(e_ctx, axis=-1, keepdims=True)
                 + jnp.sum(e_loc, axis=-1, keepdims=True))
        pv = (jnp.dot(e_ctx.astype(BF16), vx_ref[:, ks], preferred_element_type=F32)
              + jnp.dot(e_loc.astype(BF16), v_loc, preferred_element_type=F32))
        o = pv / denom
        for g in range(GQA_GROUP):
            h = kv * GQA_GROUP + g
            o_ref[:, h * HEAD_DIM:(h + 1) * HEAD_DIM] = o[g * BLOCK:(g + 1) * BLOCK].astype(BF16)


def _window_attention(p, pc, kx_blk, vx_blk, sink):
    b, seq, _ = p.shape
    nb = seq // BLOCK
    kvw = N_KV_HEADS * HEAD_DIM
    k_blk, v_blk = Q_END // kvw, K_END // kvw
    prev = lambda bb, i: (bb, jnp.maximum(i - 1, 0))
    nxt = lambda bb, i: (bb, jnp.minimum(i + 1, nb - 1))
    cur = lambda bb, i: (bb, i)

    def band(which, blk):
        return pl.BlockSpec((None, BLOCK, kvw), lambda bb, i: which(bb, i) + (blk,))

    return pl.pallas_call(
        functools.partial(_window_attn_kernel, seq=seq),
        grid=(b, nb),
        in_specs=[
            pl.BlockSpec(memory_space=pltpu.SMEM),
            pl.BlockSpec((None, BLOCK, ATTN_WIDTH), lambda bb, i: (bb, i, 0)),
            band(prev, k_blk), band(cur, k_blk), band(nxt, k_blk),
            band(prev, v_blk), band(cur, v_blk), band(nxt, v_blk),
            pl.BlockSpec((None, CTX_LEN, kvw), lambda bb, i: (bb, 0, kx_blk)),
            pl.BlockSpec((None, CTX_LEN, kvw), lambda bb, i: (bb, 0, vx_blk)),
        ],
        out_specs=pl.BlockSpec((None, BLOCK, ATTN_WIDTH), lambda bb, i: (bb, i, 0)),
        out_shape=jax.ShapeDtypeStruct((b, seq, ATTN_WIDTH), BF16),
        compiler_params=_cparams(("arbitrary", "arbitrary"), 32),
        name="window_attn",
    )(sink, p, p, p, p, p, p, p, pc, pc)


def _ctx_attn_kernel(sink_ref, q_ref, k_ref, v_ref, o_ref):
    n = q_ref.shape[0]
    for kv in range(N_KV_HEADS):
        ks = slice(kv * HEAD_DIM, (kv + 1) * HEAD_DIM)
        q4 = _stack_heads(q_ref, kv)
        s = _qk(q4, k_ref[:, ks])
        s_sink = _sink_column(sink_ref, kv, n)
        m = jnp.maximum(jnp.max(s, axis=-1, keepdims=True), s_sink)
        e = jnp.exp(s - m)
        denom = jnp.exp(s_sink - m) + jnp.sum(e, axis=-1, keepdims=True)
        o = jnp.dot(e.astype(BF16), v_ref[:, ks], preferred_element_type=F32) / denom
        for g in range(GQA_GROUP):
            h = kv * GQA_GROUP + g
            o_ref[:, h * HEAD_DIM:(h + 1) * HEAD_DIM] = o[g * n:(g + 1) * n].astype(BF16)


def _context_attention(pc, sink):
    b, n, _ = pc.shape
    kvw = N_KV_HEADS * HEAD_DIM
    return pl.pallas_call(
        _ctx_attn_kernel,
        grid=(b,),
        in_specs=[
            pl.BlockSpec(memory_space=pltpu.SMEM),
            pl.BlockSpec((None, n, ATTN_WIDTH), lambda bb: (bb, 0, 0)),
            pl.BlockSpec((None, n, kvw), lambda bb: (bb, 0, Q_END // kvw)),
            pl.BlockSpec((None, n, kvw), lambda bb: (bb, 0, K_END // kvw)),
        ],
        out_specs=pl.BlockSpec((None, n, ATTN_WIDTH), lambda bb: (bb, 0, 0)),
        out_shape=jax.ShapeDtypeStruct((b, n, ATTN_WIDTH), BF16),
        compiler_params=_cparams(("arbitrary",), 32),
        name="ctx_attn",
    )(sink, pc, pc, pc)


def _outproj_kernel(a_ref, y_ref, x_ref, m_ref, wa_ref, wy_ref, o_ref):
    mix = (jnp.dot(a_ref[...], wa_ref[...], preferred_element_type=F32)
           + jnp.dot(y_ref[...], wy_ref[...], preferred_element_type=F32))
    o_ref[...] = x_ref[...] + m_ref[2:3, :] * mix


def _out_projection(attn, hy, x, mod, mod_row, w_bf, *, tm):
    b, t, _ = x.shape
    row = (lambda bb: bb) if mod_row is None else (lambda bb: mod_row)
    return pl.pallas_call(
        _outproj_kernel,
        grid=(b, t // tm),
        in_specs=[
            pl.BlockSpec((None, tm, ATTN_WIDTH), lambda bb, i: (bb, i, 0)),
            pl.BlockSpec((None, tm, HYENA_WIDTH), lambda bb, i: (bb, i, 0)),
            pl.BlockSpec((None, tm, D_MODEL), lambda bb, i: (bb, i, 0)),
            pl.BlockSpec((None, N_MOD, D_MODEL), lambda bb, i: (row(bb), 0, 0)),
            pl.BlockSpec((ATTN_WIDTH, D_MODEL), lambda bb, i: (0, 0)),
            pl.BlockSpec((HYENA_WIDTH, D_MODEL), lambda bb, i: (1, 0)),
        ],
        out_specs=pl.BlockSpec((None, tm, D_MODEL), lambda bb, i: (bb, i, 0)),
        out_shape=jax.ShapeDtypeStruct(x.shape, F32),
        compiler_params=_cparams(("arbitrary", "arbitrary"), 48),
        name="out_proj",
    )(attn, hy, x, mod, w_bf, w_bf)


MLP_OUT_CHUNK = 512


def _mlp_kernel(x_ref, m_ref, nw_ref, w1_ref, w2_ref, o_ref, h_ref, *, n_f):
    f = pl.program_id(2)

    @pl.when(f == 0)
    def _():
        h = _rms_modulate(x_ref[...], nw_ref[...], m_ref[3:4, :], m_ref[4:5, :])
        h_ref[...] = h.astype(BF16)
        o_ref[...] = jnp.zeros_like(o_ref)

    a = jnp.maximum(jnp.dot(h_ref[...], w1_ref[...], preferred_element_type=F32), 0.0)
    a = (a * a).astype(BF16)
    for n0 in range(0, D_MODEL, MLP_OUT_CHUNK):
        sl = slice(n0, n0 + MLP_OUT_CHUNK)
        o_ref[:, sl] += jnp.dot(a, w2_ref[:, sl], preferred_element_type=F32)

    @pl.when(f == n_f - 1)
    def _():
        o_ref[...] = x_ref[...] + m_ref[5:6, :] * o_ref[...]


def _mlp(x, mod, mod_row, nw, w1_bf, w2_bf, *, tm, tf):
    b, t, _ = x.shape
    n_f = D_FF // tf
    row = (lambda bb: bb) if mod_row is None else (lambda bb: mod_row)
    return pl.pallas_call(
        functools.partial(_mlp_kernel, n_f=n_f),
        grid=(b, t // tm, n_f),
        in_specs=[
            pl.BlockSpec((None, tm, D_MODEL), lambda bb, i, f: (bb, i, 0)),
            pl.BlockSpec((None, N_MOD, D_MODEL), lambda bb, i, f: (row(bb), 0, 0)),
            pl.BlockSpec((1, D_MODEL), lambda bb, i, f: (0, 0)),
            pl.BlockSpec((D_MODEL, tf), lambda bb, i, f: (0, f)),
            pl.BlockSpec((tf, D_MODEL), lambda bb, i, f: (f, 0)),
        ],
        out_specs=pl.BlockSpec((None, tm, D_MODEL), lambda bb, i, f: (bb, i, 0)),
        out_shape=jax.ShapeDtypeStruct(x.shape, F32),
        scratch_shapes=[pltpu.VMEM((tm, D_MODEL), BF16)],
        compiler_params=_cparams(("arbitrary", "arbitrary", "arbitrary"), 56),
        name="mlp",
    )(x, mod, nw, w1_bf, w2_bf)


def _p_add(p, q):
    if p is None:
        return q
    if q is None:
        return p
    (a, sa), (b, sb) = p, q
    if sa == sb:
        return (a + b, sa)
    return (a - b, 1) if sa > 0 else (b - a, 1)


def _p_scale(p, k):
    if p is None or k == 0.0:
        return None
    a, s = p
    if k < 0:
        s, k = -s, -k
    return (a, s) if k == 1.0 else (a * k, s)


def _c_add(x, y, sign=1):
    if y is None:
        return x
    if x is None:
        x = (None, None)
    return (_p_add(x[0], _p_scale(y[0], sign)), _p_add(x[1], _p_scale(y[1], sign)))


def _snap(v):
    for t in (0.0, 1.0, -1.0):
        if abs(v - t) < 1e-12:
            return t
    return v


def _c_mulc(x, c):
    if x is None:
        return None
    cr, ci = _snap(c.real), _snap(c.imag)
    re, im = x
    if cr != 0.0 and ci != 0.0 and abs(abs(cr) - abs(ci)) < 1e-12:
        k, sr, si = abs(cr), math.copysign(1.0, cr), math.copysign(1.0, ci)
        return (_p_scale(_p_add(_p_scale(re, sr), _p_scale(im, -si)), k),
                _p_scale(_p_add(_p_scale(re, si), _p_scale(im, sr)), k))
    return (_p_add(_p_scale(re, cr), _p_scale(im, -ci)), _p_add(_p_scale(re, ci), _p_scale(im, cr)))


def _is_zero(x):
    return x is None or (x[0] is None and x[1] is None)


def _slab_fft_dit(xs, sign, need):
    n = len(xs)
    if n == 1:
        return {0: xs[0]}
    if all(_is_zero(x) for x in xs):
        return {k: None for k in need}
    h = n // 2
    sub = {k % h for k in need}
    ev = _slab_fft_dit(xs[0::2], sign, sub)
    od = _slab_fft_dit(xs[1::2], sign, sub)
    out = {}
    for k in sub:
        t = _c_mulc(od[k], cmath.exp(sign * 2j * math.pi * k / n))
        if k in need:
            out[k] = _c_add(ev[k], t)
        if k + h in need:
            out[k + h] = _c_add(ev[k], t, -1)
    return out


def _slab_fft_dif(xs, sign, need):
    n = len(xs)
    if n == 1:
        return {0: xs[0]}
    h = n // 2
    need_e = {k // 2 for k in need if k % 2 == 0}
    need_o = {k // 2 for k in need if k % 2 == 1}
    out = {}
    if need_e:
        ev = _slab_fft_dif([_c_add(xs[j], xs[j + h]) for j in range(h)], sign, need_e)
        out.update({2 * k: v for k, v in ev.items()})
    if need_o:
        dif = [_c_mulc(_c_add(xs[j], xs[j + h], -1), cmath.exp(sign * 2j * math.pi * j / n)) for j in range(h)]
        od = _slab_fft_dif(dif, sign, need_o)
        out.update({2 * k + 1: v for k, v in od.items()})
    return out


def _part_value(p, like):
    if p is None:
        return jnp.zeros_like(like)
    return p[0] if p[1] > 0 else -p[0]


HY_CT = 128
SLAB = 256
N_HI = 2 * SEQ // SLAB
N_IN = SEQ // SLAB
N_SPEC = N_HI // 2 + 1
ROW_CHUNK = 8
N_CT = HYENA_WIDTH // HY_CT
FILTER_HIDDEN = 64
EMB_PAD = 128
HIGHEST = lax.Precision.HIGHEST


def _dft_constants():
    n = 2 * SEQ
    r = np.arange(SLAB)
    fwd = np.zeros((N_SPEC, 2 * SLAB, SLAB))
    inv = np.zeros((N_SPEC, 2 * SLAB, SLAB))
    for b in range(N_SPEC):
        m = np.exp(-2j * np.pi * (np.outer(r, r) / SLAB + (r * b)[None, :] / n))
        a = m.T / n
        fwd[b] = np.concatenate([m.real, m.imag], axis=0)
        inv[b] = np.concatenate([a.real, a.imag], axis=0)
    return jnp.asarray(fwd, F32).astype(BF16), jnp.asarray(inv, F32).astype(BF16)


def _ctx_dft_constants():
    n = 2 * CTX_LEN
    k = np.arange(n)
    f = np.exp(-2j * np.pi * np.outer(k, k) / n)
    a = np.conj(f)[:CTX_LEN, :] / n
    return (jnp.asarray(np.concatenate([f.real, f.imag], axis=0), F32).astype(BF16),
            jnp.asarray(np.concatenate([a.real, a.imag], axis=0), F32).astype(BF16))


def _filter_positions(l):
    t = jnp.linspace(0.0, 1.0, l, dtype=F32)[:, None]
    bands = (FILTER_EMB - 1) // 2
    f = jnp.linspace(1e-4, bands - 1, bands, dtype=F32)[None, :]
    w = 2.0 * math.pi * jnp.arange(l, dtype=F32)[:, None] / l
    z = jnp.concatenate([t, jnp.cos(f * w), -jnp.sin(f * w)], axis=-1)
    z2 = jnp.concatenate([z, z[0:1], z[1:][::-1]], axis=0)
    return jnp.pad(z2, ((0, 0), (0, EMB_PAD - FILTER_EMB)))


def _filter_feat_kernel(z_ref, w1_ref, b1_ref, w2_ref, b2_ref, w3_ref, b3_ref, fr_ref, o_ref):
    fr = fr_ref[...]
    dot = functools.partial(jnp.dot, precision=HIGHEST, preferred_element_type=F32)
    h = jnp.sin(fr * (dot(z_ref[...], w1_ref[...]) + b1_ref[...]))
    h = jnp.sin(fr * (dot(h, w2_ref[...]) + b2_ref[...]))
    o_ref[...] = jnp.sin(fr * (dot(h, w3_ref[...]) + b3_ref[...]))


def _filter_features(emb, w1, b1, w2, b2, w3, b3, freq):
    rows = emb.shape[0]
    tr = min(rows, 1024)
    w1p = jnp.pad(w1, ((0, EMB_PAD - FILTER_EMB), (0, 0)))
    vec = lambda v: v.reshape(1, FILTER_HIDDEN)
    full = lambda shape: pl.BlockSpec(shape, lambda i: (0, 0))
    return pl.pallas_call(
        _filter_feat_kernel,
        grid=(rows // tr,),
        in_specs=[pl.BlockSpec((tr, EMB_PAD), lambda i: (i, 0)),
                  full((EMB_PAD, FILTER_HIDDEN)), full((1, FILTER_HIDDEN)),
                  full((FILTER_HIDDEN, FILTER_HIDDEN)), full((1, FILTER_HIDDEN)),
                  full((FILTER_HIDDEN, FILTER_HIDDEN)), full((1, FILTER_HIDDEN)),
                  full((1, FILTER_HIDDEN))],
        out_specs=pl.BlockSpec((tr, FILTER_HIDDEN), lambda i: (i, 0)),
        out_shape=jax.ShapeDtypeStruct((rows, FILTER_HIDDEN), F32),
        compiler_params=_cparams(("arbitrary",), 32),
        name="filter_feat",
    )(emb, w1p, vec(b1), w2, vec(b2), w3, vec(b3), vec(freq))


def _filter_taps(raw, n, l, absdelta):
    m = jnp.where(n < l, n, 2 * l - n)
    t = m.astype(F32) / (l - 1)
    return jnp.where(n == l, 0.0, raw * jnp.exp(-t * absdelta))


def _cplx_from_pq(pq, conj):
    half = pq.shape[0] // 2
    pr, pi_ = pq[:half, :HY_CT], pq[:half, HY_CT:]
    qr, qi = pq[half:, :HY_CT], pq[half:, HY_CT:]
    if conj:
        return pr + qi, pi_ - qr
    return pr - qi, qr + pi_


def _slab_forward(src_ref, n_in, real_only, dst_ref, need, scale):
    def body(rc, carry):
        rows = pl.ds(pl.multiple_of(rc * ROW_CHUNK, ROW_CHUNK), ROW_CHUNK)
        xs = []
        for a in range(N_HI):
            if a >= n_in:
                xs.append(None)
                continue
            re = src_ref[a, rows, 0:HY_CT]
            if scale is not None:
                re = re * scale
            xs.append(((re, 1), None if real_only else (src_ref[a, rows, HY_CT:2 * HY_CT], 1)))
        like = xs[0][0][0]
        out = _slab_fft_dit(xs, -1, need)
        for b in sorted(need):
            c = out[b] or (None, None)
            dst_ref[b, rows, 0:HY_CT] = _part_value(c[0], like)
            dst_ref[b, rows, HY_CT:2 * HY_CT] = _part_value(c[1], like)
        return carry

    lax.fori_loop(0, SLAB // ROW_CHUNK, body, 0)


def _slab_inverse(s_ref):
    def body(rc, carry):
        rows = pl.ds(pl.multiple_of(rc * ROW_CHUNK, ROW_CHUNK), ROW_CHUNK)
        xs = [((s_ref[b, rows, 0:HY_CT], 1), (s_ref[b, rows, HY_CT:2 * HY_CT], 1)) for b in range(N_HI)]
        like = xs[0][0][0]
        out = _slab_fft_dif(xs, 1, set(range(N_IN)))
        for a in range(N_IN):
            c = out[a] or (None, None)
            s_ref[a, rows, 0:HY_CT] = _part_value(c[0], like)
            s_ref[a, rows, HY_CT:2 * HY_CT] = _part_value(c[1], like)
        return carry

    lax.fori_loop(0, SLAB // ROW_CHUNK, body, 0)


def _filter_spec_kernel(feat_ref, wf_ref, wb_ref, ad_ref, mf_ref, o_ref, ft_ref, s_ref):
    absdelta = ad_ref[...]

    def taps(a, acc):
        rows = feat_ref[pl.ds(pl.multiple_of(a * SLAB, SLAB), SLAB), :]
        w = jnp.where(a < N_IN, wf_ref[...], wb_ref[...])
        raw = jnp.dot(rows, w, precision=HIGHEST, preferred_element_type=F32)
        n = a * SLAB + lax.broadcasted_iota(jnp.int32, raw.shape, 0)
        v = _filter_taps(raw, n, SEQ, absdelta)
        ft_ref[a] = v
        return acc + jnp.sum(jnp.abs(v), axis=0, keepdims=True)

    norm = lax.fori_loop(0, N_HI, taps, jnp.zeros((1, HY_CT), F32))
    _slab_forward(ft_ref, N_HI, True, s_ref, set(range(N_SPEC)), 1.0 / norm)

    def spec(b, carry):
        pq = jnp.dot(mf_ref[b], s_ref[b].astype(BF16), preferred_element_type=F32)
        xr, xi = _cplx_from_pq(pq, False)
        rows = pl.ds(pl.multiple_of(b * SLAB, SLAB), SLAB)
        o_ref[rows, 0:HY_CT] = xr.astype(BF16)
        o_ref[rows, HY_CT:2 * HY_CT] = xi.astype(BF16)
        return carry

    lax.fori_loop(0, N_SPEC, spec, 0)


def _filter_spectrum(feat, w_fout, absdelta, mf):
    cols = 2 * N_CT
    return pl.pallas_call(
        _filter_spec_kernel,
        grid=(HYENA_ORDER, N_CT),
        in_specs=[
            pl.BlockSpec((2 * SEQ, FILTER_HIDDEN), lambda o, c: (0, 0)),
            pl.BlockSpec((FILTER_HIDDEN, HY_CT), lambda o, c: (0, o * cols + c)),
            pl.BlockSpec((FILTER_HIDDEN, HY_CT), lambda o, c: (0, o * cols + N_CT + c)),
            pl.BlockSpec((1, HY_CT), lambda o, c: (0, c)),
            pl.BlockSpec((N_SPEC, 2 * SLAB, SLAB), lambda o, c: (0, 0, 0)),
        ],
        out_specs=pl.BlockSpec((None, None, N_SPEC * SLAB, 2 * HY_CT), lambda o, c: (o, c, 0, 0)),
        out_shape=jax.ShapeDtypeStruct((HYENA_ORDER, N_CT, N_SPEC * SLAB, 2 * HY_CT), BF16),
        scratch_shapes=[pltpu.VMEM((N_HI, SLAB, HY_CT), F32), pltpu.VMEM((N_SPEC, SLAB, 2 * HY_CT), F32)],
        compiler_params=_cparams(("arbitrary", "arbitrary"), 48),
        name="filter_spec",
    )(feat, w_fout, w_fout, absdelta, mf)


def _conv_slab(u_ref, bi, a, w, b):
    r0 = a * SLAB
    x = u_ref[bi, pl.ds(pl.multiple_of(r0, SLAB), SLAB), :].astype(F32)
    before = u_ref[bi, pl.ds(pl.multiple_of(jnp.maximum(r0 - 16, 0), 16), 16), :].astype(F32)
    after = u_ref[bi, pl.ds(pl.multiple_of(jnp.minimum(r0 + SLAB, SEQ - 16), 16), 16), :].astype(F32)
    prev_row = jnp.where(a > 0, before[15:16, :], 0.0)
    next_row = jnp.where(a < N_IN - 1, after[0:1, :], 0.0)
    row = lax.broadcasted_iota(jnp.int32, x.shape, 0)
    xm1 = jnp.where(row == 0, prev_row, pltpu.roll(x, 1, 0))
    xp1 = jnp.where(row == SLAB - 1, next_row, pltpu.roll(x, SLAB - 1, 0))
    return xm1 * w[0:1, :] + x * w[1:2, :] + xp1 * w[2:3, :] + b


def _hyena_kernel(uv_ref, u1_ref, u2_ref, cwv_ref, cw1_ref, cw2_ref, cbv_ref, cb1_ref, cb2_ref, skip_ref,
                  mf_ref, mi_ref, h_ref, o_ref, zc_ref, g_ref, s_ref):
    o = pl.program_id(2)

    @pl.when(o == 0)
    def _():
        def conv(a, carry):
            for bi in range(2):
                lanes = slice(bi * HY_CT, (bi + 1) * HY_CT)
                zc_ref[a, :, lanes] = _conv_slab(uv_ref, bi, a, cwv_ref[...], cbv_ref[...])
                g_ref[0, a, :, lanes] = _conv_slab(u1_ref, bi, a, cw1_ref[...], cb1_ref[...]).astype(BF16)
                g_ref[1, a, :, lanes] = _conv_slab(u2_ref, bi, a, cw2_ref[...], cb2_ref[...]).astype(BF16)
            return carry

        lax.fori_loop(0, N_IN, conv, 0)

    _slab_forward(zc_ref, N_IN, False, s_ref, set(range(N_HI)), None)

    def through_filter(b, j, conj):
        h = h_ref[pl.ds(pl.multiple_of(j * SLAB, SLAB), SLAB), :].astype(F32)
        hr, hi = h[:, :HY_CT], h[:, HY_CT:]
        if conj:
            hi = -hi
        xr, xi = _cplx_from_pq(jnp.dot(mf_ref[j], s_ref[b].astype(BF16), preferred_element_type=F32), conj)
        y = jnp.concatenate([xr * hr - xi * hi, xr * hi + xi * hr], axis=1).astype(BF16)
        wr, wi = _cplx_from_pq(jnp.dot(mi_ref[j], y, preferred_element_type=F32), not conj)
        s_ref[b, :, 0:HY_CT] = wr
        s_ref[b, :, HY_CT:2 * HY_CT] = wi

    through_filter(0, 0, False)
    through_filter(N_HI // 2, N_HI // 2, False)

    def pair(j, carry):
        through_filter(j, j, False)
        through_filter(N_HI - j, j, True)
        return carry

    lax.fori_loop(1, N_HI // 2, pair, 0)
    _slab_inverse(s_ref)

    sk = skip_ref[pl.ds(o, 1), :]
    skip2 = jnp.concatenate([sk, sk], axis=1)

    def gate(a, carry):
        zc_ref[a] = g_ref[o, a].astype(F32) * (s_ref[a] + skip2 * zc_ref[a])
        return carry

    lax.fori_loop(0, N_IN, gate, 0)

    @pl.when(o == HYENA_ORDER - 1)
    def _():
        def emit(a, carry):
            rows = pl.ds(pl.multiple_of(a * SLAB, SLAB), SLAB)
            for bi in range(2):
                o_ref[bi, rows, :] = zc_ref[a, :, bi * HY_CT:(bi + 1) * HY_CT].astype(BF16)
            return carry

        lax.fori_loop(0, N_IN, emit, 0)


def _hyena_conv(p, conv_w, conv_b, skip, mf, mi, hspec):
    b = p.shape[0]
    u0 = V_END // HY_CT
    cb = conv_b.reshape(1, -1)
    part = lambda k: pl.BlockSpec((2, SEQ, HY_CT), lambda c, q, o: (q, 0, u0 + k * N_CT + c))
    cw = lambda k: pl.BlockSpec((SHORT_CONV, HY_CT), lambda c, q, o: (0, k * N_CT + c))
    cbs = lambda k: pl.BlockSpec((1, HY_CT), lambda c, q, o: (0, k * N_CT + c))
    const = lambda: pl.BlockSpec((N_SPEC, 2 * SLAB, SLAB), lambda c, q, o: (0, 0, 0), pipeline_mode=pl.Buffered(1))
    return pl.pallas_call(
        _hyena_kernel,
        grid=(N_CT, b // 2, HYENA_ORDER),
        in_specs=[part(0), part(1), part(2), cw(0), cw(1), cw(2), cbs(0), cbs(1), cbs(2),
                  pl.BlockSpec((HYENA_ORDER, HY_CT), lambda c, q, o: (0, c)),
                  const(), const(),
                  pl.BlockSpec((None, None, N_SPEC * SLAB, 2 * HY_CT), lambda c, q, o: (o, c, 0, 0))],
        out_specs=pl.BlockSpec((2, SEQ, HY_CT), lambda c, q, o: (q, 0, c)),
        out_shape=jax.ShapeDtypeStruct((b, SEQ, HYENA_WIDTH), BF16),
        scratch_shapes=[pltpu.VMEM((N_IN, SLAB, 2 * HY_CT), F32),
                        pltpu.VMEM((HYENA_ORDER, N_IN, SLAB, 2 * HY_CT), BF16),
                        pltpu.VMEM((N_HI, SLAB, 2 * HY_CT), F32)],
        compiler_params=_cparams(("arbitrary", "arbitrary", "arbitrary"), 56),
        name="hyena_conv",
    )(p, p, p, conv_w, conv_w, conv_w, cb, cb, cb, skip, mf, mi, hspec)


def _ctx_hyena_kernel(uv_ref, u1_ref, u2_ref, cwv_ref, cw1_ref, cw2_ref, cbv_ref, cb1_ref, cb2_ref, skip_ref,
                      feat_ref, wf0_ref, wb0_ref, wf1_ref, wb1_ref, ad_ref, ff_ref, ai_ref, o_ref):
    l = CTX_LEN
    nb = uv_ref.shape[0]
    n = lax.broadcasted_iota(jnp.int32, (2 * l, HY_CT), 0)
    spectra = []
    for wf_ref, wb_ref in ((wf0_ref, wb0_ref), (wf1_ref, wb1_ref)):
        raw = jnp.concatenate(
            [jnp.dot(feat_ref[0:l, :], wf_ref[...], precision=HIGHEST, preferred_element_type=F32),
             jnp.dot(feat_ref[l:2 * l, :], wb_ref[...], precision=HIGHEST, preferred_element_type=F32)], axis=0)
        v = _filter_taps(raw, n, l, ad_ref[...])
        v = v / jnp.sum(jnp.abs(v), axis=0, keepdims=True)
        pq = jnp.dot(ff_ref[...], v.astype(BF16), preferred_element_type=F32)
        spectra.append((pq[:2 * l], pq[2 * l:]))

    row = lax.broadcasted_iota(jnp.int32, (l, HY_CT), 0)

    def conv(u_ref, bi, w_ref, b_ref):
        x = u_ref[bi].astype(F32)
        xm1 = jnp.where(row == 0, 0.0, pltpu.roll(x, 1, 0))
        xp1 = jnp.where(row == l - 1, 0.0, pltpu.roll(x, l - 1, 0))
        return xm1 * w_ref[0:1, :] + x * w_ref[1:2, :] + xp1 * w_ref[2:3, :] + b_ref[...]

    z = [conv(uv_ref, bi, cwv_ref, cbv_ref) for bi in range(nb)]
    gates = [[conv(u1_ref, bi, cw1_ref, cb1_ref) for bi in range(nb)],
             [conv(u2_ref, bi, cw2_ref, cb2_ref) for bi in range(nb)]]
    for o in range(HYENA_ORDER):
        hr, hi = spectra[o]
        pq = jnp.dot(ff_ref[:, 0:l], jnp.concatenate(z, axis=1).astype(BF16), preferred_element_type=F32)
        ys = []
        for q in range(nb // 2):
            xr, xi = _cplx_from_pq(pq[:, 2 * q * HY_CT:2 * (q + 1) * HY_CT], False)
            ys += [xr * hr - xi * hi, xr * hi + xi * hr]
        pq2 = jnp.dot(ai_ref[...], jnp.concatenate(ys, axis=1).astype(BF16), preferred_element_type=F32)
        y = []
        for q in range(nb // 2):
            y += list(_cplx_from_pq(pq2[:, 2 * q * HY_CT:2 * (q + 1) * HY_CT], False))
        z = [gates[o][bi] * (y[bi] + skip_ref[o:o + 1, :] * z[bi]) for bi in range(nb)]
    for bi in range(nb):
        o_ref[bi] = z[bi].astype(BF16)


def _ctx_hyena(pc, conv_w, conv_b, skip, feat, w_fout, absdelta, ff, ai):
    b, l, _ = pc.shape
    u0 = V_END // HY_CT
    cb = conv_b.reshape(1, -1)
    cols = 2 * N_CT
    part = lambda k: pl.BlockSpec((b, l, HY_CT), lambda c: (0, 0, u0 + k * N_CT + c))
    cw = lambda k: pl.BlockSpec((SHORT_CONV, HY_CT), lambda c: (0, k * N_CT + c))
    cbs = lambda k: pl.BlockSpec((1, HY_CT), lambda c: (0, k * N_CT + c))
    wcol = lambda off: pl.BlockSpec((FILTER_HIDDEN, HY_CT), lambda c: (0, off + c))
    return pl.pallas_call(
        _ctx_hyena_kernel,
        grid=(N_CT,),
        in_specs=[part(0), part(1), part(2), cw(0), cw(1), cw(2), cbs(0), cbs(1), cbs(2),
                  pl.BlockSpec((HYENA_ORDER, HY_CT), lambda c: (0, c)),
                  pl.BlockSpec((2 * l, FILTER_HIDDEN), lambda c: (0, 0)),
                  wcol(0), wcol(N_CT), wcol(cols), wcol(cols + N_CT),
                  pl.BlockSpec((1, HY_CT), lambda c: (0, c)),
                  pl.BlockSpec((4 * l, 2 * l), lambda c: (0, 0)),
                  pl.BlockSpec((2 * l, 2 * l), lambda c: (0, 0))],
        out_specs=pl.BlockSpec((b, l, HY_CT), lambda c: (0, 0, c)),
        out_shape=jax.ShapeDtypeStruct((b, l, HYENA_WIDTH), BF16),
        compiler_params=_cparams(("arbitrary",), 32),
        name="ctx_hyena",
    )(pc, pc, pc, conv_w, conv_w, conv_w, cb, cb, cb, skip, feat, w_fout, w_fout, w_fout, w_fout, absdelta, ff, ai)


def _rope_tables(l):
    rows = l // GRID_W
    row = jnp.repeat(jnp.arange(rows), GRID_W).astype(F32)
    col = jnp.tile(jnp.arange(GRID_W), rows).astype(F32)
    n_freq = HEAD_DIM // 4
    inv = ROPE_BASE ** (-jnp.arange(n_freq, dtype=F32) / n_freq)
    ang_r = row[:, None] * inv
    ang_c = col[:, None] * inv
    cos = jnp.concatenate([jnp.cos(ang_r), jnp.cos(ang_r), jnp.cos(ang_c), jnp.cos(ang_c)], axis=-1)
    sin = jnp.concatenate([-jnp.sin(ang_r), jnp.sin(ang_r), -jnp.sin(ang_c), jnp.sin(ang_c)], axis=-1)
    return cos, sin


def kernel(x, c, ctx, c_ctx, w_ada, b_ada, norm_mix, norm_mlp, w_in, q_norm, k_norm, attn_sink, conv_w, conv_b, filt_w1, filt_b1, filt_w2, filt_b2, filt_w3, filt_b3, filt_freq, filt_w_out, hyena_skip, w_out, w_mlp_in, w_mlp_out):
    cos, sin = _rope_tables(SEQ)
    emb, emb_ctx = _filter_positions(SEQ), _filter_positions(CTX_LEN)
    mf, mi = _dft_constants()
    ff_ctx, ai_ctx = _ctx_dft_constants()
    absdelta = jnp.abs(jnp.linspace(MIN_DECAY, MAX_DECAY, HYENA_WIDTH, dtype=F32)).reshape(1, HYENA_WIDTH)
    cvec =jnp.zeros((MOD_ROWS, D_MODEL), F32).at[:BATCH].set(c).at[CTX_ROW].set(c_ctx)
    mod_all = _modulation(cvec, w_ada, b_ada).reshape(DEPTH, MOD_ROWS, N_MOD, D_MODEL)
    xc = ctx
    for i in range(DEPTH):
        update_ctx = i < DEPTH - 1
        mod = mod_all[i]
        w_in_bf = w_in[i].astype(BF16)
        w_out_bf = w_out[i].astype(BF16)
        w1_bf = w_mlp_in[i].astype(BF16)
        w2_bf = w_mlp_out[i].astype(BF16)
        nmix = norm_mix[i].reshape(1, D_MODEL)
        nmlp = norm_mlp[i].reshape(1, D_MODEL)
        qn = q_norm[i].reshape(1, HEAD_DIM)
        kn = k_norm[i].reshape(1, HEAD_DIM)
        filt_mlp = (filt_w1[i], filt_b1[i], filt_w2[i], filt_b2[i], filt_w3[i], filt_b3[i], filt_freq[i])

        p = _in_projection(x, mod, None, nmix, w_in_bf, qn, kn, cos, sin,
                           tm=1024, n_off=0, n_tiles=IN_COLS // IN_TN, use_rope=True)
        if update_ctx:
            pc = _in_projection(xc, mod, CTX_ROW, nmix, w_in_bf, qn, kn, cos, sin,
                                tm=CTX_LEN, n_off=0, n_tiles=IN_COLS // IN_TN, use_rope=False)
            kx_blk, vx_blk = Q_END // KV_COLS * 2, Q_END // KV_COLS * 2 + 1
        else:
            pc = _in_projection(xc, mod, CTX_ROW, nmix, w_in_bf, qn, kn, cos, sin,
                                tm=CTX_LEN, n_off=KV_TILE, n_tiles=1, use_rope=False)
            kx_blk, vx_blk = 0, 1

        attn = _window_attention(p, pc, kx_blk, vx_blk, attn_sink[i])
        hspec = _filter_spectrum(_filter_features(emb, *filt_mlp), filt_w_out[i], absdelta, mf)
        hy = _hyena_conv(p, conv_w[i], conv_b[i], hyena_skip[i], mf, mi, hspec)
        x = _out_projection(attn, hy, x, mod, None, w_out_bf, tm=512)
        x = _mlp(x, mod, None, nmlp, w1_bf, w2_bf, tm=1024, tf=512)

        if update_ctx:
            attn_c = _context_attention(pc, attn_sink[i])
            hy_c = _ctx_hyena(pc, conv_w[i], conv_b[i], hyena_skip[i], _filter_features(emb_ctx, *filt_mlp),
                              filt_w_out[i], absdelta, ff_ctx, ai_ctx)
            xc = _out_projection(attn_c, hy_c, xc, mod, CTX_ROW, w_out_bf, tm=CTX_LEN)
            xc = _mlp(xc, mod, CTX_ROW, nmlp, w1_bf, w2_bf, tm=CTX_LEN, tf=512)
    return x
```

```python
import cmath
import functools
import math

import jax
import jax.numpy as jnp
import numpy as np
from jax import lax
from jax.experimental import pallas as pl
from jax.experimental.pallas import tpu as pltpu

D_MODEL = 2048
BATCH = 4
SEQ = 4096
DEPTH = 2
GRID_W = 64
CTX_LEN = 256
HEAD_DIM = 128
N_Q_HEADS = 8
N_KV_HEADS = 2
GQA_GROUP = N_Q_HEADS // N_KV_HEADS
ATTN_WIDTH = N_Q_HEADS * HEAD_DIM
HYENA_WIDTH = D_MODEL - ATTN_WIDTH
HYENA_ORDER = 2
SHORT_CONV = 3
FILTER_EMB = 33
DECAY_TARGET = 1e-2
FAST_DECAY_PCT = 0.3
SLOW_DECAY_PCT = 1.5
MAX_DECAY = math.log(DECAY_TARGET) / FAST_DECAY_PCT
MIN_DECAY = math.log(DECAY_TARGET) / SLOW_DECAY_PCT
WINDOW = 128
BLOCK = 128
ROPE_BASE = 10000.0
D_FF = 4 * D_MODEL
N_MOD = 6
EPS = 1e-6
NEG_INF = -1e30
Q_END = ATTN_WIDTH
K_END = Q_END + N_KV_HEADS * HEAD_DIM
V_END = K_END + N_KV_HEADS * HEAD_DIM
IN_COLS = V_END + (HYENA_ORDER + 1) * HYENA_WIDTH

MOD_ROWS = 8
CTX_ROW = BATCH
KV_COLS = 2 * N_KV_HEADS * HEAD_DIM

F32 = jnp.float32
BF16 = jnp.bfloat16
MIB = 1024 * 1024


def _cparams(sem, vmem_mib):
    return pltpu.CompilerParams(dimension_semantics=sem, vmem_limit_bytes=vmem_mib * MIB)


def _mod_kernel(c_ref, w_ref, b_ref, o_ref):
    c = c_ref[...]
    s = (c * jax.nn.sigmoid(c)).astype(BF16)
    o_ref[...] = jnp.dot(s, w_ref[...].astype(BF16), preferred_element_type=F32) + b_ref[...]


def _modulation(cvec, w_ada, b_ada):
    tn = 1024
    n_cols = N_MOD * D_MODEL
    return pl.pallas_call(
        _mod_kernel,
        grid=(DEPTH, n_cols // tn),
        in_specs=[
            pl.BlockSpec((MOD_ROWS, D_MODEL), lambda l, j: (0, 0)),
            pl.BlockSpec((None, D_MODEL, tn), lambda l, j: (l, 0, j)),
            pl.BlockSpec((None, 1, tn), lambda l, j: (l, 0, j)),
        ],
        out_specs=pl.BlockSpec((None, MOD_ROWS, tn), lambda l, j: (l, 0, j)),
        out_shape=jax.ShapeDtypeStruct((DEPTH, MOD_ROWS, n_cols), F32),
        compiler_params=_cparams(("arbitrary", "arbitrary"), 40),
        name="adaln_mod",
    )(cvec, w_ada, b_ada.reshape(DEPTH, 1, n_cols))


IN_TN = 512
Q_TILES = Q_END // IN_TN
KV_TILE = Q_TILES
HEADS_PER_TILE = IN_TN // HEAD_DIM


SUB_ROWS = 256


def _rms_modulate(xf, gain, shift):
    ms = jnp.mean(xf * xf, axis=-1, keepdims=True)
    return xf * lax.rsqrt(ms + EPS) * gain + shift


def _head_norm_rope(a, g, cos, sin):
    ms = jnp.mean(a * a, axis=-1, keepdims=True)
    y = a * lax.rsqrt(ms + EPS) * g
    if cos is None:
        return y
    lane = lax.broadcasted_iota(jnp.int32, y.shape, 1)
    partner = jnp.where((lane % 64) < 32, pltpu.roll(y, 96, 1), pltpu.roll(y, 32, 1))
    return y * cos + partner * sin


def _inproj_kernel(x_ref, m_ref, nw_ref, w_ref, qn_ref, kn_ref, cos_ref, sin_ref, o_ref, *, n_off, n_tiles, use_rope):
    gain = nw_ref[...] * (1.0 + m_ref[1:2, :])
    shift = m_ref[0:1, :]
    for r0 in range(0, x_ref.shape[0], SUB_ROWS):
        rows = slice(r0, r0 + SUB_ROWS)
        h = _rms_modulate(x_ref[rows, :], gain, shift).astype(BF16)
        cos = cos_ref[rows, :] if use_rope else None
        sin = sin_ref[rows, :] if use_rope else None
        for n in range(n_tiles):
            cols = slice(n * IN_TN, (n + 1) * IN_TN)
            acc = jnp.dot(h, w_ref[:, cols], preferred_element_type=F32)
            t = n + n_off
            if t > KV_TILE:
                o_ref[rows, cols] = acc.astype(BF16)
                continue
            gains = [qn_ref[...]] * HEADS_PER_TILE if t < Q_TILES else [kn_ref[...]] * N_KV_HEADS + [None] * N_KV_HEADS
            for c, g in enumerate(gains):
                sl = slice(c * HEAD_DIM, (c + 1) * HEAD_DIM)
                osl = slice(n * IN_TN + c * HEAD_DIM, n * IN_TN + (c + 1) * HEAD_DIM)
                if g is None:
                    o_ref[rows, osl] = acc[:, sl].astype(BF16)
                else:
                    o_ref[rows, osl] = _head_norm_rope(acc[:, sl], g, cos, sin).astype(BF16)


def _in_projection(x, mod, mod_row, nw, w_bf, layer, qn, kn, cos, sin, *, tm, n_off, n_tiles, use_rope):
    b, t, _ = x.shape
    assert n_off % n_tiles == 0 and tm % SUB_ROWS == 0
    row = (lambda bb: bb) if mod_row is None else (lambda bb: mod_row)
    kern = functools.partial(_inproj_kernel, n_off=n_off, n_tiles=n_tiles, use_rope=use_rope)
    width = n_tiles * IN_TN
    return pl.pallas_call(
        kern,
        grid=(b, t // tm),
        in_specs=[
            pl.BlockSpec((None, tm, D_MODEL), lambda bb, i: (bb, i, 0)),
            pl.BlockSpec((None, N_MOD, D_MODEL), lambda bb, i: (row(bb), 0, 0)),
            pl.BlockSpec((1, D_MODEL), lambda bb, i: (0, 0)),
            pl.BlockSpec((None, D_MODEL, width), lambda bb, i: (layer, 0, n_off // n_tiles),
                         pipeline_mode=pl.Buffered(1)),
            pl.BlockSpec((1, HEAD_DIM), lambda bb, i: (0, 0)),
            pl.BlockSpec((1, HEAD_DIM), lambda bb, i: (0, 0)),
            pl.BlockSpec((tm, HEAD_DIM), lambda bb, i: (i, 0)),
            pl.BlockSpec((tm, HEAD_DIM), lambda bb, i: (i, 0)),
        ],
        out_specs=pl.BlockSpec((None, tm, width), lambda bb, i: (bb, i, 0)),
        out_shape=jax.ShapeDtypeStruct((b, t, width), BF16),
        compiler_params=_cparams(("arbitrary", "arbitrary"), 48),
        name="in_proj",
    )(x, mod, nw, w_bf, qn, kn, cos, sin)


def _sink_column(sink_ref, kv, rows_per_head):
    rows = GQA_GROUP * rows_per_head
    head = lax.broadcasted_iota(jnp.int32, (rows, 1), 0) // rows_per_head
    col = jnp.full((rows, 1), sink_ref[kv * GQA_GROUP], F32)
    for g in range(1, GQA_GROUP):
        col = jnp.where(head == g, sink_ref[kv * GQA_GROUP + g], col)
    return col


def _stack_heads(q_ref, kv):
    return jnp.concatenate(
        [q_ref[:, (kv * GQA_GROUP + g) * HEAD_DIM:(kv * GQA_GROUP + g + 1) * HEAD_DIM]
         for g in range(GQA_GROUP)], axis=0)


def _qk(q, k):
    return lax.dot_general(q, k, (((1,), (1,)), ((), ())), preferred_element_type=F32) * (HEAD_DIM ** -0.5)


def _window_attn_kernel(sink_ref, q_ref, kp_ref, kc_ref, kn_ref, vp_ref, vc_ref, vn_ref, kx_ref, vx_ref, o_ref,
                        *, seq):
    i = pl.program_id(1)
    rows = GQA_GROUP * BLOCK
    qi = lax.broadcasted_iota(jnp.int32, (rows, 3 * BLOCK), 0) % BLOCK
    sj = lax.broadcasted_iota(jnp.int32, (rows, 3 * BLOCK), 1) - BLOCK
    key_pos = i * BLOCK + sj
    mask = (jnp.abs(sj - qi) <= WINDOW) & (key_pos >= 0) & (key_pos < seq)
    for kv in range(N_KV_HEADS):
        ks = slice(kv * HEAD_DIM, (kv + 1) * HEAD_DIM)
        k_loc = jnp.concatenate([kp_ref[:, ks], kc_ref[:, ks], kn_ref[:, ks]], axis=0)
        v_loc = jnp.concatenate([vp_ref[:, ks], vc_ref[:, ks], vn_ref[:, ks]], axis=0)
        q4 = _stack_heads(q_ref, kv)
        s_loc = jnp.where(mask, _qk(q4, k_loc), NEG_INF)
        s_ctx = _qk(q4, kx_ref[:, ks])
        s_sink = _sink_column(sink_ref, kv, BLOCK)
        m = jnp.maximum(jnp.maximum(jnp.max(s_loc, axis=-1, keepdims=True),
                                    jnp.max(s_ctx, axis=-1, keepdims=True)), s_sink)
        e_loc = jnp.exp(s_loc - m)
        e_ctx = jnp.exp(s_ctx - m)
        denom = (jnp.exp(s_sink - m) + jnp.sum(e_ctx, axis=-1, keepdims=True)
                 + jnp.sum(e_loc, axis=-1, keepdims=True))
        pv = (jnp.dot(e_ctx.astype(BF16), vx_ref[:, ks], preferred_element_type=F32)
              + jnp.dot(e_loc.astype(BF16), v_loc, preferred_element_type=F32))
        o = pv / denom
        for g in range(GQA_GROUP):
            h = kv * GQA_GROUP + g
            o_ref[:, h * HEAD_DIM:(h + 1) * HEAD_DIM] = o[g * BLOCK:(g + 1) * BLOCK].astype(BF16)


def _window_attention(p, pc, kx_blk, vx_blk, sink):
    b, seq, _ = p.shape
    nb = seq // BLOCK
    kvw = N_KV_HEADS * HEAD_DIM
    k_blk, v_blk = Q_END // kvw, K_END // kvw
    prev = lambda bb, i: (bb, jnp.maximum(i - 1, 0))
    nxt = lambda bb, i: (bb, jnp.minimum(i + 1, nb - 1))
    cur = lambda bb, i: (bb, i)

    def band(which, blk):
        return pl.BlockSpec((None, BLOCK, kvw), lambda bb, i: which(bb, i) + (blk,))

    return pl.pallas_call(
        functools.partial(_window_attn_kernel, seq=seq),
        grid=(b, nb),
        in_specs=[
            pl.BlockSpec(memory_space=pltpu.SMEM),
            pl.BlockSpec((None, BLOCK, ATTN_WIDTH), lambda bb, i: (bb, i, 0)),
            band(prev, k_blk), band(cur, k_blk), band(nxt, k_blk),
            band(prev, v_blk), band(cur, v_blk), band(nxt, v_blk),
            pl.BlockSpec((None, CTX_LEN, kvw), lambda bb, i: (bb, 0, kx_blk)),
            pl.BlockSpec((None, CTX_LEN, kvw), lambda bb, i: (bb, 0, vx_blk)),
        ],
        out_specs=pl.BlockSpec((None, BLOCK, ATTN_WIDTH), lambda bb, i: (bb, i, 0)),
        out_shape=jax.ShapeDtypeStruct((b, seq, ATTN_WIDTH), BF16),
        compiler_params=_cparams(("arbitrary", "arbitrary"), 32),
        name="window_attn",
    )(sink, p, p, p, p, p, p, p, pc, pc)


def _ctx_attn_kernel(sink_ref, q_ref, k_ref, v_ref, o_ref):
    n = q_ref.shape[0]
    for kv in range(N_KV_HEADS):
        ks = slice(kv * HEAD_DIM, (kv + 1) * HEAD_DIM)
        q4 = _stack_heads(q_ref, kv)
        s = _qk(q4, k_ref[:, ks])
        s_sink = _sink_column(sink_ref, kv, n)
        m = jnp.maximum(jnp.max(s, axis=-1, keepdims=True), s_sink)
        e = jnp.exp(s - m)
        denom = jnp.exp(s_sink - m) + jnp.sum(e, axis=-1, keepdims=True)
        o = jnp.dot(e.astype(BF16), v_ref[:, ks], preferred_element_type=F32) / denom
        for g in range(GQA_GROUP):
            h = kv * GQA_GROUP + g
            o_ref[:, h * HEAD_DIM:(h + 1) * HEAD_DIM] = o[g * n:(g + 1) * n].astype(BF16)


def _context_attention(pc, sink):
    b, n, _ = pc.shape
    kvw = N_KV_HEADS * HEAD_DIM
    return pl.pallas_call(
        _ctx_attn_kernel,
        grid=(b,),
        in_specs=[
            pl.BlockSpec(memory_space=pltpu.SMEM),
            pl.BlockSpec((None, n, ATTN_WIDTH), lambda bb: (bb, 0, 0)),
            pl.BlockSpec((None, n, kvw), lambda bb: (bb, 0, Q_END // kvw)),
            pl.BlockSpec((None, n, kvw), lambda bb: (bb, 0, K_END // kvw)),
        ],
        out_specs=pl.BlockSpec((None, n, ATTN_WIDTH), lambda bb: (bb, 0, 0)),
        out_shape=jax.ShapeDtypeStruct((b, n, ATTN_WIDTH), BF16),
        compiler_params=_cparams(("arbitrary",), 32),
        name="ctx_attn",
    )(sink, pc, pc, pc)


def _outproj_kernel(a_ref, y_ref, x_ref, m_ref, nw_ref, wa_ref, wy_ref, o_ref, h_ref):
    gain = nw_ref[...] * (1.0 + m_ref[4:5, :])
    shift = m_ref[3:4, :]
    for r0 in range(0, x_ref.shape[0], SUB_ROWS):
        rows = slice(r0, r0 + SUB_ROWS)
        mix = (jnp.dot(a_ref[rows, :], wa_ref[...], preferred_element_type=F32)
               + jnp.dot(y_ref[rows, :], wy_ref[...], preferred_element_type=F32))
        x1 = x_ref[rows, :] + m_ref[2:3, :] * mix
        o_ref[rows, :] = x1
        h_ref[rows, :] = _rms_modulate(x1, gain, shift).astype(BF16)


def _out_projection(attn, hy, x, mod, mod_row, nw, w_bf, layer, *, tm):
    b, t, _ = x.shape
    assert tm % SUB_ROWS == 0
    row = (lambda bb: bb) if mod_row is None else (lambda bb: mod_row)
    tile = lambda width: pl.BlockSpec((None, tm, width), lambda bb, i: (bb, i, 0))
    half = lambda k: pl.BlockSpec((None, D_MODEL // 2, D_MODEL), lambda bb, i: (layer, k, 0),
                                  pipeline_mode=pl.Buffered(1))
    return pl.pallas_call(
        _outproj_kernel,
        grid=(b, t // tm),
        in_specs=[
            tile(ATTN_WIDTH), tile(HYENA_WIDTH), tile(D_MODEL),
            pl.BlockSpec((None, N_MOD, D_MODEL), lambda bb, i: (row(bb), 0, 0)),
            pl.BlockSpec((1, D_MODEL), lambda bb, i: (0, 0)),
            half(0), half(1),
        ],
        out_specs=[tile(D_MODEL), tile(D_MODEL)],
        out_shape=[jax.ShapeDtypeStruct(x.shape, F32), jax.ShapeDtypeStruct(x.shape, BF16)],
        compiler_params=_cparams(("arbitrary", "arbitrary"), 48),
        name="out_proj",
    )(attn, hy, x, mod, nw, w_bf, w_bf)


MLP_OUT_CHUNK = 512


def _mlp_kernel(h_ref, x_ref, m_ref, w1_ref, w2_ref, o_ref, *, n_f):
    f = pl.program_id(2)

    @pl.when(f == 0)
    def _():
        o_ref[...] = jnp.zeros_like(o_ref)

    a = jnp.maximum(jnp.dot(h_ref[...], w1_ref[...], preferred_element_type=F32), 0.0)
    a = (a * a).astype(BF16)
    for n0 in range(0, D_MODEL, MLP_OUT_CHUNK):
        sl = slice(n0, n0 + MLP_OUT_CHUNK)
        o_ref[:, sl] += jnp.dot(a, w2_ref[:, sl], preferred_element_type=F32)

    @pl.when(f == n_f - 1)
    def _():
        o_ref[...] = x_ref[...] + m_ref[5:6, :] * o_ref[...]


def _mlp(h, x, mod, mod_row, w1_bf, w2_bf, layer, *, tm, tf):
    b, t, _ = x.shape
    n_f = D_FF // tf
    row = (lambda bb: bb) if mod_row is None else (lambda bb: mod_row)
    tile = pl.BlockSpec((None, tm, D_MODEL), lambda bb, i, f: (bb, i, 0))
    return pl.pallas_call(
        functools.partial(_mlp_kernel, n_f=n_f),
        grid=(b, t // tm, n_f),
        in_specs=[
            tile, tile,
            pl.BlockSpec((None, N_MOD, D_MODEL), lambda bb, i, f: (row(bb), 0, 0)),
            pl.BlockSpec((None, D_MODEL, tf), lambda bb, i, f: (layer, 0, f)),
            pl.BlockSpec((None, tf, D_MODEL), lambda bb, i, f: (layer, f, 0)),
        ],
        out_specs=tile,
        out_shape=jax.ShapeDtypeStruct(x.shape, F32),
        compiler_params=_cparams(("arbitrary", "arbitrary", "arbitrary"), 56),
        name="mlp",
    )(h, x, mod, w1_bf, w2_bf)


def _p_add(p, q):
    if p is None:
        return q
    if q is None:
        return p
    (a, sa), (b, sb) = p, q
    if sa == sb:
        return (a + b, sa)
    return (a - b, 1) if sa > 0 else (b - a, 1)


def _p_scale(p, k):
    if p is None or k == 0.0:
        return None
    a, s = p
    if k < 0:
        s, k = -s, -k
    return (a, s) if k == 1.0 else (a * k, s)


def _c_add(x, y, sign=1):
    if y is None:
        return x
    if x is None:
        x = (None, None)
    return (_p_add(x[0], _p_scale(y[0], sign)), _p_add(x[1], _p_scale(y[1], sign)))


def _snap(v):
    for t in (0.0, 1.0, -1.0):
        if abs(v - t) < 1e-12:
            return t
    return v


def _c_mulc(x, c):
    if x is None:
        return None
    cr, ci = _snap(c.real), _snap(c.imag)
    re, im = x
    if cr != 0.0 and ci != 0.0 and abs(abs(cr) - abs(ci)) < 1e-12:
        k, sr, si = abs(cr), math.copysign(1.0, cr), math.copysign(1.0, ci)
        return (_p_scale(_p_add(_p_scale(re, sr), _p_scale(im, -si)), k),
                _p_scale(_p_add(_p_scale(re, si), _p_scale(im, sr)), k))
    return (_p_add(_p_scale(re, cr), _p_scale(im, -ci)), _p_add(_p_scale(re, ci), _p_scale(im, cr)))


def _is_zero(x):
    return x is None or (x[0] is None and x[1] is None)


def _slab_fft_dit(xs, sign, need):
    n = len(xs)
    if n == 1:
        return {0: xs[0]}
    if all(_is_zero(x) for x in xs):
        return {k: None for k in need}
    h = n // 2
    sub = {k % h for k in need}
    ev = _slab_fft_dit(xs[0::2], sign, sub)
    od = _slab_fft_dit(xs[1::2], sign, sub)
    out = {}
    for k in sub:
        t = _c_mulc(od[k], cmath.exp(sign * 2j * math.pi * k / n))
        if k in need:
            out[k] = _c_add(ev[k], t)
        if k + h in need:
            out[k + h] = _c_add(ev[k], t, -1)
    return out


def _slab_fft_dif(xs, sign, need):
    n = len(xs)
    if n == 1:
        return {0: xs[0]}
    h = n // 2
    need_e = {k // 2 for k in need if k % 2 == 0}
    need_o = {k // 2 for k in need if k % 2 == 1}
    out = {}
    if need_e:
        ev = _slab_fft_dif([_c_add(xs[j], xs[j + h]) for j in range(h)], sign, need_e)
        out.update({2 * k: v for k, v in ev.items()})
    if need_o:
        dif = [_c_mulc(_c_add(xs[j], xs[j + h], -1), cmath.exp(sign * 2j * math.pi * j / n)) for j in range(h)]
        od = _slab_fft_dif(dif, sign, need_o)
        out.update({2 * k + 1: v for k, v in od.items()})
    return out


def _part_value(p, like):
    if p is None:
        return jnp.zeros_like(like)
    return p[0] if p[1] > 0 else -p[0]


HY_CT = 128
SLAB = 256
N_HI = 2 * SEQ // SLAB
N_IN = SEQ // SLAB
N_SPEC = N_HI // 2 + 1
ROW_CHUNK = 8
N_CT = HYENA_WIDTH // HY_CT
LO_GROUP = 3
SPEC_GROUP = 4
FILTER_HIDDEN = 64
EMB_PAD = 128
HIGHEST = lax.Precision.HIGHEST


def _dft_constants():
    n = 2 * SEQ
    r = np.arange(SLAB)
    fwd = np.zeros((N_SPEC, 2 * SLAB, SLAB))
    inv = np.zeros((N_SPEC, 2 * SLAB, SLAB))
    for b in range(N_SPEC):
        m = np.exp(-2j * np.pi * (np.outer(r, r) / SLAB + (r * b)[None, :] / n))
        a = m.T / n
        fwd[b] = np.concatenate([m.real, m.imag], axis=0)
        inv[b] = np.concatenate([a.real, a.imag], axis=0)
    return jnp.asarray(fwd, F32).astype(BF16), jnp.asarray(inv, F32).astype(BF16)


def _ctx_dft_constants():
    n = 2 * CTX_LEN
    k = np.arange(n)
    f = np.exp(-2j * np.pi * np.outer(k, k) / n)
    a = np.conj(f)[:CTX_LEN, :] / n
    return (jnp.asarray(np.concatenate([f.real, f.imag], axis=0), F32).astype(BF16),
            jnp.asarray(np.concatenate([a.real, a.imag], axis=0), F32).astype(BF16))


def _filter_positions(l):
    t = jnp.linspace(0.0, 1.0, l, dtype=F32)[:, None]
    bands = (FILTER_EMB - 1) // 2
    f = jnp.linspace(1e-4, bands - 1, bands, dtype=F32)[None, :]
    w = 2.0 * math.pi * jnp.arange(l, dtype=F32)[:, None] / l
    z = jnp.concatenate([t, jnp.cos(f * w), -jnp.sin(f * w)], axis=-1)
    z2 = jnp.concatenate([z, z[0:1], z[1:][::-1]], axis=0)
    return jnp.pad(z2, ((0, 0), (0, EMB_PAD - FILTER_EMB)))


def _filter_feat_kernel(z_ref, w1_ref, b1_ref, w2_ref, b2_ref, w3_ref, b3_ref, fr_ref, o_ref):
    fr = fr_ref[...]
    dot = functools.partial(jnp.dot, precision=HIGHEST, preferred_element_type=F32)
    h = jnp.sin(fr * (dot(z_ref[...], w1_ref[...]) + b1_ref[...]))
    h = jnp.sin(fr * (dot(h, w2_ref[...]) + b2_ref[...]))
    o_ref[...] = jnp.sin(fr * (dot(h, w3_ref[...]) + b3_ref[...]))


def _filter_features(emb, w1, b1, w2, b2, w3, b3, freq):
    rows = emb.shape[0]
    tr = min(rows, 1024)
    w1p = jnp.pad(w1, ((0, EMB_PAD - FILTER_EMB), (0, 0)))
    vec = lambda v: v.reshape(1, FILTER_HIDDEN)
    full = lambda shape: pl.BlockSpec(shape, lambda i: (0, 0))
    return pl.pallas_call(
        _filter_feat_kernel,
        grid=(rows // tr,),
        in_specs=[pl.BlockSpec((tr, EMB_PAD), lambda i: (i, 0)),
                  full((EMB_PAD, FILTER_HIDDEN)), full((1, FILTER_HIDDEN)),
                  full((FILTER_HIDDEN, FILTER_HIDDEN)), full((1, FILTER_HIDDEN)),
                  full((FILTER_HIDDEN, FILTER_HIDDEN)), full((1, FILTER_HIDDEN)),
                  full((1, FILTER_HIDDEN))],
        out_specs=pl.BlockSpec((tr, FILTER_HIDDEN), lambda i: (i, 0)),
        out_shape=jax.ShapeDtypeStruct((rows, FILTER_HIDDEN), F32),
        compiler_params=_cparams(("arbitrary",), 32),
        name="filter_feat",
    )(emb, w1p, vec(b1), w2, vec(b2), w3, vec(b3), vec(freq))


def _filter_taps(raw, n, l, absdelta):
    m = jnp.where(n < l, n, 2 * l - n)
    t = m.astype(F32) / (l - 1)
    return jnp.where(n == l, 0.0, raw * jnp.exp(-t * absdelta))


def _cplx_from_pq(pq, conj):
    half = pq.shape[0] // 2
    pr, pi_ = pq[:half, :HY_CT], pq[:half, HY_CT:]
    qr, qi = pq[half:, :HY_CT], pq[half:, HY_CT:]
    if conj:
        return pr + qi, pi_ - qr
    return pr - qi, qr + pi_


def _slab_forward(src_ref, n_in, real_only, dst_ref, need, scale):
    def body(rc, carry):
        rows = pl.ds(pl.multiple_of(rc * ROW_CHUNK, ROW_CHUNK), ROW_CHUNK)
        xs = []
        for a in range(N_HI):
            if a >= n_in:
                xs.append(None)
                continue
            re = src_ref[a, rows, 0:HY_CT]
            if scale is not None:
                re = re * scale
            xs.append(((re, 1), None if real_only else (src_ref[a, rows, HY_CT:2 * HY_CT], 1)))
        like = xs[0][0][0]
        out = _slab_fft_dit(xs, -1, need)
        for b in sorted(need):
            c = out[b] or (None, None)
            dst_ref[b, rows, 0:HY_CT] = _part_value(c[0], like)
            dst_ref[b, rows, HY_CT:2 * HY_CT] = _part_value(c[1], like)
        return carry

    lax.fori_loop(0, SLAB // ROW_CHUNK, body, 0)


def _slab_inverse(s_ref):
    def body(rc, carry):
        rows = pl.ds(pl.multiple_of(rc * ROW_CHUNK, ROW_CHUNK), ROW_CHUNK)
        xs = [((s_ref[b, rows, 0:HY_CT], 1), (s_ref[b, rows, HY_CT:2 * HY_CT], 1)) for b in range(N_HI)]
        like = xs[0][0][0]
        out = _slab_fft_dif(xs, 1, set(range(N_IN)))
        for a in range(N_IN):
            c = out[a] or (None, None)
            s_ref[a, rows, 0:HY_CT] = _part_value(c[0], like)
            s_ref[a, rows, HY_CT:2 * HY_CT] = _part_value(c[1], like)
        return carry

    lax.fori_loop(0, SLAB // ROW_CHUNK, body, 0)


def _filter_spec_kernel(feat_ref, wf_ref, wb_ref, ad_ref, mf_ref, o_ref, ft_ref, s_ref):
    absdelta = ad_ref[...]

    def taps(a, acc):
        rows = feat_ref[pl.ds(pl.multiple_of(a * SLAB, SLAB), SLAB), :]
        w = jnp.where(a < N_IN, wf_ref[...], wb_ref[...])
        raw = jnp.dot(rows.astype(BF16), w.astype(BF16), preferred_element_type=F32)
        n = a * SLAB + lax.broadcasted_iota(jnp.int32, raw.shape, 0)
        v = _filter_taps(raw, n, SEQ, absdelta)
        ft_ref[a] = v
        return acc + jnp.sum(jnp.abs(v), axis=0, keepdims=True)

    norm = lax.fori_loop(0, N_HI, taps, jnp.zeros((1, HY_CT), F32), unroll=2)
    _slab_forward(ft_ref, N_HI, True, s_ref, set(range(N_SPEC)), 1.0 / norm)

    def spec(b):
        pq = jnp.dot(mf_ref[b], s_ref[b].astype(BF16), preferred_element_type=F32)
        xr, xi = _cplx_from_pq(pq, False)
        rows = pl.ds(pl.multiple_of(b * SLAB, SLAB), SLAB)
        o_ref[rows, 0:HY_CT] = xr.astype(BF16)
        o_ref[rows, HY_CT:2 * HY_CT] = xi.astype(BF16)

    def spec_group(i, carry):
        for t in range(SPEC_GROUP):
            spec(1 + SPEC_GROUP * i + t)
        return carry

    spec(0)
    lax.fori_loop(0, (N_SPEC - 1) // SPEC_GROUP, spec_group, 0)


def _filter_spectrum(feat, w_fout, absdelta, mf):
    cols = 2 * N_CT
    return pl.pallas_call(
        _filter_spec_kernel,
        grid=(HYENA_ORDER, N_CT),
        in_specs=[
            pl.BlockSpec((2 * SEQ, FILTER_HIDDEN), lambda o, c: (0, 0)),
            pl.BlockSpec((FILTER_HIDDEN, HY_CT), lambda o, c: (0, o * cols + c)),
            pl.BlockSpec((FILTER_HIDDEN, HY_CT), lambda o, c: (0, o * cols + N_CT + c)),
            pl.BlockSpec((1, HY_CT), lambda o, c: (0, c)),
            pl.BlockSpec((N_SPEC, 2 * SLAB, SLAB), lambda o, c: (0, 0, 0)),
        ],
        out_specs=pl.BlockSpec((None, None, N_SPEC * SLAB, 2 * HY_CT), lambda o, c: (o, c, 0, 0)),
        out_shape=jax.ShapeDtypeStruct((HYENA_ORDER, N_CT, N_SPEC * SLAB, 2 * HY_CT), BF16),
        scratch_shapes=[pltpu.VMEM((N_HI, SLAB, HY_CT), F32), pltpu.VMEM((N_SPEC, SLAB, 2 * HY_CT), F32)],
        compiler_params=_cparams(("arbitrary", "arbitrary"), 48),
        name="filter_spec",
    )(feat, w_fout, w_fout, absdelta, mf)


def _conv_slab(u_ref, bi, a, w, b):
    r0 = a * SLAB
    x = u_ref[bi, pl.ds(pl.multiple_of(r0, SLAB), SLAB), :].astype(F32)
    before = u_ref[bi, pl.ds(pl.multiple_of(jnp.maximum(r0 - 16, 0), 16), 16), :].astype(F32)
    after = u_ref[bi, pl.ds(pl.multiple_of(jnp.minimum(r0 + SLAB, SEQ - 16), 16), 16), :].astype(F32)
    prev_row = jnp.where(a > 0, before[15:16, :], 0.0)
    next_row = jnp.where(a < N_IN - 1, after[0:1, :], 0.0)
    row = lax.broadcasted_iota(jnp.int32, x.shape, 0)
    xm1 = jnp.where(row == 0, prev_row, pltpu.roll(x, 1, 0))
    xp1 = jnp.where(row == SLAB - 1, next_row, pltpu.roll(x, SLAB - 1, 0))
    return xm1 * w[0:1, :] + x * w[1:2, :] + xp1 * w[2:3, :] + b


def _hyena_kernel(uv_ref, u1_ref, u2_ref, cwv_ref, cw1_ref, cw2_ref, cbv_ref, cb1_ref, cb2_ref, skip_ref,
                  mf_ref, mi_ref, h_ref, o_ref, zc_ref, g_ref, s_ref):
    o = pl.program_id(2)

    @pl.when(o == 0)
    def _():
        def conv(a, carry):
            for bi in range(2):
                lanes = slice(bi * HY_CT, (bi + 1) * HY_CT)
                zc_ref[a, :, lanes] = _conv_slab(uv_ref, bi, a, cwv_ref[...], cbv_ref[...])
                g_ref[0, a, :, lanes] = _conv_slab(u1_ref, bi, a, cw1_ref[...], cb1_ref[...]).astype(BF16)
                g_ref[1, a, :, lanes] = _conv_slab(u2_ref, bi, a, cw2_ref[...], cb2_ref[...]).astype(BF16)
            return carry

        lax.fori_loop(0, N_IN, conv, 0)

    _slab_forward(zc_ref, N_IN, False, s_ref, set(range(N_HI)), None)

    def through_filter(slabs, j):
        h = h_ref[pl.ds(pl.multiple_of(j * SLAB, SLAB), SLAB), :].astype(F32)
        hr, hi = h[:, :HY_CT], h[:, HY_CT:]
        w2 = 2 * HY_CT
        d = jnp.concatenate([s_ref[b] for b, _ in slabs], axis=1).astype(BF16)
        pq = jnp.dot(mf_ref[j], d, preferred_element_type=F32)
        ys = []
        for k, (_, conj) in enumerate(slabs):
            xr, xi = _cplx_from_pq(pq[:, k * w2:(k + 1) * w2], conj)
            hs = -hi if conj else hi
            ys += [xr * hr - xi * hs, xr * hs + xi * hr]
        pq2 = jnp.dot(mi_ref[j], jnp.concatenate(ys, axis=1).astype(BF16), preferred_element_type=F32)
        return [(b, _cplx_from_pq(pq2[:, k * w2:(k + 1) * w2], not conj)) for k, (b, conj) in enumerate(slabs)]

    def put_back(results):
        for b, (wr, wi) in results:
            s_ref[b, :, 0:HY_CT] = wr
            s_ref[b, :, HY_CT:2 * HY_CT] = wi

    put_back(through_filter([(0, False)], 0) + through_filter([(N_HI // 2, False)], N_HI // 2))

    def pair_group(i, carry):
        results = []
        for t in range(LO_GROUP):
            j = 1 + LO_GROUP * i + t
            results += through_filter([(j, False), (N_HI - j, True)], j)
        put_back(results)
        return carry

    lax.fori_loop(0, (N_HI // 2 - 1) // LO_GROUP, pair_group, 0)
    _slab_inverse(s_ref)

    sk = skip_ref[pl.ds(o, 1), :]
    skip2 = jnp.concatenate([sk, sk], axis=1)

    def gate(a, carry):
        zc_ref[a] = g_ref[o, a].astype(F32) * (s_ref[a] + skip2 * zc_ref[a])
        return carry

    lax.fori_loop(0, N_IN, gate, 0)

    @pl.when(o == HYENA_ORDER - 1)
    def _():
        def emit(a, carry):
            rows = pl.ds(pl.multiple_of(a * SLAB, SLAB), SLAB)
            for bi in range(2):
                o_ref[bi, rows, :] = zc_ref[a, :, bi * HY_CT:(bi + 1) * HY_CT].astype(BF16)
            return carry

        lax.fori_loop(0, N_IN, emit, 0)


def _hyena_conv(p, conv_w, conv_b, skip, mf, mi, hspec):
    b = p.shape[0]
    u0 = V_END // HY_CT
    cb = conv_b.reshape(1, -1)
    part = lambda k: pl.BlockSpec((2, SEQ, HY_CT), lambda c, q, o: (q, 0, u0 + k * N_CT + c))
    cw = lambda k: pl.BlockSpec((SHORT_CONV, HY_CT), lambda c, q, o: (0, k * N_CT + c))
    cbs = lambda k: pl.BlockSpec((1, HY_CT), lambda c, q, o: (0, k * N_CT + c))
    const = lambda: pl.BlockSpec((N_SPEC, 2 * SLAB, SLAB), lambda c, q, o: (0, 0, 0), pipeline_mode=pl.Buffered(1))
    return pl.pallas_call(
        _hyena_kernel,
        grid=(N_CT, b // 2, HYENA_ORDER),
        in_specs=[part(0), part(1), part(2), cw(0), cw(1), cw(2), cbs(0), cbs(1), cbs(2),
                  pl.BlockSpec((HYENA_ORDER, HY_CT), lambda c, q, o: (0, c)),
                  const(), const(),
                  pl.BlockSpec((None, None, N_SPEC * SLAB, 2 * HY_CT), lambda c, q, o: (o, c, 0, 0))],
        out_specs=pl.BlockSpec((2, SEQ, HY_CT), lambda c, q, o: (q, 0, c)),
        out_shape=jax.ShapeDtypeStruct((b, SEQ, HYENA_WIDTH), BF16),
        scratch_shapes=[pltpu.VMEM((N_IN, SLAB, 2 * HY_CT), F32),
                        pltpu.VMEM((HYENA_ORDER, N_IN, SLAB, 2 * HY_CT), BF16),
                        pltpu.VMEM((N_HI, SLAB, 2 * HY_CT), F32)],
        compiler_params=_cparams(("arbitrary", "arbitrary", "arbitrary"), 56),
        name="hyena_conv",
    )(p, p, p, conv_w, conv_w, conv_w, cb, cb, cb, skip, mf, mi, hspec)


def _ctx_hyena_kernel(uv_ref, u1_ref, u2_ref, cwv_ref, cw1_ref, cw2_ref, cbv_ref, cb1_ref, cb2_ref, skip_ref,
                      feat_ref, wf0_ref, wb0_ref, wf1_ref, wb1_ref, ad_ref, ff_ref, ai_ref, o_ref):
    l = CTX_LEN
    nb = uv_ref.shape[0]
    n = lax.broadcasted_iota(jnp.int32, (2 * l, HY_CT), 0)
    spectra = []
    for wf_ref, wb_ref in ((wf0_ref, wb0_ref), (wf1_ref, wb1_ref)):
        raw = jnp.concatenate(
            [jnp.dot(feat_ref[0:l, :], wf_ref[...], precision=HIGHEST, preferred_element_type=F32),
             jnp.dot(feat_ref[l:2 * l, :], wb_ref[...], precision=HIGHEST, preferred_element_type=F32)], axis=0)
        v = _filter_taps(raw, n, l, ad_ref[...])
        v = v / jnp.sum(jnp.abs(v), axis=0, keepdims=True)
        pq = jnp.dot(ff_ref[...], v.astype(BF16), preferred_element_type=F32)
        spectra.append((pq[:2 * l], pq[2 * l:]))

    row = lax.broadcasted_iota(jnp.int32, (l, HY_CT), 0)

    def conv(u_ref, bi, w_ref, b_ref):
        x = u_ref[bi].astype(F32)
        xm1 = jnp.where(row == 0, 0.0, pltpu.roll(x, 1, 0))
        xp1 = jnp.where(row == l - 1, 0.0, pltpu.roll(x, l - 1, 0))
        return xm1 * w_ref[0:1, :] + x * w_ref[1:2, :] + xp1 * w_ref[2:3, :] + b_ref[...]

    z = [conv(uv_ref, bi, cwv_ref, cbv_ref) for bi in range(nb)]
    gates = [[conv(u1_ref, bi, cw1_ref, cb1_ref) for bi in range(nb)],
             [conv(u2_ref, bi, cw2_ref, cb2_ref) for bi in range(nb)]]
    for o in range(HYENA_ORDER):
        hr, hi = spectra[o]
        pq = jnp.dot(ff_ref[:, 0:l], jnp.concatenate(z, axis=1).astype(BF16), preferred_element_type=F32)
        ys = []
        for q in range(nb // 2):
            xr, xi = _cplx_from_pq(pq[:, 2 * q * HY_CT:2 * (q + 1) * HY_CT], False)
            ys += [xr * hr - xi * hi, xr * hi + xi * hr]
        pq2 = jnp.dot(ai_ref[...], jnp.concatenate(ys, axis=1).astype(BF16), preferred_element_type=F32)
        y = []
        for q in range(nb // 2):
            y += list(_cplx_from_pq(pq2[:, 2 * q * HY_CT:2 * (q + 1) * HY_CT], False))
        z = [gates[o][bi] * (y[bi] + skip_ref[o:o + 1, :] * z[bi]) for bi in range(nb)]
    for bi in range(nb):
        o_ref[bi] = z[bi].astype(BF16)


def _ctx_hyena(pc, conv_w, conv_b, skip, feat, w_fout, absdelta, ff, ai):
    b, l, _ = pc.shape
    u0 = V_END // HY_CT
    cb = conv_b.reshape(1, -1)
    cols = 2 * N_CT
    part = lambda k: pl.BlockSpec((b, l, HY_CT), lambda c: (0, 0, u0 + k * N_CT + c))
    cw = lambda k: pl.BlockSpec((SHORT_CONV, HY_CT), lambda c: (0, k * N_CT + c))
    cbs = lambda k: pl.BlockSpec((1, HY_CT), lambda c: (0, k * N_CT + c))
    wcol = lambda off: pl.BlockSpec((FILTER_HIDDEN, HY_CT), lambda c: (0, off + c))
    return pl.pallas_call(
        _ctx_hyena_kernel,
        grid=(N_CT,),
        in_specs=[part(0), part(1), part(2), cw(0), cw(1), cw(2), cbs(0), cbs(1), cbs(2),
                  pl.BlockSpec((HYENA_ORDER, HY_CT), lambda c: (0, c)),
                  pl.BlockSpec((2 * l, FILTER_HIDDEN), lambda c: (0, 0)),
                  wcol(0), wcol(N_CT), wcol(cols), wcol(cols + N_CT),
                  pl.BlockSpec((1, HY_CT), lambda c: (0, c)),
                  pl.BlockSpec((4 * l, 2 * l), lambda c: (0, 0)),
                  pl.BlockSpec((2 * l, 2 * l), lambda c: (0, 0))],
        out_specs=pl.BlockSpec((b, l, HY_CT), lambda c: (0, 0, c)),
        out_shape=jax.ShapeDtypeStruct((b, l, HYENA_WIDTH), BF16),
        compiler_params=_cparams(("arbitrary",), 32),
        name="ctx_hyena",
    )(pc, pc, pc, conv_w, conv_w, conv_w, cb, cb, cb, skip, feat, w_fout, w_fout, w_fout, w_fout, absdelta, ff, ai)


def _rope_tables(l):
    rows = l // GRID_W
    row = jnp.repeat(jnp.arange(rows), GRID_W).astype(F32)
    col = jnp.tile(jnp.arange(GRID_W), rows).astype(F32)
    n_freq = HEAD_DIM // 4
    inv = ROPE_BASE ** (-jnp.arange(n_freq, dtype=F32) / n_freq)
    ang_r = row[:, None] * inv
    ang_c = col[:, None] * inv
    cos = jnp.concatenate([jnp.cos(ang_r), jnp.cos(ang_r), jnp.cos(ang_c), jnp.cos(ang_c)], axis=-1)
    sin = jnp.concatenate([-jnp.sin(ang_r), jnp.sin(ang_r), -jnp.sin(ang_c), jnp.sin(ang_c)], axis=-1)
    return cos, sin


def kernel(x, c, ctx, c_ctx, w_ada, b_ada, norm_mix, norm_mlp, w_in, q_norm, k_norm, attn_sink, conv_w, conv_b, filt_w1, filt_b1, filt_w2, filt_b2, filt_w3, filt_b3, filt_freq, filt_w_out, hyena_skip, w_out, w_mlp_in, w_mlp_out):
    cos, sin = _rope_tables(SEQ)
    emb, emb_ctx = _filter_positions(SEQ), _filter_positions(CTX_LEN)
    mf, mi = _dft_constants()
    ff_ctx, ai_ctx = _ctx_dft_constants()
    absdelta = jnp.abs(jnp.linspace(MIN_DECAY, MAX_DECAY, HYENA_WIDTH, dtype=F32)).reshape(1, HYENA_WIDTH)
    cvec = jnp.zeros((MOD_ROWS, D_MODEL), F32).at[:BATCH].set(c).at[CTX_ROW].set(c_ctx)
    mod_all = _modulation(cvec, w_ada, b_ada).reshape(DEPTH, MOD_ROWS, N_MOD, D_MODEL)
    w_in_bf = w_in.astype(BF16)
    w_out_bf = w_out.astype(BF16)
    w1_bf = w_mlp_in.astype(BF16)
    w2_bf = w_mlp_out.astype(BF16)
    xc = ctx
    for i in range(DEPTH):
        update_ctx = i < DEPTH - 1
        mod = mod_all[i]
        nmix = norm_mix[i].reshape(1, D_MODEL)
        nmlp = norm_mlp[i].reshape(1, D_MODEL)
        qn = q_norm[i].reshape(1, HEAD_DIM)
        kn = k_norm[i].reshape(1, HEAD_DIM)
        filt_mlp = (filt_w1[i], filt_b1[i], filt_w2[i], filt_b2[i], filt_w3[i], filt_b3[i], filt_freq[i])

        p = _in_projection(x, mod, None, nmix, w_in_bf, i, qn, kn, cos, sin,
                           tm=512, n_off=0, n_tiles=IN_COLS // IN_TN, use_rope=True)
        if update_ctx:
            pc = _in_projection(xc, mod, CTX_ROW, nmix, w_in_bf, i, qn, kn, cos, sin,
                                tm=CTX_LEN, n_off=0, n_tiles=IN_COLS // IN_TN, use_rope=False)
            kx_blk, vx_blk = Q_END // KV_COLS * 2, Q_END // KV_COLS * 2 + 1
        else:
            pc = _in_projection(xc, mod, CTX_ROW, nmix, w_in_bf, i, qn, kn, cos, sin,
                                tm=CTX_LEN, n_off=KV_TILE, n_tiles=1, use_rope=False)
            kx_blk, vx_blk = 0, 1

        attn = _window_attention(p, pc, kx_blk, vx_blk, attn_sink[i])
        hspec = _filter_spectrum(_filter_features(emb, *filt_mlp), filt_w_out[i], absdelta, mf)
        hy = _hyena_conv(p, conv_w[i], conv_b[i], hyena_skip[i], mf, mi, hspec)
        x, h2 = _out_projection(attn, hy, x, mod, None, nmlp, w_out_bf, i, tm=512)
        x = _mlp(h2, x, mod, None, w1_bf, w2_bf, i, tm=512, tf=1024)

        if update_ctx:
            attn_c = _context_attention(pc, attn_sink[i])
            hy_c = _ctx_hyena(pc, conv_w[i], conv_b[i], hyena_skip[i], _filter_features(emb_ctx, *filt_mlp),
                              filt_w_out[i], absdelta, ff_ctx, ai_ctx)
            xc, h2c = _out_projection(attn_c, hy_c, xc, mod, CTX_ROW, nmlp, w_out_bf, i, tm=CTX_LEN)
            xc = _mlp(h2c, xc, mod, CTX_ROW, w1_bf, w2_bf, i, tm=CTX_LEN, tf=1024)
    return x
```

```python
import cmath
import functools
import math

import jax
import jax.numpy as jnp
import numpy as np
from jax import lax
from jax.experimental import pallas as pl
from jax.experimental.pallas import tpu as pltpu

D_MODEL = 2048
BATCH = 4
SEQ = 4096
DEPTH = 2
GRID_W = 64
CTX_LEN = 256
HEAD_DIM = 128
N_Q_HEADS = 8
N_KV_HEADS = 2
GQA_GROUP = N_Q_HEADS // N_KV_HEADS
ATTN_WIDTH = N_Q_HEADS * HEAD_DIM
HYENA_WIDTH = D_MODEL - ATTN_WIDTH
HYENA_ORDER = 2
SHORT_CONV = 3
FILTER_EMB = 33
DECAY_TARGET = 1e-2
FAST_DECAY_PCT = 0.3
SLOW_DECAY_PCT = 1.5
MAX_DECAY = math.log(DECAY_TARGET) / FAST_DECAY_PCT
MIN_DECAY = math.log(DECAY_TARGET) / SLOW_DECAY_PCT
WINDOW = 128
BLOCK = 128
ROPE_BASE = 10000.0
D_FF = 4 * D_MODEL
N_MOD = 6
EPS = 1e-6
NEG_INF = -1e30
Q_END = ATTN_WIDTH
K_END = Q_END + N_KV_HEADS * HEAD_DIM
V_END = K_END + N_KV_HEADS * HEAD_DIM
IN_COLS = V_END + (HYENA_ORDER + 1) * HYENA_WIDTH

MOD_ROWS = 8
CTX_ROW = BATCH
KV_COLS = 2 * N_KV_HEADS * HEAD_DIM

F32 = jnp.float32
BF16 = jnp.bfloat16
MIB = 1024 * 1024


def _cparams(sem, vmem_mib):
    return pltpu.CompilerParams(dimension_semantics=sem, vmem_limit_bytes=vmem_mib * MIB)


def _mod_kernel(c_ref, w_ref, b_ref, o_ref):
    c = c_ref[...]
    s = (c * jax.nn.sigmoid(c)).astype(BF16)
    o_ref[...] = jnp.dot(s, w_ref[...].astype(BF16), preferred_element_type=F32) + b_ref[...]


def _modulation(cvec, w_ada, b_ada):
    tn = 1024
    n_cols = N_MOD * D_MODEL
    return pl.pallas_call(
        _mod_kernel,
        grid=(DEPTH, n_cols // tn),
        in_specs=[
            pl.BlockSpec((MOD_ROWS, D_MODEL), lambda l, j: (0, 0)),
            pl.BlockSpec((None, D_MODEL, tn), lambda l, j: (l, 0, j)),
            pl.BlockSpec((None, 1, tn), lambda l, j: (l, 0, j)),
        ],
        out_specs=pl.BlockSpec((None, MOD_ROWS, tn), lambda l, j: (l, 0, j)),
        out_shape=jax.ShapeDtypeStruct((DEPTH, MOD_ROWS, n_cols), F32),
        compiler_params=_cparams(("arbitrary", "arbitrary"), 40),
        name="adaln_mod",
    )(cvec, w_ada, b_ada.reshape(DEPTH, 1, n_cols))


IN_TN = 512
Q_TILES = Q_END // IN_TN
KV_TILE = Q_TILES
HEADS_PER_TILE = IN_TN // HEAD_DIM


SUB_ROWS = 256


def _rms_modulate(xf, gain, shift):
    ms = jnp.mean(xf * xf, axis=-1, keepdims=True)
    return xf * lax.rsqrt(ms + EPS) * gain + shift


def _head_norm_rope(a, g, cos, sin):
    ms = jnp.mean(a * a, axis=-1, keepdims=True)
    y = a * lax.rsqrt(ms + EPS) * g
    if cos is None:
        return y
    lane = lax.broadcasted_iota(jnp.int32, y.shape, 1)
    partner = jnp.where((lane % 64) < 32, pltpu.roll(y, 96, 1), pltpu.roll(y, 32, 1))
    return y * cos + partner * sin


def _inproj_kernel(x_ref, m_ref, nw_ref, w_ref, qn_ref, kn_ref, cos_ref, sin_ref, o_ref, *, n_off, n_tiles, use_rope):
    gain = nw_ref[...] * (1.0 + m_ref[1:2, :])
    shift = m_ref[0:1, :]
    for r0 in range(0, x_ref.shape[0], SUB_ROWS):
        rows = slice(r0, r0 + SUB_ROWS)
        h = _rms_modulate(x_ref[rows, :], gain, shift).astype(BF16)
        cos = cos_ref[rows, :] if use_rope else None
        sin = sin_ref[rows, :] if use_rope else None
        for n in range(n_tiles):
            cols = slice(n * IN_TN, (n + 1) * IN_TN)
            acc = jnp.dot(h, w_ref[:, cols], preferred_element_type=F32)
            t = n + n_off
            if t > KV_TILE:
                o_ref[rows, cols] = acc.astype(BF16)
                continue
            gains = [qn_ref[...]] * HEADS_PER_TILE if t < Q_TILES else [kn_ref[...]] * N_KV_HEADS + [None] * N_KV_HEADS
            for c, g in enumerate(gains):
                sl = slice(c * HEAD_DIM, (c + 1) * HEAD_DIM)
                osl = slice(n * IN_TN + c * HEAD_DIM, n * IN_TN + (c + 1) * HEAD_DIM)
                if g is None:
                    o_ref[rows, osl] = acc[:, sl].astype(BF16)
                else:
                    o_ref[rows, osl] = _head_norm_rope(acc[:, sl], g, cos, sin).astype(BF16)


def _in_projection(x, mod, mod_row, nw, w_bf, layer, qn, kn, cos, sin, *, tm, n_off, n_tiles, use_rope):
    b, t, _ = x.shape
    assert n_off % n_tiles == 0 and tm % SUB_ROWS == 0
    row = (lambda bb: bb) if mod_row is None else (lambda bb: mod_row)
    kern = functools.partial(_inproj_kernel, n_off=n_off, n_tiles=n_tiles, use_rope=use_rope)
    width = n_tiles * IN_TN
    return pl.pallas_call(
        kern,
        grid=(b, t // tm),
        in_specs=[
            pl.BlockSpec((None, tm, D_MODEL), lambda bb, i: (bb, i, 0)),
            pl.BlockSpec((None, N_MOD, D_MODEL), lambda bb, i: (row(bb), 0, 0)),
            pl.BlockSpec((1, D_MODEL), lambda bb, i: (0, 0)),
            pl.BlockSpec((None, D_MODEL, width), lambda bb, i: (layer, 0, n_off // n_tiles),
                         pipeline_mode=pl.Buffered(1)),
            pl.BlockSpec((1, HEAD_DIM), lambda bb, i: (0, 0)),
            pl.BlockSpec((1, HEAD_DIM), lambda bb, i: (0, 0)),
            pl.BlockSpec((tm, HEAD_DIM), lambda bb, i: (i, 0)),
            pl.BlockSpec((tm, HEAD_DIM), lambda bb, i: (i, 0)),
        ],
        out_specs=pl.BlockSpec((None, tm, width), lambda bb, i: (bb, i, 0)),
        out_shape=jax.ShapeDtypeStruct((b, t, width), BF16),
        compiler_params=_cparams(("arbitrary", "arbitrary"), 48),
        name="in_proj",
    )(x, mod, nw, w_bf, qn, kn, cos, sin)


def _sink_column(sink_ref, kv, rows_per_head):
    rows = GQA_GROUP * rows_per_head
    head = lax.broadcasted_iota(jnp.int32, (rows, 1), 0) // rows_per_head
    col = jnp.full((rows, 1), sink_ref[kv * GQA_GROUP], F32)
    for g in range(1, GQA_GROUP):
        col = jnp.where(head == g, sink_ref[kv * GQA_GROUP + g], col)
    return col


def _stack_heads(q_ref, kv, rows=slice(None)):
    return jnp.concatenate(
        [q_ref[rows, (kv * GQA_GROUP + g) * HEAD_DIM:(kv * GQA_GROUP + g + 1) * HEAD_DIM]
         for g in range(GQA_GROUP)], axis=0)


LOG2E = math.log2(math.e)
QK_LOG2_SCALE = HEAD_DIM ** -0.5 * LOG2E


def _softmax_pv(q4, k_all, v_all, bias, sink_col):
    t = lax.dot_general(q4, k_all, (((1,), (1,)), ((), ())), preferred_element_type=F32) * QK_LOG2_SCALE
    if bias is not None:
        t = t + bias
    sink2 = sink_col * LOG2E
    m = jnp.maximum(jnp.max(t, axis=-1, keepdims=True), sink2)
    e = jnp.exp2(t - m)
    denom = jnp.exp2(sink2 - m) + jnp.sum(e, axis=-1, keepdims=True)
    return jnp.dot(e.astype(BF16), v_all, preferred_element_type=F32) / denom


def _window_attn_kernel(sink_ref, bias0_ref, bias1_ref, q_ref, kp_ref, km_ref, kn_ref, vp_ref, vm_ref, vn_ref,
                        kx_ref, vx_ref, o_ref):
    lo, hi = slice(0, BLOCK), slice(BLOCK, 2 * BLOCK)
    for blk, bias_ref in enumerate((bias0_ref, bias1_ref)):
        rows = slice(blk * BLOCK, (blk + 1) * BLOCK)
        for kv in range(N_KV_HEADS):
            ks = slice(kv * HEAD_DIM, (kv + 1) * HEAD_DIM)
            if blk == 0:
                band_k = [kp_ref[:, ks], km_ref[lo, ks], km_ref[hi, ks]]
                band_v = [vp_ref[:, ks], vm_ref[lo, ks], vm_ref[hi, ks]]
            else:
                band_k = [km_ref[lo, ks], km_ref[hi, ks], kn_ref[:, ks]]
                band_v = [vm_ref[lo, ks], vm_ref[hi, ks], vn_ref[:, ks]]
            k_all = jnp.concatenate([kx_ref[:, ks]] + band_k, axis=0)
            v_all = jnp.concatenate([vx_ref[:, ks]] + band_v, axis=0)
            o = _softmax_pv(_stack_heads(q_ref, kv, rows), k_all, v_all, bias_ref[...],
                            _sink_column(sink_ref, kv, BLOCK))
            for g in range(GQA_GROUP):
                h = kv * GQA_GROUP + g
                o_ref[rows, h * HEAD_DIM:(h + 1) * HEAD_DIM] = o[g * BLOCK:(g + 1) * BLOCK].astype(BF16)


def _window_bias(nb):
    qi = np.arange(GQA_GROUP * BLOCK)[:, None] % BLOCK
    sj = np.arange(3 * BLOCK)[None, :] - BLOCK
    near = np.abs(sj - qi) <= WINDOW
    out = np.zeros((3, GQA_GROUP * BLOCK, CTX_LEN + 3 * BLOCK), np.float32)
    for variant, blk in enumerate((0, 1, nb - 1)):
        key_pos = blk * BLOCK + sj
        ok = near & (key_pos >= 0) & (key_pos < nb * BLOCK)
        out[variant, :, CTX_LEN:] = np.where(ok, 0.0, NEG_INF)
    return jnp.asarray(out)


def _window_attention(p, pc, kx_blk, vx_blk, sink):
    b, seq, _ = p.shape
    nb = seq // BLOCK
    kvw = N_KV_HEADS * HEAD_DIM
    k_blk, v_blk = Q_END // kvw, K_END // kvw
    steps = nb // 2
    prev = lambda bb, i: (bb, jnp.maximum(2 * i - 1, 0))
    nxt = lambda bb, i: (bb, jnp.minimum(2 * i + 2, nb - 1))

    def edge(which, blk):
        return pl.BlockSpec((None, BLOCK, kvw), lambda bb, i: which(bb, i) + (blk,))

    def mid(blk):
        return pl.BlockSpec((None, 2 * BLOCK, kvw), lambda bb, i: (bb, i, blk))

    bias_shape = (None, GQA_GROUP * BLOCK, CTX_LEN + 3 * BLOCK)
    bias = _window_bias(nb)
    return pl.pallas_call(
        _window_attn_kernel,
        grid=(b, steps),
        in_specs=[
            pl.BlockSpec(memory_space=pltpu.SMEM),
            pl.BlockSpec(bias_shape, lambda bb, i: (jnp.where(i == 0, 0, 1), 0, 0)),
            pl.BlockSpec(bias_shape, lambda bb, i: (jnp.where(i == steps - 1, 2, 1), 0, 0)),
            pl.BlockSpec((None, 2 * BLOCK, ATTN_WIDTH), lambda bb, i: (bb, i, 0)),
            edge(prev, k_blk), mid(k_blk), edge(nxt, k_blk),
            edge(prev, v_blk), mid(v_blk), edge(nxt, v_blk),
            pl.BlockSpec((None, CTX_LEN, kvw), lambda bb, i: (bb, 0, kx_blk)),
            pl.BlockSpec((None, CTX_LEN, kvw), lambda bb, i: (bb, 0, vx_blk)),
        ],
        out_specs=pl.BlockSpec((None, 2 * BLOCK, ATTN_WIDTH), lambda bb, i: (bb, i, 0)),
        out_shape=jax.ShapeDtypeStruct((b, seq, ATTN_WIDTH), BF16),
        compiler_params=_cparams(("arbitrary", "arbitrary"), 32),
        name="window_attn",
    )(sink, bias, bias, p, p, p, p, p, p, p, pc, pc)


def _ctx_attn_kernel(sink_ref, q_ref, k_ref, v_ref, o_ref):
    n = q_ref.shape[0]
    for kv in range(N_KV_HEADS):
        ks = slice(kv * HEAD_DIM, (kv + 1) * HEAD_DIM)
        o = _softmax_pv(_stack_heads(q_ref, kv), k_ref[:, ks], v_ref[:, ks], None, _sink_column(sink_ref, kv, n))
        for g in range(GQA_GROUP):
            h = kv * GQA_GROUP + g
            o_ref[:, h * HEAD_DIM:(h + 1) * HEAD_DIM] = o[g * n:(g + 1) * n].astype(BF16)


def _context_attention(pc, sink):
    b, n, _ = pc.shape
    kvw = N_KV_HEADS * HEAD_DIM
    return pl.pallas_call(
        _ctx_attn_kernel,
        grid=(b,),
        in_specs=[
            pl.BlockSpec(memory_space=pltpu.SMEM),
            pl.BlockSpec((None, n, ATTN_WIDTH), lambda bb: (bb, 0, 0)),
            pl.BlockSpec((None, n, kvw), lambda bb: (bb, 0, Q_END // kvw)),
            pl.BlockSpec((None, n, kvw), lambda bb: (bb, 0, K_END // kvw)),
        ],
        out_specs=pl.BlockSpec((None, n, ATTN_WIDTH), lambda bb: (bb, 0, 0)),
        out_shape=jax.ShapeDtypeStruct((b, n, ATTN_WIDTH), BF16),
        compiler_params=_cparams(("arbitrary",), 32),
        name="ctx_attn",
    )(sink, pc, pc, pc)


def _outproj_kernel(a_ref, y_ref, x_ref, m_ref, nw_ref, wa_ref, wy_ref, o_ref, h_ref):
    gain = nw_ref[...] * (1.0 + m_ref[4:5, :])
    shift = m_ref[3:4, :]
    for r0 in range(0, x_ref.shape[0], SUB_ROWS):
        rows = slice(r0, r0 + SUB_ROWS)
        mix = (jnp.dot(a_ref[rows, :], wa_ref[...], preferred_element_type=F32)
               + jnp.dot(y_ref[rows, :], wy_ref[...], preferred_element_type=F32))
        x1 = x_ref[rows, :] + m_ref[2:3, :] * mix
        o_ref[rows, :] = x1
        h_ref[rows, :] = _rms_modulate(x1, gain, shift).astype(BF16)


def _out_projection(attn, hy, x, mod, mod_row, nw, w_bf, layer, *, tm):
    b, t, _ = x.shape
    assert tm % SUB_ROWS == 0
    row = (lambda bb: bb) if mod_row is None else (lambda bb: mod_row)
    tile = lambda width: pl.BlockSpec((None, tm, width), lambda bb, i: (bb, i, 0))
    half = lambda k: pl.BlockSpec((None, D_MODEL // 2, D_MODEL), lambda bb, i: (layer, k, 0),
                                  pipeline_mode=pl.Buffered(1))
    return pl.pallas_call(
        _outproj_kernel,
        grid=(b, t // tm),
        in_specs=[
            tile(ATTN_WIDTH), tile(HYENA_WIDTH), tile(D_MODEL),
            pl.BlockSpec((None, N_MOD, D_MODEL), lambda bb, i: (row(bb), 0, 0)),
            pl.BlockSpec((1, D_MODEL), lambda bb, i: (0, 0)),
            half(0), half(1),
        ],
        out_specs=[tile(D_MODEL), tile(D_MODEL)],
        out_shape=[jax.ShapeDtypeStruct(x.shape, F32), jax.ShapeDtypeStruct(x.shape, BF16)],
        compiler_params=_cparams(("arbitrary", "arbitrary"), 48),
        name="out_proj",
    )(attn, hy, x, mod, nw, w_bf, w_bf)


MLP_OUT_CHUNK = 512


def _mlp_kernel(h_ref, x_ref, m_ref, w1_ref, w2_ref, o_ref, *, n_f):
    f = pl.program_id(2)

    @pl.when(f == 0)
    def _():
        o_ref[...] = jnp.zeros_like(o_ref)

    a = jnp.maximum(jnp.dot(h_ref[...], w1_ref[...], preferred_element_type=F32), 0.0)
    a = (a * a).astype(BF16)
    for n0 in range(0, D_MODEL, MLP_OUT_CHUNK):
        sl = slice(n0, n0 + MLP_OUT_CHUNK)
        o_ref[:, sl] += jnp.dot(a, w2_ref[:, sl], preferred_element_type=F32)

    @pl.when(f == n_f - 1)
    def _():
        o_ref[...] = x_ref[...] + m_ref[5:6, :] * o_ref[...]


def _mlp(h, x, mod, mod_row, w1_bf, w2_bf, layer, *, tm, tf):
    b, t, _ = x.shape
    n_f = D_FF // tf
    row = (lambda bb: bb) if mod_row is None else (lambda bb: mod_row)
    tile = pl.BlockSpec((None, tm, D_MODEL), lambda bb, i, f: (bb, i, 0))
    return pl.pallas_call(
        functools.partial(_mlp_kernel, n_f=n_f),
        grid=(b, t // tm, n_f),
        in_specs=[
            tile, tile,
            pl.BlockSpec((None, N_MOD, D_MODEL), lambda bb, i, f: (row(bb), 0, 0)),
            pl.BlockSpec((None, D_MODEL, tf), lambda bb, i, f: (layer, 0, f)),
            pl.BlockSpec((None, tf, D_MODEL), lambda bb, i, f: (layer, f, 0)),
        ],
        out_specs=tile,
        out_shape=jax.ShapeDtypeStruct(x.shape, F32),
        compiler_params=_cparams(("arbitrary", "arbitrary", "arbitrary"), 56),
        name="mlp",
    )(h, x, mod, w1_bf, w2_bf)


def _p_add(p, q):
    if p is None:
        return q
    if q is None:
        return p
    (a, sa), (b, sb) = p, q
    if sa == sb:
        return (a + b, sa)
    return (a - b, 1) if sa > 0 else (b - a, 1)


def _p_scale(p, k):
    if p is None or k == 0.0:
        return None
    a, s = p
    if k < 0:
        s, k = -s, -k
    return (a, s) if k == 1.0 else (a * k, s)


def _c_add(x, y, sign=1):
    if y is None:
        return x
    if x is None:
        x = (None, None)
    return (_p_add(x[0], _p_scale(y[0], sign)), _p_add(x[1], _p_scale(y[1], sign)))


def _snap(v):
    for t in (0.0, 1.0, -1.0):
        if abs(v - t) < 1e-12:
            return t
    return v


def _c_mulc(x, c):
    if x is None:
        return None
    cr, ci = _snap(c.real), _snap(c.imag)
    re, im = x
    if cr != 0.0 and ci != 0.0 and abs(abs(cr) - abs(ci)) < 1e-12:
        k, sr, si = abs(cr), math.copysign(1.0, cr), math.copysign(1.0, ci)
        return (_p_scale(_p_add(_p_scale(re, sr), _p_scale(im, -si)), k),
                _p_scale(_p_add(_p_scale(re, si), _p_scale(im, sr)), k))
    return (_p_add(_p_scale(re, cr), _p_scale(im, -ci)), _p_add(_p_scale(re, ci), _p_scale(im, cr)))


def _is_zero(x):
    return x is None or (x[0] is None and x[1] is None)


def _slab_fft_dit(xs, sign, need):
    n = len(xs)
    if n == 1:
        return {0: xs[0]}
    if all(_is_zero(x) for x in xs):
        return {k: None for k in need}
    h = n // 2
    sub = {k % h for k in need}
    ev = _slab_fft_dit(xs[0::2], sign, sub)
    od = _slab_fft_dit(xs[1::2], sign, sub)
    out = {}
    for k in sub:
        t = _c_mulc(od[k], cmath.exp(sign * 2j * math.pi * k / n))
        if k in need:
            out[k] = _c_add(ev[k], t)
        if k + h in need:
            out[k + h] = _c_add(ev[k], t, -1)
    return out


def _slab_fft_dif(xs, sign, need):
    n = len(xs)
    if n == 1:
        return {0: xs[0]}
    h = n // 2
    need_e = {k // 2 for k in need if k % 2 == 0}
    need_o = {k // 2 for k in need if k % 2 == 1}
    out = {}
    if need_e:
        ev = _slab_fft_dif([_c_add(xs[j], xs[j + h]) for j in range(h)], sign, need_e)
        out.update({2 * k: v for k, v in ev.items()})
    if need_o:
        dif = [_c_mulc(_c_add(xs[j], xs[j + h], -1), cmath.exp(sign * 2j * math.pi * j / n)) for j in range(h)]
        od = _slab_fft_dif(dif, sign, need_o)
        out.update({2 * k + 1: v for k, v in od.items()})
    return out


def _part_value(p, like):
    if p is None:
        return jnp.zeros_like(like)
    return p[0] if p[1] > 0 else -p[0]


HY_CT = 128
SLAB = 256
N_HI = 2 * SEQ // SLAB
N_IN = SEQ // SLAB
N_SPEC = N_HI // 2 + 1
ROW_CHUNK = 8
N_CT = HYENA_WIDTH // HY_CT
LO_GROUP = 3
SPEC_GROUP = 4
FILTER_HIDDEN = 64
EMB_PAD = 128
HIGHEST = lax.Precision.HIGHEST


def _dft_constants():
    n = 2 * SEQ
    r = np.arange(SLAB)
    fwd = np.zeros((N_SPEC, 2 * SLAB, SLAB))
    inv = np.zeros((N_SPEC, 2 * SLAB, SLAB))
    for b in range(N_SPEC):
        m = np.exp(-2j * np.pi * (np.outer(r, r) / SLAB + (r * b)[None, :] / n))
        a = m.T / n
        fwd[b] = np.concatenate([m.real, m.imag], axis=0)
        inv[b] = np.concatenate([a.real, a.imag], axis=0)
    return jnp.asarray(fwd, F32).astype(BF16), jnp.asarray(inv, F32).astype(BF16)


def _ctx_dft_constants():
    n = 2 * CTX_LEN
    k = np.arange(n)
    f = np.exp(-2j * np.pi * np.outer(k, k) / n)
    a = np.conj(f)[:CTX_LEN, :] / n
    return (jnp.asarray(np.concatenate([f.real, f.imag], axis=0), F32).astype(BF16),
            jnp.asarray(np.concatenate([a.real, a.imag], axis=0), F32).astype(BF16))


def _filter_positions(l):
    t = jnp.linspace(0.0, 1.0, l, dtype=F32)[:, None]
    bands = (FILTER_EMB - 1) // 2
    f = jnp.linspace(1e-4, bands - 1, bands, dtype=F32)[None, :]
    w = 2.0 * math.pi * jnp.arange(l, dtype=F32)[:, None] / l
    z = jnp.concatenate([t, jnp.cos(f * w), -jnp.sin(f * w)], axis=-1)
    z2 = jnp.concatenate([z, z[0:1], z[1:][::-1]], axis=0)
    return jnp.pad(z2, ((0, 0), (0, EMB_PAD - FILTER_EMB)))


def _filter_feat_kernel(z_ref, w1_ref, b1_ref, w2_ref, b2_ref, w3_ref, b3_ref, fr_ref, o_ref):
    fr = fr_ref[...]
    dot = functools.partial(jnp.dot, precision=HIGHEST, preferred_element_type=F32)
    h = jnp.sin(fr * (dot(z_ref[...], w1_ref[...]) + b1_ref[...]))
    h = jnp.sin(fr * (dot(h, w2_ref[...]) + b2_ref[...]))
    o_ref[...] = jnp.sin(fr * (dot(h, w3_ref[...]) + b3_ref[...]))


def _filter_features(emb, w1, b1, w2, b2, w3, b3, freq):
    rows = emb.shape[0]
    tr = min(rows, 1024)
    w1p = jnp.pad(w1, ((0, EMB_PAD - FILTER_EMB), (0, 0)))
    vec = lambda v: v.reshape(1, FILTER_HIDDEN)
    full = lambda shape: pl.BlockSpec(shape, lambda i: (0, 0))
    return pl.pallas_call(
        _filter_feat_kernel,
        grid=(rows // tr,),
        in_specs=[pl.BlockSpec((tr, EMB_PAD), lambda i: (i, 0)),
                  full((EMB_PAD, FILTER_HIDDEN)), full((1, FILTER_HIDDEN)),
                  full((FILTER_HIDDEN, FILTER_HIDDEN)), full((1, FILTER_HIDDEN)),
                  full((FILTER_HIDDEN, FILTER_HIDDEN)), full((1, FILTER_HIDDEN)),
                  full((1, FILTER_HIDDEN))],
        out_specs=pl.BlockSpec((tr, FILTER_HIDDEN), lambda i: (i, 0)),
        out_shape=jax.ShapeDtypeStruct((rows, FILTER_HIDDEN), F32),
        compiler_params=_cparams(("arbitrary",), 32),
        name="filter_feat",
    )(emb, w1p, vec(b1), w2, vec(b2), w3, vec(b3), vec(freq))


def _filter_taps(raw, n, l, absdelta):
    m = jnp.where(n < l, n, 2 * l - n)
    t = m.astype(F32) / (l - 1)
    return jnp.where(n == l, 0.0, raw * jnp.exp(-t * absdelta))


def _cplx_from_pq(pq, conj):
    half = pq.shape[0] // 2
    pr, pi_ = pq[:half, :HY_CT], pq[:half, HY_CT:]
    qr, qi = pq[half:, :HY_CT], pq[half:, HY_CT:]
    if conj:
        return pr + qi, pi_ - qr
    return pr - qi, qr + pi_


def _slab_forward(src_ref, n_in, real_only, dst_ref, need, scale):
    def body(rc, carry):
        rows = pl.ds(pl.multiple_of(rc * ROW_CHUNK, ROW_CHUNK), ROW_CHUNK)
        xs = []
        for a in range(N_HI):
            if a >= n_in:
                xs.append(None)
                continue
            re = src_ref[a, rows, 0:HY_CT]
            if scale is not None:
                re = re * scale
            xs.append(((re, 1), None if real_only else (src_ref[a, rows, HY_CT:2 * HY_CT], 1)))
        like = xs[0][0][0]
        out = _slab_fft_dit(xs, -1, need)
        for b in sorted(need):
            c = out[b] or (None, None)
            dst_ref[b, rows, 0:HY_CT] = _part_value(c[0], like)
            dst_ref[b, rows, HY_CT:2 * HY_CT] = _part_value(c[1], like)
        return carry

    lax.fori_loop(0, SLAB // ROW_CHUNK, body, 0)


def _slab_inverse(s_ref):
    def body(rc, carry):
        rows = pl.ds(pl.multiple_of(rc * ROW_CHUNK, ROW_CHUNK), ROW_CHUNK)
        xs = [((s_ref[b, rows, 0:HY_CT], 1), (s_ref[b, rows, HY_CT:2 * HY_CT], 1)) for b in range(N_HI)]
        like = xs[0][0][0]
        out = _slab_fft_dif(xs, 1, set(range(N_IN)))
        for a in range(N_IN):
            c = out[a] or (None, None)
            s_ref[a, rows, 0:HY_CT] = _part_value(c[0], like)
            s_ref[a, rows, HY_CT:2 * HY_CT] = _part_value(c[1], like)
        return carry

    lax.fori_loop(0, SLAB // ROW_CHUNK, body, 0)


def _filter_spec_kernel(feat_ref, wf_ref, wb_ref, ad_ref, mf_ref, o_ref, ft_ref, s_ref):
    absdelta = ad_ref[...]

    def taps(a, acc):
        rows = feat_ref[pl.ds(pl.multiple_of(a * SLAB, SLAB), SLAB), :]
        w = jnp.where(a < N_IN, wf_ref[...], wb_ref[...])
        raw = jnp.dot(rows.astype(BF16), w.astype(BF16), preferred_element_type=F32)
        n = a * SLAB + lax.broadcasted_iota(jnp.int32, raw.shape, 0)
        v = _filter_taps(raw, n, SEQ, absdelta)
        ft_ref[a] = v
        return acc + jnp.sum(jnp.abs(v), axis=0, keepdims=True)

    norm = lax.fori_loop(0, N_HI, taps, jnp.zeros((1, HY_CT), F32), unroll=2)
    _slab_forward(ft_ref, N_HI, True, s_ref, set(range(N_SPEC)), 1.0 / norm)

    def spec(b):
        pq = jnp.dot(mf_ref[b], s_ref[b].astype(BF16), preferred_element_type=F32)
        xr, xi = _cplx_from_pq(pq, False)
        rows = pl.ds(pl.multiple_of(b * SLAB, SLAB), SLAB)
        o_ref[rows, 0:HY_CT] = xr.astype(BF16)
        o_ref[rows, HY_CT:2 * HY_CT] = xi.astype(BF16)

    def spec_group(i, carry):
        for t in range(SPEC_GROUP):
            spec(1 + SPEC_GROUP * i + t)
        return carry

    spec(0)
    lax.fori_loop(0, (N_SPEC - 1) // SPEC_GROUP, spec_group, 0)


def _filter_spectrum(feat, w_fout, absdelta, mf):
    cols = 2 * N_CT
    return pl.pallas_call(
        _filter_spec_kernel,
        grid=(HYENA_ORDER, N_CT),
        in_specs=[
            pl.BlockSpec((2 * SEQ, FILTER_HIDDEN), lambda o, c: (0, 0)),
            pl.BlockSpec((FILTER_HIDDEN, HY_CT), lambda o, c: (0, o * cols + c)),
            pl.BlockSpec((FILTER_HIDDEN, HY_CT), lambda o, c: (0, o * cols + N_CT + c)),
            pl.BlockSpec((1, HY_CT), lambda o, c: (0, c)),
            pl.BlockSpec((N_SPEC, 2 * SLAB, SLAB), lambda o, c: (0, 0, 0)),
        ],
        out_specs=pl.BlockSpec((None, None, N_SPEC * SLAB, 2 * HY_CT), lambda o, c: (o, c, 0, 0)),
        out_shape=jax.ShapeDtypeStruct((HYENA_ORDER, N_CT, N_SPEC * SLAB, 2 * HY_CT), BF16),
        scratch_shapes=[pltpu.VMEM((N_HI, SLAB, HY_CT), F32), pltpu.VMEM((N_SPEC, SLAB, 2 * HY_CT), F32)],
        compiler_params=_cparams(("arbitrary", "arbitrary"), 48),
        name="filter_spec",
    )(feat, w_fout, w_fout, absdelta, mf)


def _conv_slab(u_ref, bi, a, w, b):
    r0 = a * SLAB
    x = u_ref[bi, pl.ds(pl.multiple_of(r0, SLAB), SLAB), :].astype(F32)
    before = u_ref[bi, pl.ds(pl.multiple_of(jnp.maximum(r0 - 16, 0), 16), 16), :].astype(F32)
    after = u_ref[bi, pl.ds(pl.multiple_of(jnp.minimum(r0 + SLAB, SEQ - 16), 16), 16), :].astype(F32)
    prev_row = jnp.where(a > 0, before[15:16, :], 0.0)
    next_row = jnp.where(a < N_IN - 1, after[0:1, :], 0.0)
    row = lax.broadcasted_iota(jnp.int32, x.shape, 0)
    xm1 = jnp.where(row == 0, prev_row, pltpu.roll(x, 1, 0))
    xp1 = jnp.where(row == SLAB - 1, next_row, pltpu.roll(x, SLAB - 1, 0))
    return xm1 * w[0:1, :] + x * w[1:2, :] + xp1 * w[2:3, :] + b


def _hyena_kernel(uv_ref, u1_ref, u2_ref, cwv_ref, cw1_ref, cw2_ref, cbv_ref, cb1_ref, cb2_ref, skip_ref,
                  mf_ref, mi_ref, h_ref, o_ref, zc_ref, g_ref, s_ref):
    o = pl.program_id(2)

    @pl.when(o == 0)
    def _():
        def conv(a, carry):
            for bi in range(2):
                lanes = slice(bi * HY_CT, (bi + 1) * HY_CT)
                zc_ref[a, :, lanes] = _conv_slab(uv_ref, bi, a, cwv_ref[...], cbv_ref[...])
                g_ref[0, a, :, lanes] = _conv_slab(u1_ref, bi, a, cw1_ref[...], cb1_ref[...]).astype(BF16)
                g_ref[1, a, :, lanes] = _conv_slab(u2_ref, bi, a, cw2_ref[...], cb2_ref[...]).astype(BF16)
            return carry

        lax.fori_loop(0, N_IN, conv, 0)

    _slab_forward(zc_ref, N_IN, False, s_ref, set(range(N_HI)), None)

    def through_filter(slabs, j):
        h = h_ref[pl.ds(pl.multiple_of(j * SLAB, SLAB), SLAB), :].astype(F32)
        hr, hi = h[:, :HY_CT], h[:, HY_CT:]
        w2 = 2 * HY_CT
        d = jnp.concatenate([s_ref[b] for b, _ in slabs], axis=1).astype(BF16)
        pq = jnp.dot(mf_ref[j], d, preferred_element_type=F32)
        ys = []
        for k, (_, conj) in enumerate(slabs):
            xr, xi = _cplx_from_pq(pq[:, k * w2:(k + 1) * w2], conj)
            hs = -hi if conj else hi
            ys += [xr * hr - xi * hs, xr * hs + xi * hr]
        pq2 = jnp.dot(mi_ref[j], jnp.concatenate(ys, axis=1).astype(BF16), preferred_element_type=F32)
        return [(b, _cplx_from_pq(pq2[:, k * w2:(k + 1) * w2], not conj)) for k, (b, conj) in enumerate(slabs)]

    def put_back(results):
        for b, (wr, wi) in results:
            s_ref[b, :, 0:HY_CT] = wr
            s_ref[b, :, HY_CT:2 * HY_CT] = wi

    put_back(through_filter([(0, False)], 0) + through_filter([(N_HI // 2, False)], N_HI // 2))

    def pair_group(i, carry):
        results = []
        for t in range(LO_GROUP):
            j = 1 + LO_GROUP * i + t
            results += through_filter([(j, False), (N_HI - j, True)], j)
        put_back(results)
        return carry

    lax.fori_loop(0, (N_HI // 2 - 1) // LO_GROUP, pair_group, 0)
    _slab_inverse(s_ref)

    sk = skip_ref[pl.ds(o, 1), :]
    skip2 = jnp.concatenate([sk, sk], axis=1)

    def gate(a, carry):
        zc_ref[a] = g_ref[o, a].astype(F32) * (s_ref[a] + skip2 * zc_ref[a])
        return carry

    lax.fori_loop(0, N_IN, gate, 0)

    @pl.when(o == HYENA_ORDER - 1)
    def _():
        def emit(a, carry):
            rows = pl.ds(pl.multiple_of(a * SLAB, SLAB), SLAB)
            for bi in range(2):
                o_ref[bi, rows, :] = zc_ref[a, :, bi * HY_CT:(bi + 1) * HY_CT].astype(BF16)
            return carry

        lax.fori_loop(0, N_IN, emit, 0)


def _hyena_conv(p, conv_w, conv_b, skip, mf, mi, hspec):
    b = p.shape[0]
    u0 = V_END // HY_CT
    cb = conv_b.reshape(1, -1)
    part = lambda k: pl.BlockSpec((2, SEQ, HY_CT), lambda c, q, o: (q, 0, u0 + k * N_CT + c))
    cw = lambda k: pl.BlockSpec((SHORT_CONV, HY_CT), lambda c, q, o: (0, k * N_CT + c))
    cbs = lambda k: pl.BlockSpec((1, HY_CT), lambda c, q, o: (0, k * N_CT + c))
    const = lambda: pl.BlockSpec((N_SPEC, 2 * SLAB, SLAB), lambda c, q, o: (0, 0, 0), pipeline_mode=pl.Buffered(1))
    return pl.pallas_call(
        _hyena_kernel,
        grid=(N_CT, b // 2, HYENA_ORDER),
        in_specs=[part(0), part(1), part(2), cw(0), cw(1), cw(2), cbs(0), cbs(1), cbs(2),
                  pl.BlockSpec((HYENA_ORDER, HY_CT), lambda c, q, o: (0, c)),
                  const(), const(),
                  pl.BlockSpec((None, None, N_SPEC * SLAB, 2 * HY_CT), lambda c, q, o: (o, c, 0, 0))],
        out_specs=pl.BlockSpec((2, SEQ, HY_CT), lambda c, q, o: (q, 0, c)),
        out_shape=jax.ShapeDtypeStruct((b, SEQ, HYENA_WIDTH), BF16),
        scratch_shapes=[pltpu.VMEM((N_IN, SLAB, 2 * HY_CT), F32),
                        pltpu.VMEM((HYENA_ORDER, N_IN, SLAB, 2 * HY_CT), BF16),
                        pltpu.VMEM((N_HI, SLAB, 2 * HY_CT), F32)],
        compiler_params=_cparams(("arbitrary", "arbitrary", "arbitrary"), 56),
        name="hyena_conv",
    )(p, p, p, conv_w, conv_w, conv_w, cb, cb, cb, skip, mf, mi, hspec)


def _ctx_hyena_kernel(uv_ref, u1_ref, u2_ref, cwv_ref, cw1_ref, cw2_ref, cbv_ref, cb1_ref, cb2_ref, skip_ref,
                      feat_ref, wf0_ref, wb0_ref, wf1_ref, wb1_ref, ad_ref, ff_ref, ai_ref, o_ref):
    l = CTX_LEN
    nb = uv_ref.shape[0]
    n = lax.broadcasted_iota(jnp.int32, (2 * l, HY_CT), 0)
    spectra = []
    for wf_ref, wb_ref in ((wf0_ref, wb0_ref), (wf1_ref, wb1_ref)):
        raw = jnp.concatenate(
            [jnp.dot(feat_ref[0:l, :], wf_ref[...], precision=HIGHEST, preferred_element_type=F32),
             jnp.dot(feat_ref[l:2 * l, :], wb_ref[...], precision=HIGHEST, preferred_element_type=F32)], axis=0)
        v = _filter_taps(raw, n, l, ad_ref[...])
        v = v / jnp.sum(jnp.abs(v), axis=0, keepdims=True)
        pq = jnp.dot(ff_ref[...], v.astype(BF16), preferred_element_type=F32)
        spectra.append((pq[:2 * l], pq[2 * l:]))

    row = lax.broadcasted_iota(jnp.int32, (l, HY_CT), 0)

    def conv(u_ref, bi, w_ref, b_ref):
        x = u_ref[bi].astype(F32)
        xm1 = jnp.where(row == 0, 0.0, pltpu.roll(x, 1, 0))
        xp1 = jnp.where(row == l - 1, 0.0, pltpu.roll(x, l - 1, 0))
        return xm1 * w_ref[0:1, :] + x * w_ref[1:2, :] + xp1 * w_ref[2:3, :] + b_ref[...]

    z = [conv(uv_ref, bi, cwv_ref, cbv_ref) for bi in range(nb)]
    gates = [[conv(u1_ref, bi, cw1_ref, cb1_ref) for bi in range(nb)],
             [conv(u2_ref, bi, cw2_ref, cb2_ref) for bi in range(nb)]]
    for o in range(HYENA_ORDER):
        hr, hi = spectra[o]
        pq = jnp.dot(ff_ref[:, 0:l], jnp.concatenate(z, axis=1).astype(BF16), preferred_element_type=F32)
        ys = []
        for q in range(nb // 2):
            xr, xi = _cplx_from_pq(pq[:, 2 * q * HY_CT:2 * (q + 1) * HY_CT], False)
            ys += [xr * hr - xi * hi, xr * hi + xi * hr]
        pq2 = jnp.dot(ai_ref[...], jnp.concatenate(ys, axis=1).astype(BF16), preferred_element_type=F32)
        y = []
        for q in range(nb // 2):
            y += list(_cplx_from_pq(pq2[:, 2 * q * HY_CT:2 * (q + 1) * HY_CT], False))
        z = [gates[o][bi] * (y[bi] + skip_ref[o:o + 1, :] * z[bi]) for bi in range(nb)]
    for bi in range(nb):
        o_ref[bi] = z[bi].astype(BF16)


def _ctx_hyena(pc, conv_w, conv_b, skip, feat, w_fout, absdelta, ff, ai):
    b, l, _ = pc.shape
    u0 = V_END // HY_CT
    cb = conv_b.reshape(1, -1)
    cols = 2 * N_CT
    part = lambda k: pl.BlockSpec((b, l, HY_CT), lambda c: (0, 0, u0 + k * N_CT + c))
    cw = lambda k: pl.BlockSpec((SHORT_CONV, HY_CT), lambda c: (0, k * N_CT + c))
    cbs = lambda k: pl.BlockSpec((1, HY_CT), lambda c: (0, k * N_CT + c))
    wcol = lambda off: pl.BlockSpec((FILTER_HIDDEN, HY_CT), lambda c: (0, off + c))
    return pl.pallas_call(
        _ctx_hyena_kernel,
        grid=(N_CT,),
        in_specs=[part(0), part(1), part(2), cw(0), cw(1), cw(2), cbs(0), cbs(1), cbs(2),
                  pl.BlockSpec((HYENA_ORDER, HY_CT), lambda c: (0, c)),
                  pl.BlockSpec((2 * l, FILTER_HIDDEN), lambda c: (0, 0)),
                  wcol(0), wcol(N_CT), wcol(cols), wcol(cols + N_CT),
                  pl.BlockSpec((1, HY_CT), lambda c: (0, c)),
                  pl.BlockSpec((4 * l, 2 * l), lambda c: (0, 0)),
                  pl.BlockSpec((2 * l, 2 * l), lambda c: (0, 0))],
        out_specs=pl.BlockSpec((b, l, HY_CT), lambda c: (0, 0, c)),
        out_shape=jax.ShapeDtypeStruct((b, l, HYENA_WIDTH), BF16),
        compiler_params=_cparams(("arbitrary",), 32),
        name="ctx_hyena",
    )(pc, pc, pc, conv_w, conv_w, conv_w, cb, cb, cb, skip, feat, w_fout, w_fout, w_fout, w_fout, absdelta, ff, ai)


def _rope_tables(l):
    rows = l // GRID_W
    row = jnp.repeat(jnp.arange(rows), GRID_W).astype(F32)
    col = jnp.tile(jnp.arange(GRID_W), rows).astype(F32)
    n_freq = HEAD_DIM // 4
    inv = ROPE_BASE ** (-jnp.arange(n_freq, dtype=F32) / n_freq)
    ang_r = row[:, None] * inv
    ang_c = col[:, None] * inv
    cos = jnp.concatenate([jnp.cos(ang_r), jnp.cos(ang_r), jnp.cos(ang_c), jnp.cos(ang_c)], axis=-1)
    sin = jnp.concatenate([-jnp.sin(ang_r), jnp.sin(ang_r), -jnp.sin(ang_c), jnp.sin(ang_c)], axis=-1)
    return cos, sin


def kernel(x, c, ctx, c_ctx, w_ada, b_ada, norm_mix, norm_mlp, w_in, q_norm, k_norm, attn_sink, conv_w, conv_b, filt_w1, filt_b1, filt_w2, filt_b2, filt_w3, filt_b3, filt_freq, filt_w_out, hyena_skip, w_out, w_mlp_in, w_mlp_out):
    cos, sin = _rope_tables(SEQ)
    emb, emb_ctx = _filter_positions(SEQ), _filter_positions(CTX_LEN)
    mf, mi = _dft_constants()
    ff_ctx, ai_ctx = _ctx_dft_constants()
    absdelta = jnp.abs(jnp.linspace(MIN_DECAY, MAX_DECAY, HYENA_WIDTH, dtype=F32)).reshape(1, HYENA_WIDTH)
    cvec = jnp.zeros((MOD_ROWS, D_MODEL), F32).at[:BATCH].set(c).at[CTX_ROW].set(c_ctx)
    mod_all = _modulation(cvec, w_ada, b_ada).reshape(DEPTH, MOD_ROWS, N_MOD, D_MODEL)
    w_in_bf = w_in.astype(BF16)
    w_out_bf = w_out.astype(BF16)
    w1_bf = w_mlp_in.astype(BF16)
    w2_bf = w_mlp_out.astype(BF16)
    ctx_rows = BATCH * CTX_LEN
    flat = lambda a: a.reshape(1, ctx_rows, a.shape[-1])
    per_batch = lambda a: a.reshape(BATCH, CTX_LEN, a.shape[-1])
    xc = flat(ctx)
    for i in range(DEPTH):
        update_ctx = i < DEPTH - 1
        mod = mod_all[i]
        nmix = norm_mix[i].reshape(1, D_MODEL)
        nmlp = norm_mlp[i].reshape(1, D_MODEL)
        qn = q_norm[i].reshape(1, HEAD_DIM)
        kn = k_norm[i].reshape(1, HEAD_DIM)
        filt_mlp = (filt_w1[i], filt_b1[i], filt_w2[i], filt_b2[i], filt_w3[i], filt_b3[i], filt_freq[i])

        p = _in_projection(x, mod, None, nmix, w_in_bf, i, qn, kn, cos, sin,
                           tm=512, n_off=0, n_tiles=IN_COLS // IN_TN, use_rope=True)
        if update_ctx:
            pc = _in_projection(xc, mod, CTX_ROW, nmix, w_in_bf, i, qn, kn, cos, sin,
                                tm=512, n_off=0, n_tiles=IN_COLS // IN_TN, use_rope=False)
            kx_blk, vx_blk = Q_END // KV_COLS * 2, Q_END // KV_COLS * 2 + 1
        else:
            pc = _in_projection(xc, mod, CTX_ROW, nmix, w_in_bf, i, qn, kn, cos, sin,
                                tm=512, n_off=KV_TILE, n_tiles=1, use_rope=False)
            kx_blk, vx_blk = 0, 1
        pc = per_batch(pc)

        attn = _window_attention(p, pc, kx_blk, vx_blk, attn_sink[i])
        hspec = _filter_spectrum(_filter_features(emb, *filt_mlp), filt_w_out[i], absdelta, mf)
        hy = _hyena_conv(p, conv_w[i], conv_b[i], hyena_skip[i], mf, mi, hspec)
        x, h2 = _out_projection(attn, hy, x, mod, None, nmlp, w_out_bf, i, tm=512)
        x = _mlp(h2, x, mod, None, w1_bf, w2_bf, i, tm=512, tf=1024)

        if update_ctx:
            attn_c = _context_attention(pc, attn_sink[i])
            hy_c = _ctx_hyena(pc, conv_w[i], conv_b[i], hyena_skip[i], _filter_features(emb_ctx, *filt_mlp),
                              filt_w_out[i], absdelta, ff_ctx, ai_ctx)
            xc, h2c = _out_projection(flat(attn_c), flat(hy_c), xc, mod, CTX_ROW, nmlp, w_out_bf, i, tm=512)
            xc = _mlp(h2c, xc, mod, CTX_ROW, w1_bf, w2_bf, i, tm=512, tf=1024)
    return x
```

```python
import cmath
import functools
import math

import jax
import jax.numpy as jnp
import numpy as np
from jax import lax
from jax.experimental import pallas as pl
from jax.experimental.pallas import tpu as pltpu

D_MODEL = 2048
BATCH = 4
SEQ = 4096
DEPTH = 2
GRID_W = 64
CTX_LEN = 256
HEAD_DIM = 128
N_Q_HEADS = 8
N_KV_HEADS = 2
GQA_GROUP = N_Q_HEADS // N_KV_HEADS
ATTN_WIDTH = N_Q_HEADS * HEAD_DIM
HYENA_WIDTH = D_MODEL - ATTN_WIDTH
HYENA_ORDER = 2
SHORT_CONV = 3
FILTER_EMB = 33
DECAY_TARGET = 1e-2
FAST_DECAY_PCT = 0.3
SLOW_DECAY_PCT = 1.5
MAX_DECAY = math.log(DECAY_TARGET) / FAST_DECAY_PCT
MIN_DECAY = math.log(DECAY_TARGET) / SLOW_DECAY_PCT
WINDOW = 128
BLOCK = 128
ROPE_BASE = 10000.0
D_FF = 4 * D_MODEL
N_MOD = 6
EPS = 1e-6
NEG_INF = -1e30
Q_END = ATTN_WIDTH
K_END = Q_END + N_KV_HEADS * HEAD_DIM
V_END = K_END + N_KV_HEADS * HEAD_DIM
IN_COLS = V_END + (HYENA_ORDER + 1) * HYENA_WIDTH

MOD_ROWS = 8
CTX_ROW = BATCH
KV_COLS = 2 * N_KV_HEADS * HEAD_DIM

F32 = jnp.float32
BF16 = jnp.bfloat16
MIB = 1024 * 1024


def _cparams(sem, vmem_mib):
    return pltpu.CompilerParams(dimension_semantics=sem, vmem_limit_bytes=vmem_mib * MIB)


def _mod_kernel(c_ref, w_ref, b_ref, o_ref):
    c = c_ref[...]
    s = (c * jax.nn.sigmoid(c)).astype(BF16)
    o_ref[...] = jnp.dot(s, w_ref[...].astype(BF16), preferred_element_type=F32) + b_ref[...]


def _modulation(cvec, w_ada, b_ada):
    tn = 1024
    n_cols = N_MOD * D_MODEL
    return pl.pallas_call(
        _mod_kernel,
        grid=(DEPTH, n_cols // tn),
        in_specs=[
            pl.BlockSpec((MOD_ROWS, D_MODEL), lambda l, j: (0, 0)),
            pl.BlockSpec((None, D_MODEL, tn), lambda l, j: (l, 0, j)),
            pl.BlockSpec((None, 1, tn), lambda l, j: (l, 0, j)),
        ],
        out_specs=pl.BlockSpec((None, MOD_ROWS, tn), lambda l, j: (l, 0, j)),
        out_shape=jax.ShapeDtypeStruct((DEPTH, MOD_ROWS, n_cols), F32),
        compiler_params=_cparams(("arbitrary", "arbitrary"), 40),
        name="adaln_mod",
    )(cvec, w_ada, b_ada.reshape(DEPTH, 1, n_cols))


IN_TN = 512
Q_TILES = Q_END // IN_TN
KV_TILE = Q_TILES
HEADS_PER_TILE = IN_TN // HEAD_DIM


SUB_ROWS = 256


def _rms_modulate(xf, gain, shift):
    ms = jnp.mean(xf * xf, axis=-1, keepdims=True)
    return xf * lax.rsqrt(ms + EPS) * gain + shift


def _head_norm_rope(a, g, cos, sin):
    ms = jnp.mean(a * a, axis=-1, keepdims=True)
    y = a * lax.rsqrt(ms + EPS) * g
    if cos is None:
        return y
    lane = lax.broadcasted_iota(jnp.int32, y.shape, 1)
    partner = jnp.where((lane % 64) < 32, pltpu.roll(y, 96, 1), pltpu.roll(y, 32, 1))
    return y * cos + partner * sin


def _inproj_kernel(x_ref, m_ref, nw_ref, w_ref, qn_ref, kn_ref, cos_ref, sin_ref, o_ref, *, n_off, n_tiles, use_rope):
    gain = nw_ref[...] * (1.0 + m_ref[1:2, :])
    shift = m_ref[0:1, :]
    for r0 in range(0, x_ref.shape[0], SUB_ROWS):
        rows = slice(r0, r0 + SUB_ROWS)
        h = _rms_modulate(x_ref[rows, :], gain, shift).astype(BF16)
        cos = cos_ref[rows, :] if use_rope else None
        sin = sin_ref[rows, :] if use_rope else None
        for n in range(n_tiles):
            cols = slice(n * IN_TN, (n + 1) * IN_TN)
            acc = jnp.dot(h, w_ref[:, cols], preferred_element_type=F32)
            t = n + n_off
            if t > KV_TILE:
                o_ref[rows, cols] = acc.astype(BF16)
                continue
            gains = [qn_ref[...]] * HEADS_PER_TILE if t < Q_TILES else [kn_ref[...]] * N_KV_HEADS + [None] * N_KV_HEADS
            for c, g in enumerate(gains):
                sl = slice(c * HEAD_DIM, (c + 1) * HEAD_DIM)
                osl = slice(n * IN_TN + c * HEAD_DIM, n * IN_TN + (c + 1) * HEAD_DIM)
                if g is None:
                    o_ref[rows, osl] = acc[:, sl].astype(BF16)
                else:
                    o_ref[rows, osl] = _head_norm_rope(acc[:, sl], g, cos, sin).astype(BF16)


def _in_projection(x, mod, mod_row, nw, w_bf, layer, qn, kn, cos, sin, *, tm, n_off, n_tiles, use_rope):
    b, t, _ = x.shape
    assert n_off % n_tiles == 0 and tm % SUB_ROWS == 0
    row = (lambda bb: bb) if mod_row is None else (lambda bb: mod_row)
    kern = functools.partial(_inproj_kernel, n_off=n_off, n_tiles=n_tiles, use_rope=use_rope)
    width = n_tiles * IN_TN
    return pl.pallas_call(
        kern,
        grid=(b, t // tm),
        in_specs=[
            pl.BlockSpec((None, tm, D_MODEL), lambda bb, i: (bb, i, 0)),
            pl.BlockSpec((None, N_MOD, D_MODEL), lambda bb, i: (row(bb), 0, 0)),
            pl.BlockSpec((1, D_MODEL), lambda bb, i: (0, 0)),
            pl.BlockSpec((None, D_MODEL, width), lambda bb, i: (layer, 0, n_off // n_tiles),
                         pipeline_mode=pl.Buffered(1)),
            pl.BlockSpec((1, HEAD_DIM), lambda bb, i: (0, 0)),
            pl.BlockSpec((1, HEAD_DIM), lambda bb, i: (0, 0)),
            pl.BlockSpec((tm, HEAD_DIM), lambda bb, i: (i, 0)),
            pl.BlockSpec((tm, HEAD_DIM), lambda bb, i: (i, 0)),
        ],
        out_specs=pl.BlockSpec((None, tm, width), lambda bb, i: (bb, i, 0)),
        out_shape=jax.ShapeDtypeStruct((b, t, width), BF16),
        compiler_params=_cparams(("arbitrary", "arbitrary"), 48),
        name="in_proj",
    )(x, mod, nw, w_bf, qn, kn, cos, sin)


def _sink_column(sink_ref, kv, rows_per_head):
    rows = GQA_GROUP * rows_per_head
    head = lax.broadcasted_iota(jnp.int32, (rows, 1), 0) // rows_per_head
    col = jnp.full((rows, 1), sink_ref[kv * GQA_GROUP], F32)
    for g in range(1, GQA_GROUP):
        col = jnp.where(head == g, sink_ref[kv * GQA_GROUP + g], col)
    return col


def _stack_heads(q_ref, kv, rows=slice(None)):
    return jnp.concatenate(
        [q_ref[rows, (kv * GQA_GROUP + g) * HEAD_DIM:(kv * GQA_GROUP + g + 1) * HEAD_DIM]
         for g in range(GQA_GROUP)], axis=0)


LOG2E = math.log2(math.e)
QK_LOG2_SCALE = HEAD_DIM ** -0.5 * LOG2E


def _softmax_pv(q4, k_all, v_all, bias, sink_col):
    t = lax.dot_general(q4, k_all, (((1,), (1,)), ((), ())), preferred_element_type=F32) * QK_LOG2_SCALE
    if bias is not None:
        t = t + bias
    sink2 = sink_col * LOG2E
    m = jnp.maximum(jnp.max(t, axis=-1, keepdims=True), sink2)
    e = jnp.exp2(t - m)
    denom = jnp.exp2(sink2 - m) + jnp.sum(e, axis=-1, keepdims=True)
    return jnp.dot(e.astype(BF16), v_all, preferred_element_type=F32) / denom


def _window_attn_kernel(sink_ref, bias0_ref, bias1_ref, q_ref, kp_ref, km_ref, kn_ref, vp_ref, vm_ref, vn_ref,
                        kx_ref, vx_ref, o_ref):
    lo, hi = slice(0, BLOCK), slice(BLOCK, 2 * BLOCK)
    for blk, bias_ref in enumerate((bias0_ref, bias1_ref)):
        rows = slice(blk * BLOCK, (blk + 1) * BLOCK)
        for kv in range(N_KV_HEADS):
            ks = slice(kv * HEAD_DIM, (kv + 1) * HEAD_DIM)
            if blk == 0:
                band_k = [kp_ref[:, ks], km_ref[lo, ks], km_ref[hi, ks]]
                band_v = [vp_ref[:, ks], vm_ref[lo, ks], vm_ref[hi, ks]]
            else:
                band_k = [km_ref[lo, ks], km_ref[hi, ks], kn_ref[:, ks]]
                band_v = [vm_ref[lo, ks], vm_ref[hi, ks], vn_ref[:, ks]]
            k_all = jnp.concatenate([kx_ref[:, ks]] + band_k, axis=0)
            v_all = jnp.concatenate([vx_ref[:, ks]] + band_v, axis=0)
            o = _softmax_pv(_stack_heads(q_ref, kv, rows), k_all, v_all, bias_ref[...],
                            _sink_column(sink_ref, kv, BLOCK))
            for g in range(GQA_GROUP):
                h = kv * GQA_GROUP + g
                o_ref[rows, h * HEAD_DIM:(h + 1) * HEAD_DIM] = o[g * BLOCK:(g + 1) * BLOCK].astype(BF16)


def _window_bias(nb):
    qi = np.arange(GQA_GROUP * BLOCK)[:, None] % BLOCK
    sj = np.arange(3 * BLOCK)[None, :] - BLOCK
    near = np.abs(sj - qi) <= WINDOW
    out = np.zeros((3, GQA_GROUP * BLOCK, CTX_LEN + 3 * BLOCK), np.float32)
    for variant, blk in enumerate((0, 1, nb - 1)):
        key_pos = blk * BLOCK + sj
        ok = near & (key_pos >= 0) & (key_pos < nb * BLOCK)
        out[variant, :, CTX_LEN:] = np.where(ok, 0.0, NEG_INF)
    return jnp.asarray(out)


def _window_attention(p, pc, kx_blk, vx_blk, sink):
    b, seq, _ = p.shape
    nb = seq // BLOCK
    kvw = N_KV_HEADS * HEAD_DIM
    k_blk, v_blk = Q_END // kvw, K_END // kvw
    steps = nb // 2
    prev = lambda bb, i: (bb, jnp.maximum(2 * i - 1, 0))
    nxt = lambda bb, i: (bb, jnp.minimum(2 * i + 2, nb - 1))

    def edge(which, blk):
        return pl.BlockSpec((None, BLOCK, kvw), lambda bb, i: which(bb, i) + (blk,))

    def mid(blk):
        return pl.BlockSpec((None, 2 * BLOCK, kvw), lambda bb, i: (bb, i, blk))

    bias_shape = (None, GQA_GROUP * BLOCK, CTX_LEN + 3 * BLOCK)
    bias = _window_bias(nb)
    return pl.pallas_call(
        _window_attn_kernel,
        grid=(b, steps),
        in_specs=[
            pl.BlockSpec(memory_space=pltpu.SMEM),
            pl.BlockSpec(bias_shape, lambda bb, i: (jnp.where(i == 0, 0, 1), 0, 0)),
            pl.BlockSpec(bias_shape, lambda bb, i: (jnp.where(i == steps - 1, 2, 1), 0, 0)),
            pl.BlockSpec((None, 2 * BLOCK, ATTN_WIDTH), lambda bb, i: (bb, i, 0)),
            edge(prev, k_blk), mid(k_blk), edge(nxt, k_blk),
            edge(prev, v_blk), mid(v_blk), edge(nxt, v_blk),
            pl.BlockSpec((None, CTX_LEN, kvw), lambda bb, i: (bb, 0, kx_blk)),
            pl.BlockSpec((None, CTX_LEN, kvw), lambda bb, i: (bb, 0, vx_blk)),
        ],
        out_specs=pl.BlockSpec((None, 2 * BLOCK, ATTN_WIDTH), lambda bb, i: (bb, i, 0)),
        out_shape=jax.ShapeDtypeStruct((b, seq, ATTN_WIDTH), BF16),
        compiler_params=_cparams(("arbitrary", "arbitrary"), 32),
        name="window_attn",
    )(sink, bias, bias, p, p, p, p, p, p, p, pc, pc)


def _ctx_attn_kernel(sink_ref, q_ref, k_ref, v_ref, o_ref):
    n = q_ref.shape[0]
    for kv in range(N_KV_HEADS):
        ks = slice(kv * HEAD_DIM, (kv + 1) * HEAD_DIM)
        o = _softmax_pv(_stack_heads(q_ref, kv), k_ref[:, ks], v_ref[:, ks], None, _sink_column(sink_ref, kv, n))
        for g in range(GQA_GROUP):
            h = kv * GQA_GROUP + g
            o_ref[:, h * HEAD_DIM:(h + 1) * HEAD_DIM] = o[g * n:(g + 1) * n].astype(BF16)


def _context_attention(pc, sink):
    b, n, _ = pc.shape
    kvw = N_KV_HEADS * HEAD_DIM
    return pl.pallas_call(
        _ctx_attn_kernel,
        grid=(b,),
        in_specs=[
            pl.BlockSpec(memory_space=pltpu.SMEM),
            pl.BlockSpec((None, n, ATTN_WIDTH), lambda bb: (bb, 0, 0)),
            pl.BlockSpec((None, n, kvw), lambda bb: (bb, 0, Q_END // kvw)),
            pl.BlockSpec((None, n, kvw), lambda bb: (bb, 0, K_END // kvw)),
        ],
        out_specs=pl.BlockSpec((None, n, ATTN_WIDTH), lambda bb: (bb, 0, 0)),
        out_shape=jax.ShapeDtypeStruct((b, n, ATTN_WIDTH), BF16),
        compiler_params=_cparams(("arbitrary",), 32),
        name="ctx_attn",
    )(sink, pc, pc, pc)


def _outproj_kernel(a_ref, y_ref, x_ref, m_ref, nw_ref, wa_ref, wy_ref, o_ref, h_ref):
    gain = nw_ref[...] * (1.0 + m_ref[4:5, :])
    shift = m_ref[3:4, :]
    for r0 in range(0, x_ref.shape[0], SUB_ROWS):
        rows = slice(r0, r0 + SUB_ROWS)
        mix = (jnp.dot(a_ref[rows, :], wa_ref[...], preferred_element_type=F32)
               + jnp.dot(y_ref[rows, :], wy_ref[...], preferred_element_type=F32))
        x1 = x_ref[rows, :] + m_ref[2:3, :] * mix
        o_ref[rows, :] = x1
        h_ref[rows, :] = _rms_modulate(x1, gain, shift).astype(BF16)


def _out_projection(attn, hy, x, mod, mod_row, nw, w_bf, layer, *, tm):
    b, t, _ = x.shape
    assert tm % SUB_ROWS == 0
    row = (lambda bb: bb) if mod_row is None else (lambda bb: mod_row)
    tile = lambda width: pl.BlockSpec((None, tm, width), lambda bb, i: (bb, i, 0))
    half = lambda k: pl.BlockSpec((None, D_MODEL // 2, D_MODEL), lambda bb, i: (layer, k, 0),
                                  pipeline_mode=pl.Buffered(1))
    return pl.pallas_call(
        _outproj_kernel,
        grid=(b, t // tm),
        in_specs=[
            tile(ATTN_WIDTH), tile(HYENA_WIDTH), tile(D_MODEL),
            pl.BlockSpec((None, N_MOD, D_MODEL), lambda bb, i: (row(bb), 0, 0)),
            pl.BlockSpec((1, D_MODEL), lambda bb, i: (0, 0)),
            half(0), half(1),
        ],
        out_specs=[tile(D_MODEL), tile(D_MODEL)],
        out_shape=[jax.ShapeDtypeStruct(x.shape, F32), jax.ShapeDtypeStruct(x.shape, BF16)],
        compiler_params=_cparams(("arbitrary", "arbitrary"), 48),
        name="out_proj",
    )(attn, hy, x, mod, nw, w_bf, w_bf)


MLP_OUT_CHUNK = 512


def _mlp_kernel(h_ref, x_ref, m_ref, w1_ref, w2_ref, o_ref, *, n_f):
    f = pl.program_id(2)

    @pl.when(f == 0)
    def _():
        o_ref[...] = jnp.zeros_like(o_ref)

    a = jnp.maximum(jnp.dot(h_ref[...], w1_ref[...], preferred_element_type=F32), 0.0)
    a = (a * a).astype(BF16)
    for n0 in range(0, D_MODEL, MLP_OUT_CHUNK):
        sl = slice(n0, n0 + MLP_OUT_CHUNK)
        o_ref[:, sl] += jnp.dot(a, w2_ref[:, sl], preferred_element_type=F32)

    @pl.when(f == n_f - 1)
    def _():
        o_ref[...] = x_ref[...] + m_ref[5:6, :] * o_ref[...]


def _mlp(h, x, mod, mod_row, w1_bf, w2_bf, layer, *, tm, tf):
    b, t, _ = x.shape
    n_f = D_FF // tf
    row = (lambda bb: bb) if mod_row is None else (lambda bb: mod_row)
    tile = pl.BlockSpec((None, tm, D_MODEL), lambda bb, i, f: (bb, i, 0))
    return pl.pallas_call(
        functools.partial(_mlp_kernel, n_f=n_f),
        grid=(b, t // tm, n_f),
        in_specs=[
            tile, tile,
            pl.BlockSpec((None, N_MOD, D_MODEL), lambda bb, i, f: (row(bb), 0, 0)),
            pl.BlockSpec((None, D_MODEL, tf), lambda bb, i, f: (layer, 0, f)),
            pl.BlockSpec((None, tf, D_MODEL), lambda bb, i, f: (layer, f, 0)),
        ],
        out_specs=tile,
        out_shape=jax.ShapeDtypeStruct(x.shape, F32),
        compiler_params=_cparams(("arbitrary", "arbitrary", "arbitrary"), 56),
        name="mlp",
    )(h, x, mod, w1_bf, w2_bf)


def _tile_const(k, like):
    return jnp.full(like.shape, k, F32).astype(like.dtype)


def _p_add(p, q):
    if p is None:
        return q
    if q is None:
        return p
    (a, sa), (b, sb) = p, q
    if sa == sb:
        return (a + b, sa)
    return (a - b, 1) if sa > 0 else (b - a, 1)


def _p_scale(p, k):
    if p is None or k == 0.0:
        return None
    a, s = p
    if k < 0:
        s, k = -s, -k
    return (a, s) if k == 1.0 else (a * _tile_const(k, a), s)


def _c_add(x, y, sign=1):
    if y is None:
        return x
    if x is None:
        x = (None, None)
    return (_p_add(x[0], _p_scale(y[0], sign)), _p_add(x[1], _p_scale(y[1], sign)))


def _snap(v):
    for t in (0.0, 1.0, -1.0):
        if abs(v - t) < 1e-12:
            return t
    return v


def _c_mulc(x, c):
    if x is None:
        return None
    cr, ci = _snap(c.real), _snap(c.imag)
    re, im = x
    if cr != 0.0 and ci != 0.0 and abs(abs(cr) - abs(ci)) < 1e-12:
        k, sr, si = abs(cr), math.copysign(1.0, cr), math.copysign(1.0, ci)
        return (_p_scale(_p_add(_p_scale(re, sr), _p_scale(im, -si)), k),
                _p_scale(_p_add(_p_scale(re, si), _p_scale(im, sr)), k))
    return (_p_add(_p_scale(re, cr), _p_scale(im, -ci)), _p_add(_p_scale(re, ci), _p_scale(im, cr)))


def _is_zero(x):
    return x is None or (x[0] is None and x[1] is None)


def _slab_fft_dit(xs, sign, need):
    n = len(xs)
    if n == 1:
        return {0: xs[0]}
    if all(_is_zero(x) for x in xs):
        return {k: None for k in need}
    h = n // 2
    sub = {k % h for k in need}
    ev = _slab_fft_dit(xs[0::2], sign, sub)
    od = _slab_fft_dit(xs[1::2], sign, sub)
    out = {}
    for k in sub:
        t = _c_mulc(od[k], cmath.exp(sign * 2j * math.pi * k / n))
        if k in need:
            out[k] = _c_add(ev[k], t)
        if k + h in need:
            out[k + h] = _c_add(ev[k], t, -1)
    return out


def _slab_fft_dif(xs, sign, need):
    n = len(xs)
    if n == 1:
        return {0: xs[0]}
    h = n // 2
    need_e = {k // 2 for k in need if k % 2 == 0}
    need_o = {k // 2 for k in need if k % 2 == 1}
    out = {}
    if need_e:
        ev = _slab_fft_dif([_c_add(xs[j], xs[j + h]) for j in range(h)], sign, need_e)
        out.update({2 * k: v for k, v in ev.items()})
    if need_o:
        dif = [_c_mulc(_c_add(xs[j], xs[j + h], -1), cmath.exp(sign * 2j * math.pi * j / n)) for j in range(h)]
        od = _slab_fft_dif(dif, sign, need_o)
        out.update({2 * k + 1: v for k, v in od.items()})
    return out


def _part_value(p, like):
    if p is None:
        return jnp.zeros_like(like)
    return p[0] if p[1] > 0 else -p[0]


HY_CT = 128
SLAB = 256
N_HI = 2 * SEQ // SLAB
N_IN = SEQ // SLAB
N_SPEC = N_HI // 2 + 1
ROW_CHUNK = 8
HY_ROWS = 16
N_CT = HYENA_WIDTH // HY_CT
SPEC_GROUP = 4
FILTER_HIDDEN = 64
EMB_PAD = 128
HIGHEST = lax.Precision.HIGHEST


def _dft_constants():
    n = 2 * SEQ
    r = np.arange(SLAB)
    tall = np.zeros((N_SPEC, 2 * SLAB, SLAB))
    fwd = np.zeros((N_SPEC, 2 * SLAB, 2 * SLAB))
    inv = np.zeros((N_SPEC, 2 * SLAB, 2 * SLAB))
    for b in range(N_SPEC):
        m = np.exp(-2j * np.pi * (np.outer(r, r) / SLAB + (r * b)[None, :] / n))
        a = m.T / n
        tall[b] = np.concatenate([m.real, m.imag], axis=0)
        fwd[b] = np.block([[m.real, -m.imag], [m.imag, m.real]])
        inv[b] = np.block([[a.real, a.imag], [-a.imag, a.real]])
    return tuple(jnp.asarray(t, F32).astype(BF16) for t in (tall, fwd, inv))


def _ctx_dft_constants():
    n = 2 * CTX_LEN
    k = np.arange(n)
    f = np.exp(-2j * np.pi * np.outer(k, k) / n)
    a = np.conj(f)[:CTX_LEN, :] / n
    return (jnp.asarray(np.concatenate([f.real, f.imag], axis=0), F32).astype(BF16),
            jnp.asarray(np.concatenate([a.real, a.imag], axis=0), F32).astype(BF16))


def _filter_positions(l):
    t = jnp.linspace(0.0, 1.0, l, dtype=F32)[:, None]
    bands = (FILTER_EMB - 1) // 2
    f = jnp.linspace(1e-4, bands - 1, bands, dtype=F32)[None, :]
    w = 2.0 * math.pi * jnp.arange(l, dtype=F32)[:, None] / l
    z = jnp.concatenate([t, jnp.cos(f * w), -jnp.sin(f * w)], axis=-1)
    z2 = jnp.concatenate([z, z[0:1], z[1:][::-1]], axis=0)
    return jnp.pad(z2, ((0, 0), (0, EMB_PAD - FILTER_EMB)))


def _filter_feat_kernel(z_ref, w1_ref, b1_ref, w2_ref, b2_ref, w3_ref, b3_ref, fr_ref, o_ref):
    fr = fr_ref[...]
    dot = functools.partial(jnp.dot, precision=HIGHEST, preferred_element_type=F32)
    h = jnp.sin(fr * (dot(z_ref[...], w1_ref[...]) + b1_ref[...]))
    h = jnp.sin(fr * (dot(h, w2_ref[...]) + b2_ref[...]))
    o_ref[...] = jnp.sin(fr * (dot(h, w3_ref[...]) + b3_ref[...]))


def _filter_features(emb, w1, b1, w2, b2, w3, b3, freq):
    rows = emb.shape[0]
    tr = min(rows, 1024)
    w1p = jnp.pad(w1, ((0, EMB_PAD - FILTER_EMB), (0, 0)))
    vec = lambda v: v.reshape(1, FILTER_HIDDEN)
    full = lambda shape: pl.BlockSpec(shape, lambda i: (0, 0))
    return pl.pallas_call(
        _filter_feat_kernel,
        grid=(rows // tr,),
        in_specs=[pl.BlockSpec((tr, EMB_PAD), lambda i: (i, 0)),
                  full((EMB_PAD, FILTER_HIDDEN)), full((1, FILTER_HIDDEN)),
                  full((FILTER_HIDDEN, FILTER_HIDDEN)), full((1, FILTER_HIDDEN)),
                  full((FILTER_HIDDEN, FILTER_HIDDEN)), full((1, FILTER_HIDDEN)),
                  full((1, FILTER_HIDDEN))],
        out_specs=pl.BlockSpec((tr, FILTER_HIDDEN), lambda i: (i, 0)),
        out_shape=jax.ShapeDtypeStruct((rows, FILTER_HIDDEN), F32),
        compiler_params=_cparams(("arbitrary",), 32),
        name="filter_feat",
    )(emb, w1p, vec(b1), w2, vec(b2), w3, vec(b3), vec(freq))


def _filter_taps(raw, n, l, absdelta):
    m = jnp.where(n < l, n, 2 * l - n)
    t = m.astype(F32) / (l - 1)
    return jnp.where(n == l, 0.0, raw * jnp.exp(-t * absdelta))


def _cplx_from_pq(pq, conj):
    half = pq.shape[0] // 2
    pr, pi_ = pq[:half, :HY_CT], pq[:half, HY_CT:]
    qr, qi = pq[half:, :HY_CT], pq[half:, HY_CT:]
    if conj:
        return pr + qi, pi_ - qr
    return pr - qi, qr + pi_


def _slab_forward(src_ref, n_in, real_only, dst_ref, need, scale):
    def body(rc, carry):
        rows = pl.ds(pl.multiple_of(rc * ROW_CHUNK, ROW_CHUNK), ROW_CHUNK)
        xs = []
        for a in range(N_HI):
            if a >= n_in:
                xs.append(None)
                continue
            re = src_ref[a, rows, 0:HY_CT]
            if scale is not None:
                re = re * scale
            xs.append(((re, 1), None if real_only else (src_ref[a, rows, HY_CT:2 * HY_CT], 1)))
        like = xs[0][0][0]
        out = _slab_fft_dit(xs, -1, need)
        for b in sorted(need):
            c = out[b] or (None, None)
            dst_ref[b, rows, 0:HY_CT] = _part_value(c[0], like)
            dst_ref[b, rows, HY_CT:2 * HY_CT] = _part_value(c[1], like)
        return carry

    lax.fori_loop(0, SLAB // ROW_CHUNK, body, 0)


def _filter_spec_kernel(feat_ref, wf_ref, wb_ref, ad_ref, mf_ref, o_ref, ft_ref, s_ref):
    absdelta = ad_ref[...]

    def taps(a, acc):
        rows = feat_ref[pl.ds(pl.multiple_of(a * SLAB, SLAB), SLAB), :]
        w = jnp.where(a < N_IN, wf_ref[...], wb_ref[...])
        raw = jnp.dot(rows.astype(BF16), w.astype(BF16), preferred_element_type=F32)
        n = a * SLAB + lax.broadcasted_iota(jnp.int32, raw.shape, 0)
        v = _filter_taps(raw, n, SEQ, absdelta)
        ft_ref[a] = v
        return acc + jnp.sum(jnp.abs(v), axis=0, keepdims=True)

    norm = lax.fori_loop(0, N_HI, taps, jnp.zeros((1, HY_CT), F32), unroll=2)
    _slab_forward(ft_ref, N_HI, True, s_ref, set(range(N_SPEC)), 1.0 / norm)

    def spec(b):
        pq = jnp.dot(mf_ref[b], s_ref[b].astype(BF16), preferred_element_type=F32)
        xr, xi = _cplx_from_pq(pq, False)
        rows = pl.ds(pl.multiple_of(b * SLAB, SLAB), SLAB)
        o_ref[rows, 0:HY_CT] = xr.astype(BF16)
        o_ref[rows, HY_CT:2 * HY_CT] = xi.astype(BF16)

    def spec_group(i, carry):
        for t in range(SPEC_GROUP):
            spec(1 + SPEC_GROUP * i + t)
        return carry

    spec(0)
    lax.fori_loop(0, (N_SPEC - 1) // SPEC_GROUP, spec_group, 0)


def _filter_spectrum(feat, w_fout, absdelta, mf):
    cols = 2 * N_CT
    return pl.pallas_call(
        _filter_spec_kernel,
        grid=(HYENA_ORDER, N_CT),
        in_specs=[
            pl.BlockSpec((2 * SEQ, FILTER_HIDDEN), lambda o, c: (0, 0)),
            pl.BlockSpec((FILTER_HIDDEN, HY_CT), lambda o, c: (0, o * cols + c)),
            pl.BlockSpec((FILTER_HIDDEN, HY_CT), lambda o, c: (0, o * cols + N_CT + c)),
            pl.BlockSpec((1, HY_CT), lambda o, c: (0, c)),
            pl.BlockSpec((N_SPEC, 2 * SLAB, SLAB), lambda o, c: (0, 0, 0)),
        ],
        out_specs=pl.BlockSpec((None, None, N_SPEC * SLAB, 2 * HY_CT), lambda o, c: (o, c, 0, 0)),
        out_shape=jax.ShapeDtypeStruct((HYENA_ORDER, N_CT, N_SPEC * SLAB, 2 * HY_CT), BF16),
        scratch_shapes=[pltpu.VMEM((N_HI, SLAB, HY_CT), F32), pltpu.VMEM((N_SPEC, SLAB, 2 * HY_CT), F32)],
        compiler_params=_cparams(("arbitrary", "arbitrary"), 48),
        name="filter_spec",
    )(feat, w_fout, w_fout, absdelta, mf)


def _conv_slab(u_ref, bi, a, w, b):
    r0 = a * SLAB
    x = u_ref[bi, pl.ds(pl.multiple_of(r0, SLAB), SLAB), :].astype(F32)
    before = u_ref[bi, pl.ds(pl.multiple_of(jnp.maximum(r0 - 16, 0), 16), 16), :].astype(F32)
    after = u_ref[bi, pl.ds(pl.multiple_of(jnp.minimum(r0 + SLAB, SEQ - 16), 16), 16), :].astype(F32)
    prev_row = jnp.where(a > 0, before[15:16, :], 0.0)
    next_row = jnp.where(a < N_IN - 1, after[0:1, :], 0.0)
    row = lax.broadcasted_iota(jnp.int32, x.shape, 0)
    xm1 = jnp.where(row == 0, prev_row, pltpu.roll(x, 1, 0))
    xp1 = jnp.where(row == SLAB - 1, next_row, pltpu.roll(x, SLAB - 1, 0))
    return xm1 * w[0:1, :] + x * w[1:2, :] + xp1 * w[2:3, :] + b


def _hyena_kernel(uv_ref, u1_ref, u2_ref, cwv_ref, cw1_ref, cw2_ref, cbv_ref, cb1_ref, cb2_ref, skip_ref,
                  mf_ref, mi_ref, h_ref, o_ref, zc_ref, g_ref, s_ref):
    o = pl.program_id(2)

    @pl.when(o == 0)
    def _():
        def conv(a, carry):
            for bi in range(2):
                lanes = slice(bi * HY_CT, (bi + 1) * HY_CT)
                zc_ref[a, :, lanes] = _conv_slab(uv_ref, bi, a, cwv_ref[...], cbv_ref[...]).astype(BF16)
                g_ref[0, a, :, lanes] = _conv_slab(u1_ref, bi, a, cw1_ref[...], cb1_ref[...]).astype(BF16)
                g_ref[1, a, :, lanes] = _conv_slab(u2_ref, bi, a, cw2_ref[...], cb2_ref[...]).astype(BF16)
            return carry

        lax.fori_loop(0, N_IN, conv, 0)

    def im_sign(b):
        return -1 if b > N_HI // 2 else 1

    def forward_chunk(rc, carry):
        rows = pl.ds(pl.multiple_of(rc * HY_ROWS, HY_ROWS), HY_ROWS)
        rows_im = pl.ds(pl.multiple_of(SLAB + rc * HY_ROWS, HY_ROWS), HY_ROWS)
        xs = [((zc_ref[a, rows, 0:HY_CT], 1), (zc_ref[a, rows, HY_CT:2 * HY_CT], 1)) for a in range(N_IN)]
        like = xs[0][0][0]
        out = _slab_fft_dit(xs + [None] * (N_HI - N_IN), -1, set(range(N_HI)))
        for b in range(N_HI):
            re, im = out[b]
            s_ref[b, rows, :] = _part_value(re, like)
            s_ref[b, rows_im, :] = _part_value(None if im is None else (im[0], im[1] * im_sign(b)), like)
        return carry

    lax.fori_loop(0, SLAB // HY_ROWS, forward_chunk, 0)

    def through_filter(slabs, j):
        h = h_ref[j * SLAB:(j + 1) * SLAB, :].astype(F32)
        hr = jnp.concatenate([h[:, :HY_CT]] * len(slabs), axis=1)
        hi = jnp.concatenate([h[:, HY_CT:]] * len(slabs), axis=1)
        d = s_ref[slabs[0]] if len(slabs) == 1 else jnp.concatenate([s_ref[b] for b in slabs], axis=1)
        x = jnp.dot(mf_ref[j], d, preferred_element_type=F32)
        xr, xi = x[:SLAB], x[SLAB:]
        y = jnp.concatenate([xr * hr - xi * hi, xr * hi + xi * hr], axis=0).astype(BF16)
        w = jnp.dot(mi_ref[j], y, preferred_element_type=F32).astype(BF16)
        return [(b, w[:, k * HY_CT:(k + 1) * HY_CT]) for k, b in enumerate(slabs)]

    for slabs, j in [([0], 0), ([N_HI // 2], N_HI // 2)] + [([j, N_HI - j], j) for j in range(1, N_HI // 2)]:
        for b, w in through_filter(slabs, j):
            s_ref[b] = w

    def inverse_chunk(rc, carry):
        rows = pl.ds(pl.multiple_of(rc * HY_ROWS, HY_ROWS), HY_ROWS)
        rows_im = pl.ds(pl.multiple_of(SLAB + rc * HY_ROWS, HY_ROWS), HY_ROWS)
        xs = [((s_ref[b, rows, :], 1), (s_ref[b, rows_im, :], im_sign(b))) for b in range(N_HI)]
        like = xs[0][0][0]
        out = _slab_fft_dif(xs, 1, set(range(N_IN)))
        for a in range(N_IN):
            s_ref[a, rows, :] = _part_value(out[a][0], like)
            s_ref[a, rows_im, :] = _part_value(out[a][1], like)
        return carry

    lax.fori_loop(0, SLAB // HY_ROWS, inverse_chunk, 0)

    sk = skip_ref[pl.ds(o, 1), :]
    skip2 = jnp.concatenate([sk, sk], axis=1)

    def gate(a, carry):
        y = jnp.concatenate([s_ref[a, 0:SLAB, :], s_ref[a, SLAB:2 * SLAB, :]], axis=1).astype(F32)
        z = g_ref[o, a].astype(F32) * (y + skip2 * zc_ref[a].astype(F32))
        zc_ref[a] = z.astype(BF16)
        return carry

    lax.fori_loop(0, N_IN, gate, 0)

    @pl.when(o == HYENA_ORDER - 1)
    def _():
        def emit(a, carry):
            rows = pl.ds(pl.multiple_of(a * SLAB, SLAB), SLAB)
            for bi in range(2):
                o_ref[bi, rows, :] = zc_ref[a, :, bi * HY_CT:(bi + 1) * HY_CT]
            return carry

        lax.fori_loop(0, N_IN, emit, 0)


def _hyena_conv(p, conv_w, conv_b, skip, mf, mi, hspec):
    b = p.shape[0]
    u0 = V_END // HY_CT
    cb = conv_b.reshape(1, -1)
    part = lambda k: pl.BlockSpec((2, SEQ, HY_CT), lambda c, q, o: (q, 0, u0 + k * N_CT + c))
    cw = lambda k: pl.BlockSpec((SHORT_CONV, HY_CT), lambda c, q, o: (0, k * N_CT + c))
    cbs = lambda k: pl.BlockSpec((1, HY_CT), lambda c, q, o: (0, k * N_CT + c))
    const = lambda: pl.BlockSpec((N_SPEC, 2 * SLAB, 2 * SLAB), lambda c, q, o: (0, 0, 0),
                                 pipeline_mode=pl.Buffered(1))
    return pl.pallas_call(
        _hyena_kernel,
        grid=(N_CT, b // 2, HYENA_ORDER),
        in_specs=[part(0), part(1), part(2), cw(0), cw(1), cw(2), cbs(0), cbs(1), cbs(2),
                  pl.BlockSpec((HYENA_ORDER, HY_CT), lambda c, q, o: (0, c)),
                  const(), const(),
                  pl.BlockSpec((None, None, N_SPEC * SLAB, 2 * HY_CT), lambda c, q, o: (o, c, 0, 0))],
        out_specs=pl.BlockSpec((2, SEQ, HY_CT), lambda c, q, o: (q, 0, c)),
        out_shape=jax.ShapeDtypeStruct((b, SEQ, HYENA_WIDTH), BF16),
        scratch_shapes=[pltpu.VMEM((N_IN, SLAB, 2 * HY_CT), BF16),
                        pltpu.VMEM((HYENA_ORDER, N_IN, SLAB, 2 * HY_CT), BF16),
                        pltpu.VMEM((N_HI, 2 * SLAB, HY_CT), BF16)],
        compiler_params=_cparams(("arbitrary", "arbitrary", "arbitrary"), 56),
        name="hyena_conv",
    )(p, p, p, conv_w, conv_w, conv_w, cb, cb, cb, skip, mf, mi, hspec)


def _ctx_hyena_kernel(uv_ref, u1_ref, u2_ref, cwv_ref, cw1_ref, cw2_ref, cbv_ref, cb1_ref, cb2_ref, skip_ref,
                      feat_ref, wf0_ref, wb0_ref, wf1_ref, wb1_ref, ad_ref, ff_ref, ai_ref, o_ref):
    l = CTX_LEN
    nb = uv_ref.shape[0]
    n = lax.broadcasted_iota(jnp.int32, (2 * l, HY_CT), 0)
    spectra = []
    for wf_ref, wb_ref in ((wf0_ref, wb0_ref), (wf1_ref, wb1_ref)):
        raw = jnp.concatenate(
            [jnp.dot(feat_ref[0:l, :], wf_ref[...], precision=HIGHEST, preferred_element_type=F32),
             jnp.dot(feat_ref[l:2 * l, :], wb_ref[...], precision=HIGHEST, preferred_element_type=F32)], axis=0)
        v = _filter_taps(raw, n, l, ad_ref[...])
        v = v / jnp.sum(jnp.abs(v), axis=0, keepdims=True)
        pq = jnp.dot(ff_ref[...], v.astype(BF16), preferred_element_type=F32)
        spectra.append((pq[:2 * l], pq[2 * l:]))

    row = lax.broadcasted_iota(jnp.int32, (l, HY_CT), 0)

    def conv(u_ref, bi, w_ref, b_ref):
        x = u_ref[bi].astype(F32)
        xm1 = jnp.where(row == 0, 0.0, pltpu.roll(x, 1, 0))
        xp1 = jnp.where(row == l - 1, 0.0, pltpu.roll(x, l - 1, 0))
        return xm1 * w_ref[0:1, :] + x * w_ref[1:2, :] + xp1 * w_ref[2:3, :] + b_ref[...]

    z = [conv(uv_ref, bi, cwv_ref, cbv_ref) for bi in range(nb)]
    gates = [[conv(u1_ref, bi, cw1_ref, cb1_ref) for bi in range(nb)],
             [conv(u2_ref, bi, cw2_ref, cb2_ref) for bi in range(nb)]]
    for o in range(HYENA_ORDER):
        hr, hi = spectra[o]
        pq = jnp.dot(ff_ref[:, 0:l], jnp.concatenate(z, axis=1).astype(BF16), preferred_element_type=F32)
        ys = []
        for q in range(nb // 2):
            xr, xi = _cplx_from_pq(pq[:, 2 * q * HY_CT:2 * (q + 1) * HY_CT], False)
            ys += [xr * hr - xi * hi, xr * hi + xi * hr]
        pq2 = jnp.dot(ai_ref[...], jnp.concatenate(ys, axis=1).astype(BF16), preferred_element_type=F32)
        y = []
        for q in range(nb // 2):
            y += list(_cplx_from_pq(pq2[:, 2 * q * HY_CT:2 * (q + 1) * HY_CT], False))
        z = [gates[o][bi] * (y[bi] + skip_ref[o:o + 1, :] * z[bi]) for bi in range(nb)]
    for bi in range(nb):
        o_ref[bi] = z[bi].astype(BF16)


def _ctx_hyena(pc, conv_w, conv_b, skip, feat, w_fout, absdelta, ff, ai):
    b, l, _ = pc.shape
    u0 = V_END // HY_CT
    cb = conv_b.reshape(1, -1)
    cols = 2 * N_CT
    part = lambda k: pl.BlockSpec((b, l, HY_CT), lambda c: (0, 0, u0 + k * N_CT + c))
    cw = lambda k: pl.BlockSpec((SHORT_CONV, HY_CT), lambda c: (0, k * N_CT + c))
    cbs = lambda k: pl.BlockSpec((1, HY_CT), lambda c: (0, k * N_CT + c))
    wcol = lambda off: pl.BlockSpec((FILTER_HIDDEN, HY_CT), lambda c: (0, off + c))
    return pl.pallas_call(
        _ctx_hyena_kernel,
        grid=(N_CT,),
        in_specs=[part(0), part(1), part(2), cw(0), cw(1), cw(2), cbs(0), cbs(1), cbs(2),
                  pl.BlockSpec((HYENA_ORDER, HY_CT), lambda c: (0, c)),
                  pl.BlockSpec((2 * l, FILTER_HIDDEN), lambda c: (0, 0)),
                  wcol(0), wcol(N_CT), wcol(cols), wcol(cols + N_CT),
                  pl.BlockSpec((1, HY_CT), lambda c: (0, c)),
                  pl.BlockSpec((4 * l, 2 * l), lambda c: (0, 0)),
                  pl.BlockSpec((2 * l, 2 * l), lambda c: (0, 0))],
        out_specs=pl.BlockSpec((b, l, HY_CT), lambda c: (0, 0, c)),
        out_shape=jax.ShapeDtypeStruct((b, l, HYENA_WIDTH), BF16),
        compiler_params=_cparams(("arbitrary",), 32),
        name="ctx_hyena",
    )(pc, pc, pc, conv_w, conv_w, conv_w, cb, cb, cb, skip, feat, w_fout, w_fout, w_fout, w_fout, absdelta, ff, ai)


def _rope_tables(l):
    rows = l // GRID_W
    row = jnp.repeat(jnp.arange(rows), GRID_W).astype(F32)
    col = jnp.tile(jnp.arange(GRID_W), rows).astype(F32)
    n_freq = HEAD_DIM // 4
    inv = ROPE_BASE ** (-jnp.arange(n_freq, dtype=F32) / n_freq)
    ang_r = row[:, None] * inv
    ang_c = col[:, None] * inv
    cos = jnp.concatenate([jnp.cos(ang_r), jnp.cos(ang_r), jnp.cos(ang_c), jnp.cos(ang_c)], axis=-1)
    sin = jnp.concatenate([-jnp.sin(ang_r), jnp.sin(ang_r), -jnp.sin(ang_c), jnp.sin(ang_c)], axis=-1)
    return cos, sin


def kernel(x, c, ctx, c_ctx, w_ada, b_ada, norm_mix, norm_mlp, w_in, q_norm, k_norm, attn_sink, conv_w, conv_b, filt_w1, filt_b1, filt_w2, filt_b2, filt_w3, filt_b3, filt_freq, filt_w_out, hyena_skip, w_out, w_mlp_in, w_mlp_out):
    cos, sin = _rope_tables(SEQ)
    emb, emb_ctx = _filter_positions(SEQ), _filter_positions(CTX_LEN)
    m_tall, mf, mi = _dft_constants()
    ff_ctx, ai_ctx = _ctx_dft_constants()
    absdelta = jnp.abs(jnp.linspace(MIN_DECAY, MAX_DECAY, HYENA_WIDTH, dtype=F32)).reshape(1, HYENA_WIDTH)
    cvec = jnp.zeros((MOD_ROWS, D_MODEL), F32).at[:BATCH].set(c).at[CTX_ROW].set(c_ctx)
    mod_all = _modulation(cvec, w_ada, b_ada).reshape(DEPTH, MOD_ROWS, N_MOD, D_MODEL)
    w_in_bf = w_in.astype(BF16)
    w_out_bf = w_out.astype(BF16)
    w1_bf = w_mlp_in.astype(BF16)
    w2_bf = w_mlp_out.astype(BF16)
    ctx_rows = BATCH * CTX_LEN
    flat = lambda a: a.reshape(1, ctx_rows, a.shape[-1])
    per_batch = lambda a: a.reshape(BATCH, CTX_LEN, a.shape[-1])
    xc = flat(ctx)
    for i in range(DEPTH):
        update_ctx = i < DEPTH - 1
        mod = mod_all[i]
        nmix = norm_mix[i].reshape(1, D_MODEL)
        nmlp = norm_mlp[i].reshape(1, D_MODEL)
        qn = q_norm[i].reshape(1, HEAD_DIM)
        kn = k_norm[i].reshape(1, HEAD_DIM)
        filt_mlp = (filt_w1[i], filt_b1[i], filt_w2[i], filt_b2[i], filt_w3[i], filt_b3[i], filt_freq[i])

        p = _in_projection(x, mod, None, nmix, w_in_bf, i, qn, kn, cos, sin,
                           tm=512, n_off=0, n_tiles=IN_COLS // IN_TN, use_rope=True)
        if update_ctx:
            pc = _in_projection(xc, mod, CTX_ROW, nmix, w_in_bf, i, qn, kn, cos, sin,
                                tm=512, n_off=0, n_tiles=IN_COLS // IN_TN, use_rope=False)
            kx_blk, vx_blk = Q_END // KV_COLS * 2, Q_END // KV_COLS * 2 + 1
        else:
            pc = _in_projection(xc, mod, CTX_ROW, nmix, w_in_bf, i, qn, kn, cos, sin,
                                tm=512, n_off=KV_TILE, n_tiles=1, use_rope=False)
            kx_blk, vx_blk = 0, 1
        pc = per_batch(pc)

        attn = _window_attention(p, pc, kx_blk, vx_blk, attn_sink[i])
        hspec = _filter_spectrum(_filter_features(emb, *filt_mlp), filt_w_out[i], absdelta, m_tall)
        hy = _hyena_conv(p, conv_w[i], conv_b[i], hyena_skip[i], mf, mi, hspec)
        x, h2 = _out_projection(attn, hy, x, mod, None, nmlp, w_out_bf, i, tm=512)
        x = _mlp(h2, x, mod, None, w1_bf, w2_bf, i, tm=512, tf=1024)

        if update_ctx:
            attn_c = _context_attention(pc, attn_sink[i])
            hy_c = _ctx_hyena(pc, conv_w[i], conv_b[i], hyena_skip[i], _filter_features(emb_ctx, *filt_mlp),
                              filt_w_out[i], absdelta, ff_ctx, ai_ctx)
            xc, h2c = _out_projection(flat(attn_c), flat(hy_c), xc, mod, CTX_ROW, nmlp, w_out_bf, i, tm=512)
            xc = _mlp(h2c, xc, mod, CTX_ROW, w1_bf, w2_bf, i, tm=512, tf=1024)
    return x
```

```python
import cmath
import functools
import math

import jax
import jax.numpy as jnp
import numpy as np
from jax import lax
from jax.experimental import pallas as pl
from jax.experimental.pallas import tpu as pltpu

D_MODEL = 2048
BATCH = 4
SEQ = 4096
DEPTH = 2
GRID_W = 64
CTX_LEN = 256
HEAD_DIM = 128
N_Q_HEADS = 8
N_KV_HEADS = 2
GQA_GROUP = N_Q_HEADS // N_KV_HEADS
ATTN_WIDTH = N_Q_HEADS * HEAD_DIM
HYENA_WIDTH = D_MODEL - ATTN_WIDTH
HYENA_ORDER = 2
SHORT_CONV = 3
FILTER_EMB = 33
DECAY_TARGET = 1e-2
FAST_DECAY_PCT = 0.3
SLOW_DECAY_PCT = 1.5
MAX_DECAY = math.log(DECAY_TARGET) / FAST_DECAY_PCT
MIN_DECAY = math.log(DECAY_TARGET) / SLOW_DECAY_PCT
WINDOW = 128
BLOCK = 128
ROPE_BASE = 10000.0
D_FF = 4 * D_MODEL
N_MOD = 6
EPS = 1e-6
NEG_INF = -1e30
Q_END = ATTN_WIDTH
K_END = Q_END + N_KV_HEADS * HEAD_DIM
V_END = K_END + N_KV_HEADS * HEAD_DIM
IN_COLS = V_END + (HYENA_ORDER + 1) * HYENA_WIDTH

MOD_ROWS = 8
CTX_ROW = BATCH
KV_COLS = 2 * N_KV_HEADS * HEAD_DIM

F32 = jnp.float32
BF16 = jnp.bfloat16
MIB = 1024 * 1024


def _cparams(sem, vmem_mib):
    return pltpu.CompilerParams(dimension_semantics=sem, vmem_limit_bytes=vmem_mib * MIB)


def _mod_kernel(c_ref, w_ref, b_ref, o_ref):
    c = c_ref[...]
    s = (c * jax.nn.sigmoid(c)).astype(BF16)
    o_ref[...] = jnp.dot(s, w_ref[...].astype(BF16), preferred_element_type=F32) + b_ref[...]


def _modulation(cvec, w_ada, b_ada):
    tn = 1024
    n_cols = N_MOD * D_MODEL
    return pl.pallas_call(
        _mod_kernel,
        grid=(DEPTH, n_cols // tn),
        in_specs=[
            pl.BlockSpec((MOD_ROWS, D_MODEL), lambda l, j: (0, 0)),
            pl.BlockSpec((None, D_MODEL, tn), lambda l, j: (l, 0, j)),
            pl.BlockSpec((None, 1, tn), lambda l, j: (l, 0, j)),
        ],
        out_specs=pl.BlockSpec((None, MOD_ROWS, tn), lambda l, j: (l, 0, j)),
        out_shape=jax.ShapeDtypeStruct((DEPTH, MOD_ROWS, n_cols), F32),
        compiler_params=_cparams(("arbitrary", "arbitrary"), 40),
        name="adaln_mod",
    )(cvec, w_ada, b_ada.reshape(DEPTH, 1, n_cols))


IN_TN = 512
Q_TILES = Q_END // IN_TN
KV_TILE = Q_TILES
HEADS_PER_TILE = IN_TN // HEAD_DIM


SUB_ROWS = 256


def _rms_modulate(xf, gain, shift):
    ms = jnp.mean(xf * xf, axis=-1, keepdims=True)
    return xf * lax.rsqrt(ms + EPS) * gain + shift


def _head_norm_rope(a, g, cos, sin):
    ms = jnp.mean(a * a, axis=-1, keepdims=True)
    y = a * lax.rsqrt(ms + EPS) * g
    if cos is None:
        return y
    lane = lax.broadcasted_iota(jnp.int32, y.shape, 1)
    partner = jnp.where((lane % 64) < 32, pltpu.roll(y, 96, 1), pltpu.roll(y, 32, 1))
    return y * cos + partner * sin


BF16_TILE_ROWS = 16


def _rider_plan(w, layer, steps, step_of):
    depth, r, c = w.shape
    fold = 1
    while (r * fold) % (steps * BF16_TILE_ROWS):
        fold *= 2
    assert c % (fold * 128) == 0
    rows, cols = r * fold, c // fold
    blk = rows // steps
    in_spec = pl.BlockSpec((blk, cols), lambda *g: (layer * steps + step_of(*g), 0))
    out_spec = pl.BlockSpec((blk, cols), lambda *g: (step_of(*g), 0))
    restore = lambda a: a.reshape(1, r, c)
    return w.reshape(depth * rows, cols), in_spec, out_spec, jax.ShapeDtypeStruct((rows, cols), BF16), restore


def _cast_riders(in_refs, out_refs):
    for src, dst in zip(in_refs, out_refs):
        dst[...] = src[...].astype(BF16)


def _inproj_kernel(*refs, n_off, n_tiles, use_rope, n_riders):
    x_ref, m_ref, nw_ref, w_ref, qn_ref, kn_ref, cos_ref, sin_ref = refs[:8]
    o_ref = refs[8 + n_riders]
    _cast_riders(refs[8:8 + n_riders], refs[9 + n_riders:])
    gain = nw_ref[...] * (1.0 + m_ref[1:2, :])
    shift = m_ref[0:1, :]
    for r0 in range(0, x_ref.shape[0], SUB_ROWS):
        rows = slice(r0, r0 + SUB_ROWS)
        h = _rms_modulate(x_ref[rows, :], gain, shift).astype(BF16)
        cos = cos_ref[rows, :] if use_rope else None
        sin = sin_ref[rows, :] if use_rope else None
        for n in range(n_tiles):
            cols = slice(n * IN_TN, (n + 1) * IN_TN)
            acc = jnp.dot(h, w_ref[:, cols], preferred_element_type=F32)
            t = n + n_off
            if t > KV_TILE:
                o_ref[rows, cols] = acc.astype(BF16)
                continue
            gains = [qn_ref[...]] * HEADS_PER_TILE if t < Q_TILES else [kn_ref[...]] * N_KV_HEADS + [None] * N_KV_HEADS
            for c, g in enumerate(gains):
                sl = slice(c * HEAD_DIM, (c + 1) * HEAD_DIM)
                osl = slice(n * IN_TN + c * HEAD_DIM, n * IN_TN + (c + 1) * HEAD_DIM)
                if g is None:
                    o_ref[rows, osl] = acc[:, sl].astype(BF16)
                else:
                    o_ref[rows, osl] = _head_norm_rope(acc[:, sl], g, cos, sin).astype(BF16)


def _in_projection(x, mod, mod_row, nw, w_bf, qn, kn, cos, sin, *, tm, n_off, n_tiles, use_rope, riders=()):
    b, t, _ = x.shape
    assert n_off % n_tiles == 0 and tm % SUB_ROWS == 0
    row = (lambda bb: bb) if mod_row is None else (lambda bb: mod_row)
    n_i = t // tm
    plans = [_rider_plan(w, layer, b * n_i, lambda bb, i: bb * n_i + i) for w, layer in riders]
    kern = functools.partial(_inproj_kernel, n_off=n_off, n_tiles=n_tiles, use_rope=use_rope, n_riders=len(plans))
    width = n_tiles * IN_TN
    out = pl.pallas_call(
        kern,
        grid=(b, n_i),
        in_specs=[
            pl.BlockSpec((None, tm, D_MODEL), lambda bb, i: (bb, i, 0)),
            pl.BlockSpec((None, N_MOD, D_MODEL), lambda bb, i: (row(bb), 0, 0)),
            pl.BlockSpec((1, D_MODEL), lambda bb, i: (0, 0)),
            pl.BlockSpec((None, D_MODEL, width), lambda bb, i: (0, 0, n_off // n_tiles),
                         pipeline_mode=pl.Buffered(1)),
            pl.BlockSpec((1, HEAD_DIM), lambda bb, i: (0, 0)),
            pl.BlockSpec((1, HEAD_DIM), lambda bb, i: (0, 0)),
            pl.BlockSpec((tm, HEAD_DIM), lambda bb, i: (i, 0)),
            pl.BlockSpec((tm, HEAD_DIM), lambda bb, i: (i, 0)),
        ] + [pl_[1] for pl_ in plans],
        out_specs=[pl.BlockSpec((None, tm, width), lambda bb, i: (bb, i, 0))] + [pl_[2] for pl_ in plans],
        out_shape=[jax.ShapeDtypeStruct((b, t, width), BF16)] + [pl_[3] for pl_ in plans],
        compiler_params=_cparams(("arbitrary", "arbitrary"), 56),
        name="in_proj",
    )(x, mod, nw, w_bf, qn, kn, cos, sin, *[pl_[0] for pl_ in plans])
    return out[0], [pl_[4](a) for pl_, a in zip(plans, out[1:])]


def _sink_column(sink_ref, kv, rows_per_head):
    rows = GQA_GROUP * rows_per_head
    head = lax.broadcasted_iota(jnp.int32, (rows, 1), 0) // rows_per_head
    col = jnp.full((rows, 1), sink_ref[kv * GQA_GROUP], F32)
    for g in range(1, GQA_GROUP):
        col = jnp.where(head == g, sink_ref[kv * GQA_GROUP + g], col)
    return col


def _stack_heads(q_ref, kv, rows=slice(None)):
    return jnp.concatenate(
        [q_ref[rows, (kv * GQA_GROUP + g) * HEAD_DIM:(kv * GQA_GROUP + g + 1) * HEAD_DIM]
         for g in range(GQA_GROUP)], axis=0)


LOG2E = math.log2(math.e)
QK_LOG2_SCALE = HEAD_DIM ** -0.5 * LOG2E


def _softmax_pv(q4, k_all, v_all, bias, sink_col):
    t = lax.dot_general(q4, k_all, (((1,), (1,)), ((), ())), preferred_element_type=F32) * QK_LOG2_SCALE
    if bias is not None:
        t = t + bias
    sink2 = sink_col * LOG2E
    m = jnp.maximum(jnp.max(t, axis=-1, keepdims=True), sink2)
    e = jnp.exp2(t - m)
    denom = jnp.exp2(sink2 - m) + jnp.sum(e, axis=-1, keepdims=True)
    return jnp.dot(e.astype(BF16), v_all, preferred_element_type=F32) / denom


def _window_attn_kernel(sink_ref, bias0_ref, bias1_ref, q_ref, kp_ref, km_ref, kn_ref, vp_ref, vm_ref, vn_ref,
                        kx_ref, vx_ref, o_ref):
    lo, hi = slice(0, BLOCK), slice(BLOCK, 2 * BLOCK)
    for blk, bias_ref in enumerate((bias0_ref, bias1_ref)):
        rows = slice(blk * BLOCK, (blk + 1) * BLOCK)
        for kv in range(N_KV_HEADS):
            ks = slice(kv * HEAD_DIM, (kv + 1) * HEAD_DIM)
            if blk == 0:
                band_k = [kp_ref[:, ks], km_ref[lo, ks], km_ref[hi, ks]]
                band_v = [vp_ref[:, ks], vm_ref[lo, ks], vm_ref[hi, ks]]
            else:
                band_k = [km_ref[lo, ks], km_ref[hi, ks], kn_ref[:, ks]]
                band_v = [vm_ref[lo, ks], vm_ref[hi, ks], vn_ref[:, ks]]
            k_all = jnp.concatenate([kx_ref[:, ks]] + band_k, axis=0)
            v_all = jnp.concatenate([vx_ref[:, ks]] + band_v, axis=0)
            o = _softmax_pv(_stack_heads(q_ref, kv, rows), k_all, v_all, bias_ref[...],
                            _sink_column(sink_ref, kv, BLOCK))
            for g in range(GQA_GROUP):
                h = kv * GQA_GROUP + g
                o_ref[rows, h * HEAD_DIM:(h + 1) * HEAD_DIM] = o[g * BLOCK:(g + 1) * BLOCK].astype(BF16)


def _window_bias(nb):
    qi = np.arange(GQA_GROUP * BLOCK)[:, None] % BLOCK
    sj = np.arange(3 * BLOCK)[None, :] - BLOCK
    near = np.abs(sj - qi) <= WINDOW
    out = np.zeros((3, GQA_GROUP * BLOCK, CTX_LEN + 3 * BLOCK), np.float32)
    for variant, blk in enumerate((0, 1, nb - 1)):
        key_pos = blk * BLOCK + sj
        ok = near & (key_pos >= 0) & (key_pos < nb * BLOCK)
        out[variant, :, CTX_LEN:] = np.where(ok, 0.0, NEG_INF)
    return jnp.asarray(out)


def _window_attention(p, pc, kx_blk, vx_blk, sink):
    b, seq, _ = p.shape
    nb = seq // BLOCK
    kvw = N_KV_HEADS * HEAD_DIM
    k_blk, v_blk = Q_END // kvw, K_END // kvw
    steps = nb // 2
    prev = lambda bb, i: (bb, jnp.maximum(2 * i - 1, 0))
    nxt = lambda bb, i: (bb, jnp.minimum(2 * i + 2, nb - 1))

    def edge(which, blk):
        return pl.BlockSpec((None, BLOCK, kvw), lambda bb, i: which(bb, i) + (blk,))

    def mid(blk):
        return pl.BlockSpec((None, 2 * BLOCK, kvw), lambda bb, i: (bb, i, blk))

    bias_shape = (None, GQA_GROUP * BLOCK, CTX_LEN + 3 * BLOCK)
    bias = _window_bias(nb)
    return pl.pallas_call(
        _window_attn_kernel,
        grid=(b, steps),
        in_specs=[
            pl.BlockSpec(memory_space=pltpu.SMEM),
            pl.BlockSpec(bias_shape, lambda bb, i: (jnp.where(i == 0, 0, 1), 0, 0)),
            pl.BlockSpec(bias_shape, lambda bb, i: (jnp.where(i == steps - 1, 2, 1), 0, 0)),
            pl.BlockSpec((None, 2 * BLOCK, ATTN_WIDTH), lambda bb, i: (bb, i, 0)),
            edge(prev, k_blk), mid(k_blk), edge(nxt, k_blk),
            edge(prev, v_blk), mid(v_blk), edge(nxt, v_blk),
            pl.BlockSpec((None, CTX_LEN, kvw), lambda bb, i: (bb, 0, kx_blk)),
            pl.BlockSpec((None, CTX_LEN, kvw), lambda bb, i: (bb, 0, vx_blk)),
        ],
        out_specs=pl.BlockSpec((None, 2 * BLOCK, ATTN_WIDTH), lambda bb, i: (bb, i, 0)),
        out_shape=jax.ShapeDtypeStruct((b, seq, ATTN_WIDTH), BF16),
        compiler_params=_cparams(("arbitrary", "arbitrary"), 32),
        name="window_attn",
    )(sink, bias, bias, p, p, p, p, p, p, p, pc, pc)


def _ctx_attn_kernel(sink_ref, q_ref, k_ref, v_ref, o_ref):
    n = q_ref.shape[0]
    for kv in range(N_KV_HEADS):
        ks = slice(kv * HEAD_DIM, (kv + 1) * HEAD_DIM)
        o = _softmax_pv(_stack_heads(q_ref, kv), k_ref[:, ks], v_ref[:, ks], None, _sink_column(sink_ref, kv, n))
        for g in range(GQA_GROUP):
            h = kv * GQA_GROUP + g
            o_ref[:, h * HEAD_DIM:(h + 1) * HEAD_DIM] = o[g * n:(g + 1) * n].astype(BF16)


def _context_attention(pc, sink):
    b, n, _ = pc.shape
    kvw = N_KV_HEADS * HEAD_DIM
    return pl.pallas_call(
        _ctx_attn_kernel,
        grid=(b,),
        in_specs=[
            pl.BlockSpec(memory_space=pltpu.SMEM),
            pl.BlockSpec((None, n, ATTN_WIDTH), lambda bb: (bb, 0, 0)),
            pl.BlockSpec((None, n, kvw), lambda bb: (bb, 0, Q_END // kvw)),
            pl.BlockSpec((None, n, kvw), lambda bb: (bb, 0, K_END // kvw)),
        ],
        out_specs=pl.BlockSpec((None, n, ATTN_WIDTH), lambda bb: (bb, 0, 0)),
        out_shape=jax.ShapeDtypeStruct((b, n, ATTN_WIDTH), BF16),
        compiler_params=_cparams(("arbitrary",), 32),
        name="ctx_attn",
    )(sink, pc, pc, pc)


def _outproj_kernel(a_ref, y_ref, x_ref, m_ref, nw_ref, wa_ref, wy_ref, o_ref, h_ref):
    gain = nw_ref[...] * (1.0 + m_ref[4:5, :])
    shift = m_ref[3:4, :]
    for r0 in range(0, x_ref.shape[0], SUB_ROWS):
        rows = slice(r0, r0 + SUB_ROWS)
        mix = (jnp.dot(a_ref[rows, :], wa_ref[...], preferred_element_type=F32)
               + jnp.dot(y_ref[rows, :], wy_ref[...], preferred_element_type=F32))
        x1 = x_ref[rows, :] + m_ref[2:3, :] * mix
        o_ref[rows, :] = x1
        h_ref[rows, :] = _rms_modulate(x1, gain, shift).astype(BF16)


def _out_projection(attn, hy, x, mod, mod_row, nw, w_bf, *, tm):
    b, t, _ = x.shape
    assert tm % SUB_ROWS == 0
    row = (lambda bb: bb) if mod_row is None else (lambda bb: mod_row)
    tile = lambda width: pl.BlockSpec((None, tm, width), lambda bb, i: (bb, i, 0))
    half = lambda k: pl.BlockSpec((None, D_MODEL // 2, D_MODEL), lambda bb, i: (0, k, 0),
                                  pipeline_mode=pl.Buffered(1))
    return pl.pallas_call(
        _outproj_kernel,
        grid=(b, t // tm),
        in_specs=[
            tile(ATTN_WIDTH), tile(HYENA_WIDTH), tile(D_MODEL),
            pl.BlockSpec((None, N_MOD, D_MODEL), lambda bb, i: (row(bb), 0, 0)),
            pl.BlockSpec((1, D_MODEL), lambda bb, i: (0, 0)),
            half(0), half(1),
        ],
        out_specs=[tile(D_MODEL), tile(D_MODEL)],
        out_shape=[jax.ShapeDtypeStruct(x.shape, F32), jax.ShapeDtypeStruct(x.shape, BF16)],
        compiler_params=_cparams(("arbitrary", "arbitrary"), 48),
        name="out_proj",
    )(attn, hy, x, mod, nw, w_bf, w_bf)


MLP_OUT_CHUNK = 512


def _mlp_kernel(*refs, n_f, n_riders):
    h_ref, x_ref, m_ref, w1_ref, w2_ref = refs[:5]
    o_ref = refs[5 + n_riders]
    f = pl.program_id(2)

    @pl.when(f == 0)
    def _():
        o_ref[...] = jnp.zeros_like(o_ref)

    _cast_riders(refs[5:5 + n_riders], refs[6 + n_riders:])
    a = jnp.maximum(jnp.dot(h_ref[...], w1_ref[...], preferred_element_type=F32), 0.0)
    a = (a * a).astype(BF16)
    for n0 in range(0, D_MODEL, MLP_OUT_CHUNK):
        sl = slice(n0, n0 + MLP_OUT_CHUNK)
        o_ref[:, sl] += jnp.dot(a, w2_ref[:, sl], preferred_element_type=F32)

    @pl.when(f == n_f - 1)
    def _():
        o_ref[...] = x_ref[...] + m_ref[5:6, :] * o_ref[...]


def _mlp(h, x, mod, mod_row, w1_bf, w2_bf, *, tm, tf, riders=()):
    b, t, _ = x.shape
    n_f = D_FF // tf
    n_i = t // tm
    row = (lambda bb: bb) if mod_row is None else (lambda bb: mod_row)
    tile = pl.BlockSpec((None, tm, D_MODEL), lambda bb, i, f: (bb, i, 0))
    plans = [_rider_plan(w, layer, b * n_i * n_f, lambda bb, i, f: (bb * n_i + i) * n_f + f) for w, layer in riders]
    out = pl.pallas_call(
        functools.partial(_mlp_kernel, n_f=n_f, n_riders=len(plans)),
        grid=(b, n_i, n_f),
        in_specs=[
            tile, tile,
            pl.BlockSpec((None, N_MOD, D_MODEL), lambda bb, i, f: (row(bb), 0, 0)),
            pl.BlockSpec((None, D_MODEL, tf), lambda bb, i, f: (0, 0, f)),
            pl.BlockSpec((None, tf, D_MODEL), lambda bb, i, f: (0, f, 0)),
        ] + [pl_[1] for pl_ in plans],
        out_specs=[tile] + [pl_[2] for pl_ in plans],
        out_shape=[jax.ShapeDtypeStruct(x.shape, F32)] + [pl_[3] for pl_ in plans],
        compiler_params=_cparams(("arbitrary", "arbitrary", "arbitrary"), 56),
        name="mlp",
    )(h, x, mod, w1_bf, w2_bf, *[pl_[0] for pl_ in plans])
    return out[0], [pl_[4](a) for pl_, a in zip(plans, out[1:])]


def _tile_const(k, like):
    return jnp.full(like.shape, k, F32).astype(like.dtype)


def _p_add(p, q):
    if p is None:
        return q
    if q is None:
        return p
    (a, sa), (b, sb) = p, q
    if sa == sb:
        return (a + b, sa)
    return (a - b, 1) if sa > 0 else (b - a, 1)


def _p_scale(p, k):
    if p is None or k == 0.0:
        return None
    a, s = p
    if k < 0:
        s, k = -s, -k
    return (a, s) if k == 1.0 else (a * _tile_const(k, a), s)


def _c_add(x, y, sign=1):
    if y is None:
        return x
    if x is None:
        x = (None, None)
    return (_p_add(x[0], _p_scale(y[0], sign)), _p_add(x[1], _p_scale(y[1], sign)))


def _snap(v):
    for t in (0.0, 1.0, -1.0):
        if abs(v - t) < 1e-12:
            return t
    return v


def _c_mulc(x, c):
    if x is None:
        return None
    cr, ci = _snap(c.real), _snap(c.imag)
    re, im = x
    if cr != 0.0 and ci != 0.0 and abs(abs(cr) - abs(ci)) < 1e-12:
        k, sr, si = abs(cr), math.copysign(1.0, cr), math.copysign(1.0, ci)
        return (_p_scale(_p_add(_p_scale(re, sr), _p_scale(im, -si)), k),
                _p_scale(_p_add(_p_scale(re, si), _p_scale(im, sr)), k))
    return (_p_add(_p_scale(re, cr), _p_scale(im, -ci)), _p_add(_p_scale(re, ci), _p_scale(im, cr)))


def _is_zero(x):
    return x is None or (x[0] is None and x[1] is None)


def _slab_fft_dit(xs, sign, need):
    n = len(xs)
    if n == 1:
        return {0: xs[0]}
    if all(_is_zero(x) for x in xs):
        return {k: None for k in need}
    h = n // 2
    sub = {k % h for k in need}
    ev = _slab_fft_dit(xs[0::2], sign, sub)
    od = _slab_fft_dit(xs[1::2], sign, sub)
    out = {}
    for k in sub:
        t = _c_mulc(od[k], cmath.exp(sign * 2j * math.pi * k / n))
        if k in need:
            out[k] = _c_add(ev[k], t)
        if k + h in need:
            out[k + h] = _c_add(ev[k], t, -1)
    return out


def _slab_fft_dif(xs, sign, need):
    n = len(xs)
    if n == 1:
        return {0: xs[0]}
    h = n // 2
    need_e = {k // 2 for k in need if k % 2 == 0}
    need_o = {k // 2 for k in need if k % 2 == 1}
    out = {}
    if need_e:
        ev = _slab_fft_dif([_c_add(xs[j], xs[j + h]) for j in range(h)], sign, need_e)
        out.update({2 * k: v for k, v in ev.items()})
    if need_o:
        dif = [_c_mulc(_c_add(xs[j], xs[j + h], -1), cmath.exp(sign * 2j * math.pi * j / n)) for j in range(h)]
        od = _slab_fft_dif(dif, sign, need_o)
        out.update({2 * k + 1: v for k, v in od.items()})
    return out


def _part_value(p, like):
    if p is None:
        return jnp.zeros_like(like)
    return p[0] if p[1] > 0 else -p[0]


HY_CT = 128
SLAB = 256
N_HI = 2 * SEQ // SLAB
N_IN = SEQ // SLAB
N_SPEC = N_HI // 2 + 1
ROW_CHUNK = 8
HY_ROWS = 16
N_CT = HYENA_WIDTH // HY_CT
SPEC_GROUP = 4
FILTER_HIDDEN = 64
EMB_PAD = 128
HIGHEST = lax.Precision.HIGHEST


def _dft_constants():
    n = 2 * SEQ
    r = np.arange(SLAB)
    tall = np.zeros((N_SPEC, 2 * SLAB, SLAB))
    fwd = np.zeros((N_SPEC, 2 * SLAB, 2 * SLAB))
    inv = np.zeros((N_SPEC, 2 * SLAB, 2 * SLAB))
    for b in range(N_SPEC):
        m = np.exp(-2j * np.pi * (np.outer(r, r) / SLAB + (r * b)[None, :] / n))
        a = m.T / n
        tall[b] = np.concatenate([m.real, m.imag], axis=0)
        fwd[b] = np.block([[m.real, -m.imag], [m.imag, m.real]])
        inv[b] = np.block([[a.real, a.imag], [-a.imag, a.real]])
    return tuple(jnp.asarray(t, F32).astype(BF16) for t in (tall, fwd, inv))


def _ctx_dft_constants():
    n = 2 * CTX_LEN
    k = np.arange(n)
    f = np.exp(-2j * np.pi * np.outer(k, k) / n)
    a = np.conj(f)[:CTX_LEN, :] / n
    return (jnp.asarray(np.concatenate([f.real, f.imag], axis=0), F32).astype(BF16),
            jnp.asarray(np.concatenate([a.real, a.imag], axis=0), F32).astype(BF16))


def _filter_positions(l):
    t = jnp.linspace(0.0, 1.0, l, dtype=F32)[:, None]
    bands = (FILTER_EMB - 1) // 2
    f = jnp.linspace(1e-4, bands - 1, bands, dtype=F32)[None, :]
    w = 2.0 * math.pi * jnp.arange(l, dtype=F32)[:, None] / l
    z = jnp.concatenate([t, jnp.cos(f * w), -jnp.sin(f * w)], axis=-1)
    z2 = jnp.concatenate([z, z[0:1], z[1:][::-1]], axis=0)
    return jnp.pad(z2, ((0, 0), (0, EMB_PAD - FILTER_EMB)))


def _filter_feat_kernel(z_ref, w1_ref, b1_ref, w2_ref, b2_ref, w3_ref, b3_ref, fr_ref, o_ref):
    fr = fr_ref[...]
    dot = functools.partial(jnp.dot, precision=HIGHEST, preferred_element_type=F32)
    h = jnp.sin(fr * (dot(z_ref[...], w1_ref[...]) + b1_ref[...]))
    h = jnp.sin(fr * (dot(h, w2_ref[...]) + b2_ref[...]))
    o_ref[...] = jnp.sin(fr * (dot(h, w3_ref[...]) + b3_ref[...]))


def _filter_features(emb, w1, b1, w2, b2, w3, b3, freq):
    rows = emb.shape[0]
    tr = min(rows, 1024)
    w1p = jnp.pad(w1, ((0, EMB_PAD - FILTER_EMB), (0, 0)))
    vec = lambda v: v.reshape(1, FILTER_HIDDEN)
    full = lambda shape: pl.BlockSpec(shape, lambda i: (0, 0))
    return pl.pallas_call(
        _filter_feat_kernel,
        grid=(rows // tr,),
        in_specs=[pl.BlockSpec((tr, EMB_PAD), lambda i: (i, 0)),
                  full((EMB_PAD, FILTER_HIDDEN)), full((1, FILTER_HIDDEN)),
                  full((FILTER_HIDDEN, FILTER_HIDDEN)), full((1, FILTER_HIDDEN)),
                  full((FILTER_HIDDEN, FILTER_HIDDEN)), full((1, FILTER_HIDDEN)),
                  full((1, FILTER_HIDDEN))],
        out_specs=pl.BlockSpec((tr, FILTER_HIDDEN), lambda i: (i, 0)),
        out_shape=jax.ShapeDtypeStruct((rows, FILTER_HIDDEN), F32),
        compiler_params=_cparams(("arbitrary",), 32),
        name="filter_feat",
    )(emb, w1p, vec(b1), w2, vec(b2), w3, vec(b3), vec(freq))


def _filter_taps(raw, n, l, absdelta):
    m = jnp.where(n < l, n, 2 * l - n)
    t = m.astype(F32) / (l - 1)
    return jnp.where(n == l, 0.0, raw * jnp.exp(-t * absdelta))


def _cplx_from_pq(pq, conj):
    half = pq.shape[0] // 2
    pr, pi_ = pq[:half, :HY_CT], pq[:half, HY_CT:]
    qr, qi = pq[half:, :HY_CT], pq[half:, HY_CT:]
    if conj:
        return pr + qi, pi_ - qr
    return pr - qi, qr + pi_


def _slab_forward(src_ref, n_in, real_only, dst_ref, need, scale):
    def body(rc, carry):
        rows = pl.ds(pl.multiple_of(rc * ROW_CHUNK, ROW_CHUNK), ROW_CHUNK)
        xs = []
        for a in range(N_HI):
            if a >= n_in:
                xs.append(None)
                continue
            re = src_ref[a, rows, 0:HY_CT]
            if scale is not None:
                re = re * scale
            xs.append(((re, 1), None if real_only else (src_ref[a, rows, HY_CT:2 * HY_CT], 1)))
        like = xs[0][0][0]
        out = _slab_fft_dit(xs, -1, need)
        for b in sorted(need):
            c = out[b] or (None, None)
            dst_ref[b, rows, 0:HY_CT] = _part_value(c[0], like)
            dst_ref[b, rows, HY_CT:2 * HY_CT] = _part_value(c[1], like)
        return carry

    lax.fori_loop(0, SLAB // ROW_CHUNK, body, 0)


def _filter_spec_kernel(feat_ref, wf_ref, wb_ref, ad_ref, mf_ref, o_ref, ft_ref, s_ref):
    absdelta = ad_ref[...]

    def taps(a, acc):
        rows = feat_ref[pl.ds(pl.multiple_of(a * SLAB, SLAB), SLAB), :]
        w = jnp.where(a < N_IN, wf_ref[...], wb_ref[...])
        raw = jnp.dot(rows.astype(BF16), w.astype(BF16), preferred_element_type=F32)
        n = a * SLAB + lax.broadcasted_iota(jnp.int32, raw.shape, 0)
        v = _filter_taps(raw, n, SEQ, absdelta)
        ft_ref[a] = v
        return acc + jnp.sum(jnp.abs(v), axis=0, keepdims=True)

    norm = lax.fori_loop(0, N_HI, taps, jnp.zeros((1, HY_CT), F32), unroll=2)
    _slab_forward(ft_ref, N_HI, True, s_ref, set(range(N_SPEC)), 1.0 / norm)

    def spec(b):
        pq = jnp.dot(mf_ref[b], s_ref[b].astype(BF16), preferred_element_type=F32)
        xr, xi = _cplx_from_pq(pq, False)
        rows = pl.ds(pl.multiple_of(b * SLAB, SLAB), SLAB)
        o_ref[rows, 0:HY_CT] = xr.astype(BF16)
        o_ref[rows, HY_CT:2 * HY_CT] = xi.astype(BF16)

    def spec_group(i, carry):
        for t in range(SPEC_GROUP):
            spec(1 + SPEC_GROUP * i + t)
        return carry

    spec(0)
    lax.fori_loop(0, (N_SPEC - 1) // SPEC_GROUP, spec_group, 0)


def _filter_spectrum(feat, w_fout, absdelta, mf):
    cols = 2 * N_CT
    return pl.pallas_call(
        _filter_spec_kernel,
        grid=(HYENA_ORDER, N_CT),
        in_specs=[
            pl.BlockSpec((2 * SEQ, FILTER_HIDDEN), lambda o, c: (0, 0)),
            pl.BlockSpec((FILTER_HIDDEN, HY_CT), lambda o, c: (0, o * cols + c)),
            pl.BlockSpec((FILTER_HIDDEN, HY_CT), lambda o, c: (0, o * cols + N_CT + c)),
            pl.BlockSpec((1, HY_CT), lambda o, c: (0, c)),
            pl.BlockSpec((N_SPEC, 2 * SLAB, SLAB), lambda o, c: (0, 0, 0)),
        ],
        out_specs=pl.BlockSpec((None, None, N_SPEC * SLAB, 2 * HY_CT), lambda o, c: (o, c, 0, 0)),
        out_shape=jax.ShapeDtypeStruct((HYENA_ORDER, N_CT, N_SPEC * SLAB, 2 * HY_CT), BF16),
        scratch_shapes=[pltpu.VMEM((N_HI, SLAB, HY_CT), F32), pltpu.VMEM((N_SPEC, SLAB, 2 * HY_CT), F32)],
        compiler_params=_cparams(("arbitrary", "arbitrary"), 48),
        name="filter_spec",
    )(feat, w_fout, w_fout, absdelta, mf)


def _conv_slab(u_ref, bi, a, w, b):
    r0 = a * SLAB
    x = u_ref[bi, pl.ds(pl.multiple_of(r0, SLAB), SLAB), :].astype(F32)
    before = u_ref[bi, pl.ds(pl.multiple_of(jnp.maximum(r0 - 16, 0), 16), 16), :].astype(F32)
    after = u_ref[bi, pl.ds(pl.multiple_of(jnp.minimum(r0 + SLAB, SEQ - 16), 16), 16), :].astype(F32)
    prev_row = jnp.where(a > 0, before[15:16, :], 0.0)
    next_row = jnp.where(a < N_IN - 1, after[0:1, :], 0.0)
    row = lax.broadcasted_iota(jnp.int32, x.shape, 0)
    xm1 = jnp.where(row == 0, prev_row, pltpu.roll(x, 1, 0))
    xp1 = jnp.where(row == SLAB - 1, next_row, pltpu.roll(x, SLAB - 1, 0))
    return xm1 * w[0:1, :] + x * w[1:2, :] + xp1 * w[2:3, :] + b


def _hyena_kernel(uv_ref, u1_ref, u2_ref, cwv_ref, cw1_ref, cw2_ref, cbv_ref, cb1_ref, cb2_ref, skip_ref,
                  mf_ref, mi_ref, h_ref, o_ref, zc_ref, g_ref, s_ref):
    o = pl.program_id(2)

    @pl.when(o == 0)
    def _():
        def conv(a, carry):
            for bi in range(2):
                lanes = slice(bi * HY_CT, (bi + 1) * HY_CT)
                zc_ref[a, :, lanes] = _conv_slab(uv_ref, bi, a, cwv_ref[...], cbv_ref[...]).astype(BF16)
                g_ref[0, a, :, lanes] = _conv_slab(u1_ref, bi, a, cw1_ref[...], cb1_ref[...]).astype(BF16)
                g_ref[1, a, :, lanes] = _conv_slab(u2_ref, bi, a, cw2_ref[...], cb2_ref[...]).astype(BF16)
            return carry

        lax.fori_loop(0, N_IN, conv, 0)

    def im_sign(b):
        return -1 if b > N_HI // 2 else 1

    def forward_chunk(rc, carry):
        rows = pl.ds(pl.multiple_of(rc * HY_ROWS, HY_ROWS), HY_ROWS)
        rows_im = pl.ds(pl.multiple_of(SLAB + rc * HY_ROWS, HY_ROWS), HY_ROWS)
        xs = [((zc_ref[a, rows, 0:HY_CT], 1), (zc_ref[a, rows, HY_CT:2 * HY_CT], 1)) for a in range(N_IN)]
        like = xs[0][0][0]
        out = _slab_fft_dit(xs + [None] * (N_HI - N_IN), -1, set(range(N_HI)))
        for b in range(N_HI):
            re, im = out[b]
            s_ref[b, rows, :] = _part_value(re, like)
            s_ref[b, rows_im, :] = _part_value(None if im is None else (im[0], im[1] * im_sign(b)), like)
        return carry

    lax.fori_loop(0, SLAB // HY_ROWS, forward_chunk, 0)

    def through_filter(slabs, j):
        h = h_ref[j * SLAB:(j + 1) * SLAB, :].astype(F32)
        hr = jnp.concatenate([h[:, :HY_CT]] * len(slabs), axis=1)
        hi = jnp.concatenate([h[:, HY_CT:]] * len(slabs), axis=1)
        d = s_ref[slabs[0]] if len(slabs) == 1 else jnp.concatenate([s_ref[b] for b in slabs], axis=1)
        x = jnp.dot(mf_ref[j], d, preferred_element_type=F32)
        xr, xi = x[:SLAB], x[SLAB:]
        y = jnp.concatenate([xr * hr - xi * hi, xr * hi + xi * hr], axis=0).astype(BF16)
        w = jnp.dot(mi_ref[j], y, preferred_element_type=F32).astype(BF16)
        return [(b, w[:, k * HY_CT:(k + 1) * HY_CT]) for k, b in enumerate(slabs)]

    for slabs, j in [([0], 0), ([N_HI // 2], N_HI // 2)] + [([j, N_HI - j], j) for j in range(1, N_HI // 2)]:
        for b, w in through_filter(slabs, j):
            s_ref[b] = w

    def inverse_chunk(rc, carry):
        rows = pl.ds(pl.multiple_of(rc * HY_ROWS, HY_ROWS), HY_ROWS)
        rows_im = pl.ds(pl.multiple_of(SLAB + rc * HY_ROWS, HY_ROWS), HY_ROWS)
        xs = [((s_ref[b, rows, :], 1), (s_ref[b, rows_im, :], im_sign(b))) for b in range(N_HI)]
        like = xs[0][0][0]
        out = _slab_fft_dif(xs, 1, set(range(N_IN)))
        for a in range(N_IN):
            s_ref[a, rows, :] = _part_value(out[a][0], like)
            s_ref[a, rows_im, :] = _part_value(out[a][1], like)
        return carry

    lax.fori_loop(0, SLAB // HY_ROWS, inverse_chunk, 0)

    sk = skip_ref[pl.ds(o, 1), :]
    skip2 = jnp.concatenate([sk, sk], axis=1)

    def gate(a, carry):
        y = jnp.concatenate([s_ref[a, 0:SLAB, :], s_ref[a, SLAB:2 * SLAB, :]], axis=1).astype(F32)
        z = g_ref[o, a].astype(F32) * (y + skip2 * zc_ref[a].astype(F32))
        zc_ref[a] = z.astype(BF16)
        return carry

    lax.fori_loop(0, N_IN, gate, 0)

    @pl.when(o == HYENA_ORDER - 1)
    def _():
        def emit(a, carry):
            rows = pl.ds(pl.multiple_of(a * SLAB, SLAB), SLAB)
            for bi in range(2):
                o_ref[bi, rows, :] = zc_ref[a, :, bi * HY_CT:(bi + 1) * HY_CT]
            return carry

        lax.fori_loop(0, N_IN, emit, 0)


def _hyena_conv(p, conv_w, conv_b, skip, mf, mi, hspec):
    b = p.shape[0]
    u0 = V_END // HY_CT
    cb = conv_b.reshape(1, -1)
    part = lambda k: pl.BlockSpec((2, SEQ, HY_CT), lambda c, q, o: (q, 0, u0 + k * N_CT + c))
    cw = lambda k: pl.BlockSpec((SHORT_CONV, HY_CT), lambda c, q, o: (0, k * N_CT + c))
    cbs = lambda k: pl.BlockSpec((1, HY_CT), lambda c, q, o: (0, k * N_CT + c))
    const = lambda: pl.BlockSpec((N_SPEC, 2 * SLAB, 2 * SLAB), lambda c, q, o: (0, 0, 0),
                                 pipeline_mode=pl.Buffered(1))
    return pl.pallas_call(
        _hyena_kernel,
        grid=(N_CT, b // 2, HYENA_ORDER),
        in_specs=[part(0), part(1), part(2), cw(0), cw(1), cw(2), cbs(0), cbs(1), cbs(2),
                  pl.BlockSpec((HYENA_ORDER, HY_CT), lambda c, q, o: (0, c)),
                  const(), const(),
                  pl.BlockSpec((None, None, N_SPEC * SLAB, 2 * HY_CT), lambda c, q, o: (o, c, 0, 0))],
        out_specs=pl.BlockSpec((2, SEQ, HY_CT), lambda c, q, o: (q, 0, c)),
        out_shape=jax.ShapeDtypeStruct((b, SEQ, HYENA_WIDTH), BF16),
        scratch_shapes=[pltpu.VMEM((N_IN, SLAB, 2 * HY_CT), BF16),
                        pltpu.VMEM((HYENA_ORDER, N_IN, SLAB, 2 * HY_CT), BF16),
                        pltpu.VMEM((N_HI, 2 * SLAB, HY_CT), BF16)],
        compiler_params=_cparams(("arbitrary", "arbitrary", "arbitrary"), 56),
        name="hyena_conv",
    )(p, p, p, conv_w, conv_w, conv_w, cb, cb, cb, skip, mf, mi, hspec)


def _ctx_hyena_kernel(uv_ref, u1_ref, u2_ref, cwv_ref, cw1_ref, cw2_ref, cbv_ref, cb1_ref, cb2_ref, skip_ref,
                      feat_ref, wf0_ref, wb0_ref, wf1_ref, wb1_ref, ad_ref, ff_ref, ai_ref, o_ref):
    l = CTX_LEN
    nb = uv_ref.shape[0]
    n = lax.broadcasted_iota(jnp.int32, (2 * l, HY_CT), 0)
    spectra = []
    for wf_ref, wb_ref in ((wf0_ref, wb0_ref), (wf1_ref, wb1_ref)):
        raw = jnp.concatenate(
            [jnp.dot(feat_ref[0:l, :], wf_ref[...], precision=HIGHEST, preferred_element_type=F32),
             jnp.dot(feat_ref[l:2 * l, :], wb_ref[...], precision=HIGHEST, preferred_element_type=F32)], axis=0)
        v = _filter_taps(raw, n, l, ad_ref[...])
        v = v / jnp.sum(jnp.abs(v), axis=0, keepdims=True)
        pq = jnp.dot(ff_ref[...], v.astype(BF16), preferred_element_type=F32)
        spectra.append((pq[:2 * l], pq[2 * l:]))

    row = lax.broadcasted_iota(jnp.int32, (l, HY_CT), 0)

    def conv(u_ref, bi, w_ref, b_ref):
        x = u_ref[bi].astype(F32)
        xm1 = jnp.where(row == 0, 0.0, pltpu.roll(x, 1, 0))
        xp1 = jnp.where(row == l - 1, 0.0, pltpu.roll(x, l - 1, 0))
        return xm1 * w_ref[0:1, :] + x * w_ref[1:2, :] + xp1 * w_ref[2:3, :] + b_ref[...]

    z = [conv(uv_ref, bi, cwv_ref, cbv_ref) for bi in range(nb)]
    gates = [[conv(u1_ref, bi, cw1_ref, cb1_ref) for bi in range(nb)],
             [conv(u2_ref, bi, cw2_ref, cb2_ref) for bi in range(nb)]]
    for o in range(HYENA_ORDER):
        hr, hi = spectra[o]
        pq = jnp.dot(ff_ref[:, 0:l], jnp.concatenate(z, axis=1).astype(BF16), preferred_element_type=F32)
        ys = []
        for q in range(nb // 2):
            xr, xi = _cplx_from_pq(pq[:, 2 * q * HY_CT:2 * (q + 1) * HY_CT], False)
            ys += [xr * hr - xi * hi, xr * hi + xi * hr]
        pq2 = jnp.dot(ai_ref[...], jnp.concatenate(ys, axis=1).astype(BF16), preferred_element_type=F32)
        y = []
        for q in range(nb // 2):
            y += list(_cplx_from_pq(pq2[:, 2 * q * HY_CT:2 * (q + 1) * HY_CT], False))
        z = [gates[o][bi] * (y[bi] + skip_ref[o:o + 1, :] * z[bi]) for bi in range(nb)]
    for bi in range(nb):
        o_ref[bi] = z[bi].astype(BF16)


def _ctx_hyena(pc, conv_w, conv_b, skip, feat, w_fout, absdelta, ff, ai):
    b, l, _ = pc.shape
    u0 = V_END // HY_CT
    cb = conv_b.reshape(1, -1)
    cols = 2 * N_CT
    part = lambda k: pl.BlockSpec((b, l, HY_CT), lambda c: (0, 0, u0 + k * N_CT + c))
    cw = lambda k: pl.BlockSpec((SHORT_CONV, HY_CT), lambda c: (0, k * N_CT + c))
    cbs = lambda k: pl.BlockSpec((1, HY_CT), lambda c: (0, k * N_CT + c))
    wcol = lambda off: pl.BlockSpec((FILTER_HIDDEN, HY_CT), lambda c: (0, off + c))
    return pl.pallas_call(
        _ctx_hyena_kernel,
        grid=(N_CT,),
        in_specs=[part(0), part(1), part(2), cw(0), cw(1), cw(2), cbs(0), cbs(1), cbs(2),
                  pl.BlockSpec((HYENA_ORDER, HY_CT), lambda c: (0, c)),
                  pl.BlockSpec((2 * l, FILTER_HIDDEN), lambda c: (0, 0)),
                  wcol(0), wcol(N_CT), wcol(cols), wcol(cols + N_CT),
                  pl.BlockSpec((1, HY_CT), lambda c: (0, c)),
                  pl.BlockSpec((4 * l, 2 * l), lambda c: (0, 0)),
                  pl.BlockSpec((2 * l, 2 * l), lambda c: (0, 0))],
        out_specs=pl.BlockSpec((b, l, HY_CT), lambda c: (0, 0, c)),
        out_shape=jax.ShapeDtypeStruct((b, l, HYENA_WIDTH), BF16),
        compiler_params=_cparams(("arbitrary",), 32),
        name="ctx_hyena",
    )(pc, pc, pc, conv_w, conv_w, conv_w, cb, cb, cb, skip, feat, w_fout, w_fout, w_fout, w_fout, absdelta, ff, ai)


def _rope_tables(l):
    rows = l // GRID_W
    row = jnp.repeat(jnp.arange(rows), GRID_W).astype(F32)
    col = jnp.tile(jnp.arange(GRID_W), rows).astype(F32)
    n_freq = HEAD_DIM // 4
    inv = ROPE_BASE ** (-jnp.arange(n_freq, dtype=F32) / n_freq)
    ang_r = row[:, None] * inv
    ang_c = col[:, None] * inv
    cos = jnp.concatenate([jnp.cos(ang_r), jnp.cos(ang_r), jnp.cos(ang_c), jnp.cos(ang_c)], axis=-1)
    sin = jnp.concatenate([-jnp.sin(ang_r), jnp.sin(ang_r), -jnp.sin(ang_c), jnp.sin(ang_c)], axis=-1)
    return cos, sin


def kernel(x, c, ctx, c_ctx, w_ada, b_ada, norm_mix, norm_mlp, w_in, q_norm, k_norm, attn_sink, conv_w, conv_b, filt_w1, filt_b1, filt_w2, filt_b2, filt_w3, filt_b3, filt_freq, filt_w_out, hyena_skip, w_out, w_mlp_in, w_mlp_out):
    cos, sin = _rope_tables(SEQ)
    emb, emb_ctx = _filter_positions(SEQ), _filter_positions(CTX_LEN)
    m_tall, mf, mi = _dft_constants()
    ff_ctx, ai_ctx = _ctx_dft_constants()
    absdelta = jnp.abs(jnp.linspace(MIN_DECAY, MAX_DECAY, HYENA_WIDTH, dtype=F32)).reshape(1, HYENA_WIDTH)
    cvec = jnp.zeros((MOD_ROWS, D_MODEL), F32).at[:BATCH].set(c).at[CTX_ROW].set(c_ctx)
    mod_all = _modulation(cvec, w_ada, b_ada).reshape(DEPTH, MOD_ROWS, N_MOD, D_MODEL)
    w_in_bf = w_in[0:1].astype(BF16)
    ctx_rows = BATCH * CTX_LEN
    flat = lambda a: a.reshape(1, ctx_rows, a.shape[-1])
    per_batch = lambda a: a.reshape(BATCH, CTX_LEN, a.shape[-1])
    xc = flat(ctx)
    for i in range(DEPTH):
        update_ctx = i < DEPTH - 1
        mod = mod_all[i]
        nmix = norm_mix[i].reshape(1, D_MODEL)
        nmlp = norm_mlp[i].reshape(1, D_MODEL)
        qn = q_norm[i].reshape(1, HEAD_DIM)
        kn = k_norm[i].reshape(1, HEAD_DIM)
        filt_mlp = (filt_w1[i], filt_b1[i], filt_w2[i], filt_b2[i], filt_w3[i], filt_b3[i], filt_freq[i])

        p, (w_out_bf, w1_bf, w2_bf) = _in_projection(
            x, mod, None, nmix, w_in_bf, qn, kn, cos, sin, tm=512, n_off=0, n_tiles=IN_COLS // IN_TN,
            use_rope=True, riders=[(w_out, i), (w_mlp_in, i), (w_mlp_out, i)])
        if update_ctx:
            pc, _ = _in_projection(xc, mod, CTX_ROW, nmix, w_in_bf, qn, kn, cos, sin,
                                   tm=512, n_off=0, n_tiles=IN_COLS // IN_TN, use_rope=False)
            kx_blk, vx_blk = Q_END // KV_COLS * 2, Q_END // KV_COLS * 2 + 1
        else:
            pc, _ = _in_projection(xc, mod, CTX_ROW, nmix, w_in_bf, qn, kn, cos, sin,
                                   tm=512, n_off=KV_TILE, n_tiles=1, use_rope=False)
            kx_blk, vx_blk = 0, 1
        pc = per_batch(pc)

        attn = _window_attention(p, pc, kx_blk, vx_blk, attn_sink[i])
        hspec = _filter_spectrum(_filter_features(emb, *filt_mlp), filt_w_out[i], absdelta, m_tall)
        hy = _hyena_conv(p, conv_w[i], conv_b[i], hyena_skip[i], mf, mi, hspec)
        x, h2 = _out_projection(attn, hy, x, mod, None, nmlp, w_out_bf, tm=512)
        x, next_w_in = _mlp(h2, x, mod, None, w1_bf, w2_bf, tm=512, tf=1024,
                            riders=[(w_in, i + 1)] if update_ctx else [])

        if update_ctx:
            attn_c = _context_attention(pc, attn_sink[i])
            hy_c = _ctx_hyena(pc, conv_w[i], conv_b[i], hyena_skip[i], _filter_features(emb_ctx, *filt_mlp),
                              filt_w_out[i], absdelta, ff_ctx, ai_ctx)
            xc, h2c = _out_projection(flat(attn_c), flat(hy_c), xc, mod, CTX_ROW, nmlp, w_out_bf, tm=512)
            xc, _ = _mlp(h2c, xc, mod, CTX_ROW, w1_bf, w2_bf, tm=512, tf=1024)
            w_in_bf = next_w_in[0]
    return x
```

```python
import cmath
import functools
import math

import jax
import jax.numpy as jnp
import numpy as np
from jax import lax
from jax.experimental import pallas as pl
from jax.experimental.pallas import tpu as pltpu

D_MODEL = 2048
BATCH = 4
SEQ = 4096
DEPTH = 2
GRID_W = 64
CTX_LEN = 256
HEAD_DIM = 128
N_Q_HEADS = 8
N_KV_HEADS = 2
GQA_GROUP = N_Q_HEADS // N_KV_HEADS
ATTN_WIDTH = N_Q_HEADS * HEAD_DIM
HYENA_WIDTH = D_MODEL - ATTN_WIDTH
HYENA_ORDER = 2
SHORT_CONV = 3
FILTER_EMB = 33
DECAY_TARGET = 1e-2
FAST_DECAY_PCT = 0.3
SLOW_DECAY_PCT = 1.5
MAX_DECAY = math.log(DECAY_TARGET) / FAST_DECAY_PCT
MIN_DECAY = math.log(DECAY_TARGET) / SLOW_DECAY_PCT
WINDOW = 128
BLOCK = 128
ROPE_BASE = 10000.0
D_FF = 4 * D_MODEL
N_MOD = 6
EPS = 1e-6
NEG_INF = -1e30
Q_END = ATTN_WIDTH
K_END = Q_END + N_KV_HEADS * HEAD_DIM
V_END = K_END + N_KV_HEADS * HEAD_DIM
IN_COLS = V_END + (HYENA_ORDER + 1) * HYENA_WIDTH

MOD_ROWS = 8
CTX_ROW = BATCH
KV_COLS = 2 * N_KV_HEADS * HEAD_DIM

F32 = jnp.float32
BF16 = jnp.bfloat16
MIB = 1024 * 1024


def _cparams(sem, vmem_mib):
    return pltpu.CompilerParams(dimension_semantics=sem, vmem_limit_bytes=vmem_mib * MIB)


def _mod_kernel(c_ref, w_ref, b_ref, o_ref):
    c = c_ref[...]
    s = (c * jax.nn.sigmoid(c)).astype(BF16)
    o_ref[...] = jnp.dot(s, w_ref[...].astype(BF16), preferred_element_type=F32) + b_ref[...]


def _modulation(cvec, w_ada, b_ada):
    tn = 1024
    n_cols = N_MOD * D_MODEL
    return pl.pallas_call(
        _mod_kernel,
        grid=(DEPTH, n_cols // tn),
        in_specs=[
            pl.BlockSpec((MOD_ROWS, D_MODEL), lambda l, j: (0, 0)),
            pl.BlockSpec((None, D_MODEL, tn), lambda l, j: (l, 0, j)),
            pl.BlockSpec((None, 1, tn), lambda l, j: (l, 0, j)),
        ],
        out_specs=pl.BlockSpec((None, MOD_ROWS, tn), lambda l, j: (l, 0, j)),
        out_shape=jax.ShapeDtypeStruct((DEPTH, MOD_ROWS, n_cols), F32),
        compiler_params=_cparams(("arbitrary", "arbitrary"), 40),
        name="adaln_mod",
    )(cvec, w_ada, b_ada.reshape(DEPTH, 1, n_cols))


IN_TN = 512
Q_TILES = Q_END // IN_TN
KV_TILE = Q_TILES
HEADS_PER_TILE = IN_TN // HEAD_DIM


SUB_ROWS = 256


def _rms_modulate(xf, gain, shift):
    ms = jnp.mean(xf * xf, axis=-1, keepdims=True)
    return xf * lax.rsqrt(ms + EPS) * gain + shift


def _head_norm_rope(a, g, cos, sin):
    ms = jnp.mean(a * a, axis=-1, keepdims=True)
    y = a * lax.rsqrt(ms + EPS) * g
    if cos is None:
        return y
    lane = lax.broadcasted_iota(jnp.int32, y.shape, 1)
    partner = jnp.where((lane % 64) < 32, pltpu.roll(y, 96, 1), pltpu.roll(y, 32, 1))
    return y * cos + partner * sin


BF16_TILE_ROWS = 16


def _rider_plan(w, layer, steps, step_of):
    depth, r, c = w.shape
    assert r % (steps * BF16_TILE_ROWS) == 0
    blk = r // steps
    in_spec = pl.BlockSpec((blk, c), lambda *g: (layer * steps + step_of(*g), 0))
    out_spec = pl.BlockSpec((blk, c), lambda *g: (step_of(*g), 0))
    restore = lambda a: a.reshape(1, r, c)
    return w.reshape(depth * r, c), in_spec, out_spec, jax.ShapeDtypeStruct((r, c), BF16), restore


def _cast_riders(in_refs, out_refs):
    for src, dst in zip(in_refs, out_refs):
        dst[...] = src[...].astype(BF16)


def _inproj_kernel(*refs, n_off, n_tiles, use_rope, n_riders):
    x_ref, m_ref, nw_ref, w_ref, qn_ref, kn_ref, cos_ref, sin_ref = refs[:8]
    o_ref = refs[8 + n_riders]
    _cast_riders(refs[8:8 + n_riders], refs[9 + n_riders:])
    gain = nw_ref[...] * (1.0 + m_ref[1:2, :])
    shift = m_ref[0:1, :]
    for r0 in range(0, x_ref.shape[0], SUB_ROWS):
        rows = slice(r0, r0 + SUB_ROWS)
        h = _rms_modulate(x_ref[rows, :], gain, shift).astype(BF16)
        cos = cos_ref[rows, :] if use_rope else None
        sin = sin_ref[rows, :] if use_rope else None
        for n in range(n_tiles):
            cols = slice(n * IN_TN, (n + 1) * IN_TN)
            acc = jnp.dot(h, w_ref[:, cols], preferred_element_type=F32)
            t = n + n_off
            if t > KV_TILE:
                o_ref[rows, cols] = acc.astype(BF16)
                continue
            gains = [qn_ref[...]] * HEADS_PER_TILE if t < Q_TILES else [kn_ref[...]] * N_KV_HEADS + [None] * N_KV_HEADS
            for c, g in enumerate(gains):
                sl = slice(c * HEAD_DIM, (c + 1) * HEAD_DIM)
                osl = slice(n * IN_TN + c * HEAD_DIM, n * IN_TN + (c + 1) * HEAD_DIM)
                if g is None:
                    o_ref[rows, osl] = acc[:, sl].astype(BF16)
                else:
                    o_ref[rows, osl] = _head_norm_rope(acc[:, sl], g, cos, sin).astype(BF16)


def _in_projection(x, mod, mod_row, nw, w_bf, qn, kn, cos, sin, *, tm, n_off, n_tiles, use_rope, riders=()):
    b, t, _ = x.shape
    assert n_off % n_tiles == 0 and tm % SUB_ROWS == 0
    row = (lambda bb: bb) if mod_row is None else (lambda bb: mod_row)
    n_i = t // tm
    plans = [_rider_plan(w, layer, b * n_i, lambda bb, i: bb * n_i + i) for w, layer in riders]
    kern = functools.partial(_inproj_kernel, n_off=n_off, n_tiles=n_tiles, use_rope=use_rope, n_riders=len(plans))
    width = n_tiles * IN_TN
    out = pl.pallas_call(
        kern,
        grid=(b, n_i),
        in_specs=[
            pl.BlockSpec((None, tm, D_MODEL), lambda bb, i: (bb, i, 0)),
            pl.BlockSpec((None, N_MOD, D_MODEL), lambda bb, i: (row(bb), 0, 0)),
            pl.BlockSpec((1, D_MODEL), lambda bb, i: (0, 0)),
            pl.BlockSpec((None, D_MODEL, width), lambda bb, i: (0, 0, n_off // n_tiles),
                         pipeline_mode=pl.Buffered(1)),
            pl.BlockSpec((1, HEAD_DIM), lambda bb, i: (0, 0)),
            pl.BlockSpec((1, HEAD_DIM), lambda bb, i: (0, 0)),
            pl.BlockSpec((tm, HEAD_DIM), lambda bb, i: (i, 0)),
            pl.BlockSpec((tm, HEAD_DIM), lambda bb, i: (i, 0)),
        ] + [pl_[1] for pl_ in plans],
        out_specs=[pl.BlockSpec((None, tm, width), lambda bb, i: (bb, i, 0))] + [pl_[2] for pl_ in plans],
        out_shape=[jax.ShapeDtypeStruct((b, t, width), BF16)] + [pl_[3] for pl_ in plans],
        compiler_params=_cparams(("arbitrary", "arbitrary"), 56),
        name="in_proj",
    )(x, mod, nw, w_bf, qn, kn, cos, sin, *[pl_[0] for pl_ in plans])
    return out[0], [pl_[4](a) for pl_, a in zip(plans, out[1:])]


def _sink_column(sink_ref, kv, rows_per_head):
    rows = GQA_GROUP * rows_per_head
    head = lax.broadcasted_iota(jnp.int32, (rows, 1), 0) // rows_per_head
    col = jnp.full((rows, 1), sink_ref[kv * GQA_GROUP], F32)
    for g in range(1, GQA_GROUP):
        col = jnp.where(head == g, sink_ref[kv * GQA_GROUP + g], col)
    return col


def _stack_heads(q_ref, kv, rows=slice(None)):
    return jnp.concatenate(
        [q_ref[rows, (kv * GQA_GROUP + g) * HEAD_DIM:(kv * GQA_GROUP + g + 1) * HEAD_DIM]
         for g in range(GQA_GROUP)], axis=0)


LOG2E = math.log2(math.e)
QK_LOG2_SCALE = HEAD_DIM ** -0.5 * LOG2E


def _softmax_pv(q4, k_all, v_all, bias, sink_col):
    t = lax.dot_general(q4, k_all, (((1,), (1,)), ((), ())), preferred_element_type=F32) * QK_LOG2_SCALE
    if bias is not None:
        t = t + bias
    sink2 = sink_col * LOG2E
    m = jnp.maximum(jnp.max(t, axis=-1, keepdims=True), sink2)
    e = jnp.exp2(t - m)
    denom = jnp.exp2(sink2 - m) + jnp.sum(e, axis=-1, keepdims=True)
    return jnp.dot(e.astype(BF16), v_all, preferred_element_type=F32) / denom


def _window_attn_kernel(sink_ref, bias0_ref, bias1_ref, q_ref, kp_ref, km_ref, kn_ref, vp_ref, vm_ref, vn_ref,
                        kx_ref, vx_ref, o_ref):
    lo, hi = slice(0, BLOCK), slice(BLOCK, 2 * BLOCK)
    for blk, bias_ref in enumerate((bias0_ref, bias1_ref)):
        rows = slice(blk * BLOCK, (blk + 1) * BLOCK)
        for kv in range(N_KV_HEADS):
            ks = slice(kv * HEAD_DIM, (kv + 1) * HEAD_DIM)
            if blk == 0:
                band_k = [kp_ref[:, ks], km_ref[lo, ks], km_ref[hi, ks]]
                band_v = [vp_ref[:, ks], vm_ref[lo, ks], vm_ref[hi, ks]]
            else:
                band_k = [km_ref[lo, ks], km_ref[hi, ks], kn_ref[:, ks]]
                band_v = [vm_ref[lo, ks], vm_ref[hi, ks], vn_ref[:, ks]]
            k_all = jnp.concatenate([kx_ref[:, ks]] + band_k, axis=0)
            v_all = jnp.concatenate([vx_ref[:, ks]] + band_v, axis=0)
            o = _softmax_pv(_stack_heads(q_ref, kv, rows), k_all, v_all, bias_ref[...],
                            _sink_column(sink_ref, kv, BLOCK))
            for g in range(GQA_GROUP):
                h = kv * GQA_GROUP + g
                o_ref[rows, h * HEAD_DIM:(h + 1) * HEAD_DIM] = o[g * BLOCK:(g + 1) * BLOCK].astype(BF16)


def _window_bias(nb):
    qi = np.arange(GQA_GROUP * BLOCK)[:, None] % BLOCK
    sj = np.arange(3 * BLOCK)[None, :] - BLOCK
    near = np.abs(sj - qi) <= WINDOW
    out = np.zeros((3, GQA_GROUP * BLOCK, CTX_LEN + 3 * BLOCK), np.float32)
    for variant, blk in enumerate((0, 1, nb - 1)):
        key_pos = blk * BLOCK + sj
        ok = near & (key_pos >= 0) & (key_pos < nb * BLOCK)
        out[variant, :, CTX_LEN:] = np.where(ok, 0.0, NEG_INF)
    return jnp.asarray(out)


def _window_attention(p, pc, kx_blk, vx_blk, sink):
    b, seq, _ = p.shape
    nb = seq // BLOCK
    kvw = N_KV_HEADS * HEAD_DIM
    k_blk, v_blk = Q_END // kvw, K_END // kvw
    steps = nb // 2
    prev = lambda bb, i: (bb, jnp.maximum(2 * i - 1, 0))
    nxt = lambda bb, i: (bb, jnp.minimum(2 * i + 2, nb - 1))

    def edge(which, blk):
        return pl.BlockSpec((None, BLOCK, kvw), lambda bb, i: which(bb, i) + (blk,))

    def mid(blk):
        return pl.BlockSpec((None, 2 * BLOCK, kvw), lambda bb, i: (bb, i, blk))

    bias_shape = (None, GQA_GROUP * BLOCK, CTX_LEN + 3 * BLOCK)
    bias = _window_bias(nb)
    return pl.pallas_call(
        _window_attn_kernel,
        grid=(b, steps),
        in_specs=[
            pl.BlockSpec(memory_space=pltpu.SMEM),
            pl.BlockSpec(bias_shape, lambda bb, i: (jnp.where(i == 0, 0, 1), 0, 0)),
            pl.BlockSpec(bias_shape, lambda bb, i: (jnp.where(i == steps - 1, 2, 1), 0, 0)),
            pl.BlockSpec((None, 2 * BLOCK, ATTN_WIDTH), lambda bb, i: (bb, i, 0)),
            edge(prev, k_blk), mid(k_blk), edge(nxt, k_blk),
            edge(prev, v_blk), mid(v_blk), edge(nxt, v_blk),
            pl.BlockSpec((None, CTX_LEN, kvw), lambda bb, i: (bb, 0, kx_blk)),
            pl.BlockSpec((None, CTX_LEN, kvw), lambda bb, i: (bb, 0, vx_blk)),
        ],
        out_specs=pl.BlockSpec((None, 2 * BLOCK, ATTN_WIDTH), lambda bb, i: (bb, i, 0)),
        out_shape=jax.ShapeDtypeStruct((b, seq, ATTN_WIDTH), BF16),
        compiler_params=_cparams(("arbitrary", "arbitrary"), 32),
        name="window_attn",
    )(sink, bias, bias, p, p, p, p, p, p, p, pc, pc)


def _ctx_attn_kernel(sink_ref, q_ref, k_ref, v_ref, o_ref):
    n = q_ref.shape[0]
    for kv in range(N_KV_HEADS):
        ks = slice(kv * HEAD_DIM, (kv + 1) * HEAD_DIM)
        o = _softmax_pv(_stack_heads(q_ref, kv), k_ref[:, ks], v_ref[:, ks], None, _sink_column(sink_ref, kv, n))
        for g in range(GQA_GROUP):
            h = kv * GQA_GROUP + g
            o_ref[:, h * HEAD_DIM:(h + 1) * HEAD_DIM] = o[g * n:(g + 1) * n].astype(BF16)


def _context_attention(pc, sink):
    b, n, _ = pc.shape
    kvw = N_KV_HEADS * HEAD_DIM
    return pl.pallas_call(
        _ctx_attn_kernel,
        grid=(b,),
        in_specs=[
            pl.BlockSpec(memory_space=pltpu.SMEM),
            pl.BlockSpec((None, n, ATTN_WIDTH), lambda bb: (bb, 0, 0)),
            pl.BlockSpec((None, n, kvw), lambda bb: (bb, 0, Q_END // kvw)),
            pl.BlockSpec((None, n, kvw), lambda bb: (bb, 0, K_END // kvw)),
        ],
        out_specs=pl.BlockSpec((None, n, ATTN_WIDTH), lambda bb: (bb, 0, 0)),
        out_shape=jax.ShapeDtypeStruct((b, n, ATTN_WIDTH), BF16),
        compiler_params=_cparams(("arbitrary",), 32),
        name="ctx_attn",
    )(sink, pc, pc, pc)


def _outproj_kernel(*refs, n_riders):
    a_ref, y_ref, x_ref, m_ref, nw_ref, wa_ref, wy_ref = refs[:7]
    o_ref, h_ref = refs[7 + n_riders], refs[8 + n_riders]
    _cast_riders(refs[7:7 + n_riders], refs[9 + n_riders:])
    gain = nw_ref[...] * (1.0 + m_ref[4:5, :])
    shift = m_ref[3:4, :]
    for r0 in range(0, x_ref.shape[0], SUB_ROWS):
        rows = slice(r0, r0 + SUB_ROWS)
        mix = (jnp.dot(a_ref[rows, :], wa_ref[...], preferred_element_type=F32)
               + jnp.dot(y_ref[rows, :], wy_ref[...], preferred_element_type=F32))
        x1 = x_ref[rows, :] + m_ref[2:3, :] * mix
        o_ref[rows, :] = x1
        h_ref[rows, :] = _rms_modulate(x1, gain, shift).astype(BF16)


def _out_projection(attn, hy, x, mod, mod_row, nw, w_bf, *, tm, riders=()):
    b, t, _ = x.shape
    assert tm % SUB_ROWS == 0
    n_i = t // tm
    row = (lambda bb: bb) if mod_row is None else (lambda bb: mod_row)
    tile = lambda width: pl.BlockSpec((None, tm, width), lambda bb, i: (bb, i, 0))
    half = lambda k: pl.BlockSpec((None, D_MODEL // 2, D_MODEL), lambda bb, i: (0, k, 0),
                                  pipeline_mode=pl.Buffered(1))
    plans = [_rider_plan(w, layer, b * n_i, lambda bb, i: bb * n_i + i) for w, layer in riders]
    out = pl.pallas_call(
        functools.partial(_outproj_kernel, n_riders=len(plans)),
        grid=(b, n_i),
        in_specs=[
            tile(ATTN_WIDTH), tile(HYENA_WIDTH), tile(D_MODEL),
            pl.BlockSpec((None, N_MOD, D_MODEL), lambda bb, i: (row(bb), 0, 0)),
            pl.BlockSpec((1, D_MODEL), lambda bb, i: (0, 0)),
            half(0), half(1),
        ] + [pl_[1] for pl_ in plans],
        out_specs=[tile(D_MODEL), tile(D_MODEL)] + [pl_[2] for pl_ in plans],
        out_shape=[jax.ShapeDtypeStruct(x.shape, F32), jax.ShapeDtypeStruct(x.shape, BF16)]
        + [pl_[3] for pl_ in plans],
        compiler_params=_cparams(("arbitrary", "arbitrary"), 48),
        name="out_proj",
    )(attn, hy, x, mod, nw, w_bf, w_bf, *[pl_[0] for pl_ in plans])
    return out[0], out[1], [pl_[4](a) for pl_, a in zip(plans, out[2:])]


MLP_OUT_CHUNK = 512


def _mlp_kernel(h_ref, x_ref, m_ref, w1_ref, w2_ref, o_ref, *, n_f):
    f = pl.program_id(2)

    @pl.when(f == 0)
    def _():
        o_ref[...] = jnp.zeros_like(o_ref)

    a = jnp.maximum(jnp.dot(h_ref[...], w1_ref[...], preferred_element_type=F32), 0.0)
    a = (a * a).astype(BF16)
    for n0 in range(0, D_MODEL, MLP_OUT_CHUNK):
        sl = slice(n0, n0 + MLP_OUT_CHUNK)
        o_ref[:, sl] += jnp.dot(a, w2_ref[:, sl], preferred_element_type=F32)

    @pl.when(f == n_f - 1)
    def _():
        o_ref[...] = x_ref[...] + m_ref[5:6, :] * o_ref[...]


def _mlp(h, x, mod, mod_row, w1_bf, w2_bf, *, tm, tf):
    b, t, _ = x.shape
    n_f = D_FF // tf
    row = (lambda bb: bb) if mod_row is None else (lambda bb: mod_row)
    tile = pl.BlockSpec((None, tm, D_MODEL), lambda bb, i, f: (bb, i, 0))
    return pl.pallas_call(
        functools.partial(_mlp_kernel, n_f=n_f),
        grid=(b, t // tm, n_f),
        in_specs=[
            tile, tile,
            pl.BlockSpec((None, N_MOD, D_MODEL), lambda bb, i, f: (row(bb), 0, 0)),
            pl.BlockSpec((None, D_MODEL, tf), lambda bb, i, f: (0, 0, f)),
            pl.BlockSpec((None, tf, D_MODEL), lambda bb, i, f: (0, f, 0)),
        ],
        out_specs=tile,
        out_shape=jax.ShapeDtypeStruct(x.shape, F32),
        compiler_params=_cparams(("arbitrary", "arbitrary", "arbitrary"), 56),
        name="mlp",
    )(h, x, mod, w1_bf, w2_bf)


def _tile_const(k, like):
    return jnp.full(like.shape, k, F32).astype(like.dtype)


def _p_add(p, q):
    if p is None:
        return q
    if q is None:
        return p
    (a, sa), (b, sb) = p, q
    if sa == sb:
        return (a + b, sa)
    return (a - b, 1) if sa > 0 else (b - a, 1)


def _p_scale(p, k):
    if p is None or k == 0.0:
        return None
    a, s = p
    if k < 0:
        s, k = -s, -k
    return (a, s) if k == 1.0 else (a * _tile_const(k, a), s)


def _c_add(x, y, sign=1):
    if y is None:
        return x
    if x is None:
        x = (None, None)
    return (_p_add(x[0], _p_scale(y[0], sign)), _p_add(x[1], _p_scale(y[1], sign)))


def _snap(v):
    for t in (0.0, 1.0, -1.0):
        if abs(v - t) < 1e-12:
            return t
    return v


def _c_mulc(x, c):
    if x is None:
        return None
    cr, ci = _snap(c.real), _snap(c.imag)
    re, im = x
    if cr != 0.0 and ci != 0.0 and abs(abs(cr) - abs(ci)) < 1e-12:
        k, sr, si = abs(cr), math.copysign(1.0, cr), math.copysign(1.0, ci)
        return (_p_scale(_p_add(_p_scale(re, sr), _p_scale(im, -si)), k),
                _p_scale(_p_add(_p_scale(re, si), _p_scale(im, sr)), k))
    return (_p_add(_p_scale(re, cr), _p_scale(im, -ci)), _p_add(_p_scale(re, ci), _p_scale(im, cr)))


def _is_zero(x):
    return x is None or (x[0] is None and x[1] is None)


def _slab_fft_dit(xs, sign, need):
    n = len(xs)
    if n == 1:
        return {0: xs[0]}
    if all(_is_zero(x) for x in xs):
        return {k: None for k in need}
    h = n // 2
    sub = {k % h for k in need}
    ev = _slab_fft_dit(xs[0::2], sign, sub)
    od = _slab_fft_dit(xs[1::2], sign, sub)
    out = {}
    for k in sub:
        t = _c_mulc(od[k], cmath.exp(sign * 2j * math.pi * k / n))
        if k in need:
            out[k] = _c_add(ev[k], t)
        if k + h in need:
            out[k + h] = _c_add(ev[k], t, -1)
    return out


def _slab_fft_dif(xs, sign, need):
    n = len(xs)
    if n == 1:
        return {0: xs[0]}
    h = n // 2
    need_e = {k // 2 for k in need if k % 2 == 0}
    need_o = {k // 2 for k in need if k % 2 == 1}
    out = {}
    if need_e:
        ev = _slab_fft_dif([_c_add(xs[j], xs[j + h]) for j in range(h)], sign, need_e)
        out.update({2 * k: v for k, v in ev.items()})
    if need_o:
        dif = [_c_mulc(_c_add(xs[j], xs[j + h], -1), cmath.exp(sign * 2j * math.pi * j / n)) for j in range(h)]
        od = _slab_fft_dif(dif, sign, need_o)
        out.update({2 * k + 1: v for k, v in od.items()})
    return out


def _part_value(p, like):
    if p is None:
        return jnp.zeros_like(like)
    return p[0] if p[1] > 0 else -p[0]


HY_CT = 128
SLAB = 256
N_HI = 2 * SEQ // SLAB
N_IN = SEQ // SLAB
N_SPEC = N_HI // 2 + 1
ROW_CHUNK = 8
HY_ROWS = 16
N_CT = HYENA_WIDTH // HY_CT
SPEC_GROUP = 4
FILTER_HIDDEN = 64
EMB_PAD = 128
HIGHEST = lax.Precision.HIGHEST


def _dft_constants():
    n = 2 * SEQ
    r = np.arange(SLAB)
    tall = np.zeros((N_SPEC, 2 * SLAB, SLAB))
    fwd = np.zeros((N_SPEC, 2 * SLAB, 2 * SLAB))
    inv = np.zeros((N_SPEC, 2 * SLAB, 2 * SLAB))
    for b in range(N_SPEC):
        m = np.exp(-2j * np.pi * (np.outer(r, r) / SLAB + (r * b)[None, :] / n))
        a = m.T / n
        tall[b] = np.concatenate([m.real, m.imag], axis=0)
        fwd[b] = np.block([[m.real, -m.imag], [m.imag, m.real]])
        inv[b] = np.block([[a.real, a.imag], [-a.imag, a.real]])
    return tuple(jnp.asarray(t, F32).astype(BF16) for t in (tall, fwd, inv))


def _ctx_dft_constants():
    n = 2 * CTX_LEN
    k = np.arange(n)
    f = np.exp(-2j * np.pi * np.outer(k, k) / n)
    a = np.conj(f)[:CTX_LEN, :] / n
    return (jnp.asarray(np.concatenate([f.real, f.imag], axis=0), F32).astype(BF16),
            jnp.asarray(np.concatenate([a.real, a.imag], axis=0), F32).astype(BF16))


def _filter_positions(l):
    t = np.linspace(0.0, 1.0, l)[:, None]
    bands = (FILTER_EMB - 1) // 2
    f = np.linspace(1e-4, bands - 1, bands)[None, :]
    w = 2.0 * math.pi * np.arange(l)[:, None] / l
    z = np.concatenate([t, np.cos(f * w), -np.sin(f * w)], axis=-1)
    z2 = np.concatenate([z, z[0:1], z[1:][::-1]], axis=0)
    return jnp.asarray(np.pad(z2, ((0, 0), (0, EMB_PAD - FILTER_EMB))), F32)


def _filter_feat_kernel(z_ref, w1_ref, b1_ref, w2_ref, b2_ref, w3_ref, b3_ref, fr_ref, o_ref):
    fr = fr_ref[...]
    dot = functools.partial(jnp.dot, precision=HIGHEST, preferred_element_type=F32)
    h = jnp.sin(fr * (dot(z_ref[...], w1_ref[...]) + b1_ref[...]))
    h = jnp.sin(fr * (dot(h, w2_ref[...]) + b2_ref[...]))
    o_ref[...] = jnp.sin(fr * (dot(h, w3_ref[...]) + b3_ref[...]))


def _filter_features(emb, w1, b1, w2, b2, w3, b3, freq):
    rows = emb.shape[0]
    tr = min(rows, 1024)
    w1p = jnp.pad(w1, ((0, EMB_PAD - FILTER_EMB), (0, 0)))
    vec = lambda v: v.reshape(1, FILTER_HIDDEN)
    full = lambda shape: pl.BlockSpec(shape, lambda i: (0, 0))
    return pl.pallas_call(
        _filter_feat_kernel,
        grid=(rows // tr,),
        in_specs=[pl.BlockSpec((tr, EMB_PAD), lambda i: (i, 0)),
                  full((EMB_PAD, FILTER_HIDDEN)), full((1, FILTER_HIDDEN)),
                  full((FILTER_HIDDEN, FILTER_HIDDEN)), full((1, FILTER_HIDDEN)),
                  full((FILTER_HIDDEN, FILTER_HIDDEN)), full((1, FILTER_HIDDEN)),
                  full((1, FILTER_HIDDEN))],
        out_specs=pl.BlockSpec((tr, FILTER_HIDDEN), lambda i: (i, 0)),
        out_shape=jax.ShapeDtypeStruct((rows, FILTER_HIDDEN), F32),
        compiler_params=_cparams(("arbitrary",), 32),
        name="filter_feat",
    )(emb, w1p, vec(b1), w2, vec(b2), w3, vec(b3), vec(freq))


def _filter_taps(raw, n, l, absdelta):
    m = jnp.where(n < l, n, 2 * l - n)
    t = m.astype(F32) / (l - 1)
    return jnp.where(n == l, 0.0, raw * jnp.exp(-t * absdelta))


def _cplx_from_pq(pq, conj):
    half = pq.shape[0] // 2
    pr, pi_ = pq[:half, :HY_CT], pq[:half, HY_CT:]
    qr, qi = pq[half:, :HY_CT], pq[half:, HY_CT:]
    if conj:
        return pr + qi, pi_ - qr
    return pr - qi, qr + pi_


def _slab_forward(src_ref, n_in, real_only, dst_ref, need, scale):
    def body(rc, carry):
        rows = pl.ds(pl.multiple_of(rc * ROW_CHUNK, ROW_CHUNK), ROW_CHUNK)
        xs = []
        for a in range(N_HI):
            if a >= n_in:
                xs.append(None)
                continue
            re = src_ref[a, rows, 0:HY_CT]
            if scale is not None:
                re = re * scale
            xs.append(((re, 1), None if real_only else (src_ref[a, rows, HY_CT:2 * HY_CT], 1)))
        like = xs[0][0][0]
        out = _slab_fft_dit(xs, -1, need)
        for b in sorted(need):
            c = out[b] or (None, None)
            dst_ref[b, rows, 0:HY_CT] = _part_value(c[0], like)
            dst_ref[b, rows, HY_CT:2 * HY_CT] = _part_value(c[1], like)
        return carry

    lax.fori_loop(0, SLAB // ROW_CHUNK, body, 0)


def _filter_spec_kernel(feat_ref, wf_ref, wb_ref, ad_ref, mf_ref, o_ref, ft_ref, s_ref):
    absdelta = ad_ref[...]

    def taps(a, acc):
        rows = feat_ref[pl.ds(pl.multiple_of(a * SLAB, SLAB), SLAB), :]
        w = jnp.where(a < N_IN, wf_ref[...], wb_ref[...])
        raw = jnp.dot(rows.astype(BF16), w.astype(BF16), preferred_element_type=F32)
        n = a * SLAB + lax.broadcasted_iota(jnp.int32, raw.shape, 0)
        v = _filter_taps(raw, n, SEQ, absdelta)
        ft_ref[a] = v
        return acc + jnp.sum(jnp.abs(v), axis=0, keepdims=True)

    norm = lax.fori_loop(0, N_HI, taps, jnp.zeros((1, HY_CT), F32), unroll=2)
    _slab_forward(ft_ref, N_HI, True, s_ref, set(range(N_SPEC)), 1.0 / norm)

    def spec(b):
        pq = jnp.dot(mf_ref[b], s_ref[b].astype(BF16), preferred_element_type=F32)
        xr, xi = _cplx_from_pq(pq, False)
        rows = pl.ds(pl.multiple_of(b * SLAB, SLAB), SLAB)
        o_ref[rows, 0:HY_CT] = xr.astype(BF16)
        o_ref[rows, HY_CT:2 * HY_CT] = xi.astype(BF16)

    def spec_group(i, carry):
        for t in range(SPEC_GROUP):
            spec(1 + SPEC_GROUP * i + t)
        return carry

    spec(0)
    lax.fori_loop(0, (N_SPEC - 1) // SPEC_GROUP, spec_group, 0)


def _filter_spectrum(feat, w_fout, absdelta, mf):
    cols = 2 * N_CT
    return pl.pallas_call(
        _filter_spec_kernel,
        grid=(HYENA_ORDER, N_CT),
        in_specs=[
            pl.BlockSpec((2 * SEQ, FILTER_HIDDEN), lambda o, c: (0, 0)),
            pl.BlockSpec((FILTER_HIDDEN, HY_CT), lambda o, c: (0, o * cols + c)),
            pl.BlockSpec((FILTER_HIDDEN, HY_CT), lambda o, c: (0, o * cols + N_CT + c)),
            pl.BlockSpec((1, HY_CT), lambda o, c: (0, c)),
            pl.BlockSpec((N_SPEC, 2 * SLAB, SLAB), lambda o, c: (0, 0, 0)),
        ],
        out_specs=pl.BlockSpec((None, None, N_SPEC * SLAB, 2 * HY_CT), lambda o, c: (o, c, 0, 0)),
        out_shape=jax.ShapeDtypeStruct((HYENA_ORDER, N_CT, N_SPEC * SLAB, 2 * HY_CT), BF16),
        scratch_shapes=[pltpu.VMEM((N_HI, SLAB, HY_CT), F32), pltpu.VMEM((N_SPEC, SLAB, 2 * HY_CT), F32)],
        compiler_params=_cparams(("arbitrary", "arbitrary"), 48),
        name="filter_spec",
    )(feat, w_fout, w_fout, absdelta, mf)


def _conv_slab(u_ref, bi, a, w, b):
    r0 = a * SLAB
    x = u_ref[bi, pl.ds(pl.multiple_of(r0, SLAB), SLAB), :].astype(F32)
    before = u_ref[bi, pl.ds(pl.multiple_of(jnp.maximum(r0 - 16, 0), 16), 16), :].astype(F32)
    after = u_ref[bi, pl.ds(pl.multiple_of(jnp.minimum(r0 + SLAB, SEQ - 16), 16), 16), :].astype(F32)
    prev_row = jnp.where(a > 0, before[15:16, :], 0.0)
    next_row = jnp.where(a < N_IN - 1, after[0:1, :], 0.0)
    row = lax.broadcasted_iota(jnp.int32, x.shape, 0)
    xm1 = jnp.where(row == 0, prev_row, pltpu.roll(x, 1, 0))
    xp1 = jnp.where(row == SLAB - 1, next_row, pltpu.roll(x, SLAB - 1, 0))
    return xm1 * w[0:1, :] + x * w[1:2, :] + xp1 * w[2:3, :] + b


def _hyena_kernel(uv_ref, u1_ref, u2_ref, cwv_ref, cw1_ref, cw2_ref, cbv_ref, cb1_ref, cb2_ref, skip_ref,
                  mf_ref, mi_ref, h_ref, o_ref, zc_ref, g_ref, s_ref):
    o = pl.program_id(2)

    @pl.when(o == 0)
    def _():
        def conv(a, carry):
            for bi in range(2):
                lanes = slice(bi * HY_CT, (bi + 1) * HY_CT)
                zc_ref[a, :, lanes] = _conv_slab(uv_ref, bi, a, cwv_ref[...], cbv_ref[...]).astype(BF16)
                g_ref[0, a, :, lanes] = _conv_slab(u1_ref, bi, a, cw1_ref[...], cb1_ref[...]).astype(BF16)
                g_ref[1, a, :, lanes] = _conv_slab(u2_ref, bi, a, cw2_ref[...], cb2_ref[...]).astype(BF16)
            return carry

        lax.fori_loop(0, N_IN, conv, 0)

    def im_sign(b):
        return -1 if b > N_HI // 2 else 1

    def forward_chunk(rc, carry):
        rows = pl.ds(pl.multiple_of(rc * HY_ROWS, HY_ROWS), HY_ROWS)
        rows_im = pl.ds(pl.multiple_of(SLAB + rc * HY_ROWS, HY_ROWS), HY_ROWS)
        xs = [((zc_ref[a, rows, 0:HY_CT], 1), (zc_ref[a, rows, HY_CT:2 * HY_CT], 1)) for a in range(N_IN)]
        like = xs[0][0][0]
        out = _slab_fft_dit(xs + [None] * (N_HI - N_IN), -1, set(range(N_HI)))
        for b in range(N_HI):
            re, im = out[b]
            s_ref[b, rows, :] = _part_value(re, like)
            s_ref[b, rows_im, :] = _part_value(None if im is None else (im[0], im[1] * im_sign(b)), like)
        return carry

    lax.fori_loop(0, SLAB // HY_ROWS, forward_chunk, 0)

    def through_filter(slabs, j):
        h = h_ref[j * SLAB:(j + 1) * SLAB, :].astype(F32)
        hr = jnp.concatenate([h[:, :HY_CT]] * len(slabs), axis=1)
        hi = jnp.concatenate([h[:, HY_CT:]] * len(slabs), axis=1)
        d = s_ref[slabs[0]] if len(slabs) == 1 else jnp.concatenate([s_ref[b] for b in slabs], axis=1)
        x = jnp.dot(mf_ref[j], d, preferred_element_type=F32)
        xr, xi = x[:SLAB], x[SLAB:]
        y = jnp.concatenate([xr * hr - xi * hi, xr * hi + xi * hr], axis=0).astype(BF16)
        w = jnp.dot(mi_ref[j], y, preferred_element_type=F32).astype(BF16)
        return [(b, w[:, k * HY_CT:(k + 1) * HY_CT]) for k, b in enumerate(slabs)]

    for slabs, j in [([0], 0), ([N_HI // 2], N_HI // 2)] + [([j, N_HI - j], j) for j in range(1, N_HI // 2)]:
        for b, w in through_filter(slabs, j):
            s_ref[b] = w

    def inverse_chunk(rc, carry):
        rows = pl.ds(pl.multiple_of(rc * HY_ROWS, HY_ROWS), HY_ROWS)
        rows_im = pl.ds(pl.multiple_of(SLAB + rc * HY_ROWS, HY_ROWS), HY_ROWS)
        xs = [((s_ref[b, rows, :], 1), (s_ref[b, rows_im, :], im_sign(b))) for b in range(N_HI)]
        like = xs[0][0][0]
        out = _slab_fft_dif(xs, 1, set(range(N_IN)))
        for a in range(N_IN):
            s_ref[a, rows, :] = _part_value(out[a][0], like)
            s_ref[a, rows_im, :] = _part_value(out[a][1], like)
        return carry

    lax.fori_loop(0, SLAB // HY_ROWS, inverse_chunk, 0)

    sk = skip_ref[pl.ds(o, 1), :]
    skip2 = jnp.concatenate([sk, sk], axis=1)

    def gate(a, carry):
        y = jnp.concatenate([s_ref[a, 0:SLAB, :], s_ref[a, SLAB:2 * SLAB, :]], axis=1).astype(F32)
        z = g_ref[o, a].astype(F32) * (y + skip2 * zc_ref[a].astype(F32))
        zc_ref[a] = z.astype(BF16)
        return carry

    lax.fori_loop(0, N_IN, gate, 0)

    @pl.when(o == HYENA_ORDER - 1)
    def _():
        def emit(a, carry):
            rows = pl.ds(pl.multiple_of(a * SLAB, SLAB), SLAB)
            for bi in range(2):
                o_ref[bi, rows, :] = zc_ref[a, :, bi * HY_CT:(bi + 1) * HY_CT]
            return carry

        lax.fori_loop(0, N_IN, emit, 0)


def _hyena_conv(p, conv_w, conv_b, skip, mf, mi, hspec):
    b = p.shape[0]
    u0 = V_END // HY_CT
    cb = conv_b.reshape(1, -1)
    part = lambda k: pl.BlockSpec((2, SEQ, HY_CT), lambda c, q, o: (q, 0, u0 + k * N_CT + c))
    cw = lambda k: pl.BlockSpec((SHORT_CONV, HY_CT), lambda c, q, o: (0, k * N_CT + c))
    cbs = lambda k: pl.BlockSpec((1, HY_CT), lambda c, q, o: (0, k * N_CT + c))
    const = lambda: pl.BlockSpec((N_SPEC, 2 * SLAB, 2 * SLAB), lambda c, q, o: (0, 0, 0),
                                 pipeline_mode=pl.Buffered(1))
    return pl.pallas_call(
        _hyena_kernel,
        grid=(N_CT, b // 2, HYENA_ORDER),
        in_specs=[part(0), part(1), part(2), cw(0), cw(1), cw(2), cbs(0), cbs(1), cbs(2),
                  pl.BlockSpec((HYENA_ORDER, HY_CT), lambda c, q, o: (0, c)),
                  const(), const(),
                  pl.BlockSpec((None, None, N_SPEC * SLAB, 2 * HY_CT), lambda c, q, o: (o, c, 0, 0))],
        out_specs=pl.BlockSpec((2, SEQ, HY_CT), lambda c, q, o: (q, 0, c)),
        out_shape=jax.ShapeDtypeStruct((b, SEQ, HYENA_WIDTH), BF16),
        scratch_shapes=[pltpu.VMEM((N_IN, SLAB, 2 * HY_CT), BF16),
                        pltpu.VMEM((HYENA_ORDER, N_IN, SLAB, 2 * HY_CT), BF16),
                        pltpu.VMEM((N_HI, 2 * SLAB, HY_CT), BF16)],
        compiler_params=_cparams(("arbitrary", "arbitrary", "arbitrary"), 56),
        name="hyena_conv",
    )(p, p, p, conv_w, conv_w, conv_w, cb, cb, cb, skip, mf, mi, hspec)


def _ctx_hyena_kernel(uv_ref, u1_ref, u2_ref, cwv_ref, cw1_ref, cw2_ref, cbv_ref, cb1_ref, cb2_ref, skip_ref,
                      feat_ref, wf0_ref, wb0_ref, wf1_ref, wb1_ref, ad_ref, ff_ref, ai_ref, o_ref):
    l = CTX_LEN
    nb = uv_ref.shape[0]
    n = lax.broadcasted_iota(jnp.int32, (2 * l, HY_CT), 0)
    spectra = []
    for wf_ref, wb_ref in ((wf0_ref, wb0_ref), (wf1_ref, wb1_ref)):
        raw = jnp.concatenate(
            [jnp.dot(feat_ref[0:l, :], wf_ref[...], precision=HIGHEST, preferred_element_type=F32),
             jnp.dot(feat_ref[l:2 * l, :], wb_ref[...], precision=HIGHEST, preferred_element_type=F32)], axis=0)
        v = _filter_taps(raw, n, l, ad_ref[...])
        v = v / jnp.sum(jnp.abs(v), axis=0, keepdims=True)
        pq = jnp.dot(ff_ref[...], v.astype(BF16), preferred_element_type=F32)
        spectra.append((pq[:2 * l], pq[2 * l:]))

    row = lax.broadcasted_iota(jnp.int32, (l, HY_CT), 0)

    def conv(u_ref, bi, w_ref, b_ref):
        x = u_ref[bi].astype(F32)
        xm1 = jnp.where(row == 0, 0.0, pltpu.roll(x, 1, 0))
        xp1 = jnp.where(row == l - 1, 0.0, pltpu.roll(x, l - 1, 0))
        return xm1 * w_ref[0:1, :] + x * w_ref[1:2, :] + xp1 * w_ref[2:3, :] + b_ref[...]

    z = [conv(uv_ref, bi, cwv_ref, cbv_ref) for bi in range(nb)]
    gates = [[conv(u1_ref, bi, cw1_ref, cb1_ref) for bi in range(nb)],
             [conv(u2_ref, bi, cw2_ref, cb2_ref) for bi in range(nb)]]
    for o in range(HYENA_ORDER):
        hr, hi = spectra[o]
        pq = jnp.dot(ff_ref[:, 0:l], jnp.concatenate(z, axis=1).astype(BF16), preferred_element_type=F32)
        ys = []
        for q in range(nb // 2):
            xr, xi = _cplx_from_pq(pq[:, 2 * q * HY_CT:2 * (q + 1) * HY_CT], False)
            ys += [xr * hr - xi * hi, xr * hi + xi * hr]
        pq2 = jnp.dot(ai_ref[...], jnp.concatenate(ys, axis=1).astype(BF16), preferred_element_type=F32)
        y = []
        for q in range(nb // 2):
            y += list(_cplx_from_pq(pq2[:, 2 * q * HY_CT:2 * (q + 1) * HY_CT], False))
        z = [gates[o][bi] * (y[bi] + skip_ref[o:o + 1, :] * z[bi]) for bi in range(nb)]
    for bi in range(nb):
        o_ref[bi] = z[bi].astype(BF16)


def _ctx_hyena(pc, conv_w, conv_b, skip, feat, w_fout, absdelta, ff, ai):
    b, l, _ = pc.shape
    u0 = V_END // HY_CT
    cb = conv_b.reshape(1, -1)
    cols = 2 * N_CT
    part = lambda k: pl.BlockSpec((b, l, HY_CT), lambda c: (0, 0, u0 + k * N_CT + c))
    cw = lambda k: pl.BlockSpec((SHORT_CONV, HY_CT), lambda c: (0, k * N_CT + c))
    cbs = lambda k: pl.BlockSpec((1, HY_CT), lambda c: (0, k * N_CT + c))
    wcol = lambda off: pl.BlockSpec((FILTER_HIDDEN, HY_CT), lambda c: (0, off + c))
    return pl.pallas_call(
        _ctx_hyena_kernel,
        grid=(N_CT,),
        in_specs=[part(0), part(1), part(2), cw(0), cw(1), cw(2), cbs(0), cbs(1), cbs(2),
                  pl.BlockSpec((HYENA_ORDER, HY_CT), lambda c: (0, c)),
                  pl.BlockSpec((2 * l, FILTER_HIDDEN), lambda c: (0, 0)),
                  wcol(0), wcol(N_CT), wcol(cols), wcol(cols + N_CT),
                  pl.BlockSpec((1, HY_CT), lambda c: (0, c)),
                  pl.BlockSpec((4 * l, 2 * l), lambda c: (0, 0)),
                  pl.BlockSpec((2 * l, 2 * l), lambda c: (0, 0))],
        out_specs=pl.BlockSpec((b, l, HY_CT), lambda c: (0, 0, c)),
        out_shape=jax.ShapeDtypeStruct((b, l, HYENA_WIDTH), BF16),
        compiler_params=_cparams(("arbitrary",), 32),
        name="ctx_hyena",
    )(pc, pc, pc, conv_w, conv_w, conv_w, cb, cb, cb, skip, feat, w_fout, w_fout, w_fout, w_fout, absdelta, ff, ai)


def _rope_tables(l):
    rows = l // GRID_W
    row = np.repeat(np.arange(rows), GRID_W).astype(np.float64)
    col = np.tile(np.arange(GRID_W), rows).astype(np.float64)
    n_freq = HEAD_DIM // 4
    inv = ROPE_BASE ** (-np.arange(n_freq) / n_freq)
    ang_r = row[:, None] * inv
    ang_c = col[:, None] * inv
    cos = np.concatenate([np.cos(ang_r), np.cos(ang_r), np.cos(ang_c), np.cos(ang_c)], axis=-1)
    sin = np.concatenate([-np.sin(ang_r), np.sin(ang_r), -np.sin(ang_c), np.sin(ang_c)], axis=-1)
    return jnp.asarray(cos, F32), jnp.asarray(sin, F32)


def kernel(x, c, ctx, c_ctx, w_ada, b_ada, norm_mix, norm_mlp, w_in, q_norm, k_norm, attn_sink, conv_w, conv_b, filt_w1, filt_b1, filt_w2, filt_b2, filt_w3, filt_b3, filt_freq, filt_w_out, hyena_skip, w_out, w_mlp_in, w_mlp_out):
    cos, sin = _rope_tables(SEQ)
    emb, emb_ctx = _filter_positions(SEQ), _filter_positions(CTX_LEN)
    m_tall, mf, mi = _dft_constants()
    ff_ctx, ai_ctx = _ctx_dft_constants()
    absdelta = jnp.abs(jnp.linspace(MIN_DECAY, MAX_DECAY, HYENA_WIDTH, dtype=F32)).reshape(1, HYENA_WIDTH)
    cvec = jnp.zeros((MOD_ROWS, D_MODEL), F32).at[:BATCH].set(c).at[CTX_ROW].set(c_ctx)
    mod_all = _modulation(cvec, w_ada, b_ada).reshape(DEPTH, MOD_ROWS, N_MOD, D_MODEL)
    w_in_bf = w_in[0:1].astype(BF16)
    ctx_rows = BATCH * CTX_LEN
    flat = lambda a: a.reshape(1, ctx_rows, a.shape[-1])
    per_batch = lambda a: a.reshape(BATCH, CTX_LEN, a.shape[-1])
    xc = flat(ctx)
    for i in range(DEPTH):
        update_ctx = i < DEPTH - 1
        mod = mod_all[i]
        nmix = norm_mix[i].reshape(1, D_MODEL)
        nmlp = norm_mlp[i].reshape(1, D_MODEL)
        qn = q_norm[i].reshape(1, HEAD_DIM)
        kn = k_norm[i].reshape(1, HEAD_DIM)
        filt_mlp = (filt_w1[i], filt_b1[i], filt_w2[i], filt_b2[i], filt_w3[i], filt_b3[i], filt_freq[i])

        p, (w_out_bf, w1_bf, w2_bf) = _in_projection(
            x, mod, None, nmix, w_in_bf, qn, kn, cos, sin, tm=512, n_off=0, n_tiles=IN_COLS // IN_TN,
            use_rope=True, riders=[(w_out, i), (w_mlp_in, i), (w_mlp_out, i)])
        if update_ctx:
            pc, _ = _in_projection(xc, mod, CTX_ROW, nmix, w_in_bf, qn, kn, cos, sin,
                                   tm=512, n_off=0, n_tiles=IN_COLS // IN_TN, use_rope=False)
            kx_blk, vx_blk = Q_END // KV_COLS * 2, Q_END // KV_COLS * 2 + 1
        else:
            pc, _ = _in_projection(xc, mod, CTX_ROW, nmix, w_in_bf, qn, kn, cos, sin,
                                   tm=512, n_off=KV_TILE, n_tiles=1, use_rope=False)
            kx_blk, vx_blk = 0, 1
        pc = per_batch(pc)

        attn = _window_attention(p, pc, kx_blk, vx_blk, attn_sink[i])
        hspec = _filter_spectrum(_filter_features(emb, *filt_mlp), filt_w_out[i], absdelta, m_tall)
        hy = _hyena_conv(p, conv_w[i], conv_b[i], hyena_skip[i], mf, mi, hspec)
        x, h2, next_w_in = _out_projection(attn, hy, x, mod, None, nmlp, w_out_bf, tm=512,
                                           riders=[(w_in, i + 1)] if update_ctx else [])
        x = _mlp(h2, x, mod, None, w1_bf, w2_bf, tm=512, tf=1024)

        if update_ctx:
            attn_c = _context_attention(pc, attn_sink[i])
            hy_c = _ctx_hyena(pc, conv_w[i], conv_b[i], hyena_skip[i], _filter_features(emb_ctx, *filt_mlp),
                              filt_w_out[i], absdelta, ff_ctx, ai_ctx)
            xc, h2c, _ = _out_projection(flat(attn_c), flat(hy_c), xc, mod, CTX_ROW, nmlp, w_out_bf, tm=512)
            xc = _mlp(h2c, xc, mod, CTX_ROW, w1_bf, w2_bf, tm=512, tf=1024)
            w_in_bf = next_w_in[0]
    return x
```

```python
import cmath
import functools
import math

import jax
import jax.numpy as jnp
import numpy as np
from jax import lax
from jax.experimental import pallas as pl
from jax.experimental.pallas import tpu as pltpu

D_MODEL = 2048
BATCH = 4
SEQ = 4096
DEPTH = 2
GRID_W = 64
CTX_LEN = 256
HEAD_DIM = 128
N_Q_HEADS = 8
N_KV_HEADS = 2
GQA_GROUP = N_Q_HEADS // N_KV_HEADS
ATTN_WIDTH = N_Q_HEADS * HEAD_DIM
HYENA_WIDTH = D_MODEL - ATTN_WIDTH
HYENA_ORDER = 2
SHORT_CONV = 3
FILTER_EMB = 33
DECAY_TARGET = 1e-2
FAST_DECAY_PCT = 0.3
SLOW_DECAY_PCT = 1.5
MAX_DECAY = math.log(DECAY_TARGET) / FAST_DECAY_PCT
MIN_DECAY = math.log(DECAY_TARGET) / SLOW_DECAY_PCT
WINDOW = 128
BLOCK = 128
ROPE_BASE = 10000.0
D_FF = 4 * D_MODEL
N_MOD = 6
EPS = 1e-6
NEG_INF = -1e30
Q_END = ATTN_WIDTH
K_END = Q_END + N_KV_HEADS * HEAD_DIM
V_END = K_END + N_KV_HEADS * HEAD_DIM
IN_COLS = V_END + (HYENA_ORDER + 1) * HYENA_WIDTH

MOD_ROWS = 8
CTX_ROW = BATCH
KV_COLS = 2 * N_KV_HEADS * HEAD_DIM

F32 = jnp.float32
BF16 = jnp.bfloat16
MIB = 1024 * 1024


def _cparams(sem, vmem_mib):
    return pltpu.CompilerParams(dimension_semantics=sem, vmem_limit_bytes=vmem_mib * MIB)


def _mod_kernel(c_ref, w_ref, b_ref, o_ref):
    c = c_ref[...]
    s = (c * jax.nn.sigmoid(c)).astype(BF16)
    o_ref[...] = jnp.dot(s, w_ref[...].astype(BF16), preferred_element_type=F32) + b_ref[...]


def _modulation(cvec, w_ada, b_ada):
    tn = 1024
    n_cols = N_MOD * D_MODEL
    return pl.pallas_call(
        _mod_kernel,
        grid=(DEPTH, n_cols // tn),
        in_specs=[
            pl.BlockSpec((MOD_ROWS, D_MODEL), lambda l, j: (0, 0)),
            pl.BlockSpec((None, D_MODEL, tn), lambda l, j: (l, 0, j)),
            pl.BlockSpec((None, 1, tn), lambda l, j: (l, 0, j)),
        ],
        out_specs=pl.BlockSpec((None, MOD_ROWS, tn), lambda l, j: (l, 0, j)),
        out_shape=jax.ShapeDtypeStruct((DEPTH, MOD_ROWS, n_cols), F32),
        compiler_params=_cparams(("arbitrary", "arbitrary"), 40),
        name="adaln_mod",
    )(cvec, w_ada, b_ada.reshape(DEPTH, 1, n_cols))


IN_TN = 512
Q_TILES = Q_END // IN_TN
KV_TILE = Q_TILES
HEADS_PER_TILE = IN_TN // HEAD_DIM


SUB_ROWS = 256


def _rms_modulate(xf, gain, shift):
    ms = jnp.mean(xf * xf, axis=-1, keepdims=True)
    return xf * lax.rsqrt(ms + EPS) * gain + shift


def _head_norm_rope(a, g, cos, sin):
    ms = jnp.mean(a * a, axis=-1, keepdims=True)
    y = a * lax.rsqrt(ms + EPS) * g
    if cos is None:
        return y
    lane = lax.broadcasted_iota(jnp.int32, y.shape, 1)
    partner = jnp.where((lane % 64) < 32, pltpu.roll(y, 96, 1), pltpu.roll(y, 32, 1))
    return y * cos + partner * sin


BF16_TILE_ROWS = 16


def _rider_plan(w, layer, steps, step_of):
    depth, r, c = w.shape
    assert r % (steps * BF16_TILE_ROWS) == 0
    blk = r // steps
    in_spec = pl.BlockSpec((blk, c), lambda *g: (layer * steps + step_of(*g), 0))
    out_spec = pl.BlockSpec((blk, c), lambda *g: (step_of(*g), 0))
    restore = lambda a: a.reshape(1, r, c)
    return w.reshape(depth * r, c), in_spec, out_spec, jax.ShapeDtypeStruct((r, c), BF16), restore


def _cast_riders(in_refs, out_refs):
    for src, dst in zip(in_refs, out_refs):
        dst[...] = src[...].astype(BF16)


def _inproj_kernel(*refs, n_off, n_tiles, use_rope, n_riders):
    x_ref, m_ref, nw_ref, w_ref, qn_ref, kn_ref, cos_ref, sin_ref = refs[:8]
    o_ref = refs[8 + n_riders]
    _cast_riders(refs[8:8 + n_riders], refs[9 + n_riders:])
    gain = nw_ref[...] * (1.0 + m_ref[1:2, :])
    shift = m_ref[0:1, :]
    for r0 in range(0, x_ref.shape[0], SUB_ROWS):
        rows = slice(r0, r0 + SUB_ROWS)
        h = _rms_modulate(x_ref[rows, :], gain, shift).astype(BF16)
        cos = cos_ref[rows, :] if use_rope else None
        sin = sin_ref[rows, :] if use_rope else None
        for n in range(n_tiles):
            cols = slice(n * IN_TN, (n + 1) * IN_TN)
            acc = jnp.dot(h, w_ref[:, cols], preferred_element_type=F32)
            t = n + n_off
            if t > KV_TILE:
                o_ref[rows, cols] = acc.astype(BF16)
                continue
            gains = [qn_ref[...]] * HEADS_PER_TILE if t < Q_TILES else [kn_ref[...]] * N_KV_HEADS + [None] * N_KV_HEADS
            for c, g in enumerate(gains):
                sl = slice(c * HEAD_DIM, (c + 1) * HEAD_DIM)
                osl = slice(n * IN_TN + c * HEAD_DIM, n * IN_TN + (c + 1) * HEAD_DIM)
                if g is None:
                    o_ref[rows, osl] = acc[:, sl].astype(BF16)
                else:
                    o_ref[rows, osl] = _head_norm_rope(acc[:, sl], g, cos, sin).astype(BF16)


def _in_projection(x, mod, mod_row, nw, w_bf, qn, kn, cos, sin, *, tm, n_off, n_tiles, use_rope, riders=()):
    b, t, _ = x.shape
    assert n_off % n_tiles == 0 and tm % SUB_ROWS == 0
    row = (lambda bb: bb) if mod_row is None else (lambda bb: mod_row)
    n_i = t // tm
    plans = [_rider_plan(w, layer, b * n_i, lambda bb, i: bb * n_i + i) for w, layer in riders]
    kern = functools.partial(_inproj_kernel, n_off=n_off, n_tiles=n_tiles, use_rope=use_rope, n_riders=len(plans))
    width = n_tiles * IN_TN
    out = pl.pallas_call(
        kern,
        grid=(b, n_i),
        in_specs=[
            pl.BlockSpec((None, tm, D_MODEL), lambda bb, i: (bb, i, 0)),
            pl.BlockSpec((None, N_MOD, D_MODEL), lambda bb, i: (row(bb), 0, 0)),
            pl.BlockSpec((1, D_MODEL), lambda bb, i: (0, 0)),
            pl.BlockSpec((None, D_MODEL, width), lambda bb, i: (0, 0, n_off // n_tiles),
                         pipeline_mode=pl.Buffered(1)),
            pl.BlockSpec((1, HEAD_DIM), lambda bb, i: (0, 0)),
            pl.BlockSpec((1, HEAD_DIM), lambda bb, i: (0, 0)),
            pl.BlockSpec((tm, HEAD_DIM), lambda bb, i: (i, 0)),
            pl.BlockSpec((tm, HEAD_DIM), lambda bb, i: (i, 0)),
        ] + [pl_[1] for pl_ in plans],
        out_specs=[pl.BlockSpec((None, tm, width), lambda bb, i: (bb, i, 0))] + [pl_[2] for pl_ in plans],
        out_shape=[jax.ShapeDtypeStruct((b, t, width), BF16)] + [pl_[3] for pl_ in plans],
        compiler_params=_cparams(("arbitrary", "arbitrary"), 56),
        name="in_proj",
    )(x, mod, nw, w_bf, qn, kn, cos, sin, *[pl_[0] for pl_ in plans])
    return out[0], [pl_[4](a) for pl_, a in zip(plans, out[1:])]


def _sink_column(sink_ref, kv, rows_per_head):
    rows = GQA_GROUP * rows_per_head
    head = lax.broadcasted_iota(jnp.int32, (rows, 1), 0) // rows_per_head
    col = jnp.full((rows, 1), sink_ref[kv * GQA_GROUP], F32)
    for g in range(1, GQA_GROUP):
        col = jnp.where(head == g, sink_ref[kv * GQA_GROUP + g], col)
    return col


def _stack_heads(q_ref, kv, rows=slice(None)):
    return jnp.concatenate(
        [q_ref[rows, (kv * GQA_GROUP + g) * HEAD_DIM:(kv * GQA_GROUP + g + 1) * HEAD_DIM]
         for g in range(GQA_GROUP)], axis=0)


LOG2E = math.log2(math.e)
QK_LOG2_SCALE = HEAD_DIM ** -0.5 * LOG2E


def _softmax_pv(q4, k_all, v_all, bias, sink_col):
    t = lax.dot_general(q4, k_all, (((1,), (1,)), ((), ())), preferred_element_type=F32) * QK_LOG2_SCALE
    if bias is not None:
        t = t + bias
    sink2 = sink_col * LOG2E
    m = jnp.maximum(jnp.max(t, axis=-1, keepdims=True), sink2)
    e = jnp.exp2(t - m)
    denom = jnp.exp2(sink2 - m) + jnp.sum(e, axis=-1, keepdims=True)
    return jnp.dot(e.astype(BF16), v_all, preferred_element_type=F32) / denom


Q_BLOCKS = 4


def _window_attn_kernel(sink_ref, bias_first_ref, bias_mid_ref, bias_last_ref, q_ref, kp_ref, km_ref, kn_ref,
                        vp_ref, vm_ref, vn_ref, kx_ref, vx_ref, o_ref):
    for blk in range(Q_BLOCKS):
        bias_ref = bias_first_ref if blk == 0 else bias_last_ref if blk == Q_BLOCKS - 1 else bias_mid_ref
        rows = slice(blk * BLOCK, (blk + 1) * BLOCK)
        for kv in range(N_KV_HEADS):
            ks = slice(kv * HEAD_DIM, (kv + 1) * HEAD_DIM)

            def band(prev_ref, mid_ref, next_ref):
                pieces = [prev_ref[:, ks]] + [mid_ref[j * BLOCK:(j + 1) * BLOCK, ks] for j in range(Q_BLOCKS)]
                return (pieces + [next_ref[:, ks]])[blk:blk + 3]

            k_all = jnp.concatenate([kx_ref[:, ks]] + band(kp_ref, km_ref, kn_ref), axis=0)
            v_all = jnp.concatenate([vx_ref[:, ks]] + band(vp_ref, vm_ref, vn_ref), axis=0)
            o = _softmax_pv(_stack_heads(q_ref, kv, rows), k_all, v_all, bias_ref[...],
                            _sink_column(sink_ref, kv, BLOCK))
            for g in range(GQA_GROUP):
                h = kv * GQA_GROUP + g
                o_ref[rows, h * HEAD_DIM:(h + 1) * HEAD_DIM] = o[g * BLOCK:(g + 1) * BLOCK].astype(BF16)


def _window_bias(nb):
    qi = np.arange(GQA_GROUP * BLOCK)[:, None] % BLOCK
    sj = np.arange(3 * BLOCK)[None, :] - BLOCK
    near = np.abs(sj - qi) <= WINDOW
    out = np.zeros((3, GQA_GROUP * BLOCK, CTX_LEN + 3 * BLOCK), np.float32)
    for variant, blk in enumerate((0, 1, nb - 1)):
        key_pos = blk * BLOCK + sj
        ok = near & (key_pos >= 0) & (key_pos < nb * BLOCK)
        out[variant, :, CTX_LEN:] = np.where(ok, 0.0, NEG_INF)
    return jnp.asarray(out)


def _window_attention(p, pc, kx_blk, vx_blk, sink):
    b, seq, _ = p.shape
    nb = seq // BLOCK
    kvw = N_KV_HEADS * HEAD_DIM
    k_blk, v_blk = Q_END // kvw, K_END // kvw
    assert nb % Q_BLOCKS == 0 and Q_BLOCKS >= 2
    steps = nb // Q_BLOCKS
    prev = lambda bb, i: (bb, jnp.maximum(Q_BLOCKS * i - 1, 0))
    nxt = lambda bb, i: (bb, jnp.minimum(Q_BLOCKS * (i + 1), nb - 1))

    def edge(which, blk):
        return pl.BlockSpec((None, BLOCK, kvw), lambda bb, i: which(bb, i) + (blk,))

    def mid(blk):
        return pl.BlockSpec((None, Q_BLOCKS * BLOCK, kvw), lambda bb, i: (bb, i, blk))

    bias_shape = (None, GQA_GROUP * BLOCK, CTX_LEN + 3 * BLOCK)
    bias = _window_bias(nb)
    return pl.pallas_call(
        _window_attn_kernel,
        grid=(b, steps),
        in_specs=[
            pl.BlockSpec(memory_space=pltpu.SMEM),
            pl.BlockSpec(bias_shape, lambda bb, i: (jnp.where(i == 0, 0, 1), 0, 0)),
            pl.BlockSpec(bias_shape, lambda bb, i: (1, 0, 0)),
            pl.BlockSpec(bias_shape, lambda bb, i: (jnp.where(i == steps - 1, 2, 1), 0, 0)),
            pl.BlockSpec((None, Q_BLOCKS * BLOCK, ATTN_WIDTH), lambda bb, i: (bb, i, 0)),
            edge(prev, k_blk), mid(k_blk), edge(nxt, k_blk),
            edge(prev, v_blk), mid(v_blk), edge(nxt, v_blk),
            pl.BlockSpec((None, CTX_LEN, kvw), lambda bb, i: (bb, 0, kx_blk)),
            pl.BlockSpec((None, CTX_LEN, kvw), lambda bb, i: (bb, 0, vx_blk)),
        ],
        out_specs=pl.BlockSpec((None, Q_BLOCKS * BLOCK, ATTN_WIDTH), lambda bb, i: (bb, i, 0)),
        out_shape=jax.ShapeDtypeStruct((b, seq, ATTN_WIDTH), BF16),
        compiler_params=_cparams(("arbitrary", "arbitrary"), 32),
        name="window_attn",
    )(sink, bias, bias, bias, p, p, p, p, p, p, p, pc, pc)


def _ctx_attn_kernel(sink_ref, q_ref, k_ref, v_ref, o_ref):
    n = q_ref.shape[0]
    for kv in range(N_KV_HEADS):
        ks = slice(kv * HEAD_DIM, (kv + 1) * HEAD_DIM)
        o = _softmax_pv(_stack_heads(q_ref, kv), k_ref[:, ks], v_ref[:, ks], None, _sink_column(sink_ref, kv, n))
        for g in range(GQA_GROUP):
            h = kv * GQA_GROUP + g
            o_ref[:, h * HEAD_DIM:(h + 1) * HEAD_DIM] = o[g * n:(g + 1) * n].astype(BF16)


def _context_attention(pc, sink):
    b, n, _ = pc.shape
    kvw = N_KV_HEADS * HEAD_DIM
    return pl.pallas_call(
        _ctx_attn_kernel,
        grid=(b,),
        in_specs=[
            pl.BlockSpec(memory_space=pltpu.SMEM),
            pl.BlockSpec((None, n, ATTN_WIDTH), lambda bb: (bb, 0, 0)),
            pl.BlockSpec((None, n, kvw), lambda bb: (bb, 0, Q_END // kvw)),
            pl.BlockSpec((None, n, kvw), lambda bb: (bb, 0, K_END // kvw)),
        ],
        out_specs=pl.BlockSpec((None, n, ATTN_WIDTH), lambda bb: (bb, 0, 0)),
        out_shape=jax.ShapeDtypeStruct((b, n, ATTN_WIDTH), BF16),
        compiler_params=_cparams(("arbitrary",), 32),
        name="ctx_attn",
    )(sink, pc, pc, pc)


def _outproj_kernel(*refs, n_riders):
    a_ref, y_ref, x_ref, m_ref, nw_ref, wa_ref, wy_ref = refs[:7]
    o_ref, h_ref = refs[7 + n_riders], refs[8 + n_riders]
    _cast_riders(refs[7:7 + n_riders], refs[9 + n_riders:])
    gain = nw_ref[...] * (1.0 + m_ref[4:5, :])
    shift = m_ref[3:4, :]
    for r0 in range(0, x_ref.shape[0], SUB_ROWS):
        rows = slice(r0, r0 + SUB_ROWS)
        mix = (jnp.dot(a_ref[rows, :], wa_ref[...], preferred_element_type=F32)
               + jnp.dot(y_ref[rows, :], wy_ref[...], preferred_element_type=F32))
        x1 = x_ref[rows, :] + m_ref[2:3, :] * mix
        o_ref[rows, :] = x1
        h_ref[rows, :] = _rms_modulate(x1, gain, shift).astype(BF16)


def _out_projection(attn, hy, x, mod, mod_row, nw, w_bf, *, tm, riders=()):
    b, t, _ = x.shape
    assert tm % SUB_ROWS == 0
    n_i = t // tm
    row = (lambda bb: bb) if mod_row is None else (lambda bb: mod_row)
    tile = lambda width: pl.BlockSpec((None, tm, width), lambda bb, i: (bb, i, 0))
    half = lambda k: pl.BlockSpec((None, D_MODEL // 2, D_MODEL), lambda bb, i: (0, k, 0),
                                  pipeline_mode=pl.Buffered(1))
    plans = [_rider_plan(w, layer, b * n_i, lambda bb, i: bb * n_i + i) for w, layer in riders]
    out = pl.pallas_call(
        functools.partial(_outproj_kernel, n_riders=len(plans)),
        grid=(b, n_i),
        in_specs=[
            tile(ATTN_WIDTH), tile(HYENA_WIDTH), tile(D_MODEL),
            pl.BlockSpec((None, N_MOD, D_MODEL), lambda bb, i: (row(bb), 0, 0)),
            pl.BlockSpec((1, D_MODEL), lambda bb, i: (0, 0)),
            half(0), half(1),
        ] + [pl_[1] for pl_ in plans],
        out_specs=[tile(D_MODEL), tile(D_MODEL)] + [pl_[2] for pl_ in plans],
        out_shape=[jax.ShapeDtypeStruct(x.shape, F32), jax.ShapeDtypeStruct(x.shape, BF16)]
        + [pl_[3] for pl_ in plans],
        compiler_params=_cparams(("arbitrary", "arbitrary"), 48),
        name="out_proj",
    )(attn, hy, x, mod, nw, w_bf, w_bf, *[pl_[0] for pl_ in plans])
    return out[0], out[1], [pl_[4](a) for pl_, a in zip(plans, out[2:])]


MLP_OUT_CHUNK = 512


def _mlp_kernel(h_ref, x_ref, m_ref, w1_ref, w2_ref, o_ref, *, n_f):
    f = pl.program_id(2)

    @pl.when(f == 0)
    def _():
        o_ref[...] = jnp.zeros_like(o_ref)

    a = jnp.maximum(jnp.dot(h_ref[...], w1_ref[...], preferred_element_type=F32), 0.0)
    a = (a * a).astype(BF16)
    for n0 in range(0, D_MODEL, MLP_OUT_CHUNK):
        sl = slice(n0, n0 + MLP_OUT_CHUNK)
        o_ref[:, sl] += jnp.dot(a, w2_ref[:, sl], preferred_element_type=F32)

    @pl.when(f == n_f - 1)
    def _():
        o_ref[...] = x_ref[...] + m_ref[5:6, :] * o_ref[...]


def _mlp(h, x, mod, mod_row, w1_bf, w2_bf, *, tm, tf):
    b, t, _ = x.shape
    n_f = D_FF // tf
    row = (lambda bb: bb) if mod_row is None else (lambda bb: mod_row)
    tile = pl.BlockSpec((None, tm, D_MODEL), lambda bb, i, f: (bb, i, 0))
    return pl.pallas_call(
        functools.partial(_mlp_kernel, n_f=n_f),
        grid=(b, t // tm, n_f),
        in_specs=[
            tile, tile,
            pl.BlockSpec((None, N_MOD, D_MODEL), lambda bb, i, f: (row(bb), 0, 0)),
            pl.BlockSpec((None, D_MODEL, tf), lambda bb, i, f: (0, 0, f)),
            pl.BlockSpec((None, tf, D_MODEL), lambda bb, i, f: (0, f, 0)),
        ],
        out_specs=tile,
        out_shape=jax.ShapeDtypeStruct(x.shape, F32),
        compiler_params=_cparams(("arbitrary", "arbitrary", "arbitrary"), 56),
        name="mlp",
    )(h, x, mod, w1_bf, w2_bf)


def _tile_const(k, like):
    return jnp.full(like.shape, k, F32).astype(like.dtype)


def _p_add(p, q):
    if p is None:
        return q
    if q is None:
        return p
    (a, sa), (b, sb) = p, q
    if sa == sb:
        return (a + b, sa)
    return (a - b, 1) if sa > 0 else (b - a, 1)


def _p_scale(p, k):
    if p is None or k == 0.0:
        return None
    a, s = p
    if k < 0:
        s, k = -s, -k
    return (a, s) if k == 1.0 else (a * _tile_const(k, a), s)


def _c_add(x, y, sign=1):
    if y is None:
        return x
    if x is None:
        x = (None, None)
    return (_p_add(x[0], _p_scale(y[0], sign)), _p_add(x[1], _p_scale(y[1], sign)))


def _snap(v):
    for t in (0.0, 1.0, -1.0):
        if abs(v - t) < 1e-12:
            return t
    return v


def _c_mulc(x, c):
    if x is None:
        return None
    cr, ci = _snap(c.real), _snap(c.imag)
    re, im = x
    if cr != 0.0 and ci != 0.0 and abs(abs(cr) - abs(ci)) < 1e-12:
        k, sr, si = abs(cr), math.copysign(1.0, cr), math.copysign(1.0, ci)
        return (_p_scale(_p_add(_p_scale(re, sr), _p_scale(im, -si)), k),
                _p_scale(_p_add(_p_scale(re, si), _p_scale(im, sr)), k))
    return (_p_add(_p_scale(re, cr), _p_scale(im, -ci)), _p_add(_p_scale(re, ci), _p_scale(im, cr)))


def _is_zero(x):
    return x is None or (x[0] is None and x[1] is None)


def _slab_fft_dit(xs, sign, need):
    n = len(xs)
    if n == 1:
        return {0: xs[0]}
    if all(_is_zero(x) for x in xs):
        return {k: None for k in need}
    h = n // 2
    sub = {k % h for k in need}
    ev = _slab_fft_dit(xs[0::2], sign, sub)
    od = _slab_fft_dit(xs[1::2], sign, sub)
    out = {}
    for k in sub:
        t = _c_mulc(od[k], cmath.exp(sign * 2j * math.pi * k / n))
        if k in need:
            out[k] = _c_add(ev[k], t)
        if k + h in need:
            out[k + h] = _c_add(ev[k], t, -1)
    return out


def _slab_fft_dif(xs, sign, need):
    n = len(xs)
    if n == 1:
        return {0: xs[0]}
    h = n // 2
    need_e = {k // 2 for k in need if k % 2 == 0}
    need_o = {k // 2 for k in need if k % 2 == 1}
    out = {}
    if need_e:
        ev = _slab_fft_dif([_c_add(xs[j], xs[j + h]) for j in range(h)], sign, need_e)
        out.update({2 * k: v for k, v in ev.items()})
    if need_o:
        dif = [_c_mulc(_c_add(xs[j], xs[j + h], -1), cmath.exp(sign * 2j * math.pi * j / n)) for j in range(h)]
        od = _slab_fft_dif(dif, sign, need_o)
        out.update({2 * k + 1: v for k, v in od.items()})
    return out


def _part_value(p, like):
    if p is None:
        return jnp.zeros_like(like)
    return p[0] if p[1] > 0 else -p[0]


HY_CT = 128
SLAB = 256
N_HI = 2 * SEQ // SLAB
N_IN = SEQ // SLAB
N_SPEC = N_HI // 2 + 1
ROW_CHUNK = 8
HY_ROWS = 16
N_CT = HYENA_WIDTH // HY_CT
SPEC_GROUP = 4
FILTER_HIDDEN = 64
EMB_PAD = 128
HIGHEST = lax.Precision.HIGHEST


def _dft_constants():
    n = 2 * SEQ
    r = np.arange(SLAB)
    tall = np.zeros((N_SPEC, 2 * SLAB, SLAB))
    fwd = np.zeros((N_SPEC, 2 * SLAB, 2 * SLAB))
    inv = np.zeros((N_SPEC, 2 * SLAB, 2 * SLAB))
    for b in range(N_SPEC):
        m = np.exp(-2j * np.pi * (np.outer(r, r) / SLAB + (r * b)[None, :] / n))
        a = m.T / n
        tall[b] = np.concatenate([m.real, m.imag], axis=0)
        fwd[b] = np.block([[m.real, -m.imag], [m.imag, m.real]])
        inv[b] = np.block([[a.real, a.imag], [-a.imag, a.real]])
    return tuple(jnp.asarray(t, F32).astype(BF16) for t in (tall, fwd, inv))


def _ctx_dft_constants():
    n = 2 * CTX_LEN
    k = np.arange(n)
    f = np.exp(-2j * np.pi * np.outer(k, k) / n)
    a = np.conj(f)[:CTX_LEN, :] / n
    return (jnp.asarray(np.concatenate([f.real, f.imag], axis=0), F32).astype(BF16),
            jnp.asarray(np.concatenate([a.real, a.imag], axis=0), F32).astype(BF16))


def _filter_positions(l):
    t = np.linspace(0.0, 1.0, l)[:, None]
    bands = (FILTER_EMB - 1) // 2
    f = np.linspace(1e-4, bands - 1, bands)[None, :]
    w = 2.0 * math.pi * np.arange(l)[:, None] / l
    z = np.concatenate([t, np.cos(f * w), -np.sin(f * w)], axis=-1)
    z2 = np.concatenate([z, z[0:1], z[1:][::-1]], axis=0)
    return jnp.asarray(np.pad(z2, ((0, 0), (0, EMB_PAD - FILTER_EMB))), F32)


def _filter_feat_kernel(z_ref, w1_ref, b1_ref, w2_ref, b2_ref, w3_ref, b3_ref, fr_ref, o_ref):
    fr = fr_ref[...]
    dot = functools.partial(jnp.dot, precision=HIGHEST, preferred_element_type=F32)
    h = jnp.sin(fr * (dot(z_ref[...], w1_ref[...]) + b1_ref[...]))
    h = jnp.sin(fr * (dot(h, w2_ref[...]) + b2_ref[...]))
    o_ref[...] = jnp.sin(fr * (dot(h, w3_ref[...]) + b3_ref[...]))


def _filter_features(emb, w1, b1, w2, b2, w3, b3, freq):
    rows = emb.shape[0]
    tr = min(rows, 1024)
    w1p = jnp.pad(w1, ((0, EMB_PAD - FILTER_EMB), (0, 0)))
    vec = lambda v: v.reshape(1, FILTER_HIDDEN)
    full = lambda shape: pl.BlockSpec(shape, lambda i: (0, 0))
    return pl.pallas_call(
        _filter_feat_kernel,
        grid=(rows // tr,),
        in_specs=[pl.BlockSpec((tr, EMB_PAD), lambda i: (i, 0)),
                  full((EMB_PAD, FILTER_HIDDEN)), full((1, FILTER_HIDDEN)),
                  full((FILTER_HIDDEN, FILTER_HIDDEN)), full((1, FILTER_HIDDEN)),
                  full((FILTER_HIDDEN, FILTER_HIDDEN)), full((1, FILTER_HIDDEN)),
                  full((1, FILTER_HIDDEN))],
        out_specs=pl.BlockSpec((tr, FILTER_HIDDEN), lambda i: (i, 0)),
        out_shape=jax.ShapeDtypeStruct((rows, FILTER_HIDDEN), F32),
        compiler_params=_cparams(("arbitrary",), 32),
        name="filter_feat",
    )(emb, w1p, vec(b1), w2, vec(b2), w3, vec(b3), vec(freq))


def _filter_taps(raw, n, l, absdelta):
    m = jnp.where(n < l, n, 2 * l - n)
    t = m.astype(F32) / (l - 1)
    return jnp.where(n == l, 0.0, raw * jnp.exp(-t * absdelta))


def _cplx_from_pq(pq, conj):
    half = pq.shape[0] // 2
    pr, pi_ = pq[:half, :HY_CT], pq[:half, HY_CT:]
    qr, qi = pq[half:, :HY_CT], pq[half:, HY_CT:]
    if conj:
        return pr + qi, pi_ - qr
    return pr - qi, qr + pi_


def _slab_forward(src_ref, n_in, real_only, dst_ref, need, scale):
    def body(rc, carry):
        rows = pl.ds(pl.multiple_of(rc * ROW_CHUNK, ROW_CHUNK), ROW_CHUNK)
        xs = []
        for a in range(N_HI):
            if a >= n_in:
                xs.append(None)
                continue
            re = src_ref[a, rows, 0:HY_CT]
            if scale is not None:
                re = re * scale
            xs.append(((re, 1), None if real_only else (src_ref[a, rows, HY_CT:2 * HY_CT], 1)))
        like = xs[0][0][0]
        out = _slab_fft_dit(xs, -1, need)
        for b in sorted(need):
            c = out[b] or (None, None)
            dst_ref[b, rows, 0:HY_CT] = _part_value(c[0], like)
            dst_ref[b, rows, HY_CT:2 * HY_CT] = _part_value(c[1], like)
        return carry

    lax.fori_loop(0, SLAB // ROW_CHUNK, body, 0)


def _filter_spec_kernel(feat_ref, wf_ref, wb_ref, ad_ref, mf_ref, o_ref, ft_ref, s_ref):
    rate = ad_ref[...] * (-1.0 / (SEQ - 1))
    row = lax.broadcasted_iota(jnp.int32, (SLAB, HY_CT), 0)
    row_f = row.astype(F32)
    first_row = row == 0

    def taps(a, acc):
        rows = feat_ref[pl.ds(pl.multiple_of(a * SLAB, SLAB), SLAB), :]
        fwd = a < N_IN
        w = jnp.where(fwd, wf_ref[...], wb_ref[...])
        raw = jnp.dot(rows.astype(BF16), w.astype(BF16), preferred_element_type=F32)
        base = jnp.where(fwd, a * SLAB, 2 * SEQ - a * SLAB).astype(F32)
        step = jnp.where(fwd, 1.0, -1.0)
        v = raw * jnp.exp((base + step * row_f) * rate)
        v = jnp.where(jnp.logical_and(a == N_IN, first_row), 0.0, v)
        ft_ref[a] = v
        return acc + jnp.sum(jnp.abs(v), axis=0, keepdims=True)

    norm = lax.fori_loop(0, N_HI, taps, jnp.zeros((1, HY_CT), F32), unroll=2)
    _slab_forward(ft_ref, N_HI, True, s_ref, set(range(N_SPEC)), 1.0 / norm)

    def spec(b):
        pq = jnp.dot(mf_ref[b], s_ref[b].astype(BF16), preferred_element_type=F32)
        xr, xi = _cplx_from_pq(pq, False)
        rows = pl.ds(pl.multiple_of(b * SLAB, SLAB), SLAB)
        o_ref[rows, 0:HY_CT] = xr.astype(BF16)
        o_ref[rows, HY_CT:2 * HY_CT] = xi.astype(BF16)

    def spec_group(i, carry):
        for t in range(SPEC_GROUP):
            spec(1 + SPEC_GROUP * i + t)
        return carry

    spec(0)
    lax.fori_loop(0, (N_SPEC - 1) // SPEC_GROUP, spec_group, 0)


def _filter_spectrum(feat, w_fout, absdelta, mf):
    cols = 2 * N_CT
    return pl.pallas_call(
        _filter_spec_kernel,
        grid=(HYENA_ORDER, N_CT),
        in_specs=[
            pl.BlockSpec((2 * SEQ, FILTER_HIDDEN), lambda o, c: (0, 0)),
            pl.BlockSpec((FILTER_HIDDEN, HY_CT), lambda o, c: (0, o * cols + c)),
            pl.BlockSpec((FILTER_HIDDEN, HY_CT), lambda o, c: (0, o * cols + N_CT + c)),
            pl.BlockSpec((1, HY_CT), lambda o, c: (0, c)),
            pl.BlockSpec((N_SPEC, 2 * SLAB, SLAB), lambda o, c: (0, 0, 0)),
        ],
        out_specs=pl.BlockSpec((None, None, N_SPEC * SLAB, 2 * HY_CT), lambda o, c: (o, c, 0, 0)),
        out_shape=jax.ShapeDtypeStruct((HYENA_ORDER, N_CT, N_SPEC * SLAB, 2 * HY_CT), BF16),
        scratch_shapes=[pltpu.VMEM((N_HI, SLAB, HY_CT), F32), pltpu.VMEM((N_SPEC, SLAB, 2 * HY_CT), F32)],
        compiler_params=_cparams(("arbitrary", "arbitrary"), 48),
        name="filter_spec",
    )(feat, w_fout, w_fout, absdelta, mf)


HALO = 8


def _conv_slab(u_ref, bi, a, w, b, st):
    r0 = a * SLAB
    x = u_ref[bi, pl.ds(pl.multiple_of(r0, SLAB), SLAB), :].astype(F32)
    before = u_ref[bi, pl.ds(pl.multiple_of(jnp.maximum(r0 - 16, 0), 16), 16), :].astype(F32)
    after = u_ref[bi, pl.ds(pl.multiple_of(jnp.minimum(r0 + SLAB, SEQ - 16), 16), 16), :].astype(F32)
    st[0:HALO, :] = jnp.where(a > 0, before[16 - HALO:16, :], 0.0)
    st[HALO:HALO + SLAB, :] = x
    st[HALO + SLAB:2 * HALO + SLAB, :] = jnp.where(a < N_IN - 1, after[0:HALO, :], 0.0)
    xm1 = st[HALO - 1:HALO - 1 + SLAB, :]
    xp1 = st[HALO + 1:HALO + 1 + SLAB, :]
    return xm1 * w[0:1, :] + x * w[1:2, :] + xp1 * w[2:3, :] + b


def _hyena_kernel(uv_ref, u1_ref, u2_ref, cwv_ref, cw1_ref, cw2_ref, cbv_ref, cb1_ref, cb2_ref, skip_ref,
                  mf_ref, mi_ref, h_ref, o_ref, zc_ref, g_ref, s_ref, st_ref):
    o = pl.program_id(2)

    @pl.when(o == 0)
    def _():
        def conv(a, carry):
            for bi in range(2):
                lanes = slice(bi * HY_CT, (bi + 1) * HY_CT)
                st = lambda k: st_ref.at[3 * bi + k]
                zc_ref[a, :, lanes] = _conv_slab(uv_ref, bi, a, cwv_ref[...], cbv_ref[...], st(0)).astype(BF16)
                g_ref[0, a, :, lanes] = _conv_slab(u1_ref, bi, a, cw1_ref[...], cb1_ref[...], st(1)).astype(BF16)
                g_ref[1, a, :, lanes] = _conv_slab(u2_ref, bi, a, cw2_ref[...], cb2_ref[...], st(2)).astype(BF16)
            return carry

        lax.fori_loop(0, N_IN, conv, 0)

    def im_sign(b):
        return -1 if b > N_HI // 2 else 1

    def forward_chunk(rc, carry):
        rows = pl.ds(pl.multiple_of(rc * HY_ROWS, HY_ROWS), HY_ROWS)
        rows_im = pl.ds(pl.multiple_of(SLAB + rc * HY_ROWS, HY_ROWS), HY_ROWS)
        xs = [((zc_ref[a, rows, 0:HY_CT], 1), (zc_ref[a, rows, HY_CT:2 * HY_CT], 1)) for a in range(N_IN)]
        like = xs[0][0][0]
        out = _slab_fft_dit(xs + [None] * (N_HI - N_IN), -1, set(range(N_HI)))
        for b in range(N_HI):
            re, im = out[b]
            s_ref[b, rows, :] = _part_value(re, like)
            s_ref[b, rows_im, :] = _part_value(None if im is None else (im[0], im[1] * im_sign(b)), like)
        return carry

    lax.fori_loop(0, SLAB // HY_ROWS, forward_chunk, 0)

    def through_filter(slabs, j):
        h = h_ref[j * SLAB:(j + 1) * SLAB, :].astype(F32)
        hr = jnp.concatenate([h[:, :HY_CT]] * len(slabs), axis=1)
        hi = jnp.concatenate([h[:, HY_CT:]] * len(slabs), axis=1)
        d = s_ref[slabs[0]] if len(slabs) == 1 else jnp.concatenate([s_ref[b] for b in slabs], axis=1)
        x = jnp.dot(mf_ref[j], d, preferred_element_type=F32)
        xr, xi = x[:SLAB], x[SLAB:]
        y = jnp.concatenate([xr * hr - xi * hi, xr * hi + xi * hr], axis=0).astype(BF16)
        w = jnp.dot(mi_ref[j], y, preferred_element_type=F32).astype(BF16)
        return [(b, w[:, k * HY_CT:(k + 1) * HY_CT]) for k, b in enumerate(slabs)]

    for slabs, j in [([0], 0), ([N_HI // 2], N_HI // 2)] + [([j, N_HI - j], j) for j in range(1, N_HI // 2)]:
        for b, w in through_filter(slabs, j):
            s_ref[b] = w

    def inverse_chunk(rc, carry):
        rows = pl.ds(pl.multiple_of(rc * HY_ROWS, HY_ROWS), HY_ROWS)
        rows_im = pl.ds(pl.multiple_of(SLAB + rc * HY_ROWS, HY_ROWS), HY_ROWS)
        xs = [((s_ref[b, rows, :], 1), (s_ref[b, rows_im, :], im_sign(b))) for b in range(N_HI)]
        like = xs[0][0][0]
        out = _slab_fft_dif(xs, 1, set(range(N_IN)))
        for a in range(N_IN):
            s_ref[a, rows, :] = _part_value(out[a][0], like)
            s_ref[a, rows_im, :] = _part_value(out[a][1], like)
        return carry

    lax.fori_loop(0, SLAB // HY_ROWS, inverse_chunk, 0)

    sk = skip_ref[pl.ds(o, 1), :]
    skip2 = jnp.concatenate([sk, sk], axis=1)

    def gate(a, carry):
        y = jnp.concatenate([s_ref[a, 0:SLAB, :], s_ref[a, SLAB:2 * SLAB, :]], axis=1).astype(F32)
        z = g_ref[o, a].astype(F32) * (y + skip2 * zc_ref[a].astype(F32))
        zc_ref[a] = z.astype(BF16)
        return carry

    lax.fori_loop(0, N_IN, gate, 0)

    @pl.when(o == HYENA_ORDER - 1)
    def _():
        def emit(a, carry):
            rows = pl.ds(pl.multiple_of(a * SLAB, SLAB), SLAB)
            for bi in range(2):
                o_ref[bi, rows, :] = zc_ref[a, :, bi * HY_CT:(bi + 1) * HY_CT]
            return carry

        lax.fori_loop(0, N_IN, emit, 0)


def _hyena_conv(p, conv_w, conv_b, skip, mf, mi, hspec):
    b = p.shape[0]
    u0 = V_END // HY_CT
    cb = conv_b.reshape(1, -1)
    part = lambda k: pl.BlockSpec((2, SEQ, HY_CT), lambda c, q, o: (q, 0, u0 + k * N_CT + c))
    cw = lambda k: pl.BlockSpec((SHORT_CONV, HY_CT), lambda c, q, o: (0, k * N_CT + c))
    cbs = lambda k: pl.BlockSpec((1, HY_CT), lambda c, q, o: (0, k * N_CT + c))
    const = lambda: pl.BlockSpec((N_SPEC, 2 * SLAB, 2 * SLAB), lambda c, q, o: (0, 0, 0),
                                 pipeline_mode=pl.Buffered(1))
    return pl.pallas_call(
        _hyena_kernel,
        grid=(N_CT, b // 2, HYENA_ORDER),
        in_specs=[part(0), part(1), part(2), cw(0), cw(1), cw(2), cbs(0), cbs(1), cbs(2),
                  pl.BlockSpec((HYENA_ORDER, HY_CT), lambda c, q, o: (0, c)),
                  const(), const(),
                  pl.BlockSpec((None, None, N_SPEC * SLAB, 2 * HY_CT), lambda c, q, o: (o, c, 0, 0))],
        out_specs=pl.BlockSpec((2, SEQ, HY_CT), lambda c, q, o: (q, 0, c)),
        out_shape=jax.ShapeDtypeStruct((b, SEQ, HYENA_WIDTH), BF16),
        scratch_shapes=[pltpu.VMEM((N_IN, SLAB, 2 * HY_CT), BF16),
                        pltpu.VMEM((HYENA_ORDER, N_IN, SLAB, 2 * HY_CT), BF16),
                        pltpu.VMEM((N_HI, 2 * SLAB, HY_CT), BF16),
                        pltpu.VMEM((2 * (HYENA_ORDER + 1), SLAB + 2 * HALO, HY_CT), F32)],
        compiler_params=_cparams(("arbitrary", "arbitrary", "arbitrary"), 56),
        name="hyena_conv",
    )(p, p, p, conv_w, conv_w, conv_w, cb, cb, cb, skip, mf, mi, hspec)


def _ctx_hyena_kernel(uv_ref, u1_ref, u2_ref, cwv_ref, cw1_ref, cw2_ref, cbv_ref, cb1_ref, cb2_ref, skip_ref,
                      feat_ref, wf0_ref, wb0_ref, wf1_ref, wb1_ref, ad_ref, ff_ref, ai_ref, o_ref):
    l = CTX_LEN
    nb = uv_ref.shape[0]
    n = lax.broadcasted_iota(jnp.int32, (2 * l, HY_CT), 0)
    spectra = []
    for wf_ref, wb_ref in ((wf0_ref, wb0_ref), (wf1_ref, wb1_ref)):
        raw = jnp.concatenate(
            [jnp.dot(feat_ref[0:l, :], wf_ref[...], precision=HIGHEST, preferred_element_type=F32),
             jnp.dot(feat_ref[l:2 * l, :], wb_ref[...], precision=HIGHEST, preferred_element_type=F32)], axis=0)
        v = _filter_taps(raw, n, l, ad_ref[...])
        v = v / jnp.sum(jnp.abs(v), axis=0, keepdims=True)
        pq = jnp.dot(ff_ref[...], v.astype(BF16), preferred_element_type=F32)
        spectra.append((pq[:2 * l], pq[2 * l:]))

    row = lax.broadcasted_iota(jnp.int32, (l, HY_CT), 0)

    def conv(u_ref, bi, w_ref, b_ref):
        x = u_ref[bi].astype(F32)
        xm1 = jnp.where(row == 0, 0.0, pltpu.roll(x, 1, 0))
        xp1 = jnp.where(row == l - 1, 0.0, pltpu.roll(x, l - 1, 0))
        return xm1 * w_ref[0:1, :] + x * w_ref[1:2, :] + xp1 * w_ref[2:3, :] + b_ref[...]

    z = [conv(uv_ref, bi, cwv_ref, cbv_ref) for bi in range(nb)]
    gates = [[conv(u1_ref, bi, cw1_ref, cb1_ref) for bi in range(nb)],
             [conv(u2_ref, bi, cw2_ref, cb2_ref) for bi in range(nb)]]
    for o in range(HYENA_ORDER):
        hr, hi = spectra[o]
        pq = jnp.dot(ff_ref[:, 0:l], jnp.concatenate(z, axis=1).astype(BF16), preferred_element_type=F32)
        ys = []
        for q in range(nb // 2):
            xr, xi = _cplx_from_pq(pq[:, 2 * q * HY_CT:2 * (q + 1) * HY_CT], False)
            ys += [xr * hr - xi * hi, xr * hi + xi * hr]
        pq2 = jnp.dot(ai_ref[...], jnp.concatenate(ys, axis=1).astype(BF16), preferred_element_type=F32)
        y = []
        for q in range(nb // 2):
            y += list(_cplx_from_pq(pq2[:, 2 * q * HY_CT:2 * (q + 1) * HY_CT], False))
        z = [gates[o][bi] * (y[bi] + skip_ref[o:o + 1, :] * z[bi]) for bi in range(nb)]
    for bi in range(nb):
        o_ref[bi] = z[bi].astype(BF16)


def _ctx_hyena(pc, conv_w, conv_b, skip, feat, w_fout, absdelta, ff, ai):
    b, l, _ = pc.shape
    u0 = V_END // HY_CT
    cb = conv_b.reshape(1, -1)
    cols = 2 * N_CT
    part = lambda k: pl.BlockSpec((b, l, HY_CT), lambda c: (0, 0, u0 + k * N_CT + c))
    cw = lambda k: pl.BlockSpec((SHORT_CONV, HY_CT), lambda c: (0, k * N_CT + c))
    cbs = lambda k: pl.BlockSpec((1, HY_CT), lambda c: (0, k * N_CT + c))
    wcol = lambda off: pl.BlockSpec((FILTER_HIDDEN, HY_CT), lambda c: (0, off + c))
    return pl.pallas_call(
        _ctx_hyena_kernel,
        grid=(N_CT,),
        in_specs=[part(0), part(1), part(2), cw(0), cw(1), cw(2), cbs(0), cbs(1), cbs(2),
                  pl.BlockSpec((HYENA_ORDER, HY_CT), lambda c: (0, c)),
                  pl.BlockSpec((2 * l, FILTER_HIDDEN), lambda c: (0, 0)),
                  wcol(0), wcol(N_CT), wcol(cols), wcol(cols + N_CT),
                  pl.BlockSpec((1, HY_CT), lambda c: (0, c)),
                  pl.BlockSpec((4 * l, 2 * l), lambda c: (0, 0)),
                  pl.BlockSpec((2 * l, 2 * l), lambda c: (0, 0))],
        out_specs=pl.BlockSpec((b, l, HY_CT), lambda c: (0, 0, c)),
        out_shape=jax.ShapeDtypeStruct((b, l, HYENA_WIDTH), BF16),
        compiler_params=_cparams(("arbitrary",), 32),
        name="ctx_hyena",
    )(pc, pc, pc, conv_w, conv_w, conv_w, cb, cb, cb, skip, feat, w_fout, w_fout, w_fout, w_fout, absdelta, ff, ai)


def _rope_tables(l):
    rows = l // GRID_W
    row = np.repeat(np.arange(rows), GRID_W).astype(np.float64)
    col = np.tile(np.arange(GRID_W), rows).astype(np.float64)
    n_freq = HEAD_DIM // 4
    inv = ROPE_BASE ** (-np.arange(n_freq) / n_freq)
    ang_r = row[:, None] * inv
    ang_c = col[:, None] * inv
    cos = np.concatenate([np.cos(ang_r), np.cos(ang_r), np.cos(ang_c), np.cos(ang_c)], axis=-1)
    sin = np.concatenate([-np.sin(ang_r), np.sin(ang_r), -np.sin(ang_c), np.sin(ang_c)], axis=-1)
    return jnp.asarray(cos, F32), jnp.asarray(sin, F32)


def kernel(x, c, ctx, c_ctx, w_ada, b_ada, norm_mix, norm_mlp, w_in, q_norm, k_norm, attn_sink, conv_w, conv_b, filt_w1, filt_b1, filt_w2, filt_b2, filt_w3, filt_b3, filt_freq, filt_w_out, hyena_skip, w_out, w_mlp_in, w_mlp_out):
    cos, sin = _rope_tables(SEQ)
    emb, emb_ctx = _filter_positions(SEQ), _filter_positions(CTX_LEN)
    m_tall, mf, mi = _dft_constants()
    ff_ctx, ai_ctx = _ctx_dft_constants()
    absdelta = jnp.abs(jnp.linspace(MIN_DECAY, MAX_DECAY, HYENA_WIDTH, dtype=F32)).reshape(1, HYENA_WIDTH)
    cvec = jnp.zeros((MOD_ROWS, D_MODEL), F32).at[:BATCH].set(c).at[CTX_ROW].set(c_ctx)
    mod_all = _modulation(cvec, w_ada, b_ada).reshape(DEPTH, MOD_ROWS, N_MOD, D_MODEL)
    w_in_bf = w_in[0:1].astype(BF16)
    ctx_rows = BATCH * CTX_LEN
    flat = lambda a: a.reshape(1, ctx_rows, a.shape[-1])
    per_batch = lambda a: a.reshape(BATCH, CTX_LEN, a.shape[-1])
    xc = flat(ctx)
    for i in range(DEPTH):
        update_ctx = i < DEPTH - 1
        mod = mod_all[i]
        nmix = norm_mix[i].reshape(1, D_MODEL)
        nmlp = norm_mlp[i].reshape(1, D_MODEL)
        qn = q_norm[i].reshape(1, HEAD_DIM)
        kn = k_norm[i].reshape(1, HEAD_DIM)
        filt_mlp = (filt_w1[i], filt_b1[i], filt_w2[i], filt_b2[i], filt_w3[i], filt_b3[i], filt_freq[i])

        p, (w_out_bf, w1_bf, w2_bf) = _in_projection(
            x, mod, None, nmix, w_in_bf, qn, kn, cos, sin, tm=512, n_off=0, n_tiles=IN_COLS // IN_TN,
            use_rope=True, riders=[(w_out, i), (w_mlp_in, i), (w_mlp_out, i)])
        if update_ctx:
            pc, _ = _in_projection(xc, mod, CTX_ROW, nmix, w_in_bf, qn, kn, cos, sin,
                                   tm=512, n_off=0, n_tiles=IN_COLS // IN_TN, use_rope=False)
            kx_blk, vx_blk = Q_END // KV_COLS * 2, Q_END // KV_COLS * 2 + 1
        else:
            pc, _ = _in_projection(xc, mod, CTX_ROW, nmix, w_in_bf, qn, kn, cos, sin,
                                   tm=512, n_off=KV_TILE, n_tiles=1, use_rope=False)
            kx_blk, vx_blk = 0, 1
        pc = per_batch(pc)

        attn = _window_attention(p, pc, kx_blk, vx_blk, attn_sink[i])
        hspec = _filter_spectrum(_filter_features(emb, *filt_mlp), filt_w_out[i], absdelta, m_tall)
        hy = _hyena_conv(p, conv_w[i], conv_b[i], hyena_skip[i], mf, mi, hspec)
        x, h2, next_w_in = _out_projection(attn, hy, x, mod, None, nmlp, w_out_bf, tm=512,
                                           riders=[(w_in, i + 1)] if update_ctx else [])
        x = _mlp(h2, x, mod, None, w1_bf, w2_bf, tm=512, tf=1024)

        if update_ctx:
            attn_c = _context_attention(pc, attn_sink[i])
            hy_c = _ctx_hyena(pc, conv_w[i], conv_b[i], hyena_skip[i], _filter_features(emb_ctx, *filt_mlp),
                              filt_w_out[i], absdelta, ff_ctx, ai_ctx)
            xc, h2c, _ = _out_projection(flat(attn_c), flat(hy_c), xc, mod, CTX_ROW, nmlp, w_out_bf, tm=512)
            xc = _mlp(h2c, xc, mod, CTX_ROW, w1_bf, w2_bf, tm=512, tf=1024)
            w_in_bf = next_w_in[0]
    return x
```

```python
import cmath
import functools
import math

import jax
import jax.numpy as jnp
import numpy as np
from jax import lax
from jax.experimental import pallas as pl
from jax.experimental.pallas import tpu as pltpu

D_MODEL = 2048
BATCH = 4
SEQ = 4096
DEPTH = 2
GRID_W = 64
CTX_LEN = 256
HEAD_DIM = 128
N_Q_HEADS = 8
N_KV_HEADS = 2
GQA_GROUP = N_Q_HEADS // N_KV_HEADS
ATTN_WIDTH = N_Q_HEADS * HEAD_DIM
HYENA_WIDTH = D_MODEL - ATTN_WIDTH
HYENA_ORDER = 2
SHORT_CONV = 3
FILTER_EMB = 33
DECAY_TARGET = 1e-2
FAST_DECAY_PCT = 0.3
SLOW_DECAY_PCT = 1.5
MAX_DECAY = math.log(DECAY_TARGET) / FAST_DECAY_PCT
MIN_DECAY = math.log(DECAY_TARGET) / SLOW_DECAY_PCT
WINDOW = 128
BLOCK = 128
ROPE_BASE = 10000.0
D_FF = 4 * D_MODEL
N_MOD = 6
EPS = 1e-6
NEG_INF = -1e30
Q_END = ATTN_WIDTH
K_END = Q_END + N_KV_HEADS * HEAD_DIM
V_END = K_END + N_KV_HEADS * HEAD_DIM
IN_COLS = V_END + (HYENA_ORDER + 1) * HYENA_WIDTH

MOD_ROWS = 8
CTX_ROW = BATCH
KV_COLS = 2 * N_KV_HEADS * HEAD_DIM

F32 = jnp.float32
BF16 = jnp.bfloat16
MIB = 1024 * 1024


def _cparams(sem, vmem_mib):
    return pltpu.CompilerParams(dimension_semantics=sem, vmem_limit_bytes=vmem_mib * MIB)


def _mod_kernel(c_ref, w_ref, b_ref, o_ref):
    c = c_ref[...]
    s = (c * jax.nn.sigmoid(c)).astype(BF16)
    o_ref[...] = jnp.dot(s, w_ref[...].astype(BF16), preferred_element_type=F32) + b_ref[...]


def _modulation(cvec, w_ada, b_ada):
    tn = 1024
    n_cols = N_MOD * D_MODEL
    return pl.pallas_call(
        _mod_kernel,
        grid=(DEPTH, n_cols // tn),
        in_specs=[
            pl.BlockSpec((MOD_ROWS, D_MODEL), lambda l, j: (0, 0)),
            pl.BlockSpec((None, D_MODEL, tn), lambda l, j: (l, 0, j)),
            pl.BlockSpec((None, 1, tn), lambda l, j: (l, 0, j)),
        ],
        out_specs=pl.BlockSpec((None, MOD_ROWS, tn), lambda l, j: (l, 0, j)),
        out_shape=jax.ShapeDtypeStruct((DEPTH, MOD_ROWS, n_cols), F32),
        compiler_params=_cparams(("arbitrary", "arbitrary"), 40),
        name="adaln_mod",
    )(cvec, w_ada, b_ada.reshape(DEPTH, 1, n_cols))


IN_TN = 512
Q_TILES = Q_END // IN_TN
KV_TILE = Q_TILES
HEADS_PER_TILE = IN_TN // HEAD_DIM


SUB_ROWS = 256


def _rms_modulate(xf, gain, shift):
    ms = jnp.mean(xf * xf, axis=-1, keepdims=True)
    return xf * lax.rsqrt(ms + EPS) * gain + shift


def _head_norm_rope(a, g, cos, sin):
    ms = jnp.mean(a * a, axis=-1, keepdims=True)
    y = a * lax.rsqrt(ms + EPS) * g
    if cos is None:
        return y
    lane = lax.broadcasted_iota(jnp.int32, y.shape, 1)
    partner = jnp.where((lane % 64) < 32, pltpu.roll(y, 96, 1), pltpu.roll(y, 32, 1))
    return y * cos + partner * sin


BF16_TILE_ROWS = 16


def _rider_plan(w, layer, steps, step_of):
    depth, r, c = w.shape
    assert r % (steps * BF16_TILE_ROWS) == 0
    blk = r // steps
    in_spec = pl.BlockSpec((blk, c), lambda *g: (layer * steps + step_of(*g), 0))
    out_spec = pl.BlockSpec((blk, c), lambda *g: (step_of(*g), 0))
    restore = lambda a: a.reshape(1, r, c)
    return w.reshape(depth * r, c), in_spec, out_spec, jax.ShapeDtypeStruct((r, c), BF16), restore


def _cast_riders(in_refs, out_refs):
    for src, dst in zip(in_refs, out_refs):
        dst[...] = src[...].astype(BF16)


def _inproj_kernel(*refs, n_off, n_tiles, use_rope, n_riders):
    x_ref, m_ref, nw_ref, w_ref, qn_ref, kn_ref, cos_ref, sin_ref = refs[:8]
    o_ref = refs[8 + n_riders]
    _cast_riders(refs[8:8 + n_riders], refs[9 + n_riders:])
    gain = nw_ref[...] * (1.0 + m_ref[1:2, :])
    shift = m_ref[0:1, :]
    for r0 in range(0, x_ref.shape[0], SUB_ROWS):
        rows = slice(r0, r0 + SUB_ROWS)
        h = _rms_modulate(x_ref[rows, :], gain, shift).astype(BF16)
        cos = cos_ref[rows, :] if use_rope else None
        sin = sin_ref[rows, :] if use_rope else None
        for n in range(n_tiles):
            cols = slice(n * IN_TN, (n + 1) * IN_TN)
            acc = jnp.dot(h, w_ref[:, cols], preferred_element_type=F32)
            t = n + n_off
            if t > KV_TILE:
                o_ref[rows, cols] = acc.astype(BF16)
                continue
            gains = [qn_ref[...]] * HEADS_PER_TILE if t < Q_TILES else [kn_ref[...]] * N_KV_HEADS + [None] * N_KV_HEADS
            for c, g in enumerate(gains):
                sl = slice(c * HEAD_DIM, (c + 1) * HEAD_DIM)
                osl = slice(n * IN_TN + c * HEAD_DIM, n * IN_TN + (c + 1) * HEAD_DIM)
                if g is None:
                    o_ref[rows, osl] = acc[:, sl].astype(BF16)
                else:
                    o_ref[rows, osl] = _head_norm_rope(acc[:, sl], g, cos, sin).astype(BF16)


def _in_projection(x, mod, mod_row, nw, w_bf, qn, kn, cos, sin, *, tm, n_off, n_tiles, use_rope, riders=()):
    b, t, _ = x.shape
    assert n_off % n_tiles == 0 and tm % SUB_ROWS == 0
    row = (lambda bb: bb) if mod_row is None else (lambda bb: mod_row)
    n_i = t // tm
    plans = [_rider_plan(w, layer, b * n_i, lambda bb, i: bb * n_i + i) for w, layer in riders]
    kern = functools.partial(_inproj_kernel, n_off=n_off, n_tiles=n_tiles, use_rope=use_rope, n_riders=len(plans))
    width = n_tiles * IN_TN
    out = pl.pallas_call(
        kern,
        grid=(b, n_i),
        in_specs=[
            pl.BlockSpec((None, tm, D_MODEL), lambda bb, i: (bb, i, 0)),
            pl.BlockSpec((None, N_MOD, D_MODEL), lambda bb, i: (row(bb), 0, 0)),
            pl.BlockSpec((1, D_MODEL), lambda bb, i: (0, 0)),
            pl.BlockSpec((None, D_MODEL, width), lambda bb, i: (0, 0, n_off // n_tiles),
                         pipeline_mode=pl.Buffered(1)),
            pl.BlockSpec((1, HEAD_DIM), lambda bb, i: (0, 0)),
            pl.BlockSpec((1, HEAD_DIM), lambda bb, i: (0, 0)),
            pl.BlockSpec((tm, HEAD_DIM), lambda bb, i: (i, 0)),
            pl.BlockSpec((tm, HEAD_DIM), lambda bb, i: (i, 0)),
        ] + [pl_[1] for pl_ in plans],
        out_specs=[pl.BlockSpec((None, tm, width), lambda bb, i: (bb, i, 0))] + [pl_[2] for pl_ in plans],
        out_shape=[jax.ShapeDtypeStruct((b, t, width), BF16)] + [pl_[3] for pl_ in plans],
        compiler_params=_cparams(("arbitrary", "arbitrary"), 56),
        name="in_proj",
    )(x, mod, nw, w_bf, qn, kn, cos, sin, *[pl_[0] for pl_ in plans])
    return out[0], [pl_[4](a) for pl_, a in zip(plans, out[1:])]


def _sink_column(sink_ref, kv, rows_per_head):
    rows = GQA_GROUP * rows_per_head
    head = lax.broadcasted_iota(jnp.int32, (rows, 1), 0) // rows_per_head
    col = jnp.full((rows, 1), sink_ref[kv * GQA_GROUP], F32)
    for g in range(1, GQA_GROUP):
        col = jnp.where(head == g, sink_ref[kv * GQA_GROUP + g], col)
    return col


def _stack_heads(q_ref, kv, rows=slice(None)):
    return jnp.concatenate(
        [q_ref[rows, (kv * GQA_GROUP + g) * HEAD_DIM:(kv * GQA_GROUP + g + 1) * HEAD_DIM]
         for g in range(GQA_GROUP)], axis=0)


LOG2E = math.log2(math.e)
QK_LOG2_SCALE = HEAD_DIM ** -0.5 * LOG2E


def _softmax_pv(q4, k_all, v_all, bias, sink_col):
    t = lax.dot_general(q4, k_all, (((1,), (1,)), ((), ())), preferred_element_type=F32) * QK_LOG2_SCALE
    if bias is not None:
        t = t + bias
    sink2 = sink_col * LOG2E
    m = jnp.maximum(jnp.max(t, axis=-1, keepdims=True), sink2)
    e = jnp.exp2(t - m)
    denom = jnp.exp2(sink2 - m) + jnp.sum(e, axis=-1, keepdims=True)
    return jnp.dot(e.astype(BF16), v_all, preferred_element_type=F32) / denom


Q_BLOCKS = 4


def _window_attn_kernel(sink_ref, bias_first_ref, bias_mid_ref, bias_last_ref, q_ref, kp_ref, km_ref, kn_ref,
                        vp_ref, vm_ref, vn_ref, kx_ref, vx_ref, o_ref):
    for blk in range(Q_BLOCKS):
        bias_ref = bias_first_ref if blk == 0 else bias_last_ref if blk == Q_BLOCKS - 1 else bias_mid_ref
        rows = slice(blk * BLOCK, (blk + 1) * BLOCK)
        for kv in range(N_KV_HEADS):
            ks = slice(kv * HEAD_DIM, (kv + 1) * HEAD_DIM)

            def band(prev_ref, mid_ref, next_ref):
                pieces = [prev_ref[:, ks]] + [mid_ref[j * BLOCK:(j + 1) * BLOCK, ks] for j in range(Q_BLOCKS)]
                return (pieces + [next_ref[:, ks]])[blk:blk + 3]

            k_all = jnp.concatenate([kx_ref[:, ks]] + band(kp_ref, km_ref, kn_ref), axis=0)
            v_all = jnp.concatenate([vx_ref[:, ks]] + band(vp_ref, vm_ref, vn_ref), axis=0)
            o = _softmax_pv(_stack_heads(q_ref, kv, rows), k_all, v_all, bias_ref[...],
                            _sink_column(sink_ref, kv, BLOCK))
            for g in range(GQA_GROUP):
                h = kv * GQA_GROUP + g
                o_ref[rows, h * HEAD_DIM:(h + 1) * HEAD_DIM] = o[g * BLOCK:(g + 1) * BLOCK].astype(BF16)


def _window_bias(nb):
    qi = np.arange(GQA_GROUP * BLOCK)[:, None] % BLOCK
    sj = np.arange(3 * BLOCK)[None, :] - BLOCK
    near = np.abs(sj - qi) <= WINDOW
    out = np.zeros((3, GQA_GROUP * BLOCK, CTX_LEN + 3 * BLOCK), np.float32)
    for variant, blk in enumerate((0, 1, nb - 1)):
        key_pos = blk * BLOCK + sj
        ok = near & (key_pos >= 0) & (key_pos < nb * BLOCK)
        out[variant, :, CTX_LEN:] = np.where(ok, 0.0, NEG_INF)
    return jnp.asarray(out)


def _window_attention(p, pc, kx_blk, vx_blk, sink):
    b, seq, _ = p.shape
    nb = seq // BLOCK
    kvw = N_KV_HEADS * HEAD_DIM
    k_blk, v_blk = Q_END // kvw, K_END // kvw
    assert nb % Q_BLOCKS == 0 and Q_BLOCKS >= 2
    steps = nb // Q_BLOCKS
    prev = lambda bb, i: (bb, jnp.maximum(Q_BLOCKS * i - 1, 0))
    nxt = lambda bb, i: (bb, jnp.minimum(Q_BLOCKS * (i + 1), nb - 1))

    def edge(which, blk):
        return pl.BlockSpec((None, BLOCK, kvw), lambda bb, i: which(bb, i) + (blk,))

    def mid(blk):
        return pl.BlockSpec((None, Q_BLOCKS * BLOCK, kvw), lambda bb, i: (bb, i, blk))

    bias_shape = (None, GQA_GROUP * BLOCK, CTX_LEN + 3 * BLOCK)
    bias = _window_bias(nb)
    return pl.pallas_call(
        _window_attn_kernel,
        grid=(b, steps),
        in_specs=[
            pl.BlockSpec(memory_space=pltpu.SMEM),
            pl.BlockSpec(bias_shape, lambda bb, i: (jnp.where(i == 0, 0, 1), 0, 0)),
            pl.BlockSpec(bias_shape, lambda bb, i: (1, 0, 0)),
            pl.BlockSpec(bias_shape, lambda bb, i: (jnp.where(i == steps - 1, 2, 1), 0, 0)),
            pl.BlockSpec((None, Q_BLOCKS * BLOCK, ATTN_WIDTH), lambda bb, i: (bb, i, 0)),
            edge(prev, k_blk), mid(k_blk), edge(nxt, k_blk),
            edge(prev, v_blk), mid(v_blk), edge(nxt, v_blk),
            pl.BlockSpec((None, CTX_LEN, kvw), lambda bb, i: (bb, 0, kx_blk)),
            pl.BlockSpec((None, CTX_LEN, kvw), lambda bb, i: (bb, 0, vx_blk)),
        ],
        out_specs=pl.BlockSpec((None, Q_BLOCKS * BLOCK, ATTN_WIDTH), lambda bb, i: (bb, i, 0)),
        out_shape=jax.ShapeDtypeStruct((b, seq, ATTN_WIDTH), BF16),
        compiler_params=_cparams(("arbitrary", "arbitrary"), 32),
        name="window_attn",
    )(sink, bias, bias, bias, p, p, p, p, p, p, p, pc, pc)


def _ctx_attn_kernel(sink_ref, q_ref, k_ref, v_ref, o_ref):
    n = q_ref.shape[0]
    for kv in range(N_KV_HEADS):
        ks = slice(kv * HEAD_DIM, (kv + 1) * HEAD_DIM)
        o = _softmax_pv(_stack_heads(q_ref, kv), k_ref[:, ks], v_ref[:, ks], None, _sink_column(sink_ref, kv, n))
        for g in range(GQA_GROUP):
            h = kv * GQA_GROUP + g
            o_ref[:, h * HEAD_DIM:(h + 1) * HEAD_DIM] = o[g * n:(g + 1) * n].astype(BF16)


def _context_attention(pc, sink):
    b, n, _ = pc.shape
    kvw = N_KV_HEADS * HEAD_DIM
    return pl.pallas_call(
        _ctx_attn_kernel,
        grid=(b,),
        in_specs=[
            pl.BlockSpec(memory_space=pltpu.SMEM),
            pl.BlockSpec((None, n, ATTN_WIDTH), lambda bb: (bb, 0, 0)),
            pl.BlockSpec((None, n, kvw), lambda bb: (bb, 0, Q_END // kvw)),
            pl.BlockSpec((None, n, kvw), lambda bb: (bb, 0, K_END // kvw)),
        ],
        out_specs=pl.BlockSpec((None, n, ATTN_WIDTH), lambda bb: (bb, 0, 0)),
        out_shape=jax.ShapeDtypeStruct((b, n, ATTN_WIDTH), BF16),
        compiler_params=_cparams(("arbitrary",), 32),
        name="ctx_attn",
    )(sink, pc, pc, pc)


def _outproj_kernel(*refs, n_riders):
    a_ref, y_ref, x_ref, m_ref, nw_ref, wa_ref, wy_ref = refs[:7]
    o_ref, h_ref = refs[7 + n_riders], refs[8 + n_riders]
    _cast_riders(refs[7:7 + n_riders], refs[9 + n_riders:])
    gain = nw_ref[...] * (1.0 + m_ref[4:5, :])
    shift = m_ref[3:4, :]
    for r0 in range(0, x_ref.shape[0], SUB_ROWS):
        rows = slice(r0, r0 + SUB_ROWS)
        mix = (jnp.dot(a_ref[rows, :], wa_ref[...], preferred_element_type=F32)
               + jnp.dot(y_ref[rows, :], wy_ref[...], preferred_element_type=F32))
        x1 = x_ref[rows, :] + m_ref[2:3, :] * mix
        o_ref[rows, :] = x1
        h_ref[rows, :] = _rms_modulate(x1, gain, shift).astype(BF16)


def _out_projection(attn, hy, x, mod, mod_row, nw, w_bf, *, tm, riders=()):
    b, t, _ = x.shape
    assert tm % SUB_ROWS == 0
    n_i = t // tm
    row = (lambda bb: bb) if mod_row is None else (lambda bb: mod_row)
    tile = lambda width: pl.BlockSpec((None, tm, width), lambda bb, i: (bb, i, 0))
    half = lambda k: pl.BlockSpec((None, D_MODEL // 2, D_MODEL), lambda bb, i: (0, k, 0),
                                  pipeline_mode=pl.Buffered(1))
    plans = [_rider_plan(w, layer, b * n_i, lambda bb, i: bb * n_i + i) for w, layer in riders]
    out = pl.pallas_call(
        functools.partial(_outproj_kernel, n_riders=len(plans)),
        grid=(b, n_i),
        in_specs=[
            tile(ATTN_WIDTH), tile(HYENA_WIDTH), tile(D_MODEL),
            pl.BlockSpec((None, N_MOD, D_MODEL), lambda bb, i: (row(bb), 0, 0)),
            pl.BlockSpec((1, D_MODEL), lambda bb, i: (0, 0)),
            half(0), half(1),
        ] + [pl_[1] for pl_ in plans],
        out_specs=[tile(D_MODEL), tile(D_MODEL)] + [pl_[2] for pl_ in plans],
        out_shape=[jax.ShapeDtypeStruct(x.shape, F32), jax.ShapeDtypeStruct(x.shape, BF16)]
        + [pl_[3] for pl_ in plans],
        compiler_params=_cparams(("arbitrary", "arbitrary"), 48),
        name="out_proj",
    )(attn, hy, x, mod, nw, w_bf, w_bf, *[pl_[0] for pl_ in plans])
    return out[0], out[1], [pl_[4](a) for pl_, a in zip(plans, out[2:])]


MLP_OUT_CHUNK = 512


def _mlp_kernel(h_ref, x_ref, m_ref, w1_ref, w2_ref, o_ref, *, n_f):
    f = pl.program_id(2)

    @pl.when(f == 0)
    def _():
        o_ref[...] = jnp.zeros_like(o_ref)

    a = jnp.maximum(jnp.dot(h_ref[...], w1_ref[...], preferred_element_type=F32), 0.0)
    a = (a * a).astype(BF16)
    for n0 in range(0, D_MODEL, MLP_OUT_CHUNK):
        sl = slice(n0, n0 + MLP_OUT_CHUNK)
        o_ref[:, sl] += jnp.dot(a, w2_ref[:, sl], preferred_element_type=F32)

    @pl.when(f == n_f - 1)
    def _():
        o_ref[...] = x_ref[...] + m_ref[5:6, :] * o_ref[...]


def _mlp(h, x, mod, mod_row, w1_bf, w2_bf, *, tm, tf):
    b, t, _ = x.shape
    n_f = D_FF // tf
    row = (lambda bb: bb) if mod_row is None else (lambda bb: mod_row)
    tile = pl.BlockSpec((None, tm, D_MODEL), lambda bb, i, f: (bb, i, 0))
    return pl.pallas_call(
        functools.partial(_mlp_kernel, n_f=n_f),
        grid=(b, t // tm, n_f),
        in_specs=[
            tile, tile,
            pl.BlockSpec((None, N_MOD, D_MODEL), lambda bb, i, f: (row(bb), 0, 0)),
            pl.BlockSpec((None, D_MODEL, tf), lambda bb, i, f: (0, 0, f)),
            pl.BlockSpec((None, tf, D_MODEL), lambda bb, i, f: (0, f, 0)),
        ],
        out_specs=tile,
        out_shape=jax.ShapeDtypeStruct(x.shape, F32),
        compiler_params=_cparams(("arbitrary", "arbitrary", "arbitrary"), 56),
        name="mlp",
    )(h, x, mod, w1_bf, w2_bf)


def _tile_const(k, like):
    return jnp.full(like.shape, k, F32).astype(like.dtype)


def _p_add(p, q):
    if p is None:
        return q
    if q is None:
        return p
    (a, sa), (b, sb) = p, q
    if sa == sb:
        return (a + b, sa)
    return (a - b, 1) if sa > 0 else (b - a, 1)


def _p_scale(p, k):
    if p is None or k == 0.0:
        return None
    a, s = p
    if k < 0:
        s, k = -s, -k
    return (a, s) if k == 1.0 else (a * _tile_const(k, a), s)


def _c_add(x, y, sign=1):
    if y is None:
        return x
    if x is None:
        x = (None, None)
    return (_p_add(x[0], _p_scale(y[0], sign)), _p_add(x[1], _p_scale(y[1], sign)))


def _snap(v):
    for t in (0.0, 1.0, -1.0):
        if abs(v - t) < 1e-12:
            return t
    return v


def _c_mulc(x, c):
    if x is None:
        return None
    cr, ci = _snap(c.real), _snap(c.imag)
    re, im = x
    if cr != 0.0 and ci != 0.0 and abs(abs(cr) - abs(ci)) < 1e-12:
        k, sr, si = abs(cr), math.copysign(1.0, cr), math.copysign(1.0, ci)
        return (_p_scale(_p_add(_p_scale(re, sr), _p_scale(im, -si)), k),
                _p_scale(_p_add(_p_scale(re, si), _p_scale(im, sr)), k))
    return (_p_add(_p_scale(re, cr), _p_scale(im, -ci)), _p_add(_p_scale(re, ci), _p_scale(im, cr)))


def _is_zero(x):
    return x is None or (x[0] is None and x[1] is None)


def _slab_fft_dit(xs, sign, need):
    n = len(xs)
    if n == 1:
        return {0: xs[0]}
    if all(_is_zero(x) for x in xs):
        return {k: None for k in need}
    h = n // 2
    sub = {k % h for k in need}
    ev = _slab_fft_dit(xs[0::2], sign, sub)
    od = _slab_fft_dit(xs[1::2], sign, sub)
    out = {}
    for k in sub:
        t = _c_mulc(od[k], cmath.exp(sign * 2j * math.pi * k / n))
        if k in need:
            out[k] = _c_add(ev[k], t)
        if k + h in need:
            out[k + h] = _c_add(ev[k], t, -1)
    return out


def _slab_fft_dif(xs, sign, need):
    n = len(xs)
    if n == 1:
        return {0: xs[0]}
    h = n // 2
    need_e = {k // 2 for k in need if k % 2 == 0}
    need_o = {k // 2 for k in need if k % 2 == 1}
    out = {}
    if need_e:
        ev = _slab_fft_dif([_c_add(xs[j], xs[j + h]) for j in range(h)], sign, need_e)
        out.update({2 * k: v for k, v in ev.items()})
    if need_o:
        dif = [_c_mulc(_c_add(xs[j], xs[j + h], -1), cmath.exp(sign * 2j * math.pi * j / n)) for j in range(h)]
        od = _slab_fft_dif(dif, sign, need_o)
        out.update({2 * k + 1: v for k, v in od.items()})
    return out


def _part_value(p, like):
    if p is None:
        return jnp.zeros_like(like)
    return p[0] if p[1] > 0 else -p[0]


HY_CT = 128
SLAB = 256
N_HI = 2 * SEQ // SLAB
N_IN = SEQ // SLAB
N_SPEC = N_HI // 2 + 1
ROW_CHUNK = 8
HY_ROWS = 16
N_CT = HYENA_WIDTH // HY_CT
SPEC_GROUP = 4
FILTER_HIDDEN = 64
EMB_PAD = 128
HIGHEST = lax.Precision.HIGHEST


def _dft_constants():
    n = 2 * SEQ
    r = np.arange(SLAB)
    tall = np.zeros((N_SPEC, 2 * SLAB, SLAB))
    fwd = np.zeros((N_SPEC, 2 * SLAB, 2 * SLAB))
    inv = np.zeros((N_SPEC, 2 * SLAB, 2 * SLAB))
    for b in range(N_SPEC):
        m = np.exp(-2j * np.pi * (np.outer(r, r) / SLAB + (r * b)[None, :] / n))
        a = m.T / n
        tall[b] = np.concatenate([m.real, m.imag], axis=0)
        fwd[b] = np.block([[m.real, -m.imag], [m.imag, m.real]])
        inv[b] = np.block([[a.real, a.imag], [-a.imag, a.real]])
    return tuple(jnp.asarray(t, F32).astype(BF16) for t in (tall, fwd, inv))


def _ctx_dft_constants():
    n = 2 * CTX_LEN
    k = np.arange(n)
    f = np.exp(-2j * np.pi * np.outer(k, k) / n)
    a = np.conj(f)[:CTX_LEN, :] / n
    return (jnp.asarray(np.concatenate([f.real, f.imag], axis=0), F32).astype(BF16),
            jnp.asarray(np.concatenate([a.real, a.imag], axis=0), F32).astype(BF16))


def _filter_positions(l):
    t = np.linspace(0.0, 1.0, l)[:, None]
    bands = (FILTER_EMB - 1) // 2
    f = np.linspace(1e-4, bands - 1, bands)[None, :]
    w = 2.0 * math.pi * np.arange(l)[:, None] / l
    z = np.concatenate([t, np.cos(f * w), -np.sin(f * w)], axis=-1)
    z2 = np.concatenate([z, z[0:1], z[1:][::-1]], axis=0)
    return jnp.asarray(np.pad(z2, ((0, 0), (0, EMB_PAD - FILTER_EMB))), F32)


def _filter_feat_kernel(za_ref, zb_ref, w1_ref, b1_ref, w2_ref, b2_ref, w3_ref, b3_ref, fr_ref, o_ref):
    fr = fr_ref[...]
    dot = functools.partial(jnp.dot, precision=HIGHEST, preferred_element_type=F32)
    z = jnp.concatenate([za_ref[...], zb_ref[...]], axis=1)
    h = jnp.sin(fr * (dot(z, w1_ref[...]) + b1_ref[...]))
    h = jnp.sin(fr * (dot(h, w2_ref[...]) + b2_ref[...]))
    o_ref[...] = jnp.sin(fr * (dot(h, w3_ref[...]) + b3_ref[...]))


def _filter_features(emb, w1, b1, w2, b2, w3, b3, freq):
    l = emb.shape[0] // 2
    tr = min(l, 1024)
    nblk = l // tr
    diag2 = lambda w: jnp.kron(jnp.eye(2, dtype=F32), w)
    w1p = jnp.pad(w1, ((0, EMB_PAD - FILTER_EMB), (0, 0)))
    vec = lambda v: jnp.tile(v.reshape(1, FILTER_HIDDEN), (1, 2))
    full = lambda shape: pl.BlockSpec(shape, lambda i: (0, 0))
    wide = 2 * FILTER_HIDDEN
    return pl.pallas_call(
        _filter_feat_kernel,
        grid=(nblk,),
        in_specs=[pl.BlockSpec((tr, EMB_PAD), lambda i: (i, 0)),
                  pl.BlockSpec((tr, EMB_PAD), lambda i: (i + nblk, 0)),
                  full((2 * EMB_PAD, wide)), full((1, wide)),
                  full((wide, wide)), full((1, wide)),
                  full((wide, wide)), full((1, wide)),
                  full((1, wide))],
        out_specs=pl.BlockSpec((tr, wide), lambda i: (i, 0)),
        out_shape=jax.ShapeDtypeStruct((l, wide), F32),
        compiler_params=_cparams(("arbitrary",), 32),
        name="filter_feat",
    )(emb, emb, diag2(w1p), vec(b1), diag2(w2), vec(b2), diag2(w3), vec(b3), vec(freq))


def _filter_taps(raw, n, l, absdelta):
    m = jnp.where(n < l, n, 2 * l - n)
    t = m.astype(F32) / (l - 1)
    return jnp.where(n == l, 0.0, raw * jnp.exp(-t * absdelta))


def _cplx_from_pq(pq, conj):
    half = pq.shape[0] // 2
    pr, pi_ = pq[:half, :HY_CT], pq[:half, HY_CT:]
    qr, qi = pq[half:, :HY_CT], pq[half:, HY_CT:]
    if conj:
        return pr + qi, pi_ - qr
    return pr - qi, qr + pi_


def _slab_forward(src_ref, n_in, real_only, dst_ref, need, scale):
    def body(rc, carry):
        rows = pl.ds(pl.multiple_of(rc * ROW_CHUNK, ROW_CHUNK), ROW_CHUNK)
        xs = []
        for a in range(N_HI):
            if a >= n_in:
                xs.append(None)
                continue
            re = src_ref[a, rows, 0:HY_CT]
            if scale is not None:
                re = re * scale
            xs.append(((re, 1), None if real_only else (src_ref[a, rows, HY_CT:2 * HY_CT], 1)))
        like = xs[0][0][0]
        out = _slab_fft_dit(xs, -1, need)
        for b in sorted(need):
            c = out[b] or (None, None)
            dst_ref[b, rows, 0:HY_CT] = _part_value(c[0], like)
            dst_ref[b, rows, HY_CT:2 * HY_CT] = _part_value(c[1], like)
        return carry

    lax.fori_loop(0, SLAB // ROW_CHUNK, body, 0)


def _filter_spec_kernel(feat_ref, wf_ref, wb_ref, ad_ref, mf_ref, o_ref, ft_ref, s_ref):
    rate = ad_ref[...] * (-1.0 / (SEQ - 1))
    row = lax.broadcasted_iota(jnp.int32, (SLAB, HY_CT), 0)
    row_f = row.astype(F32)
    first_row = row == 0
    no_w = jnp.zeros((FILTER_HIDDEN, HY_CT), F32)
    w_fwd = jnp.concatenate([wf_ref[...], no_w], axis=0).astype(BF16)
    w_bwd = jnp.concatenate([no_w, wb_ref[...]], axis=0).astype(BF16)

    def taps(a, acc):
        fwd = a < N_IN
        src = jnp.where(fwd, a, a - N_IN)
        rows = feat_ref[pl.ds(pl.multiple_of(src * SLAB, SLAB), SLAB), :]
        raw = jnp.dot(rows.astype(BF16), jnp.where(fwd, w_fwd, w_bwd), preferred_element_type=F32)
        base = jnp.where(fwd, a * SLAB, 2 * SEQ - a * SLAB).astype(F32)
        step = jnp.where(fwd, 1.0, -1.0)
        v = raw * jnp.exp((base + step * row_f) * rate)
        v = jnp.where(jnp.logical_and(a == N_IN, first_row), 0.0, v)
        ft_ref[a] = v
        return acc + jnp.sum(jnp.abs(v), axis=0, keepdims=True)

    norm = lax.fori_loop(0, N_HI, taps, jnp.zeros((1, HY_CT), F32), unroll=2)
    _slab_forward(ft_ref, N_HI, True, s_ref, set(range(N_SPEC)), 1.0 / norm)

    def spec(b):
        pq = jnp.dot(mf_ref[b], s_ref[b].astype(BF16), preferred_element_type=F32)
        xr, xi = _cplx_from_pq(pq, False)
        rows = pl.ds(pl.multiple_of(b * SLAB, SLAB), SLAB)
        o_ref[rows, 0:HY_CT] = xr.astype(BF16)
        o_ref[rows, HY_CT:2 * HY_CT] = xi.astype(BF16)

    def spec_group(i, carry):
        for t in range(SPEC_GROUP):
            spec(1 + SPEC_GROUP * i + t)
        return carry

    spec(0)
    lax.fori_loop(0, (N_SPEC - 1) // SPEC_GROUP, spec_group, 0)


def _filter_spectrum(feat, w_fout, absdelta, mf):
    cols = 2 * N_CT
    return pl.pallas_call(
        _filter_spec_kernel,
        grid=(HYENA_ORDER, N_CT),
        in_specs=[
            pl.BlockSpec((SEQ, 2 * FILTER_HIDDEN), lambda o, c: (0, 0)),
            pl.BlockSpec((FILTER_HIDDEN, HY_CT), lambda o, c: (0, o * cols + c)),
            pl.BlockSpec((FILTER_HIDDEN, HY_CT), lambda o, c: (0, o * cols + N_CT + c)),
            pl.BlockSpec((1, HY_CT), lambda o, c: (0, c)),
            pl.BlockSpec((N_SPEC, 2 * SLAB, SLAB), lambda o, c: (0, 0, 0)),
        ],
        out_specs=pl.BlockSpec((None, None, N_SPEC * SLAB, 2 * HY_CT), lambda o, c: (o, c, 0, 0)),
        out_shape=jax.ShapeDtypeStruct((HYENA_ORDER, N_CT, N_SPEC * SLAB, 2 * HY_CT), BF16),
        scratch_shapes=[pltpu.VMEM((N_HI, SLAB, HY_CT), F32), pltpu.VMEM((N_SPEC, SLAB, 2 * HY_CT), F32)],
        compiler_params=_cparams(("arbitrary", "arbitrary"), 48),
        name="filter_spec",
    )(feat, w_fout, w_fout, absdelta, mf)


HALO = 8


def _conv_slab(u_ref, bi, a, w, b, st):
    r0 = a * SLAB
    x = u_ref[bi, pl.ds(pl.multiple_of(r0, SLAB), SLAB), :].astype(F32)
    before = u_ref[bi, pl.ds(pl.multiple_of(jnp.maximum(r0 - 16, 0), 16), 16), :].astype(F32)
    after = u_ref[bi, pl.ds(pl.multiple_of(jnp.minimum(r0 + SLAB, SEQ - 16), 16), 16), :].astype(F32)
    st[0:HALO, :] = jnp.where(a > 0, before[16 - HALO:16, :], 0.0)
    st[HALO:HALO + SLAB, :] = x
    st[HALO + SLAB:2 * HALO + SLAB, :] = jnp.where(a < N_IN - 1, after[0:HALO, :], 0.0)
    xm1 = st[HALO - 1:HALO - 1 + SLAB, :]
    xp1 = st[HALO + 1:HALO + 1 + SLAB, :]
    return xm1 * w[0:1, :] + x * w[1:2, :] + xp1 * w[2:3, :] + b


def _hyena_kernel(uv_ref, u1_ref, u2_ref, cwv_ref, cw1_ref, cw2_ref, cbv_ref, cb1_ref, cb2_ref, skip_ref,
                  mf_ref, mi_ref, h_ref, o_ref, zc_ref, g_ref, s_ref, st_ref):
    o = pl.program_id(2)

    @pl.when(o == 0)
    def _():
        def conv(a, carry):
            for bi in range(2):
                lanes = slice(bi * HY_CT, (bi + 1) * HY_CT)
                st = lambda k: st_ref.at[3 * bi + k]
                zc_ref[a, :, lanes] = _conv_slab(uv_ref, bi, a, cwv_ref[...], cbv_ref[...], st(0)).astype(BF16)
                g_ref[0, a, :, lanes] = _conv_slab(u1_ref, bi, a, cw1_ref[...], cb1_ref[...], st(1)).astype(BF16)
                g_ref[1, a, :, lanes] = _conv_slab(u2_ref, bi, a, cw2_ref[...], cb2_ref[...], st(2)).astype(BF16)
            return carry

        lax.fori_loop(0, N_IN, conv, 0)

    def im_sign(b):
        return -1 if b > N_HI // 2 else 1

    def forward_chunk(rc, carry):
        rows = pl.ds(pl.multiple_of(rc * HY_ROWS, HY_ROWS), HY_ROWS)
        rows_im = pl.ds(pl.multiple_of(SLAB + rc * HY_ROWS, HY_ROWS), HY_ROWS)
        xs = [((zc_ref[a, rows, 0:HY_CT], 1), (zc_ref[a, rows, HY_CT:2 * HY_CT], 1)) for a in range(N_IN)]
        like = xs[0][0][0]
        out = _slab_fft_dit(xs + [None] * (N_HI - N_IN), -1, set(range(N_HI)))
        for b in range(N_HI):
            re, im = out[b]
            s_ref[b, rows, :] = _part_value(re, like)
            s_ref[b, rows_im, :] = _part_value(None if im is None else (im[0], im[1] * im_sign(b)), like)
        return carry

    lax.fori_loop(0, SLAB // HY_ROWS, forward_chunk, 0)

    def through_filter(slabs, j):
        h = h_ref[j * SLAB:(j + 1) * SLAB, :].astype(F32)
        hr = jnp.concatenate([h[:, :HY_CT]] * len(slabs), axis=1)
        hi = jnp.concatenate([h[:, HY_CT:]] * len(slabs), axis=1)
        d = s_ref[slabs[0]] if len(slabs) == 1 else jnp.concatenate([s_ref[b] for b in slabs], axis=1)
        x = jnp.dot(mf_ref[j], d, preferred_element_type=F32)
        xr, xi = x[:SLAB], x[SLAB:]
        y = jnp.concatenate([xr * hr - xi * hi, xr * hi + xi * hr], axis=0).astype(BF16)
        w = jnp.dot(mi_ref[j], y, preferred_element_type=F32).astype(BF16)
        return [(b, w[:, k * HY_CT:(k + 1) * HY_CT]) for k, b in enumerate(slabs)]

    for slabs, j in [([0], 0), ([N_HI // 2], N_HI // 2)] + [([j, N_HI - j], j) for j in range(1, N_HI // 2)]:
        for b, w in through_filter(slabs, j):
            s_ref[b] = w

    def inverse_chunk(rc, carry):
        rows = pl.ds(pl.multiple_of(rc * HY_ROWS, HY_ROWS), HY_ROWS)
        rows_im = pl.ds(pl.multiple_of(SLAB + rc * HY_ROWS, HY_ROWS), HY_ROWS)
        xs = [((s_ref[b, rows, :], 1), (s_ref[b, rows_im, :], im_sign(b))) for b in range(N_HI)]
        like = xs[0][0][0]
        out = _slab_fft_dif(xs, 1, set(range(N_IN)))
        for a in range(N_IN):
            s_ref[a, rows, :] = _part_value(out[a][0], like)
            s_ref[a, rows_im, :] = _part_value(out[a][1], like)
        return carry

    lax.fori_loop(0, SLAB // HY_ROWS, inverse_chunk, 0)

    sk = skip_ref[pl.ds(o, 1), :]
    skip2 = jnp.concatenate([sk, sk], axis=1)

    def gate(a, carry):
        y = jnp.concatenate([s_ref[a, 0:SLAB, :], s_ref[a, SLAB:2 * SLAB, :]], axis=1).astype(F32)
        z = g_ref[o, a].astype(F32) * (y + skip2 * zc_ref[a].astype(F32))
        zc_ref[a] = z.astype(BF16)
        return carry

    lax.fori_loop(0, N_IN, gate, 0)

    @pl.when(o == HYENA_ORDER - 1)
    def _():
        def emit(a, carry):
            rows = pl.ds(pl.multiple_of(a * SLAB, SLAB), SLAB)
            for bi in range(2):
                o_ref[bi, rows, :] = zc_ref[a, :, bi * HY_CT:(bi + 1) * HY_CT]
            return carry

        lax.fori_loop(0, N_IN, emit, 0)


def _hyena_conv(p, conv_w, conv_b, skip, mf, mi, hspec):
    b = p.shape[0]
    u0 = V_END // HY_CT
    cb = conv_b.reshape(1, -1)
    part = lambda k: pl.BlockSpec((2, SEQ, HY_CT), lambda c, q, o: (q, 0, u0 + k * N_CT + c))
    cw = lambda k: pl.BlockSpec((SHORT_CONV, HY_CT), lambda c, q, o: (0, k * N_CT + c))
    cbs = lambda k: pl.BlockSpec((1, HY_CT), lambda c, q, o: (0, k * N_CT + c))
    const = lambda: pl.BlockSpec((N_SPEC, 2 * SLAB, 2 * SLAB), lambda c, q, o: (0, 0, 0),
                                 pipeline_mode=pl.Buffered(1))
    return pl.pallas_call(
        _hyena_kernel,
        grid=(N_CT, b // 2, HYENA_ORDER),
        in_specs=[part(0), part(1), part(2), cw(0), cw(1), cw(2), cbs(0), cbs(1), cbs(2),
                  pl.BlockSpec((HYENA_ORDER, HY_CT), lambda c, q, o: (0, c)),
                  const(), const(),
                  pl.BlockSpec((None, None, N_SPEC * SLAB, 2 * HY_CT), lambda c, q, o: (o, c, 0, 0))],
        out_specs=pl.BlockSpec((2, SEQ, HY_CT), lambda c, q, o: (q, 0, c)),
        out_shape=jax.ShapeDtypeStruct((b, SEQ, HYENA_WIDTH), BF16),
        scratch_shapes=[pltpu.VMEM((N_IN, SLAB, 2 * HY_CT), BF16),
                        pltpu.VMEM((HYENA_ORDER, N_IN, SLAB, 2 * HY_CT), BF16),
                        pltpu.VMEM((N_HI, 2 * SLAB, HY_CT), BF16),
                        pltpu.VMEM((2 * (HYENA_ORDER + 1), SLAB + 2 * HALO, HY_CT), F32)],
        compiler_params=_cparams(("arbitrary", "arbitrary", "arbitrary"), 56),
        name="hyena_conv",
    )(p, p, p, conv_w, conv_w, conv_w, cb, cb, cb, skip, mf, mi, hspec)


def _ctx_hyena_kernel(uv_ref, u1_ref, u2_ref, cwv_ref, cw1_ref, cw2_ref, cbv_ref, cb1_ref, cb2_ref, skip_ref,
                      feat_ref, wf0_ref, wb0_ref, wf1_ref, wb1_ref, ad_ref, ff_ref, ai_ref, o_ref):
    l = CTX_LEN
    nb = uv_ref.shape[0]
    n = lax.broadcasted_iota(jnp.int32, (2 * l, HY_CT), 0)
    spectra = []
    no_w = jnp.zeros((FILTER_HIDDEN, HY_CT), F32)
    for wf_ref, wb_ref in ((wf0_ref, wb0_ref), (wf1_ref, wb1_ref)):
        raw = jnp.concatenate(
            [jnp.dot(feat_ref[...], jnp.concatenate([wf_ref[...], no_w], axis=0), precision=HIGHEST,
                     preferred_element_type=F32),
             jnp.dot(feat_ref[...], jnp.concatenate([no_w, wb_ref[...]], axis=0), precision=HIGHEST,
                     preferred_element_type=F32)], axis=0)
        v = _filter_taps(raw, n, l, ad_ref[...])
        v = v / jnp.sum(jnp.abs(v), axis=0, keepdims=True)
        pq = jnp.dot(ff_ref[...], v.astype(BF16), preferred_element_type=F32)
        spectra.append((pq[:2 * l], pq[2 * l:]))

    row = lax.broadcasted_iota(jnp.int32, (l, HY_CT), 0)

    def conv(u_ref, bi, w_ref, b_ref):
        x = u_ref[bi].astype(F32)
        xm1 = jnp.where(row == 0, 0.0, pltpu.roll(x, 1, 0))
        xp1 = jnp.where(row == l - 1, 0.0, pltpu.roll(x, l - 1, 0))
        return xm1 * w_ref[0:1, :] + x * w_ref[1:2, :] + xp1 * w_ref[2:3, :] + b_ref[...]

    z = [conv(uv_ref, bi, cwv_ref, cbv_ref) for bi in range(nb)]
    gates = [[conv(u1_ref, bi, cw1_ref, cb1_ref) for bi in range(nb)],
             [conv(u2_ref, bi, cw2_ref, cb2_ref) for bi in range(nb)]]
    for o in range(HYENA_ORDER):
        hr, hi = spectra[o]
        pq = jnp.dot(ff_ref[:, 0:l], jnp.concatenate(z, axis=1).astype(BF16), preferred_element_type=F32)
        ys = []
        for q in range(nb // 2):
            xr, xi = _cplx_from_pq(pq[:, 2 * q * HY_CT:2 * (q + 1) * HY_CT], False)
            ys += [xr * hr - xi * hi, xr * hi + xi * hr]
        pq2 = jnp.dot(ai_ref[...], jnp.concatenate(ys, axis=1).astype(BF16), preferred_element_type=F32)
        y = []
        for q in range(nb // 2):
            y += list(_cplx_from_pq(pq2[:, 2 * q * HY_CT:2 * (q + 1) * HY_CT], False))
        z = [gates[o][bi] * (y[bi] + skip_ref[o:o + 1, :] * z[bi]) for bi in range(nb)]
    for bi in range(nb):
        o_ref[bi] = z[bi].astype(BF16)


def _ctx_hyena(pc, conv_w, conv_b, skip, feat, w_fout, absdelta, ff, ai):
    b, l, _ = pc.shape
    u0 = V_END // HY_CT
    cb = conv_b.reshape(1, -1)
    cols = 2 * N_CT
    part = lambda k: pl.BlockSpec((b, l, HY_CT), lambda c: (0, 0, u0 + k * N_CT + c))
    cw = lambda k: pl.BlockSpec((SHORT_CONV, HY_CT), lambda c: (0, k * N_CT + c))
    cbs = lambda k: pl.BlockSpec((1, HY_CT), lambda c: (0, k * N_CT + c))
    wcol = lambda off: pl.BlockSpec((FILTER_HIDDEN, HY_CT), lambda c: (0, off + c))
    return pl.pallas_call(
        _ctx_hyena_kernel,
        grid=(N_CT,),
        in_specs=[part(0), part(1), part(2), cw(0), cw(1), cw(2), cbs(0), cbs(1), cbs(2),
                  pl.BlockSpec((HYENA_ORDER, HY_CT), lambda c: (0, c)),
                  pl.BlockSpec((l, 2 * FILTER_HIDDEN), lambda c: (0, 0)),
                  wcol(0), wcol(N_CT), wcol(cols), wcol(cols + N_CT),
                  pl.BlockSpec((1, HY_CT), lambda c: (0, c)),
                  pl.BlockSpec((4 * l, 2 * l), lambda c: (0, 0)),
                  pl.BlockSpec((2 * l, 2 * l), lambda c: (0, 0))],
        out_specs=pl.BlockSpec((b, l, HY_CT), lambda c: (0, 0, c)),
        out_shape=jax.ShapeDtypeStruct((b, l, HYENA_WIDTH), BF16),
        compiler_params=_cparams(("arbitrary",), 32),
        name="ctx_hyena",
    )(pc, pc, pc, conv_w, conv_w, conv_w, cb, cb, cb, skip, feat, w_fout, w_fout, w_fout, w_fout, absdelta, ff, ai)


def _rope_tables(l):
    rows = l // GRID_W
    row = np.repeat(np.arange(rows), GRID_W).astype(np.float64)
    col = np.tile(np.arange(GRID_W), rows).astype(np.float64)
    n_freq = HEAD_DIM // 4
    inv = ROPE_BASE ** (-np.arange(n_freq) / n_freq)
    ang_r = row[:, None] * inv
    ang_c = col[:, None] * inv
    cos = np.concatenate([np.cos(ang_r), np.cos(ang_r), np.cos(ang_c), np.cos(ang_c)], axis=-1)
    sin = np.concatenate([-np.sin(ang_r), np.sin(ang_r), -np.sin(ang_c), np.sin(ang_c)], axis=-1)
    return jnp.asarray(cos, F32), jnp.asarray(sin, F32)


def kernel(x, c, ctx, c_ctx, w_ada, b_ada, norm_mix, norm_mlp, w_in, q_norm, k_norm, attn_sink, conv_w, conv_b, filt_w1, filt_b1, filt_w2, filt_b2, filt_w3, filt_b3, filt_freq, filt_w_out, hyena_skip, w_out, w_mlp_in, w_mlp_out):
    cos, sin = _rope_tables(SEQ)
    emb, emb_ctx = _filter_positions(SEQ), _filter_positions(CTX_LEN)
    m_tall, mf, mi = _dft_constants()
    ff_ctx, ai_ctx = _ctx_dft_constants()
    absdelta = jnp.abs(jnp.linspace(MIN_DECAY, MAX_DECAY, HYENA_WIDTH, dtype=F32)).reshape(1, HYENA_WIDTH)
    cvec = jnp.zeros((MOD_ROWS, D_MODEL), F32).at[:BATCH].set(c).at[CTX_ROW].set(c_ctx)
    mod_all = _modulation(cvec, w_ada, b_ada).reshape(DEPTH, MOD_ROWS, N_MOD, D_MODEL)
    w_in_bf = w_in[0:1].astype(BF16)
    ctx_rows = BATCH * CTX_LEN
    flat = lambda a: a.reshape(1, ctx_rows, a.shape[-1])
    per_batch = lambda a: a.reshape(BATCH, CTX_LEN, a.shape[-1])
    xc = flat(ctx)
    for i in range(DEPTH):
        update_ctx = i < DEPTH - 1
        mod = mod_all[i]
        nmix = norm_mix[i].reshape(1, D_MODEL)
        nmlp = norm_mlp[i].reshape(1, D_MODEL)
        qn = q_norm[i].reshape(1, HEAD_DIM)
        kn = k_norm[i].reshape(1, HEAD_DIM)
        filt_mlp = (filt_w1[i], filt_b1[i], filt_w2[i], filt_b2[i], filt_w3[i], filt_b3[i], filt_freq[i])

        p, (w_out_bf, w1_bf, w2_bf) = _in_projection(
            x, mod, None, nmix, w_in_bf, qn, kn, cos, sin, tm=512, n_off=0, n_tiles=IN_COLS // IN_TN,
            use_rope=True, riders=[(w_out, i), (w_mlp_in, i), (w_mlp_out, i)])
        if update_ctx:
            pc, _ = _in_projection(xc, mod, CTX_ROW, nmix, w_in_bf, qn, kn, cos, sin,
                                   tm=512, n_off=0, n_tiles=IN_COLS // IN_TN, use_rope=False)
            kx_blk, vx_blk = Q_END // KV_COLS * 2, Q_END // KV_COLS * 2 + 1
        else:
            pc, _ = _in_projection(xc, mod, CTX_ROW, nmix, w_in_bf, qn, kn, cos, sin,
                                   tm=512, n_off=KV_TILE, n_tiles=1, use_rope=False)
            kx_blk, vx_blk = 0, 1
        pc = per_batch(pc)

        attn = _window_attention(p, pc, kx_blk, vx_blk, attn_sink[i])
        hspec = _filter_spectrum(_filter_features(emb, *filt_mlp), filt_w_out[i], absdelta, m_tall)
        hy = _hyena_conv(p, conv_w[i], conv_b[i], hyena_skip[i], mf, mi, hspec)
        x, h2, next_w_in = _out_projection(attn, hy, x, mod, None, nmlp, w_out_bf, tm=512,
                                           riders=[(w_in, i + 1)] if update_ctx else [])
        x = _mlp(h2, x, mod, None, w1_bf, w2_bf, tm=512, tf=1024)

        if update_ctx:
            attn_c = _context_attention(pc, attn_sink[i])
            hy_c = _ctx_hyena(pc, conv_w[i], conv_b[i], hyena_skip[i], _filter_features(emb_ctx, *filt_mlp),
                              filt_w_out[i], absdelta, ff_ctx, ai_ctx)
            xc, h2c, _ = _out_projection(flat(attn_c), flat(hy_c), xc, mod, CTX_ROW, nmlp, w_out_bf, tm=512)
            xc = _mlp(h2c, xc, mod, CTX_ROW, w1_bf, w2_bf, tm=512, tf=1024)
            w_in_bf = next_w_in[0]
    return x
```

```python
import cmath
import functools
import math

import jax
import jax.numpy as jnp
import numpy as np
from jax import lax
from jax.experimental import pallas as pl
from jax.experimental.pallas import tpu as pltpu

D_MODEL = 2048
BATCH = 4
SEQ = 4096
DEPTH = 2
GRID_W = 64
CTX_LEN = 256
HEAD_DIM = 128
N_Q_HEADS = 8
N_KV_HEADS = 2
GQA_GROUP = N_Q_HEADS // N_KV_HEADS
ATTN_WIDTH = N_Q_HEADS * HEAD_DIM
HYENA_WIDTH = D_MODEL - ATTN_WIDTH
HYENA_ORDER = 2
SHORT_CONV = 3
FILTER_EMB = 33
DECAY_TARGET = 1e-2
FAST_DECAY_PCT = 0.3
SLOW_DECAY_PCT = 1.5
MAX_DECAY = math.log(DECAY_TARGET) / FAST_DECAY_PCT
MIN_DECAY = math.log(DECAY_TARGET) / SLOW_DECAY_PCT
WINDOW = 128
BLOCK = 128
ROPE_BASE = 10000.0
D_FF = 4 * D_MODEL
N_MOD = 6
EPS = 1e-6
NEG_INF = -1e30
Q_END = ATTN_WIDTH
K_END = Q_END + N_KV_HEADS * HEAD_DIM
V_END = K_END + N_KV_HEADS * HEAD_DIM
IN_COLS = V_END + (HYENA_ORDER + 1) * HYENA_WIDTH

MOD_ROWS = 8
CTX_ROW = BATCH
KV_COLS = 2 * N_KV_HEADS * HEAD_DIM

F32 = jnp.float32
BF16 = jnp.bfloat16
MIB = 1024 * 1024


def _cparams(sem, vmem_mib):
    return pltpu.CompilerParams(dimension_semantics=sem, vmem_limit_bytes=vmem_mib * MIB)


def _mod_kernel(c_ref, w_ref, b_ref, o_ref):
    c = c_ref[...]
    s = (c * jax.nn.sigmoid(c)).astype(BF16)
    o_ref[...] = jnp.dot(s, w_ref[...].astype(BF16), preferred_element_type=F32) + b_ref[...]


def _modulation(cvec, w_ada, b_ada):
    tn = 1024
    n_cols = N_MOD * D_MODEL
    return pl.pallas_call(
        _mod_kernel,
        grid=(DEPTH, n_cols // tn),
        in_specs=[
            pl.BlockSpec((MOD_ROWS, D_MODEL), lambda l, j: (0, 0)),
            pl.BlockSpec((None, D_MODEL, tn), lambda l, j: (l, 0, j)),
            pl.BlockSpec((None, 1, tn), lambda l, j: (l, 0, j)),
        ],
        out_specs=pl.BlockSpec((None, MOD_ROWS, tn), lambda l, j: (l, 0, j)),
        out_shape=jax.ShapeDtypeStruct((DEPTH, MOD_ROWS, n_cols), F32),
        compiler_params=_cparams(("arbitrary", "arbitrary"), 40),
        name="adaln_mod",
    )(cvec, w_ada, b_ada.reshape(DEPTH, 1, n_cols))


IN_TN = 512
Q_TILES = Q_END // IN_TN
KV_TILE = Q_TILES
HEADS_PER_TILE = IN_TN // HEAD_DIM


SUB_ROWS = 256


def _rms_modulate(xf, gain, shift):
    ms = jnp.mean(xf * xf, axis=-1, keepdims=True)
    return xf * lax.rsqrt(ms + EPS) * gain + shift


def _head_norm_rope(a, g, cos, sin):
    ms = jnp.mean(a * a, axis=-1, keepdims=True)
    y = a * lax.rsqrt(ms + EPS) * g
    if cos is None:
        return y
    lane = lax.broadcasted_iota(jnp.int32, y.shape, 1)
    partner = jnp.where((lane % 64) < 32, pltpu.roll(y, 96, 1), pltpu.roll(y, 32, 1))
    return y * cos + partner * sin


BF16_TILE_ROWS = 16


def _rider_plan(w, layer, steps, step_of):
    depth, r, c = w.shape
    assert r % (steps * BF16_TILE_ROWS) == 0
    blk = r // steps
    in_spec = pl.BlockSpec((blk, c), lambda *g: (layer * steps + step_of(*g), 0))
    out_spec = pl.BlockSpec((blk, c), lambda *g: (step_of(*g), 0))
    restore = lambda a: a.reshape(1, r, c)
    return w.reshape(depth * r, c), in_spec, out_spec, jax.ShapeDtypeStruct((r, c), BF16), restore


def _cast_riders(in_refs, out_refs):
    for src, dst in zip(in_refs, out_refs):
        dst[...] = src[...].astype(BF16)


def _inproj_kernel(*refs, n_off, n_tiles, use_rope, n_riders):
    x_ref, m_ref, nw_ref, w_ref, qn_ref, kn_ref, cos_ref, sin_ref = refs[:8]
    o_ref = refs[8 + n_riders]
    _cast_riders(refs[8:8 + n_riders], refs[9 + n_riders:])
    gain = nw_ref[...] * (1.0 + m_ref[1:2, :])
    shift = m_ref[0:1, :]
    for r0 in range(0, x_ref.shape[0], SUB_ROWS):
        rows = slice(r0, r0 + SUB_ROWS)
        h = _rms_modulate(x_ref[rows, :], gain, shift).astype(BF16)
        cos = cos_ref[rows, :] if use_rope else None
        sin = sin_ref[rows, :] if use_rope else None
        for n in range(n_tiles):
            cols = slice(n * IN_TN, (n + 1) * IN_TN)
            acc = jnp.dot(h, w_ref[:, cols], preferred_element_type=F32)
            t = n + n_off
            if t > KV_TILE:
                o_ref[rows, cols] = acc.astype(BF16)
                continue
            gains = [qn_ref[...]] * HEADS_PER_TILE if t < Q_TILES else [kn_ref[...]] * N_KV_HEADS + [None] * N_KV_HEADS
            for c, g in enumerate(gains):
                sl = slice(c * HEAD_DIM, (c + 1) * HEAD_DIM)
                osl = slice(n * IN_TN + c * HEAD_DIM, n * IN_TN + (c + 1) * HEAD_DIM)
                if g is None:
                    o_ref[rows, osl] = acc[:, sl].astype(BF16)
                else:
                    o_ref[rows, osl] = _head_norm_rope(acc[:, sl], g, cos, sin).astype(BF16)


def _in_projection(x, mod, mod_row, nw, w_bf, qn, kn, cos, sin, *, tm, n_off, n_tiles, use_rope, riders=()):
    b, t, _ = x.shape
    assert n_off % n_tiles == 0 and tm % SUB_ROWS == 0
    row = (lambda bb: bb) if mod_row is None else (lambda bb: mod_row)
    n_i = t // tm
    plans = [_rider_plan(w, layer, b * n_i, lambda bb, i: bb * n_i + i) for w, layer in riders]
    kern = functools.partial(_inproj_kernel, n_off=n_off, n_tiles=n_tiles, use_rope=use_rope, n_riders=len(plans))
    width = n_tiles * IN_TN
    out = pl.pallas_call(
        kern,
        grid=(b, n_i),
        in_specs=[
            pl.BlockSpec((None, tm, D_MODEL), lambda bb, i: (bb, i, 0)),
            pl.BlockSpec((None, N_MOD, D_MODEL), lambda bb, i: (row(bb), 0, 0)),
            pl.BlockSpec((1, D_MODEL), lambda bb, i: (0, 0)),
            pl.BlockSpec((None, D_MODEL, width), lambda bb, i: (0, 0, n_off // n_tiles),
                         pipeline_mode=pl.Buffered(1)),
            pl.BlockSpec((1, HEAD_DIM), lambda bb, i: (0, 0)),
            pl.BlockSpec((1, HEAD_DIM), lambda bb, i: (0, 0)),
            pl.BlockSpec((tm, HEAD_DIM), lambda bb, i: (i, 0)),
            pl.BlockSpec((tm, HEAD_DIM), lambda bb, i: (i, 0)),
        ] + [pl_[1] for pl_ in plans],
        out_specs=[pl.BlockSpec((None, tm, width), lambda bb, i: (bb, i, 0))] + [pl_[2] for pl_ in plans],
        out_shape=[jax.ShapeDtypeStruct((b, t, width), BF16)] + [pl_[3] for pl_ in plans],
        compiler_params=_cparams(("arbitrary", "arbitrary"), 56),
        name="in_proj",
    )(x, mod, nw, w_bf, qn, kn, cos, sin, *[pl_[0] for pl_ in plans])
    return out[0], [pl_[4](a) for pl_, a in zip(plans, out[1:])]


def _sink_column(sink_ref, kv, rows_per_head):
    rows = GQA_GROUP * rows_per_head
    head = lax.broadcasted_iota(jnp.int32, (rows, 1), 0) // rows_per_head
    col = jnp.full((rows, 1), sink_ref[kv * GQA_GROUP], F32)
    for g in range(1, GQA_GROUP):
        col = jnp.where(head == g, sink_ref[kv * GQA_GROUP + g], col)
    return col


def _stack_heads(q_ref, kv, rows=slice(None)):
    return jnp.concatenate(
        [q_ref[rows, (kv * GQA_GROUP + g) * HEAD_DIM:(kv * GQA_GROUP + g + 1) * HEAD_DIM]
         for g in range(GQA_GROUP)], axis=0)


LOG2E = math.log2(math.e)
QK_LOG2_SCALE = HEAD_DIM ** -0.5 * LOG2E


def _softmax_pv(q4, k_all, v_all, bias, sink_col):
    t = lax.dot_general(q4, k_all, (((1,), (1,)), ((), ())), preferred_element_type=F32) * QK_LOG2_SCALE
    if bias is not None:
        t = t + bias
    sink2 = sink_col * LOG2E
    m = jnp.maximum(jnp.max(t, axis=-1, keepdims=True), sink2)
    e = jnp.exp2(t - m)
    denom = jnp.exp2(sink2 - m) + jnp.sum(e, axis=-1, keepdims=True)
    return jnp.dot(e.astype(BF16), v_all, preferred_element_type=F32) / denom


Q_BLOCKS = 4


def _window_attn_kernel(sink_ref, bias_first_ref, bias_mid_ref, bias_last_ref, q_ref, kp_ref, km_ref, kn_ref,
                        vp_ref, vm_ref, vn_ref, kx_ref, vx_ref, o_ref):
    for blk in range(Q_BLOCKS):
        bias_ref = bias_first_ref if blk == 0 else bias_last_ref if blk == Q_BLOCKS - 1 else bias_mid_ref
        rows = slice(blk * BLOCK, (blk + 1) * BLOCK)
        for kv in range(N_KV_HEADS):
            ks = slice(kv * HEAD_DIM, (kv + 1) * HEAD_DIM)

            def band(prev_ref, mid_ref, next_ref):
                pieces = [prev_ref[:, ks]] + [mid_ref[j * BLOCK:(j + 1) * BLOCK, ks] for j in range(Q_BLOCKS)]
                return (pieces + [next_ref[:, ks]])[blk:blk + 3]

            k_all = jnp.concatenate([kx_ref[:, ks]] + band(kp_ref, km_ref, kn_ref), axis=0)
            v_all = jnp.concatenate([vx_ref[:, ks]] + band(vp_ref, vm_ref, vn_ref), axis=0)
            o = _softmax_pv(_stack_heads(q_ref, kv, rows), k_all, v_all, bias_ref[...],
                            _sink_column(sink_ref, kv, BLOCK))
            for g in range(GQA_GROUP):
                h = kv * GQA_GROUP + g
                o_ref[rows, h * HEAD_DIM:(h + 1) * HEAD_DIM] = o[g * BLOCK:(g + 1) * BLOCK].astype(BF16)


def _window_bias(nb):
    qi = np.arange(GQA_GROUP * BLOCK)[:, None] % BLOCK
    sj = np.arange(3 * BLOCK)[None, :] - BLOCK
    near = np.abs(sj - qi) <= WINDOW
    out = np.zeros((3, GQA_GROUP * BLOCK, CTX_LEN + 3 * BLOCK), np.float32)
    for variant, blk in enumerate((0, 1, nb - 1)):
        key_pos = blk * BLOCK + sj
        ok = near & (key_pos >= 0) & (key_pos < nb * BLOCK)
        out[variant, :, CTX_LEN:] = np.where(ok, 0.0, NEG_INF)
    return jnp.asarray(out)


def _window_attention(p, pc, kx_blk, vx_blk, sink):
    b, seq, _ = p.shape
    nb = seq // BLOCK
    kvw = N_KV_HEADS * HEAD_DIM
    k_blk, v_blk = Q_END // kvw, K_END // kvw
    assert nb % Q_BLOCKS == 0 and Q_BLOCKS >= 2
    steps = nb // Q_BLOCKS
    prev = lambda bb, i: (bb, jnp.maximum(Q_BLOCKS * i - 1, 0))
    nxt = lambda bb, i: (bb, jnp.minimum(Q_BLOCKS * (i + 1), nb - 1))

    def edge(which, blk):
        return pl.BlockSpec((None, BLOCK, kvw), lambda bb, i: which(bb, i) + (blk,))

    def mid(blk):
        return pl.BlockSpec((None, Q_BLOCKS * BLOCK, kvw), lambda bb, i: (bb, i, blk))

    bias_shape = (None, GQA_GROUP * BLOCK, CTX_LEN + 3 * BLOCK)
    bias = _window_bias(nb)
    return pl.pallas_call(
        _window_attn_kernel,
        grid=(b, steps),
        in_specs=[
            pl.BlockSpec(memory_space=pltpu.SMEM),
            pl.BlockSpec(bias_shape, lambda bb, i: (jnp.where(i == 0, 0, 1), 0, 0)),
            pl.BlockSpec(bias_shape, lambda bb, i: (1, 0, 0)),
            pl.BlockSpec(bias_shape, lambda bb, i: (jnp.where(i == steps - 1, 2, 1), 0, 0)),
            pl.BlockSpec((None, Q_BLOCKS * BLOCK, ATTN_WIDTH), lambda bb, i: (bb, i, 0)),
            edge(prev, k_blk), mid(k_blk), edge(nxt, k_blk),
            edge(prev, v_blk), mid(v_blk), edge(nxt, v_blk),
            pl.BlockSpec((None, CTX_LEN, kvw), lambda bb, i: (bb, 0, kx_blk)),
            pl.BlockSpec((None, CTX_LEN, kvw), lambda bb, i: (bb, 0, vx_blk)),
        ],
        out_specs=pl.BlockSpec((None, Q_BLOCKS * BLOCK, ATTN_WIDTH), lambda bb, i: (bb, i, 0)),
        out_shape=jax.ShapeDtypeStruct((b, seq, ATTN_WIDTH), BF16),
        compiler_params=_cparams(("arbitrary", "arbitrary"), 32),
        name="window_attn",
    )(sink, bias, bias, bias, p, p, p, p, p, p, p, pc, pc)


def _ctx_attn_kernel(sink_ref, q_ref, k_ref, v_ref, o_ref):
    n = q_ref.shape[0]
    for kv in range(N_KV_HEADS):
        ks = slice(kv * HEAD_DIM, (kv + 1) * HEAD_DIM)
        o = _softmax_pv(_stack_heads(q_ref, kv), k_ref[:, ks], v_ref[:, ks], None, _sink_column(sink_ref, kv, n))
        for g in range(GQA_GROUP):
            h = kv * GQA_GROUP + g
            o_ref[:, h * HEAD_DIM:(h + 1) * HEAD_DIM] = o[g * n:(g + 1) * n].astype(BF16)


def _context_attention(pc, sink):
    b, n, _ = pc.shape
    kvw = N_KV_HEADS * HEAD_DIM
    return pl.pallas_call(
        _ctx_attn_kernel,
        grid=(b,),
        in_specs=[
            pl.BlockSpec(memory_space=pltpu.SMEM),
            pl.BlockSpec((None, n, ATTN_WIDTH), lambda bb: (bb, 0, 0)),
            pl.BlockSpec((None, n, kvw), lambda bb: (bb, 0, Q_END // kvw)),
            pl.BlockSpec((None, n, kvw), lambda bb: (bb, 0, K_END // kvw)),
        ],
        out_specs=pl.BlockSpec((None, n, ATTN_WIDTH), lambda bb: (bb, 0, 0)),
        out_shape=jax.ShapeDtypeStruct((b, n, ATTN_WIDTH), BF16),
        compiler_params=_cparams(("arbitrary",), 32),
        name="ctx_attn",
    )(sink, pc, pc, pc)


def _outproj_kernel(*refs, n_riders):
    a_ref, y_ref, x_ref, m_ref, nw_ref, wa_ref, wy_ref = refs[:7]
    o_ref, h_ref = refs[7 + n_riders], refs[8 + n_riders]
    _cast_riders(refs[7:7 + n_riders], refs[9 + n_riders:])
    gain = nw_ref[...] * (1.0 + m_ref[4:5, :])
    shift = m_ref[3:4, :]
    for r0 in range(0, x_ref.shape[0], SUB_ROWS):
        rows = slice(r0, r0 + SUB_ROWS)
        mix = (jnp.dot(a_ref[rows, :], wa_ref[...], preferred_element_type=F32)
               + jnp.dot(y_ref[rows, :], wy_ref[...], preferred_element_type=F32))
        x1 = x_ref[rows, :] + m_ref[2:3, :] * mix
        o_ref[rows, :] = x1
        h_ref[rows, :] = _rms_modulate(x1, gain, shift).astype(BF16)


def _out_projection(attn, hy, x, mod, mod_row, nw, w_bf, *, tm, riders=()):
    b, t, _ = x.shape
    assert tm % SUB_ROWS == 0
    n_i = t // tm
    row = (lambda bb: bb) if mod_row is None else (lambda bb: mod_row)
    tile = lambda width: pl.BlockSpec((None, tm, width), lambda bb, i: (bb, i, 0))
    half = lambda k: pl.BlockSpec((None, D_MODEL // 2, D_MODEL), lambda bb, i: (0, k, 0),
                                  pipeline_mode=pl.Buffered(1))
    plans = [_rider_plan(w, layer, b * n_i, lambda bb, i: bb * n_i + i) for w, layer in riders]
    out = pl.pallas_call(
        functools.partial(_outproj_kernel, n_riders=len(plans)),
        grid=(b, n_i),
        in_specs=[
            tile(ATTN_WIDTH), tile(HYENA_WIDTH), tile(D_MODEL),
            pl.BlockSpec((None, N_MOD, D_MODEL), lambda bb, i: (row(bb), 0, 0)),
            pl.BlockSpec((1, D_MODEL), lambda bb, i: (0, 0)),
            half(0), half(1),
        ] + [pl_[1] for pl_ in plans],
        out_specs=[tile(D_MODEL), tile(D_MODEL)] + [pl_[2] for pl_ in plans],
        out_shape=[jax.ShapeDtypeStruct(x.shape, F32), jax.ShapeDtypeStruct(x.shape, BF16)]
        + [pl_[3] for pl_ in plans],
        compiler_params=_cparams(("arbitrary", "arbitrary"), 48),
        name="out_proj",
    )(attn, hy, x, mod, nw, w_bf, w_bf, *[pl_[0] for pl_ in plans])
    return out[0], out[1], [pl_[4](a) for pl_, a in zip(plans, out[2:])]


MLP_OUT_CHUNK = 512


def _mlp_kernel(h_ref, x_ref, m_ref, w1_ref, w2_ref, o_ref, *, n_f):
    f = pl.program_id(2)

    @pl.when(f == 0)
    def _():
        o_ref[...] = jnp.zeros_like(o_ref)

    a = jnp.maximum(jnp.dot(h_ref[...], w1_ref[...], preferred_element_type=F32), 0.0)
    a = (a * a).astype(BF16)
    for n0 in range(0, D_MODEL, MLP_OUT_CHUNK):
        sl = slice(n0, n0 + MLP_OUT_CHUNK)
        o_ref[:, sl] += jnp.dot(a, w2_ref[:, sl], preferred_element_type=F32)

    @pl.when(f == n_f - 1)
    def _():
        o_ref[...] = x_ref[...] + m_ref[5:6, :] * o_ref[...]


def _mlp(h, x, mod, mod_row, w1_bf, w2_bf, *, tm, tf):
    b, t, _ = x.shape
    n_f = D_FF // tf
    row = (lambda bb: bb) if mod_row is None else (lambda bb: mod_row)
    tile = pl.BlockSpec((None, tm, D_MODEL), lambda bb, i, f: (bb, i, 0))
    return pl.pallas_call(
        functools.partial(_mlp_kernel, n_f=n_f),
        grid=(b, t // tm, n_f),
        in_specs=[
            tile, tile,
            pl.BlockSpec((None, N_MOD, D_MODEL), lambda bb, i, f: (row(bb), 0, 0)),
            pl.BlockSpec((None, D_MODEL, tf), lambda bb, i, f: (0, 0, f)),
            pl.BlockSpec((None, tf, D_MODEL), lambda bb, i, f: (0, f, 0)),
        ],
        out_specs=tile,
        out_shape=jax.ShapeDtypeStruct(x.shape, F32),
        compiler_params=_cparams(("arbitrary", "arbitrary", "arbitrary"), 56),
        name="mlp",
    )(h, x, mod, w1_bf, w2_bf)


def _tile_const(k, like):
    return jnp.full(like.shape, k, F32).astype(like.dtype)


def _p_add(p, q):
    if p is None:
        return q
    if q is None:
        return p
    (a, sa), (b, sb) = p, q
    if sa == sb:
        return (a + b, sa)
    return (a - b, 1) if sa > 0 else (b - a, 1)


def _p_scale(p, k):
    if p is None or k == 0.0:
        return None
    a, s = p
    if k < 0:
        s, k = -s, -k
    return (a, s) if k == 1.0 else (a * _tile_const(k, a), s)


def _c_add(x, y, sign=1):
    if y is None:
        return x
    if x is None:
        x = (None, None)
    return (_p_add(x[0], _p_scale(y[0], sign)), _p_add(x[1], _p_scale(y[1], sign)))


def _snap(v):
    for t in (0.0, 1.0, -1.0):
        if abs(v - t) < 1e-12:
            return t
    return v


def _c_mulc(x, c):
    if x is None:
        return None
    cr, ci = _snap(c.real), _snap(c.imag)
    re, im = x
    if cr != 0.0 and ci != 0.0 and abs(abs(cr) - abs(ci)) < 1e-12:
        k, sr, si = abs(cr), math.copysign(1.0, cr), math.copysign(1.0, ci)
        return (_p_scale(_p_add(_p_scale(re, sr), _p_scale(im, -si)), k),
                _p_scale(_p_add(_p_scale(re, si), _p_scale(im, sr)), k))
    return (_p_add(_p_scale(re, cr), _p_scale(im, -ci)), _p_add(_p_scale(re, ci), _p_scale(im, cr)))


def _is_zero(x):
    return x is None or (x[0] is None and x[1] is None)


def _slab_fft_dit(xs, sign, need):
    n = len(xs)
    if n == 1:
        return {0: xs[0]}
    if all(_is_zero(x) for x in xs):
        return {k: None for k in need}
    h = n // 2
    sub = {k % h for k in need}
    ev = _slab_fft_dit(xs[0::2], sign, sub)
    od = _slab_fft_dit(xs[1::2], sign, sub)
    out = {}
    for k in sub:
        t = _c_mulc(od[k], cmath.exp(sign * 2j * math.pi * k / n))
        if k in need:
            out[k] = _c_add(ev[k], t)
        if k + h in need:
            out[k + h] = _c_add(ev[k], t, -1)
    return out


def _slab_fft_dif(xs, sign, need):
    n = len(xs)
    if n == 1:
        return {0: xs[0]}
    h = n // 2
    need_e = {k // 2 for k in need if k % 2 == 0}
    need_o = {k // 2 for k in need if k % 2 == 1}
    out = {}
    if need_e:
        ev = _slab_fft_dif([_c_add(xs[j], xs[j + h]) for j in range(h)], sign, need_e)
        out.update({2 * k: v for k, v in ev.items()})
    if need_o:
        dif = [_c_mulc(_c_add(xs[j], xs[j + h], -1), cmath.exp(sign * 2j * math.pi * j / n)) for j in range(h)]
        od = _slab_fft_dif(dif, sign, need_o)
        out.update({2 * k + 1: v for k, v in od.items()})
    return out


def _part_value(p, like):
    if p is None:
        return jnp.zeros_like(like)
    return p[0] if p[1] > 0 else -p[0]


HY_CT = 128
SLAB = 256
N_HI = 2 * SEQ // SLAB
N_IN = SEQ // SLAB
N_SPEC = N_HI // 2 + 1
ROW_CHUNK = 8
HY_ROWS = 16
SLAB_PAD = HY_ROWS
S_IM = SLAB + SLAB_PAD
N_CT = HYENA_WIDTH // HY_CT
SPEC_GROUP = 4
FILTER_HIDDEN = 64
EMB_PAD = 128
HIGHEST = lax.Precision.HIGHEST


def _dft_constants():
    n = 2 * SEQ
    r = np.arange(SLAB)
    tall = np.zeros((N_SPEC, 2 * SLAB, SLAB))
    fwd = np.zeros((N_SPEC, 2 * SLAB, 2 * SLAB))
    inv = np.zeros((N_SPEC, 2 * SLAB, 2 * SLAB))
    for b in range(N_SPEC):
        m = np.exp(-2j * np.pi * (np.outer(r, r) / SLAB + (r * b)[None, :] / n))
        a = m.T / n
        tall[b] = np.concatenate([m.real, m.imag], axis=0)
        fwd[b] = np.block([[m.real, -m.imag], [m.imag, m.real]])
        inv[b] = np.block([[a.real, a.imag], [-a.imag, a.real]])
    return tuple(jnp.asarray(t, F32).astype(BF16) for t in (tall, fwd, inv))


def _ctx_dft_constants():
    n = 2 * CTX_LEN
    k = np.arange(n)
    f = np.exp(-2j * np.pi * np.outer(k, k) / n)
    a = np.conj(f)[:CTX_LEN, :] / n
    return (jnp.asarray(np.concatenate([f.real, f.imag], axis=0), F32).astype(BF16),
            jnp.asarray(np.concatenate([a.real, a.imag], axis=0), F32).astype(BF16))


def _filter_positions(l):
    t = np.linspace(0.0, 1.0, l)[:, None]
    bands = (FILTER_EMB - 1) // 2
    f = np.linspace(1e-4, bands - 1, bands)[None, :]
    w = 2.0 * math.pi * np.arange(l)[:, None] / l
    z = np.concatenate([t, np.cos(f * w), -np.sin(f * w)], axis=-1)
    z2 = np.concatenate([z, z[0:1], z[1:][::-1]], axis=0)
    return jnp.asarray(np.pad(z2, ((0, 0), (0, EMB_PAD - FILTER_EMB))), F32)


def _filter_feat_kernel(za_ref, zb_ref, w1_ref, b1_ref, w2_ref, b2_ref, w3_ref, b3_ref, fr_ref, o_ref):
    fr = fr_ref[...]
    dot = functools.partial(jnp.dot, precision=HIGHEST, preferred_element_type=F32)
    z = jnp.concatenate([za_ref[...], zb_ref[...]], axis=1)
    h = jnp.sin(fr * (dot(z, w1_ref[...]) + b1_ref[...]))
    h = jnp.sin(fr * (dot(h, w2_ref[...]) + b2_ref[...]))
    o_ref[...] = jnp.sin(fr * (dot(h, w3_ref[...]) + b3_ref[...]))


def _filter_features(emb, w1, b1, w2, b2, w3, b3, freq):
    l = emb.shape[0] // 2
    tr = min(l, 1024)
    nblk = l // tr
    diag2 = lambda w: jnp.kron(jnp.eye(2, dtype=F32), w)
    w1p = jnp.pad(w1, ((0, EMB_PAD - FILTER_EMB), (0, 0)))
    vec = lambda v: jnp.tile(v.reshape(1, FILTER_HIDDEN), (1, 2))
    full = lambda shape: pl.BlockSpec(shape, lambda i: (0, 0))
    wide = 2 * FILTER_HIDDEN
    return pl.pallas_call(
        _filter_feat_kernel,
        grid=(nblk,),
        in_specs=[pl.BlockSpec((tr, EMB_PAD), lambda i: (i, 0)),
                  pl.BlockSpec((tr, EMB_PAD), lambda i: (i + nblk, 0)),
                  full((2 * EMB_PAD, wide)), full((1, wide)),
                  full((wide, wide)), full((1, wide)),
                  full((wide, wide)), full((1, wide)),
                  full((1, wide))],
        out_specs=pl.BlockSpec((tr, wide), lambda i: (i, 0)),
        out_shape=jax.ShapeDtypeStruct((l, wide), F32),
        compiler_params=_cparams(("arbitrary",), 32),
        name="filter_feat",
    )(emb, emb, diag2(w1p), vec(b1), diag2(w2), vec(b2), diag2(w3), vec(b3), vec(freq))


def _filter_taps(raw, n, l, absdelta):
    m = jnp.where(n < l, n, 2 * l - n)
    t = m.astype(F32) / (l - 1)
    return jnp.where(n == l, 0.0, raw * jnp.exp(-t * absdelta))


def _cplx_from_pq(pq, conj):
    half = pq.shape[0] // 2
    pr, pi_ = pq[:half, :HY_CT], pq[:half, HY_CT:]
    qr, qi = pq[half:, :HY_CT], pq[half:, HY_CT:]
    if conj:
        return pr + qi, pi_ - qr
    return pr - qi, qr + pi_


def _slab_forward(src_ref, n_in, real_only, dst_ref, need, scale):
    def body(rc, carry):
        rows = pl.ds(pl.multiple_of(rc * ROW_CHUNK, ROW_CHUNK), ROW_CHUNK)
        xs = []
        for a in range(N_HI):
            if a >= n_in:
                xs.append(None)
                continue
            re = src_ref[a, rows, 0:HY_CT]
            if scale is not None:
                re = re * scale
            xs.append(((re, 1), None if real_only else (src_ref[a, rows, HY_CT:2 * HY_CT], 1)))
        like = xs[0][0][0]
        out = _slab_fft_dit(xs, -1, need)
        for b in sorted(need):
            c = out[b] or (None, None)
            dst_ref[b, rows, 0:HY_CT] = _part_value(c[0], like)
            dst_ref[b, rows, HY_CT:2 * HY_CT] = _part_value(c[1], like)
        return carry

    lax.fori_loop(0, SLAB // ROW_CHUNK, body, 0)


def _filter_spec_kernel(feat_ref, wf_ref, wb_ref, ad_ref, mf_ref, o_ref, ft_ref, s_ref):
    rate = ad_ref[...] * (-1.0 / (SEQ - 1))
    row = lax.broadcasted_iota(jnp.int32, (SLAB, HY_CT), 0)
    row_f = row.astype(F32)
    first_row = row == 0
    no_w = jnp.zeros((FILTER_HIDDEN, HY_CT), F32)
    w_fwd = jnp.concatenate([wf_ref[...], no_w], axis=0).astype(BF16)
    w_bwd = jnp.concatenate([no_w, wb_ref[...]], axis=0).astype(BF16)

    def taps(a, acc):
        fwd = a < N_IN
        src = jnp.where(fwd, a, a - N_IN)
        rows = feat_ref[pl.ds(pl.multiple_of(src * SLAB, SLAB), SLAB), :]
        raw = jnp.dot(rows.astype(BF16), jnp.where(fwd, w_fwd, w_bwd), preferred_element_type=F32)
        base = jnp.where(fwd, a * SLAB, 2 * SEQ - a * SLAB).astype(F32)
        step = jnp.where(fwd, 1.0, -1.0)
        v = raw * jnp.exp((base + step * row_f) * rate)
        v = jnp.where(jnp.logical_and(a == N_IN, first_row), 0.0, v)
        ft_ref[a] = v
        return acc + jnp.sum(jnp.abs(v), axis=0, keepdims=True)

    norm = lax.fori_loop(0, N_HI, taps, jnp.zeros((1, HY_CT), F32), unroll=2)
    _slab_forward(ft_ref, N_HI, True, s_ref, set(range(N_SPEC)), 1.0 / norm)

    def spec(b):
        pq = jnp.dot(mf_ref[b], s_ref[b].astype(BF16), preferred_element_type=F32)
        xr, xi = _cplx_from_pq(pq, False)
        rows = pl.ds(pl.multiple_of(b * SLAB, SLAB), SLAB)
        o_ref[rows, 0:HY_CT] = xr.astype(BF16)
        o_ref[rows, HY_CT:2 * HY_CT] = xi.astype(BF16)

    def spec_group(i, carry):
        for t in range(SPEC_GROUP):
            spec(1 + SPEC_GROUP * i + t)
        return carry

    spec(0)
    lax.fori_loop(0, (N_SPEC - 1) // SPEC_GROUP, spec_group, 0)


def _filter_spectrum(feat, w_fout, absdelta, mf):
    cols = 2 * N_CT
    return pl.pallas_call(
        _filter_spec_kernel,
        grid=(HYENA_ORDER, N_CT),
        in_specs=[
            pl.BlockSpec((SEQ, 2 * FILTER_HIDDEN), lambda o, c: (0, 0)),
            pl.BlockSpec((FILTER_HIDDEN, HY_CT), lambda o, c: (0, o * cols + c)),
            pl.BlockSpec((FILTER_HIDDEN, HY_CT), lambda o, c: (0, o * cols + N_CT + c)),
            pl.BlockSpec((1, HY_CT), lambda o, c: (0, c)),
            pl.BlockSpec((N_SPEC, 2 * SLAB, SLAB), lambda o, c: (0, 0, 0)),
        ],
        out_specs=pl.BlockSpec((None, None, N_SPEC * SLAB, 2 * HY_CT), lambda o, c: (o, c, 0, 0)),
        out_shape=jax.ShapeDtypeStruct((HYENA_ORDER, N_CT, N_SPEC * SLAB, 2 * HY_CT), BF16),
        scratch_shapes=[pltpu.VMEM((N_HI, SLAB, HY_CT), F32), pltpu.VMEM((N_SPEC, SLAB, 2 * HY_CT), F32)],
        compiler_params=_cparams(("arbitrary", "arbitrary"), 48),
        name="filter_spec",
    )(feat, w_fout, w_fout, absdelta, mf)


HALO = 8


def _conv_slab(u_ref, bi, a, w, b, st):
    r0 = a * SLAB
    x = u_ref[bi, pl.ds(pl.multiple_of(r0, SLAB), SLAB), :].astype(F32)
    before = u_ref[bi, pl.ds(pl.multiple_of(jnp.maximum(r0 - 16, 0), 16), 16), :].astype(F32)
    after = u_ref[bi, pl.ds(pl.multiple_of(jnp.minimum(r0 + SLAB, SEQ - 16), 16), 16), :].astype(F32)
    st[0:HALO, :] = jnp.where(a > 0, before[16 - HALO:16, :], 0.0)
    st[HALO:HALO + SLAB, :] = x
    st[HALO + SLAB:2 * HALO + SLAB, :] = jnp.where(a < N_IN - 1, after[0:HALO, :], 0.0)
    xm1 = st[HALO - 1:HALO - 1 + SLAB, :]
    xp1 = st[HALO + 1:HALO + 1 + SLAB, :]
    return xm1 * w[0:1, :] + x * w[1:2, :] + xp1 * w[2:3, :] + b


def _hyena_kernel(uv_ref, u1_ref, u2_ref, cwv_ref, cw1_ref, cw2_ref, cbv_ref, cb1_ref, cb2_ref, skip_ref,
                  mf_ref, mi_ref, h_ref, o_ref, zc_ref, g_ref, s_ref, st_ref):
    o = pl.program_id(2)

    @pl.when(o == 0)
    def _():
        def conv(a, carry):
            for bi in range(2):
                lanes = slice(bi * HY_CT, (bi + 1) * HY_CT)
                st = lambda k: st_ref.at[3 * bi + k]
                zc_ref[a, 0:SLAB, lanes] = _conv_slab(uv_ref, bi, a, cwv_ref[...], cbv_ref[...], st(0)).astype(BF16)
                g_ref[0, a, 0:SLAB, lanes] = _conv_slab(u1_ref, bi, a, cw1_ref[...], cb1_ref[...], st(1)).astype(BF16)
                g_ref[1, a, 0:SLAB, lanes] = _conv_slab(u2_ref, bi, a, cw2_ref[...], cb2_ref[...], st(2)).astype(BF16)
            return carry

        lax.fori_loop(0, N_IN, conv, 0)

    def im_sign(b):
        return -1 if b > N_HI // 2 else 1

    def forward_chunk(rc, carry):
        rows = pl.ds(pl.multiple_of(rc * HY_ROWS, HY_ROWS), HY_ROWS)
        rows_im = pl.ds(pl.multiple_of(S_IM + rc * HY_ROWS, HY_ROWS), HY_ROWS)
        xs = [((zc_ref[a, rows, 0:HY_CT], 1), (zc_ref[a, rows, HY_CT:2 * HY_CT], 1)) for a in range(N_IN)]
        like = xs[0][0][0]
        out = _slab_fft_dit(xs + [None] * (N_HI - N_IN), -1, set(range(N_HI)))
        for b in range(N_HI):
            re, im = out[b]
            s_ref[b, rows, :] = _part_value(re, like)
            s_ref[b, rows_im, :] = _part_value(None if im is None else (im[0], im[1] * im_sign(b)), like)
        return carry

    lax.fori_loop(0, SLAB // HY_ROWS, forward_chunk, 0)

    def through_filter(slabs, j):
        h = h_ref[j * SLAB:(j + 1) * SLAB, :].astype(F32)
        hr = jnp.concatenate([h[:, :HY_CT]] * len(slabs), axis=1)
        hi = jnp.concatenate([h[:, HY_CT:]] * len(slabs), axis=1)
        load = lambda b: jnp.concatenate([s_ref[b, 0:SLAB, :], s_ref[b, S_IM:S_IM + SLAB, :]], axis=0)
        d = load(slabs[0]) if len(slabs) == 1 else jnp.concatenate([load(b) for b in slabs], axis=1)
        x = jnp.dot(mf_ref[j], d, preferred_element_type=F32)
        xr, xi = x[:SLAB], x[SLAB:]
        y = jnp.concatenate([xr * hr - xi * hi, xr * hi + xi * hr], axis=0).astype(BF16)
        w = jnp.dot(mi_ref[j], y, preferred_element_type=F32).astype(BF16)
        return [(b, w[:, k * HY_CT:(k + 1) * HY_CT]) for k, b in enumerate(slabs)]

    for slabs, j in [([0], 0), ([N_HI // 2], N_HI // 2)] + [([j, N_HI - j], j) for j in range(1, N_HI // 2)]:
        for b, w in through_filter(slabs, j):
            s_ref[b, 0:SLAB, :] = w[:SLAB]
            s_ref[b, S_IM:S_IM + SLAB, :] = w[SLAB:]

    def inverse_chunk(rc, carry):
        rows = pl.ds(pl.multiple_of(rc * HY_ROWS, HY_ROWS), HY_ROWS)
        rows_im = pl.ds(pl.multiple_of(S_IM + rc * HY_ROWS, HY_ROWS), HY_ROWS)
        xs = [((s_ref[b, rows, :], 1), (s_ref[b, rows_im, :], im_sign(b))) for b in range(N_HI)]
        like = xs[0][0][0]
        out = _slab_fft_dif(xs, 1, set(range(N_IN)))
        for a in range(N_IN):
            s_ref[a, rows, :] = _part_value(out[a][0], like)
            s_ref[a, rows_im, :] = _part_value(out[a][1], like)
        return carry

    lax.fori_loop(0, SLAB // HY_ROWS, inverse_chunk, 0)

    sk = skip_ref[pl.ds(o, 1), :]
    skip2 = jnp.concatenate([sk, sk], axis=1)

    def gate(a, carry):
        y = jnp.concatenate([s_ref[a, 0:SLAB, :], s_ref[a, S_IM:S_IM + SLAB, :]], axis=1).astype(F32)
        z = g_ref[o, a, 0:SLAB, :].astype(F32) * (y + skip2 * zc_ref[a, 0:SLAB, :].astype(F32))
        zc_ref[a, 0:SLAB, :] = z.astype(BF16)
        return carry

    lax.fori_loop(0, N_IN, gate, 0)

    @pl.when(o == HYENA_ORDER - 1)
    def _():
        def emit(a, carry):
            rows = pl.ds(pl.multiple_of(a * SLAB, SLAB), SLAB)
            for bi in range(2):
                o_ref[bi, rows, :] = zc_ref[a, 0:SLAB, bi * HY_CT:(bi + 1) * HY_CT]
            return carry

        lax.fori_loop(0, N_IN, emit, 0)


def _hyena_conv(p, conv_w, conv_b, skip, mf, mi, hspec):
    b = p.shape[0]
    u0 = V_END // HY_CT
    cb = conv_b.reshape(1, -1)
    part = lambda k: pl.BlockSpec((2, SEQ, HY_CT), lambda c, q, o: (q, 0, u0 + k * N_CT + c))
    cw = lambda k: pl.BlockSpec((SHORT_CONV, HY_CT), lambda c, q, o: (0, k * N_CT + c))
    cbs = lambda k: pl.BlockSpec((1, HY_CT), lambda c, q, o: (0, k * N_CT + c))
    const = lambda: pl.BlockSpec((N_SPEC, 2 * SLAB, 2 * SLAB), lambda c, q, o: (0, 0, 0),
                                 pipeline_mode=pl.Buffered(1))
    return pl.pallas_call(
        _hyena_kernel,
        grid=(N_CT, b // 2, HYENA_ORDER),
        in_specs=[part(0), part(1), part(2), cw(0), cw(1), cw(2), cbs(0), cbs(1), cbs(2),
                  pl.BlockSpec((HYENA_ORDER, HY_CT), lambda c, q, o: (0, c)),
                  const(), const(),
                  pl.BlockSpec((None, None, N_SPEC * SLAB, 2 * HY_CT), lambda c, q, o: (o, c, 0, 0))],
        out_specs=pl.BlockSpec((2, SEQ, HY_CT), lambda c, q, o: (q, 0, c)),
        out_shape=jax.ShapeDtypeStruct((b, SEQ, HYENA_WIDTH), BF16),
        scratch_shapes=[pltpu.VMEM((N_IN, SLAB + SLAB_PAD, 2 * HY_CT), BF16),
                        pltpu.VMEM((HYENA_ORDER, N_IN, SLAB + SLAB_PAD, 2 * HY_CT), BF16),
                        pltpu.VMEM((N_HI, 2 * (SLAB + SLAB_PAD), HY_CT), BF16),
                        pltpu.VMEM((2 * (HYENA_ORDER + 1), SLAB + 2 * HALO, HY_CT), F32)],
        compiler_params=_cparams(("arbitrary", "arbitrary", "arbitrary"), 56),
        name="hyena_conv",
    )(p, p, p, conv_w, conv_w, conv_w, cb, cb, cb, skip, mf, mi, hspec)


def _ctx_hyena_kernel(uv_ref, u1_ref, u2_ref, cwv_ref, cw1_ref, cw2_ref, cbv_ref, cb1_ref, cb2_ref, skip_ref,
                      feat_ref, wf0_ref, wb0_ref, wf1_ref, wb1_ref, ad_ref, ff_ref, ai_ref, o_ref):
    l = CTX_LEN
    nb = uv_ref.shape[0]
    n = lax.broadcasted_iota(jnp.int32, (2 * l, HY_CT), 0)
    spectra = []
    no_w = jnp.zeros((FILTER_HIDDEN, HY_CT), F32)
    for wf_ref, wb_ref in ((wf0_ref, wb0_ref), (wf1_ref, wb1_ref)):
        raw = jnp.concatenate(
            [jnp.dot(feat_ref[...], jnp.concatenate([wf_ref[...], no_w], axis=0), precision=HIGHEST,
                     preferred_element_type=F32),
             jnp.dot(feat_ref[...], jnp.concatenate([no_w, wb_ref[...]], axis=0), precision=HIGHEST,
                     preferred_element_type=F32)], axis=0)
        v = _filter_taps(raw, n, l, ad_ref[...])
        v = v / jnp.sum(jnp.abs(v), axis=0, keepdims=True)
        pq = jnp.dot(ff_ref[...], v.astype(BF16), preferred_element_type=F32)
        spectra.append((pq[:2 * l], pq[2 * l:]))

    row = lax.broadcasted_iota(jnp.int32, (l, HY_CT), 0)

    def conv(u_ref, bi, w_ref, b_ref):
        x = u_ref[bi].astype(F32)
        xm1 = jnp.where(row == 0, 0.0, pltpu.roll(x, 1, 0))
        xp1 = jnp.where(row == l - 1, 0.0, pltpu.roll(x, l - 1, 0))
        return xm1 * w_ref[0:1, :] + x * w_ref[1:2, :] + xp1 * w_ref[2:3, :] + b_ref[...]

    z = [conv(uv_ref, bi, cwv_ref, cbv_ref) for bi in range(nb)]
    gates = [[conv(u1_ref, bi, cw1_ref, cb1_ref) for bi in range(nb)],
             [conv(u2_ref, bi, cw2_ref, cb2_ref) for bi in range(nb)]]
    for o in range(HYENA_ORDER):
        hr, hi = spectra[o]
        pq = jnp.dot(ff_ref[:, 0:l], jnp.concatenate(z, axis=1).astype(BF16), preferred_element_type=F32)
        ys = []
        for q in range(nb // 2):
            xr, xi = _cplx_from_pq(pq[:, 2 * q * HY_CT:2 * (q + 1) * HY_CT], False)
            ys += [xr * hr - xi * hi, xr * hi + xi * hr]
        pq2 = jnp.dot(ai_ref[...], jnp.concatenate(ys, axis=1).astype(BF16), preferred_element_type=F32)
        y = []
        for q in range(nb // 2):
            y += list(_cplx_from_pq(pq2[:, 2 * q * HY_CT:2 * (q + 1) * HY_CT], False))
        z = [gates[o][bi] * (y[bi] + skip_ref[o:o + 1, :] * z[bi]) for bi in range(nb)]
    for bi in range(nb):
        o_ref[bi] = z[bi].astype(BF16)


def _ctx_hyena(pc, conv_w, conv_b, skip, feat, w_fout, absdelta, ff, ai):
    b, l, _ = pc.shape
    u0 = V_END // HY_CT
    cb = conv_b.reshape(1, -1)
    cols = 2 * N_CT
    part = lambda k: pl.BlockSpec((b, l, HY_CT), lambda c: (0, 0, u0 + k * N_CT + c))
    cw = lambda k: pl.BlockSpec((SHORT_CONV, HY_CT), lambda c: (0, k * N_CT + c))
    cbs = lambda k: pl.BlockSpec((1, HY_CT), lambda c: (0, k * N_CT + c))
    wcol = lambda off: pl.BlockSpec((FILTER_HIDDEN, HY_CT), lambda c: (0, off + c))
    return pl.pallas_call(
        _ctx_hyena_kernel,
        grid=(N_CT,),
        in_specs=[part(0), part(1), part(2), cw(0), cw(1), cw(2), cbs(0), cbs(1), cbs(2),
                  pl.BlockSpec((HYENA_ORDER, HY_CT), lambda c: (0, c)),
                  pl.BlockSpec((l, 2 * FILTER_HIDDEN), lambda c: (0, 0)),
                  wcol(0), wcol(N_CT), wcol(cols), wcol(cols + N_CT),
                  pl.BlockSpec((1, HY_CT), lambda c: (0, c)),
                  pl.BlockSpec((4 * l, 2 * l), lambda c: (0, 0)),
                  pl.BlockSpec((2 * l, 2 * l), lambda c: (0, 0))],
        out_specs=pl.BlockSpec((b, l, HY_CT), lambda c: (0, 0, c)),
        out_shape=jax.ShapeDtypeStruct((b, l, HYENA_WIDTH), BF16),
        compiler_params=_cparams(("arbitrary",), 32),
        name="ctx_hyena",
    )(pc, pc, pc, conv_w, conv_w, conv_w, cb, cb, cb, skip, feat, w_fout, w_fout, w_fout, w_fout, absdelta, ff, ai)


def _rope_tables(l):
    rows = l // GRID_W
    row = np.repeat(np.arange(rows), GRID_W).astype(np.float64)
    col = np.tile(np.arange(GRID_W), rows).astype(np.float64)
    n_freq = HEAD_DIM // 4
    inv = ROPE_BASE ** (-np.arange(n_freq) / n_freq)
    ang_r = row[:, None] * inv
    ang_c = col[:, None] * inv
    cos = np.concatenate([np.cos(ang_r), np.cos(ang_r), np.cos(ang_c), np.cos(ang_c)], axis=-1)
    sin = np.concatenate([-np.sin(ang_r), np.sin(ang_r), -np.sin(ang_c), np.sin(ang_c)], axis=-1)
    return jnp.asarray(cos, F32), jnp.asarray(sin, F32)


def kernel(x, c, ctx, c_ctx, w_ada, b_ada, norm_mix, norm_mlp, w_in, q_norm, k_norm, attn_sink, conv_w, conv_b, filt_w1, filt_b1, filt_w2, filt_b2, filt_w3, filt_b3, filt_freq, filt_w_out, hyena_skip, w_out, w_mlp_in, w_mlp_out):
    cos, sin = _rope_tables(SEQ)
    emb, emb_ctx = _filter_positions(SEQ), _filter_positions(CTX_LEN)
    m_tall, mf, mi = _dft_constants()
    ff_ctx, ai_ctx = _ctx_dft_constants()
    absdelta = jnp.abs(jnp.linspace(MIN_DECAY, MAX_DECAY, HYENA_WIDTH, dtype=F32)).reshape(1, HYENA_WIDTH)
    cvec = jnp.zeros((MOD_ROWS, D_MODEL), F32).at[:BATCH].set(c).at[CTX_ROW].set(c_ctx)
    mod_all = _modulation(cvec, w_ada, b_ada).reshape(DEPTH, MOD_ROWS, N_MOD, D_MODEL)
    w_in_bf = w_in[0:1].astype(BF16)
    ctx_rows = BATCH * CTX_LEN
    flat = lambda a: a.reshape(1, ctx_rows, a.shape[-1])
    per_batch = lambda a: a.reshape(BATCH, CTX_LEN, a.shape[-1])
    xc = flat(ctx)
    for i in range(DEPTH):
        update_ctx = i < DEPTH - 1
        mod = mod_all[i]
        nmix = norm_mix[i].reshape(1, D_MODEL)
        nmlp = norm_mlp[i].reshape(1, D_MODEL)
        qn = q_norm[i].reshape(1, HEAD_DIM)
        kn = k_norm[i].reshape(1, HEAD_DIM)
        filt_mlp = (filt_w1[i], filt_b1[i], filt_w2[i], filt_b2[i], filt_w3[i], filt_b3[i], filt_freq[i])

        p, (w_out_bf, w1_bf, w2_bf) = _in_projection(
            x, mod, None, nmix, w_in_bf, qn, kn, cos, sin, tm=512, n_off=0, n_tiles=IN_COLS // IN_TN,
            use_rope=True, riders=[(w_out, i), (w_mlp_in, i), (w_mlp_out, i)])
        if update_ctx:
            pc, _ = _in_projection(xc, mod, CTX_ROW, nmix, w_in_bf, qn, kn, cos, sin,
                                   tm=512, n_off=0, n_tiles=IN_COLS // IN_TN, use_rope=False)
            kx_blk, vx_blk = Q_END // KV_COLS * 2, Q_END // KV_COLS * 2 + 1
        else:
            pc, _ = _in_projection(xc, mod, CTX_ROW, nmix, w_in_bf, qn, kn, cos, sin,
                                   tm=512, n_off=KV_TILE, n_tiles=1, use_rope=False)
            kx_blk, vx_blk = 0, 1
        pc = per_batch(pc)

        attn = _window_attention(p, pc, kx_blk, vx_blk, attn_sink[i])
        hspec = _filter_spectrum(_filter_features(emb, *filt_mlp), filt_w_out[i], absdelta, m_tall)
        hy = _hyena_conv(p, conv_w[i], conv_b[i], hyena_skip[i], mf, mi, hspec)
        x, h2, next_w_in = _out_projection(attn, hy, x, mod, None, nmlp, w_out_bf, tm=512,
                                           riders=[(w_in, i + 1)] if update_ctx else [])
        x = _mlp(h2, x, mod, None, w1_bf, w2_bf, tm=512, tf=1024)

        if update_ctx:
            attn_c = _context_attention(pc, attn_sink[i])
            hy_c = _ctx_hyena(pc, conv_w[i], conv_b[i], hyena_skip[i], _filter_features(emb_ctx, *filt_mlp),
                              filt_w_out[i], absdelta, ff_ctx, ai_ctx)
            xc, h2c, _ = _out_projection(flat(attn_c), flat(hy_c), xc, mod, CTX_ROW, nmlp, w_out_bf, tm=512)
            xc = _mlp(h2c, xc, mod, CTX_ROW, w1_bf, w2_bf, tm=512, tf=1024)
            w_in_bf = next_w_in[0]
    return x
```

```python
import cmath
import functools
import math

import jax
import jax.numpy as jnp
import numpy as np
from jax import lax
from jax.experimental import pallas as pl
from jax.experimental.pallas import tpu as pltpu

D_MODEL = 2048
BATCH = 4
SEQ = 4096
DEPTH = 2
GRID_W = 64
CTX_LEN = 256
HEAD_DIM = 128
N_Q_HEADS = 8
N_KV_HEADS = 2
GQA_GROUP = N_Q_HEADS // N_KV_HEADS
ATTN_WIDTH = N_Q_HEADS * HEAD_DIM
HYENA_WIDTH = D_MODEL - ATTN_WIDTH
HYENA_ORDER = 2
SHORT_CONV = 3
FILTER_EMB = 33
DECAY_TARGET = 1e-2
FAST_DECAY_PCT = 0.3
SLOW_DECAY_PCT = 1.5
MAX_DECAY = math.log(DECAY_TARGET) / FAST_DECAY_PCT
MIN_DECAY = math.log(DECAY_TARGET) / SLOW_DECAY_PCT
WINDOW = 128
BLOCK = 128
ROPE_BASE = 10000.0
D_FF = 4 * D_MODEL
N_MOD = 6
EPS = 1e-6
NEG_INF = -1e30
Q_END = ATTN_WIDTH
K_END = Q_END + N_KV_HEADS * HEAD_DIM
V_END = K_END + N_KV_HEADS * HEAD_DIM
IN_COLS = V_END + (HYENA_ORDER + 1) * HYENA_WIDTH

MOD_ROWS = 8
CTX_ROW = BATCH
KV_COLS = 2 * N_KV_HEADS * HEAD_DIM

F32 = jnp.float32
BF16 = jnp.bfloat16
MIB = 1024 * 1024


def _cparams(sem, vmem_mib):
    return pltpu.CompilerParams(dimension_semantics=sem, vmem_limit_bytes=vmem_mib * MIB)


def _mod_kernel(c_ref, w_ref, b_ref, o_ref):
    c = c_ref[...]
    s = (c * jax.nn.sigmoid(c)).astype(BF16)
    o_ref[...] = jnp.dot(s, w_ref[...].astype(BF16), preferred_element_type=F32) + b_ref[...]


def _modulation(cvec, w_ada, b_ada):
    tn = 1024
    n_cols = N_MOD * D_MODEL
    return pl.pallas_call(
        _mod_kernel,
        grid=(DEPTH, n_cols // tn),
        in_specs=[
            pl.BlockSpec((MOD_ROWS, D_MODEL), lambda l, j: (0, 0)),
            pl.BlockSpec((None, D_MODEL, tn), lambda l, j: (l, 0, j)),
            pl.BlockSpec((None, 1, tn), lambda l, j: (l, 0, j)),
        ],
        out_specs=pl.BlockSpec((None, MOD_ROWS, tn), lambda l, j: (l, 0, j)),
        out_shape=jax.ShapeDtypeStruct((DEPTH, MOD_ROWS, n_cols), F32),
        compiler_params=_cparams(("arbitrary", "arbitrary"), 40),
        name="adaln_mod",
    )(cvec, w_ada, b_ada.reshape(DEPTH, 1, n_cols))


IN_TN = 512
Q_TILES = Q_END // IN_TN
KV_TILE = Q_TILES
HEADS_PER_TILE = IN_TN // HEAD_DIM


SUB_ROWS = 256


def _rms_modulate(xf, gain, shift):
    ms = jnp.mean(xf * xf, axis=-1, keepdims=True)
    return xf * lax.rsqrt(ms + EPS) * gain + shift


def _head_norm_rope(a, g, cos, sin):
    ms = jnp.mean(a * a, axis=-1, keepdims=True)
    y = a * lax.rsqrt(ms + EPS) * g
    if cos is None:
        return y
    lane = lax.broadcasted_iota(jnp.int32, y.shape, 1)
    partner = jnp.where((lane % 64) < 32, pltpu.roll(y, 96, 1), pltpu.roll(y, 32, 1))
    return y * cos + partner * sin


BF16_TILE_ROWS = 16


def _rider_plan(w, layer, steps, step_of):
    depth, r, c = w.shape
    assert r % (steps * BF16_TILE_ROWS) == 0
    blk = r // steps
    in_spec = pl.BlockSpec((blk, c), lambda *g: (layer * steps + step_of(*g), 0))
    out_spec = pl.BlockSpec((blk, c), lambda *g: (step_of(*g), 0))
    restore = lambda a: a.reshape(1, r, c)
    return w.reshape(depth * r, c), in_spec, out_spec, jax.ShapeDtypeStruct((r, c), BF16), restore


def _cast_riders(in_refs, out_refs):
    for src, dst in zip(in_refs, out_refs):
        dst[...] = src[...].astype(BF16)


def _inproj_kernel(*refs, n_off, n_tiles, use_rope, n_riders):
    x_ref, m_ref, nw_ref, w_ref, qn_ref, kn_ref, cos_ref, sin_ref = refs[:8]
    o_ref = refs[8 + n_riders]
    _cast_riders(refs[8:8 + n_riders], refs[9 + n_riders:])
    gain = nw_ref[...] * (1.0 + m_ref[1:2, :])
    shift = m_ref[0:1, :]
    for r0 in range(0, x_ref.shape[0], SUB_ROWS):
        rows = slice(r0, r0 + SUB_ROWS)
        h = _rms_modulate(x_ref[rows, :], gain, shift).astype(BF16)
        cos = cos_ref[rows, :] if use_rope else None
        sin = sin_ref[rows, :] if use_rope else None
        for n in range(n_tiles):
            cols = slice(n * IN_TN, (n + 1) * IN_TN)
            acc = jnp.dot(h, w_ref[:, cols], preferred_element_type=F32)
            t = n + n_off
            if t > KV_TILE:
                o_ref[rows, cols] = acc.astype(BF16)
                continue
            gains = [qn_ref[...]] * HEADS_PER_TILE if t < Q_TILES else [kn_ref[...]] * N_KV_HEADS + [None] * N_KV_HEADS
            for c, g in enumerate(gains):
                sl = slice(c * HEAD_DIM, (c + 1) * HEAD_DIM)
                osl = slice(n * IN_TN + c * HEAD_DIM, n * IN_TN + (c + 1) * HEAD_DIM)
                if g is None:
                    o_ref[rows, osl] = acc[:, sl].astype(BF16)
                else:
                    o_ref[rows, osl] = _head_norm_rope(acc[:, sl], g, cos, sin).astype(BF16)


def _in_projection(x, mod, mod_row, nw, w_bf, qn, kn, cos, sin, *, tm, n_off, n_tiles, use_rope, riders=()):
    b, t, _ = x.shape
    assert n_off % n_tiles == 0 and tm % SUB_ROWS == 0
    row = (lambda bb: bb) if mod_row is None else (lambda bb: mod_row)
    n_i = t // tm
    plans = [_rider_plan(w, layer, b * n_i, lambda bb, i: bb * n_i + i) for w, layer in riders]
    kern = functools.partial(_inproj_kernel, n_off=n_off, n_tiles=n_tiles, use_rope=use_rope, n_riders=len(plans))
    width = n_tiles * IN_TN
    out = pl.pallas_call(
        kern,
        grid=(b, n_i),
        in_specs=[
            pl.BlockSpec((None, tm, D_MODEL), lambda bb, i: (bb, i, 0)),
            pl.BlockSpec((None, N_MOD, D_MODEL), lambda bb, i: (row(bb), 0, 0)),
            pl.BlockSpec((1, D_MODEL), lambda bb, i: (0, 0)),
            pl.BlockSpec((None, D_MODEL, width), lambda bb, i: (0, 0, n_off // n_tiles),
                         pipeline_mode=pl.Buffered(1)),
            pl.BlockSpec((1, HEAD_DIM), lambda bb, i: (0, 0)),
            pl.BlockSpec((1, HEAD_DIM), lambda bb, i: (0, 0)),
            pl.BlockSpec((tm, HEAD_DIM), lambda bb, i: (i, 0)),
            pl.BlockSpec((tm, HEAD_DIM), lambda bb, i: (i, 0)),
        ] + [pl_[1] for pl_ in plans],
        out_specs=[pl.BlockSpec((None, tm, width), lambda bb, i: (bb, i, 0))] + [pl_[2] for pl_ in plans],
        out_shape=[jax.ShapeDtypeStruct((b, t, width), BF16)] + [pl_[3] for pl_ in plans],
        compiler_params=_cparams(("arbitrary", "arbitrary"), 56),
        name="in_proj",
    )(x, mod, nw, w_bf, qn, kn, cos, sin, *[pl_[0] for pl_ in plans])
    return out[0], [pl_[4](a) for pl_, a in zip(plans, out[1:])]


def _sink_column(sink_ref, kv, rows_per_head):
    rows = GQA_GROUP * rows_per_head
    head = lax.broadcasted_iota(jnp.int32, (rows, 1), 0) // rows_per_head
    col = jnp.full((rows, 1), sink_ref[kv * GQA_GROUP], F32)
    for g in range(1, GQA_GROUP):
        col = jnp.where(head == g, sink_ref[kv * GQA_GROUP + g], col)
    return col


def _stack_heads(q_ref, kv, rows=slice(None)):
    return jnp.concatenate(
        [q_ref[rows, (kv * GQA_GROUP + g) * HEAD_DIM:(kv * GQA_GROUP + g + 1) * HEAD_DIM]
         for g in range(GQA_GROUP)], axis=0)


LOG2E = math.log2(math.e)
QK_LOG2_SCALE = HEAD_DIM ** -0.5 * LOG2E


def _softmax_pv(q4, k_all, v_all, bias, sink_col):
    t = lax.dot_general(q4, k_all, (((1,), (1,)), ((), ())), preferred_element_type=F32) * QK_LOG2_SCALE
    if bias is not None:
        t = t + bias
    sink2 = sink_col * LOG2E
    m = jnp.maximum(jnp.max(t, axis=-1, keepdims=True), sink2)
    e = jnp.exp2(t - m).astype(BF16)
    v_ones = jnp.concatenate([v_all, jnp.ones_like(v_all)], axis=1)
    pv = jnp.dot(e, v_ones, preferred_element_type=F32)
    d = v_all.shape[1]
    return pv[:, :d] / (pv[:, d:] + jnp.exp2(sink2 - m))


Q_BLOCKS = 4


def _window_attn_kernel(sink_ref, bias_first_ref, bias_mid_ref, bias_last_ref, q_ref, kp_ref, km_ref, kn_ref,
                        vp_ref, vm_ref, vn_ref, kx_ref, vx_ref, o_ref):
    for blk in range(Q_BLOCKS):
        bias_ref = bias_first_ref if blk == 0 else bias_last_ref if blk == Q_BLOCKS - 1 else bias_mid_ref
        rows = slice(blk * BLOCK, (blk + 1) * BLOCK)
        for kv in range(N_KV_HEADS):
            ks = slice(kv * HEAD_DIM, (kv + 1) * HEAD_DIM)

            def band(prev_ref, mid_ref, next_ref):
                pieces = [prev_ref[:, ks]] + [mid_ref[j * BLOCK:(j + 1) * BLOCK, ks] for j in range(Q_BLOCKS)]
                return (pieces + [next_ref[:, ks]])[blk:blk + 3]

            k_all = jnp.concatenate([kx_ref[:, ks]] + band(kp_ref, km_ref, kn_ref), axis=0)
            v_all = jnp.concatenate([vx_ref[:, ks]] + band(vp_ref, vm_ref, vn_ref), axis=0)
            o = _softmax_pv(_stack_heads(q_ref, kv, rows), k_all, v_all, bias_ref[...],
                            _sink_column(sink_ref, kv, BLOCK))
            for g in range(GQA_GROUP):
                h = kv * GQA_GROUP + g
                o_ref[rows, h * HEAD_DIM:(h + 1) * HEAD_DIM] = o[g * BLOCK:(g + 1) * BLOCK].astype(BF16)


def _window_bias(nb):
    qi = np.arange(GQA_GROUP * BLOCK)[:, None] % BLOCK
    sj = np.arange(3 * BLOCK)[None, :] - BLOCK
    near = np.abs(sj - qi) <= WINDOW
    out = np.zeros((3, GQA_GROUP * BLOCK, CTX_LEN + 3 * BLOCK), np.float32)
    for variant, blk in enumerate((0, 1, nb - 1)):
        key_pos = blk * BLOCK + sj
        ok = near & (key_pos >= 0) & (key_pos < nb * BLOCK)
        out[variant, :, CTX_LEN:] = np.where(ok, 0.0, NEG_INF)
    return jnp.asarray(out)


def _window_attention(p, pc, kx_blk, vx_blk, sink):
    b, seq, _ = p.shape
    nb = seq // BLOCK
    kvw = N_KV_HEADS * HEAD_DIM
    k_blk, v_blk = Q_END // kvw, K_END // kvw
    assert nb % Q_BLOCKS == 0 and Q_BLOCKS >= 2
    steps = nb // Q_BLOCKS
    prev = lambda bb, i: (bb, jnp.maximum(Q_BLOCKS * i - 1, 0))
    nxt = lambda bb, i: (bb, jnp.minimum(Q_BLOCKS * (i + 1), nb - 1))

    def edge(which, blk):
        return pl.BlockSpec((None, BLOCK, kvw), lambda bb, i: which(bb, i) + (blk,))

    def mid(blk):
        return pl.BlockSpec((None, Q_BLOCKS * BLOCK, kvw), lambda bb, i: (bb, i, blk))

    bias_shape = (None, GQA_GROUP * BLOCK, CTX_LEN + 3 * BLOCK)
    bias = _window_bias(nb)
    return pl.pallas_call(
        _window_attn_kernel,
        grid=(b, steps),
        in_specs=[
            pl.BlockSpec(memory_space=pltpu.SMEM),
            pl.BlockSpec(bias_shape, lambda bb, i: (jnp.where(i == 0, 0, 1), 0, 0)),
            pl.BlockSpec(bias_shape, lambda bb, i: (1, 0, 0)),
            pl.BlockSpec(bias_shape, lambda bb, i: (jnp.where(i == steps - 1, 2, 1), 0, 0)),
            pl.BlockSpec((None, Q_BLOCKS * BLOCK, ATTN_WIDTH), lambda bb, i: (bb, i, 0)),
            edge(prev, k_blk), mid(k_blk), edge(nxt, k_blk),
            edge(prev, v_blk), mid(v_blk), edge(nxt, v_blk),
            pl.BlockSpec((None, CTX_LEN, kvw), lambda bb, i: (bb, 0, kx_blk)),
            pl.BlockSpec((None, CTX_LEN, kvw), lambda bb, i: (bb, 0, vx_blk)),
        ],
        out_specs=pl.BlockSpec((None, Q_BLOCKS * BLOCK, ATTN_WIDTH), lambda bb, i: (bb, i, 0)),
        out_shape=jax.ShapeDtypeStruct((b, seq, ATTN_WIDTH), BF16),
        compiler_params=_cparams(("arbitrary", "arbitrary"), 32),
        name="window_attn",
    )(sink, bias, bias, bias, p, p, p, p, p, p, p, pc, pc)


def _ctx_attn_kernel(sink_ref, q_ref, k_ref, v_ref, o_ref):
    n = q_ref.shape[0]
    for kv in range(N_KV_HEADS):
        ks = slice(kv * HEAD_DIM, (kv + 1) * HEAD_DIM)
        o = _softmax_pv(_stack_heads(q_ref, kv), k_ref[:, ks], v_ref[:, ks], None, _sink_column(sink_ref, kv, n))
        for g in range(GQA_GROUP):
            h = kv * GQA_GROUP + g
            o_ref[:, h * HEAD_DIM:(h + 1) * HEAD_DIM] = o[g * n:(g + 1) * n].astype(BF16)


def _context_attention(pc, sink):
    b, n, _ = pc.shape
    kvw = N_KV_HEADS * HEAD_DIM
    return pl.pallas_call(
        _ctx_attn_kernel,
        grid=(b,),
        in_specs=[
            pl.BlockSpec(memory_space=pltpu.SMEM),
            pl.BlockSpec((None, n, ATTN_WIDTH), lambda bb: (bb, 0, 0)),
            pl.BlockSpec((None, n, kvw), lambda bb: (bb, 0, Q_END // kvw)),
            pl.BlockSpec((None, n, kvw), lambda bb: (bb, 0, K_END // kvw)),
        ],
        out_specs=pl.BlockSpec((None, n, ATTN_WIDTH), lambda bb: (bb, 0, 0)),
        out_shape=jax.ShapeDtypeStruct((b, n, ATTN_WIDTH), BF16),
        compiler_params=_cparams(("arbitrary",), 32),
        name="ctx_attn",
    )(sink, pc, pc, pc)


def _outproj_kernel(*refs, n_riders):
    a_ref, y_ref, x_ref, m_ref, nw_ref, wa_ref, wy_ref = refs[:7]
    o_ref, h_ref = refs[7 + n_riders], refs[8 + n_riders]
    _cast_riders(refs[7:7 + n_riders], refs[9 + n_riders:])
    gain = nw_ref[...] * (1.0 + m_ref[4:5, :])
    shift = m_ref[3:4, :]
    for r0 in range(0, x_ref.shape[0], SUB_ROWS):
        rows = slice(r0, r0 + SUB_ROWS)
        mix = (jnp.dot(a_ref[rows, :], wa_ref[...], preferred_element_type=F32)
               + jnp.dot(y_ref[rows, :], wy_ref[...], preferred_element_type=F32))
        x1 = x_ref[rows, :] + m_ref[2:3, :] * mix
        o_ref[rows, :] = x1
        h_ref[rows, :] = _rms_modulate(x1, gain, shift).astype(BF16)


def _out_projection(attn, hy, x, mod, mod_row, nw, w_bf, *, tm, riders=()):
    b, t, _ = x.shape
    assert tm % SUB_ROWS == 0
    n_i = t // tm
    row = (lambda bb: bb) if mod_row is None else (lambda bb: mod_row)
    tile = lambda width: pl.BlockSpec((None, tm, width), lambda bb, i: (bb, i, 0))
    half = lambda k: pl.BlockSpec((None, D_MODEL // 2, D_MODEL), lambda bb, i: (0, k, 0),
                                  pipeline_mode=pl.Buffered(1))
    plans = [_rider_plan(w, layer, b * n_i, lambda bb, i: bb * n_i + i) for w, layer in riders]
    out = pl.pallas_call(
        functools.partial(_outproj_kernel, n_riders=len(plans)),
        grid=(b, n_i),
        in_specs=[
            tile(ATTN_WIDTH), tile(HYENA_WIDTH), tile(D_MODEL),
            pl.BlockSpec((None, N_MOD, D_MODEL), lambda bb, i: (row(bb), 0, 0)),
            pl.BlockSpec((1, D_MODEL), lambda bb, i: (0, 0)),
            half(0), half(1),
        ] + [pl_[1] for pl_ in plans],
        out_specs=[tile(D_MODEL), tile(D_MODEL)] + [pl_[2] for pl_ in plans],
        out_shape=[jax.ShapeDtypeStruct(x.shape, F32), jax.ShapeDtypeStruct(x.shape, BF16)]
        + [pl_[3] for pl_ in plans],
        compiler_params=_cparams(("arbitrary", "arbitrary"), 48),
        name="out_proj",
    )(attn, hy, x, mod, nw, w_bf, w_bf, *[pl_[0] for pl_ in plans])
    return out[0], out[1], [pl_[4](a) for pl_, a in zip(plans, out[2:])]


MLP_OUT_CHUNK = 512


def _mlp_kernel(h_ref, x_ref, m_ref, w1_ref, w2_ref, o_ref, *, n_f):
    f = pl.program_id(2)

    @pl.when(f == 0)
    def _():
        o_ref[...] = jnp.zeros_like(o_ref)

    a = jnp.maximum(jnp.dot(h_ref[...], w1_ref[...], preferred_element_type=F32), 0.0)
    a = (a * a).astype(BF16)
    for n0 in range(0, D_MODEL, MLP_OUT_CHUNK):
        sl = slice(n0, n0 + MLP_OUT_CHUNK)
        o_ref[:, sl] += jnp.dot(a, w2_ref[:, sl], preferred_element_type=F32)

    @pl.when(f == n_f - 1)
    def _():
        o_ref[...] = x_ref[...] + m_ref[5:6, :] * o_ref[...]


def _mlp(h, x, mod, mod_row, w1_bf, w2_bf, *, tm, tf):
    b, t, _ = x.shape
    n_f = D_FF // tf
    row = (lambda bb: bb) if mod_row is None else (lambda bb: mod_row)
    tile = pl.BlockSpec((None, tm, D_MODEL), lambda bb, i, f: (bb, i, 0))
    return pl.pallas_call(
        functools.partial(_mlp_kernel, n_f=n_f),
        grid=(b, t // tm, n_f),
        in_specs=[
            tile, tile,
            pl.BlockSpec((None, N_MOD, D_MODEL), lambda bb, i, f: (row(bb), 0, 0)),
            pl.BlockSpec((None, D_MODEL, tf), lambda bb, i, f: (0, 0, f)),
            pl.BlockSpec((None, tf, D_MODEL), lambda bb, i, f: (0, f, 0)),
        ],
        out_specs=tile,
        out_shape=jax.ShapeDtypeStruct(x.shape, F32),
        compiler_params=_cparams(("arbitrary", "arbitrary", "arbitrary"), 56),
        name="mlp",
    )(h, x, mod, w1_bf, w2_bf)


def _tile_const(k, like):
    return jnp.full(like.shape, k, F32).astype(like.dtype)


def _p_add(p, q):
    if p is None:
        return q
    if q is None:
        return p
    (a, sa), (b, sb) = p, q
    if sa == sb:
        return (a + b, sa)
    return (a - b, 1) if sa > 0 else (b - a, 1)


def _p_scale(p, k):
    if p is None or k == 0.0:
        return None
    a, s = p
    if k < 0:
        s, k = -s, -k
    return (a, s) if k == 1.0 else (a * _tile_const(k, a), s)


def _c_add(x, y, sign=1):
    if y is None:
        return x
    if x is None:
        x = (None, None)
    return (_p_add(x[0], _p_scale(y[0], sign)), _p_add(x[1], _p_scale(y[1], sign)))


def _snap(v):
    for t in (0.0, 1.0, -1.0):
        if abs(v - t) < 1e-12:
            return t
    return v


def _c_mulc(x, c):
    if x is None:
        return None
    cr, ci = _snap(c.real), _snap(c.imag)
    re, im = x
    if cr != 0.0 and ci != 0.0 and abs(abs(cr) - abs(ci)) < 1e-12:
        k, sr, si = abs(cr), math.copysign(1.0, cr), math.copysign(1.0, ci)
        return (_p_scale(_p_add(_p_scale(re, sr), _p_scale(im, -si)), k),
                _p_scale(_p_add(_p_scale(re, si), _p_scale(im, sr)), k))
    return (_p_add(_p_scale(re, cr), _p_scale(im, -ci)), _p_add(_p_scale(re, ci), _p_scale(im, cr)))


def _is_zero(x):
    return x is None or (x[0] is None and x[1] is None)


def _slab_fft_dit(xs, sign, need):
    n = len(xs)
    if n == 1:
        return {0: xs[0]}
    if all(_is_zero(x) for x in xs):
        return {k: None for k in need}
    h = n // 2
    sub = {k % h for k in need}
    ev = _slab_fft_dit(xs[0::2], sign, sub)
    od = _slab_fft_dit(xs[1::2], sign, sub)
    out = {}
    for k in sub:
        t = _c_mulc(od[k], cmath.exp(sign * 2j * math.pi * k / n))
        if k in need:
            out[k] = _c_add(ev[k], t)
        if k + h in need:
            out[k + h] = _c_add(ev[k], t, -1)
    return out


def _slab_fft_dif(xs, sign, need):
    n = len(xs)
    if n == 1:
        return {0: xs[0]}
    h = n // 2
    need_e = {k // 2 for k in need if k % 2 == 0}
    need_o = {k // 2 for k in need if k % 2 == 1}
    out = {}
    if need_e:
        ev = _slab_fft_dif([_c_add(xs[j], xs[j + h]) for j in range(h)], sign, need_e)
        out.update({2 * k: v for k, v in ev.items()})
    if need_o:
        dif = [_c_mulc(_c_add(xs[j], xs[j + h], -1), cmath.exp(sign * 2j * math.pi * j / n)) for j in range(h)]
        od = _slab_fft_dif(dif, sign, need_o)
        out.update({2 * k + 1: v for k, v in od.items()})
    return out


def _part_value(p, like):
    if p is None:
        return jnp.zeros_like(like)
    return p[0] if p[1] > 0 else -p[0]


HY_CT = 128
SLAB = 256
N_HI = 2 * SEQ // SLAB
N_IN = SEQ // SLAB
N_SPEC = N_HI // 2 + 1
ROW_CHUNK = 8
HY_ROWS = 16
S_IM = SLAB
N_CT = HYENA_WIDTH // HY_CT
SPEC_GROUP = 4
FILTER_HIDDEN = 64
EMB_PAD = 128
HIGHEST = lax.Precision.HIGHEST


def _dft_constants():
    n = 2 * SEQ
    r = np.arange(SLAB)
    tall = np.zeros((N_SPEC, 2 * SLAB, SLAB))
    fwd = np.zeros((N_SPEC, 2 * SLAB, 2 * SLAB))
    inv = np.zeros((N_SPEC, 2 * SLAB, 2 * SLAB))
    for b in range(N_SPEC):
        m = np.exp(-2j * np.pi * (np.outer(r, r) / SLAB + (r * b)[None, :] / n))
        a = m.T / n
        tall[b] = np.concatenate([m.real, m.imag], axis=0)
        fwd[b] = np.block([[m.real, -m.imag], [m.imag, m.real]])
        inv[b] = np.block([[a.real, a.imag], [-a.imag, a.real]])
    return tuple(jnp.asarray(t, F32).astype(BF16) for t in (tall, fwd, inv))


def _ctx_dft_constants():
    n = 2 * CTX_LEN
    k = np.arange(n)
    f = np.exp(-2j * np.pi * np.outer(k, k) / n)
    a = np.conj(f)[:CTX_LEN, :] / n
    return (jnp.asarray(np.concatenate([f.real, f.imag], axis=0), F32).astype(BF16),
            jnp.asarray(np.concatenate([a.real, a.imag], axis=0), F32).astype(BF16))


def _filter_positions(l):
    t = np.linspace(0.0, 1.0, l)[:, None]
    bands = (FILTER_EMB - 1) // 2
    f = np.linspace(1e-4, bands - 1, bands)[None, :]
    w = 2.0 * math.pi * np.arange(l)[:, None] / l
    z = np.concatenate([t, np.cos(f * w), -np.sin(f * w)], axis=-1)
    z2 = np.concatenate([z, z[0:1], z[1:][::-1]], axis=0)
    return jnp.asarray(np.pad(z2, ((0, 0), (0, EMB_PAD - FILTER_EMB))), F32)


def _filter_feat_kernel(za_ref, zb_ref, w1_ref, b1_ref, w2_ref, b2_ref, w3_ref, b3_ref, fr_ref, o_ref):
    fr = fr_ref[...]
    dot = functools.partial(jnp.dot, precision=HIGHEST, preferred_element_type=F32)
    z = jnp.concatenate([za_ref[...], zb_ref[...]], axis=1)
    h = jnp.sin(fr * (dot(z, w1_ref[...]) + b1_ref[...]))
    h = jnp.sin(fr * (dot(h, w2_ref[...]) + b2_ref[...]))
    o_ref[...] = jnp.sin(fr * (dot(h, w3_ref[...]) + b3_ref[...]))


def _filter_features(emb, w1, b1, w2, b2, w3, b3, freq):
    l = emb.shape[0] // 2
    tr = min(l, 1024)
    nblk = l // tr
    diag2 = lambda w: jnp.kron(jnp.eye(2, dtype=F32), w)
    w1p = jnp.pad(w1, ((0, EMB_PAD - FILTER_EMB), (0, 0)))
    vec = lambda v: jnp.tile(v.reshape(1, FILTER_HIDDEN), (1, 2))
    full = lambda shape: pl.BlockSpec(shape, lambda i: (0, 0))
    wide = 2 * FILTER_HIDDEN
    return pl.pallas_call(
        _filter_feat_kernel,
        grid=(nblk,),
        in_specs=[pl.BlockSpec((tr, EMB_PAD), lambda i: (i, 0)),
                  pl.BlockSpec((tr, EMB_PAD), lambda i: (i + nblk, 0)),
                  full((2 * EMB_PAD, wide)), full((1, wide)),
                  full((wide, wide)), full((1, wide)),
                  full((wide, wide)), full((1, wide)),
                  full((1, wide))],
        out_specs=pl.BlockSpec((tr, wide), lambda i: (i, 0)),
        out_shape=jax.ShapeDtypeStruct((l, wide), F32),
        compiler_params=_cparams(("arbitrary",), 32),
        name="filter_feat",
    )(emb, emb, diag2(w1p), vec(b1), diag2(w2), vec(b2), diag2(w3), vec(b3), vec(freq))


def _filter_taps(raw, n, l, absdelta):
    m = jnp.where(n < l, n, 2 * l - n)
    t = m.astype(F32) / (l - 1)
    return jnp.where(n == l, 0.0, raw * jnp.exp(-t * absdelta))


def _cplx_from_pq(pq, conj):
    half = pq.shape[0] // 2
    pr, pi_ = pq[:half, :HY_CT], pq[:half, HY_CT:]
    qr, qi = pq[half:, :HY_CT], pq[half:, HY_CT:]
    if conj:
        return pr + qi, pi_ - qr
    return pr - qi, qr + pi_


def _slab_forward(src_ref, n_in, real_only, dst_ref, need, scale):
    def body(rc, carry):
        rows = pl.ds(pl.multiple_of(rc * ROW_CHUNK, ROW_CHUNK), ROW_CHUNK)
        xs = []
        for a in range(N_HI):
            if a >= n_in:
                xs.append(None)
                continue
            re = src_ref[a, rows, 0:HY_CT]
            if scale is not None:
                re = re * scale
            xs.append(((re, 1), None if real_only else (src_ref[a, rows, HY_CT:2 * HY_CT], 1)))
        like = xs[0][0][0]
        out = _slab_fft_dit(xs, -1, need)
        for b in sorted(need):
            c = out[b] or (None, None)
            dst_ref[b, rows, 0:HY_CT] = _part_value(c[0], like)
            dst_ref[b, rows, HY_CT:2 * HY_CT] = _part_value(c[1], like)
        return carry

    lax.fori_loop(0, SLAB // ROW_CHUNK, body, 0)


def _filter_spec_kernel(feat_ref, wf_ref, wb_ref, ad_ref, mf_ref, o_ref, ft_ref, s_ref):
    rate = ad_ref[...] * (-1.0 / (SEQ - 1))
    row = lax.broadcasted_iota(jnp.int32, (SLAB, HY_CT), 0)
    row_f = row.astype(F32)
    first_row = row == 0
    no_w = jnp.zeros((FILTER_HIDDEN, HY_CT), F32)
    w_fwd = jnp.concatenate([wf_ref[...], no_w], axis=0).astype(BF16)
    w_bwd = jnp.concatenate([no_w, wb_ref[...]], axis=0).astype(BF16)

    def taps(a, acc):
        fwd = a < N_IN
        src = jnp.where(fwd, a, a - N_IN)
        rows = feat_ref[pl.ds(pl.multiple_of(src * SLAB, SLAB), SLAB), :]
        raw = jnp.dot(rows.astype(BF16), jnp.where(fwd, w_fwd, w_bwd), preferred_element_type=F32)
        base = jnp.where(fwd, a * SLAB, 2 * SEQ - a * SLAB).astype(F32)
        step = jnp.where(fwd, 1.0, -1.0)
        v = raw * jnp.exp((base + step * row_f) * rate)
        v = jnp.where(jnp.logical_and(a == N_IN, first_row), 0.0, v)
        ft_ref[a] = v
        return acc + jnp.sum(jnp.abs(v), axis=0, keepdims=True)

    norm = lax.fori_loop(0, N_HI, taps, jnp.zeros((1, HY_CT), F32), unroll=2)
    _slab_forward(ft_ref, N_HI, True, s_ref, set(range(N_SPEC)), 1.0 / norm)

    def spec(b):
        pq = jnp.dot(mf_ref[b], s_ref[b].astype(BF16), preferred_element_type=F32)
        xr, xi = _cplx_from_pq(pq, False)
        rows = pl.ds(pl.multiple_of(b * SLAB, SLAB), SLAB)
        o_ref[rows, 0:HY_CT] = xr.astype(BF16)
        o_ref[rows, HY_CT:2 * HY_CT] = xi.astype(BF16)

    def spec_group(i, carry):
        for t in range(SPEC_GROUP):
            spec(1 + SPEC_GROUP * i + t)
        return carry

    spec(0)
    lax.fori_loop(0, (N_SPEC - 1) // SPEC_GROUP, spec_group, 0)


def _filter_spectrum(feat, w_fout, absdelta, mf):
    cols = 2 * N_CT
    return pl.pallas_call(
        _filter_spec_kernel,
        grid=(HYENA_ORDER, N_CT),
        in_specs=[
            pl.BlockSpec((SEQ, 2 * FILTER_HIDDEN), lambda o, c: (0, 0)),
            pl.BlockSpec((FILTER_HIDDEN, HY_CT), lambda o, c: (0, o * cols + c)),
            pl.BlockSpec((FILTER_HIDDEN, HY_CT), lambda o, c: (0, o * cols + N_CT + c)),
            pl.BlockSpec((1, HY_CT), lambda o, c: (0, c)),
            pl.BlockSpec((N_SPEC, 2 * SLAB, SLAB), lambda o, c: (0, 0, 0)),
        ],
        out_specs=pl.BlockSpec((None, None, N_SPEC * SLAB, 2 * HY_CT), lambda o, c: (o, c, 0, 0)),
        out_shape=jax.ShapeDtypeStruct((HYENA_ORDER, N_CT, N_SPEC * SLAB, 2 * HY_CT), BF16),
        scratch_shapes=[pltpu.VMEM((N_HI, SLAB, HY_CT), F32), pltpu.VMEM((N_SPEC, SLAB, 2 * HY_CT), F32)],
        compiler_params=_cparams(("arbitrary", "arbitrary"), 48),
        name="filter_spec",
    )(feat, w_fout, w_fout, absdelta, mf)


HALO = 8


def _conv_slab(u_ref, bi, a, w, b, st):
    r0 = a * SLAB
    x = u_ref[bi, pl.ds(pl.multiple_of(r0, SLAB), SLAB), :].astype(F32)
    before = u_ref[bi, pl.ds(pl.multiple_of(jnp.maximum(r0 - 16, 0), 16), 16), :].astype(F32)
    after = u_ref[bi, pl.ds(pl.multiple_of(jnp.minimum(r0 + SLAB, SEQ - 16), 16), 16), :].astype(F32)
    st[0:HALO, :] = jnp.where(a > 0, before[16 - HALO:16, :], 0.0)
    st[HALO:HALO + SLAB, :] = x
    st[HALO + SLAB:2 * HALO + SLAB, :] = jnp.where(a < N_IN - 1, after[0:HALO, :], 0.0)
    xm1 = st[HALO - 1:HALO - 1 + SLAB, :]
    xp1 = st[HALO + 1:HALO + 1 + SLAB, :]
    return xm1 * w[0:1, :] + x * w[1:2, :] + xp1 * w[2:3, :] + b


def _hyena_kernel(uv_ref, u1_ref, u2_ref, cwv_ref, cw1_ref, cw2_ref, cbv_ref, cb1_ref, cb2_ref, skip_ref,
                  mf_ref, mi_ref, h_ref, o_ref, zc_ref, g_ref, s_ref, st_ref):
    o = pl.program_id(2)

    @pl.when(o == 0)
    def _():
        def conv(a, carry):
            for bi in range(2):
                lanes = slice(bi * HY_CT, (bi + 1) * HY_CT)
                st = lambda k: st_ref.at[3 * bi + k]
                zc_ref[a, 0:SLAB, lanes] = _conv_slab(uv_ref, bi, a, cwv_ref[...], cbv_ref[...], st(0)).astype(BF16)
                g_ref[0, a, 0:SLAB, lanes] = _conv_slab(u1_ref, bi, a, cw1_ref[...], cb1_ref[...], st(1)).astype(BF16)
                g_ref[1, a, 0:SLAB, lanes] = _conv_slab(u2_ref, bi, a, cw2_ref[...], cb2_ref[...], st(2)).astype(BF16)
            return carry

        lax.fori_loop(0, N_IN, conv, 0)

    def im_sign(b):
        return -1 if b > N_HI // 2 else 1

    def forward_chunk(rc, carry):
        rows = pl.ds(pl.multiple_of(rc * HY_ROWS, HY_ROWS), HY_ROWS)
        rows_im = pl.ds(pl.multiple_of(S_IM + rc * HY_ROWS, HY_ROWS), HY_ROWS)
        xs = [((zc_ref[a, rows, 0:HY_CT], 1), (zc_ref[a, rows, HY_CT:2 * HY_CT], 1)) for a in range(N_IN)]
        like = xs[0][0][0]
        out = _slab_fft_dit(xs + [None] * (N_HI - N_IN), -1, set(range(N_HI)))
        for b in range(N_HI):
            re, im = out[b]
            s_ref[b, rows, :] = _part_value(re, like)
            s_ref[b, rows_im, :] = _part_value(None if im is None else (im[0], im[1] * im_sign(b)), like)
        return carry

    lax.fori_loop(0, SLAB // HY_ROWS, forward_chunk, 0)

    def through_filter(slabs, j):
        h = h_ref[j * SLAB:(j + 1) * SLAB, :].astype(F32)
        hr = jnp.concatenate([h[:, :HY_CT]] * len(slabs), axis=1)
        hi = jnp.concatenate([h[:, HY_CT:]] * len(slabs), axis=1)
        load = lambda b: jnp.concatenate([s_ref[b, 0:SLAB, :], s_ref[b, S_IM:S_IM + SLAB, :]], axis=0)
        d = load(slabs[0]) if len(slabs) == 1 else jnp.concatenate([load(b) for b in slabs], axis=1)
        x = jnp.dot(mf_ref[j], d, preferred_element_type=F32)
        xr, xi = x[:SLAB], x[SLAB:]
        y = jnp.concatenate([xr * hr - xi * hi, xr * hi + xi * hr], axis=0).astype(BF16)
        w = jnp.dot(mi_ref[j], y, preferred_element_type=F32).astype(BF16)
        return [(b, w[:, k * HY_CT:(k + 1) * HY_CT]) for k, b in enumerate(slabs)]

    for slabs, j in [([0], 0), ([N_HI // 2], N_HI // 2)] + [([j, N_HI - j], j) for j in range(1, N_HI // 2)]:
        for b, w in through_filter(slabs, j):
            s_ref[b, 0:SLAB, :] = w[:SLAB]
            s_ref[b, S_IM:S_IM + SLAB, :] = w[SLAB:]

    def inverse_chunk(rc, carry):
        rows = pl.ds(pl.multiple_of(rc * HY_ROWS, HY_ROWS), HY_ROWS)
        rows_im = pl.ds(pl.multiple_of(S_IM + rc * HY_ROWS, HY_ROWS), HY_ROWS)
        xs = [((s_ref[b, rows, :], 1), (s_ref[b, rows_im, :], im_sign(b))) for b in range(N_HI)]
        like = xs[0][0][0]
        out = _slab_fft_dif(xs, 1, set(range(N_IN)))
        for a in range(N_IN):
            s_ref[a, rows, :] = _part_value(out[a][0], like)
            s_ref[a, rows_im, :] = _part_value(out[a][1], like)
        return carry

    lax.fori_loop(0, SLAB // HY_ROWS, inverse_chunk, 0)

    sk = skip_ref[pl.ds(o, 1), :]
    skip2 = jnp.concatenate([sk, sk], axis=1)

    def gate(a, carry):
        y = jnp.concatenate([s_ref[a, 0:SLAB, :], s_ref[a, S_IM:S_IM + SLAB, :]], axis=1).astype(F32)
        z = g_ref[o, a, 0:SLAB, :].astype(F32) * (y + skip2 * zc_ref[a, 0:SLAB, :].astype(F32))
        zc_ref[a, 0:SLAB, :] = z.astype(BF16)
        return carry

    lax.fori_loop(0, N_IN, gate, 0)

    @pl.when(o == HYENA_ORDER - 1)
    def _():
        def emit(a, carry):
            rows = pl.ds(pl.multiple_of(a * SLAB, SLAB), SLAB)
            for bi in range(2):
                o_ref[bi, rows, :] = zc_ref[a, 0:SLAB, bi * HY_CT:(bi + 1) * HY_CT]
            return carry

        lax.fori_loop(0, N_IN, emit, 0)


def _hyena_conv(p, conv_w, conv_b, skip, mf, mi, hspec):
    b = p.shape[0]
    u0 = V_END // HY_CT
    cb = conv_b.reshape(1, -1)
    part = lambda k: pl.BlockSpec((2, SEQ, HY_CT), lambda c, q, o: (q, 0, u0 + k * N_CT + c))
    cw = lambda k: pl.BlockSpec((SHORT_CONV, HY_CT), lambda c, q, o: (0, k * N_CT + c))
    cbs = lambda k: pl.BlockSpec((1, HY_CT), lambda c, q, o: (0, k * N_CT + c))
    const = lambda: pl.BlockSpec((N_SPEC, 2 * SLAB, 2 * SLAB), lambda c, q, o: (0, 0, 0),
                                 pipeline_mode=pl.Buffered(1))
    return pl.pallas_call(
        _hyena_kernel,
        grid=(N_CT, b // 2, HYENA_ORDER),
        in_specs=[part(0), part(1), part(2), cw(0), cw(1), cw(2), cbs(0), cbs(1), cbs(2),
                  pl.BlockSpec((HYENA_ORDER, HY_CT), lambda c, q, o: (0, c)),
                  const(), const(),
                  pl.BlockSpec((None, None, N_SPEC * SLAB, 2 * HY_CT), lambda c, q, o: (o, c, 0, 0))],
        out_specs=pl.BlockSpec((2, SEQ, HY_CT), lambda c, q, o: (q, 0, c)),
        out_shape=jax.ShapeDtypeStruct((b, SEQ, HYENA_WIDTH), BF16),
        scratch_shapes=[pltpu.VMEM((N_IN, SLAB, 2 * HY_CT), BF16),
                        pltpu.VMEM((HYENA_ORDER, N_IN, SLAB, 2 * HY_CT), BF16),
                        pltpu.VMEM((N_HI, 2 * SLAB, HY_CT), BF16),
                        pltpu.VMEM((2 * (HYENA_ORDER + 1), SLAB + 2 * HALO, HY_CT), F32)],
        compiler_params=_cparams(("arbitrary", "arbitrary", "arbitrary"), 56),
        name="hyena_conv",
    )(p, p, p, conv_w, conv_w, conv_w, cb, cb, cb, skip, mf, mi, hspec)


def _ctx_hyena_kernel(uv_ref, u1_ref, u2_ref, cwv_ref, cw1_ref, cw2_ref, cbv_ref, cb1_ref, cb2_ref, skip_ref,
                      feat_ref, wf0_ref, wb0_ref, wf1_ref, wb1_ref, ad_ref, ff_ref, ai_ref, o_ref):
    l = CTX_LEN
    nb = uv_ref.shape[0]
    n = lax.broadcasted_iota(jnp.int32, (2 * l, HY_CT), 0)
    spectra = []
    no_w = jnp.zeros((FILTER_HIDDEN, HY_CT), F32)
    for wf_ref, wb_ref in ((wf0_ref, wb0_ref), (wf1_ref, wb1_ref)):
        raw = jnp.concatenate(
            [jnp.dot(feat_ref[...], jnp.concatenate([wf_ref[...], no_w], axis=0), precision=HIGHEST,
                     preferred_element_type=F32),
             jnp.dot(feat_ref[...], jnp.concatenate([no_w, wb_ref[...]], axis=0), precision=HIGHEST,
                     preferred_element_type=F32)], axis=0)
        v = _filter_taps(raw, n, l, ad_ref[...])
        v = v / jnp.sum(jnp.abs(v), axis=0, keepdims=True)
        pq = jnp.dot(ff_ref[...], v.astype(BF16), preferred_element_type=F32)
        spectra.append((pq[:2 * l], pq[2 * l:]))

    row = lax.broadcasted_iota(jnp.int32, (l, HY_CT), 0)

    def conv(u_ref, bi, w_ref, b_ref):
        x = u_ref[bi].astype(F32)
        xm1 = jnp.where(row == 0, 0.0, pltpu.roll(x, 1, 0))
        xp1 = jnp.where(row == l - 1, 0.0, pltpu.roll(x, l - 1, 0))
        return xm1 * w_ref[0:1, :] + x * w_ref[1:2, :] + xp1 * w_ref[2:3, :] + b_ref[...]

    z = [conv(uv_ref, bi, cwv_ref, cbv_ref) for bi in range(nb)]
    gates = [[conv(u1_ref, bi, cw1_ref, cb1_ref) for bi in range(nb)],
             [conv(u2_ref, bi, cw2_ref, cb2_ref) for bi in range(nb)]]
    for o in range(HYENA_ORDER):
        hr, hi = spectra[o]
        pq = jnp.dot(ff_ref[:, 0:l], jnp.concatenate(z, axis=1).astype(BF16), preferred_element_type=F32)
        ys = []
        for q in range(nb // 2):
            xr, xi = _cplx_from_pq(pq[:, 2 * q * HY_CT:2 * (q + 1) * HY_CT], False)
            ys += [xr * hr - xi * hi, xr * hi + xi * hr]
        pq2 = jnp.dot(ai_ref[...], jnp.concatenate(ys, axis=1).astype(BF16), preferred_element_type=F32)
        y = []
        for q in range(nb // 2):
            y += list(_cplx_from_pq(pq2[:, 2 * q * HY_CT:2 * (q + 1) * HY_CT], False))
        z = [gates[o][bi] * (y[bi] + skip_ref[o:o + 1, :] * z[bi]) for bi in range(nb)]
    for bi in range(nb):
        o_ref[bi] = z[bi].astype(BF16)


def _ctx_hyena(pc, conv_w, conv_b, skip, feat, w_fout, absdelta, ff, ai):
    b, l, _ = pc.shape
    u0 = V_END // HY_CT
    cb = conv_b.reshape(1, -1)
    cols = 2 * N_CT
    part = lambda k: pl.BlockSpec((b, l, HY_CT), lambda c: (0, 0, u0 + k * N_CT + c))
    cw = lambda k: pl.BlockSpec((SHORT_CONV, HY_CT), lambda c: (0, k * N_CT + c))
    cbs = lambda k: pl.BlockSpec((1, HY_CT), lambda c: (0, k * N_CT + c))
    wcol = lambda off: pl.BlockSpec((FILTER_HIDDEN, HY_CT), lambda c: (0, off + c))
    return pl.pallas_call(
        _ctx_hyena_kernel,
        grid=(N_CT,),
        in_specs=[part(0), part(1), part(2), cw(0), cw(1), cw(2), cbs(0), cbs(1), cbs(2),
                  pl.BlockSpec((HYENA_ORDER, HY_CT), lambda c: (0, c)),
                  pl.BlockSpec((l, 2 * FILTER_HIDDEN), lambda c: (0, 0)),
                  wcol(0), wcol(N_CT), wcol(cols), wcol(cols + N_CT),
                  pl.BlockSpec((1, HY_CT), lambda c: (0, c)),
                  pl.BlockSpec((4 * l, 2 * l), lambda c: (0, 0)),
                  pl.BlockSpec((2 * l, 2 * l), lambda c: (0, 0))],
        out_specs=pl.BlockSpec((b, l, HY_CT), lambda c: (0, 0, c)),
        out_shape=jax.ShapeDtypeStruct((b, l, HYENA_WIDTH), BF16),
        compiler_params=_cparams(("arbitrary",), 32),
        name="ctx_hyena",
    )(pc, pc, pc, conv_w, conv_w, conv_w, cb, cb, cb, skip, feat, w_fout, w_fout, w_fout, w_fout, absdelta, ff, ai)


def _rope_tables(l):
    rows = l // GRID_W
    row = np.repeat(np.arange(rows), GRID_W).astype(np.float64)
    col = np.tile(np.arange(GRID_W), rows).astype(np.float64)
    n_freq = HEAD_DIM // 4
    inv = ROPE_BASE ** (-np.arange(n_freq) / n_freq)
    ang_r = row[:, None] * inv
    ang_c = col[:, None] * inv
    cos = np.concatenate([np.cos(ang_r), np.cos(ang_r), np.cos(ang_c), np.cos(ang_c)], axis=-1)
    sin = np.concatenate([-np.sin(ang_r), np.sin(ang_r), -np.sin(ang_c), np.sin(ang_c)], axis=-1)
    return jnp.asarray(cos, F32), jnp.asarray(sin, F32)


def kernel(x, c, ctx, c_ctx, w_ada, b_ada, norm_mix, norm_mlp, w_in, q_norm, k_norm, attn_sink, conv_w, conv_b, filt_w1, filt_b1, filt_w2, filt_b2, filt_w3, filt_b3, filt_freq, filt_w_out, hyena_skip, w_out, w_mlp_in, w_mlp_out):
    cos, sin = _rope_tables(SEQ)
    emb, emb_ctx = _filter_positions(SEQ), _filter_positions(CTX_LEN)
    m_tall, mf, mi = _dft_constants()
    ff_ctx, ai_ctx = _ctx_dft_constants()
    absdelta = jnp.abs(jnp.linspace(MIN_DECAY, MAX_DECAY, HYENA_WIDTH, dtype=F32)).reshape(1, HYENA_WIDTH)
    cvec = jnp.zeros((MOD_ROWS, D_MODEL), F32).at[:BATCH].set(c).at[CTX_ROW].set(c_ctx)
    mod_all = _modulation(cvec, w_ada, b_ada).reshape(DEPTH, MOD_ROWS, N_MOD, D_MODEL)
    w_in_bf = w_in[0:1].astype(BF16)
    ctx_rows = BATCH * CTX_LEN
    flat = lambda a: a.reshape(1, ctx_rows, a.shape[-1])
    per_batch = lambda a: a.reshape(BATCH, CTX_LEN, a.shape[-1])
    xc = flat(ctx)
    for i in range(DEPTH):
        update_ctx = i < DEPTH - 1
        mod = mod_all[i]
        nmix = norm_mix[i].reshape(1, D_MODEL)
        nmlp = norm_mlp[i].reshape(1, D_MODEL)
        qn = q_norm[i].reshape(1, HEAD_DIM)
        kn = k_norm[i].reshape(1, HEAD_DIM)
        filt_mlp = (filt_w1[i], filt_b1[i], filt_w2[i], filt_b2[i], filt_w3[i], filt_b3[i], filt_freq[i])

        p, (w_out_bf, w1_bf, w2_bf) = _in_projection(
            x, mod, None, nmix, w_in_bf, qn, kn, cos, sin, tm=512, n_off=0, n_tiles=IN_COLS // IN_TN,
            use_rope=True, riders=[(w_out, i), (w_mlp_in, i), (w_mlp_out, i)])
        if update_ctx:
            pc, _ = _in_projection(xc, mod, CTX_ROW, nmix, w_in_bf, qn, kn, cos, sin,
                                   tm=512, n_off=0, n_tiles=IN_COLS // IN_TN, use_rope=False)
            kx_blk, vx_blk = Q_END // KV_COLS * 2, Q_END // KV_COLS * 2 + 1
        else:
            pc, _ = _in_projection(xc, mod, CTX_ROW, nmix, w_in_bf, qn, kn, cos, sin,
                                   tm=512, n_off=KV_TILE, n_tiles=1, use_rope=False)
            kx_blk, vx_blk = 0, 1
        pc = per_batch(pc)

        attn = _window_attention(p, pc, kx_blk, vx_blk, attn_sink[i])
        hspec = _filter_spectrum(_filter_features(emb, *filt_mlp), filt_w_out[i], absdelta, m_tall)
        hy = _hyena_conv(p, conv_w[i], conv_b[i], hyena_skip[i], mf, mi, hspec)
        x, h2, next_w_in = _out_projection(attn, hy, x, mod, None, nmlp, w_out_bf, tm=512,
                                           riders=[(w_in, i + 1)] if update_ctx else [])
        x = _mlp(h2, x, mod, None, w1_bf, w2_bf, tm=1024, tf=512)

        if update_ctx:
            attn_c = _context_attention(pc, attn_sink[i])
            hy_c = _ctx_hyena(pc, conv_w[i], conv_b[i], hyena_skip[i], _filter_features(emb_ctx, *filt_mlp),
                              filt_w_out[i], absdelta, ff_ctx, ai_ctx)
            xc, h2c, _ = _out_projection(flat(attn_c), flat(hy_c), xc, mod, CTX_ROW, nmlp, w_out_bf, tm=512)
            xc = _mlp(h2c, xc, mod, CTX_ROW, w1_bf, w2_bf, tm=512, tf=1024)
            w_in_bf = next_w_in[0]
    return x
```

```python
import cmath
import functools
import math

import jax
import jax.numpy as jnp
import numpy as np
from jax import lax
from jax.experimental import pallas as pl
from jax.experimental.pallas import tpu as pltpu

D_MODEL = 2048
BATCH = 4
SEQ = 4096
DEPTH = 2
GRID_W = 64
CTX_LEN = 256
HEAD_DIM = 128
N_Q_HEADS = 8
N_KV_HEADS = 2
GQA_GROUP = N_Q_HEADS // N_KV_HEADS
ATTN_WIDTH = N_Q_HEADS * HEAD_DIM
HYENA_WIDTH = D_MODEL - ATTN_WIDTH
HYENA_ORDER = 2
SHORT_CONV = 3
FILTER_EMB = 33
DECAY_TARGET = 1e-2
FAST_DECAY_PCT = 0.3
SLOW_DECAY_PCT = 1.5
MAX_DECAY = math.log(DECAY_TARGET) / FAST_DECAY_PCT
MIN_DECAY = math.log(DECAY_TARGET) / SLOW_DECAY_PCT
WINDOW = 128
BLOCK = 128
ROPE_BASE = 10000.0
D_FF = 4 * D_MODEL
N_MOD = 6
EPS = 1e-6
NEG_INF = -1e30
Q_END = ATTN_WIDTH
K_END = Q_END + N_KV_HEADS * HEAD_DIM
V_END = K_END + N_KV_HEADS * HEAD_DIM
IN_COLS = V_END + (HYENA_ORDER + 1) * HYENA_WIDTH

MOD_ROWS = 8
CTX_ROW = BATCH
KV_COLS = 2 * N_KV_HEADS * HEAD_DIM

F32 = jnp.float32
BF16 = jnp.bfloat16
MIB = 1024 * 1024


def _cparams(sem, vmem_mib):
    return pltpu.CompilerParams(dimension_semantics=sem, vmem_limit_bytes=vmem_mib * MIB)


def _mod_kernel(c_ref, w_ref, b_ref, o_ref):
    c = c_ref[...]
    s = (c * jax.nn.sigmoid(c)).astype(BF16)
    o_ref[...] = jnp.dot(s, w_ref[...].astype(BF16), preferred_element_type=F32) + b_ref[...]


def _modulation(cvec, w_ada, b_ada):
    tn = 1024
    n_cols = N_MOD * D_MODEL
    return pl.pallas_call(
        _mod_kernel,
        grid=(DEPTH, n_cols // tn),
        in_specs=[
            pl.BlockSpec((MOD_ROWS, D_MODEL), lambda l, j: (0, 0)),
            pl.BlockSpec((None, D_MODEL, tn), lambda l, j: (l, 0, j)),
            pl.BlockSpec((None, 1, tn), lambda l, j: (l, 0, j)),
        ],
        out_specs=pl.BlockSpec((None, MOD_ROWS, tn), lambda l, j: (l, 0, j)),
        out_shape=jax.ShapeDtypeStruct((DEPTH, MOD_ROWS, n_cols), F32),
        compiler_params=_cparams(("arbitrary", "arbitrary"), 40),
        name="adaln_mod",
    )(cvec, w_ada, b_ada.reshape(DEPTH, 1, n_cols))


IN_TN = 512
Q_TILES = Q_END // IN_TN
KV_TILE = Q_TILES
HEADS_PER_TILE = IN_TN // HEAD_DIM


SUB_ROWS = 256


def _rms_modulate(xf, gain, shift):
    ms = jnp.mean(xf * xf, axis=-1, keepdims=True)
    return xf * lax.rsqrt(ms + EPS) * gain + shift


def _head_norm_rope(a, g, cos, sin):
    ms = jnp.mean(a * a, axis=-1, keepdims=True)
    y = a * lax.rsqrt(ms + EPS) * g
    if cos is None:
        return y
    lane = lax.broadcasted_iota(jnp.int32, y.shape, 1)
    partner = jnp.where((lane % 64) < 32, pltpu.roll(y, 96, 1), pltpu.roll(y, 32, 1))
    return y * cos + partner * sin


BF16_TILE_ROWS = 16


def _rider_plan(w, layer, steps, step_of):
    depth, r, c = w.shape
    assert r % (steps * BF16_TILE_ROWS) == 0
    blk = r // steps
    in_spec = pl.BlockSpec((blk, c), lambda *g: (layer * steps + step_of(*g), 0))
    out_spec = pl.BlockSpec((blk, c), lambda *g: (step_of(*g), 0))
    restore = lambda a: a.reshape(1, r, c)
    return w.reshape(depth * r, c), in_spec, out_spec, jax.ShapeDtypeStruct((r, c), BF16), restore


def _cast_riders(in_refs, out_refs):
    for src, dst in zip(in_refs, out_refs):
        dst[...] = src[...].astype(BF16)


def _inproj_kernel(*refs, n_off, n_tiles, use_rope, n_riders):
    x_ref, m_ref, nw_ref, w_ref, qn_ref, kn_ref, cos_ref, sin_ref = refs[:8]
    o_ref = refs[8 + n_riders]
    _cast_riders(refs[8:8 + n_riders], refs[9 + n_riders:])
    gain = nw_ref[...] * (1.0 + m_ref[1:2, :])
    shift = m_ref[0:1, :]
    for r0 in range(0, x_ref.shape[0], SUB_ROWS):
        rows = slice(r0, r0 + SUB_ROWS)
        h = _rms_modulate(x_ref[rows, :], gain, shift).astype(BF16)
        cos = cos_ref[rows, :] if use_rope else None
        sin = sin_ref[rows, :] if use_rope else None
        for n in range(n_tiles):
            cols = slice(n * IN_TN, (n + 1) * IN_TN)
            acc = jnp.dot(h, w_ref[:, cols], preferred_element_type=F32)
            t = n + n_off
            if t > KV_TILE:
                o_ref[rows, cols] = acc.astype(BF16)
                continue
            gains = [qn_ref[...]] * HEADS_PER_TILE if t < Q_TILES else [kn_ref[...]] * N_KV_HEADS + [None] * N_KV_HEADS
            for c, g in enumerate(gains):
                sl = slice(c * HEAD_DIM, (c + 1) * HEAD_DIM)
                osl = slice(n * IN_TN + c * HEAD_DIM, n * IN_TN + (c + 1) * HEAD_DIM)
                if g is None:
                    o_ref[rows, osl] = acc[:, sl].astype(BF16)
                else:
                    o_ref[rows, osl] = _head_norm_rope(acc[:, sl], g, cos, sin).astype(BF16)


def _in_projection(x, mod, mod_row, nw, w_bf, qn, kn, cos, sin, *, tm, n_off, n_tiles, use_rope, riders=()):
    b, t, _ = x.shape
    assert n_off % n_tiles == 0 and tm % SUB_ROWS == 0
    row = (lambda bb: bb) if mod_row is None else (lambda bb: mod_row)
    n_i = t // tm
    plans = [_rider_plan(w, layer, b * n_i, lambda bb, i: bb * n_i + i) for w, layer in riders]
    kern = functools.partial(_inproj_kernel, n_off=n_off, n_tiles=n_tiles, use_rope=use_rope, n_riders=len(plans))
    width = n_tiles * IN_TN
    out = pl.pallas_call(
        kern,
        grid=(b, n_i),
        in_specs=[
            pl.BlockSpec((None, tm, D_MODEL), lambda bb, i: (bb, i, 0)),
            pl.BlockSpec((None, N_MOD, D_MODEL), lambda bb, i: (row(bb), 0, 0)),
            pl.BlockSpec((1, D_MODEL), lambda bb, i: (0, 0)),
            pl.BlockSpec((None, D_MODEL, width), lambda bb, i: (0, 0, n_off // n_tiles),
                         pipeline_mode=pl.Buffered(1)),
            pl.BlockSpec((1, HEAD_DIM), lambda bb, i: (0, 0)),
            pl.BlockSpec((1, HEAD_DIM), lambda bb, i: (0, 0)),
            pl.BlockSpec((tm, HEAD_DIM), lambda bb, i: (i, 0)),
            pl.BlockSpec((tm, HEAD_DIM), lambda bb, i: (i, 0)),
        ] + [pl_[1] for pl_ in plans],
        out_specs=[pl.BlockSpec((None, tm, width), lambda bb, i: (bb, i, 0))] + [pl_[2] for pl_ in plans],
        out_shape=[jax.ShapeDtypeStruct((b, t, width), BF16)] + [pl_[3] for pl_ in plans],
        compiler_params=_cparams(("arbitrary", "arbitrary"), 56),
        name="in_proj",
    )(x, mod, nw, w_bf, qn, kn, cos, sin, *[pl_[0] for pl_ in plans])
    return out[0], [pl_[4](a) for pl_, a in zip(plans, out[1:])]


def _sink_column(sink_ref, kv, rows_per_head):
    rows = GQA_GROUP * rows_per_head
    head = lax.broadcasted_iota(jnp.int32, (rows, 1), 0) // rows_per_head
    col = jnp.full((rows, 1), sink_ref[kv * GQA_GROUP], F32)
    for g in range(1, GQA_GROUP):
        col = jnp.where(head == g, sink_ref[kv * GQA_GROUP + g], col)
    return col


def _stack_heads(q_ref, kv, rows=slice(None)):
    return jnp.concatenate(
        [q_ref[rows, (kv * GQA_GROUP + g) * HEAD_DIM:(kv * GQA_GROUP + g + 1) * HEAD_DIM]
         for g in range(GQA_GROUP)], axis=0)


LOG2E = math.log2(math.e)
QK_LOG2_SCALE = HEAD_DIM ** -0.5 * LOG2E


def _softmax_pv(q4, k_all, v_all, bias, sink_col):
    t = lax.dot_general(q4, k_all, (((1,), (1,)), ((), ())), preferred_element_type=F32) * QK_LOG2_SCALE
    if bias is not None:
        t = t + bias
    sink2 = sink_col * LOG2E
    m = jnp.maximum(jnp.max(t, axis=-1, keepdims=True), sink2)
    e = jnp.exp2(t - m).astype(BF16)
    v_ones = jnp.concatenate([v_all, jnp.ones_like(v_all)], axis=1)
    pv = jnp.dot(e, v_ones, preferred_element_type=F32)
    d = v_all.shape[1]
    return pv[:, :d] / (pv[:, d:] + jnp.exp2(sink2 - m))


Q_BLOCKS = 4


def _window_attn_kernel(sink_ref, bias_first_ref, bias_mid_ref, bias_last_ref, q_ref, kp_ref, km_ref, kn_ref,
                        vp_ref, vm_ref, vn_ref, kx_ref, vx_ref, o_ref):
    for blk in range(Q_BLOCKS):
        bias_ref = bias_first_ref if blk == 0 else bias_last_ref if blk == Q_BLOCKS - 1 else bias_mid_ref
        rows = slice(blk * BLOCK, (blk + 1) * BLOCK)
        for kv in range(N_KV_HEADS):
            ks = slice(kv * HEAD_DIM, (kv + 1) * HEAD_DIM)

            def band(prev_ref, mid_ref, next_ref):
                pieces = [prev_ref[:, ks]] + [mid_ref[j * BLOCK:(j + 1) * BLOCK, ks] for j in range(Q_BLOCKS)]
                return (pieces + [next_ref[:, ks]])[blk:blk + 3]

            k_all = jnp.concatenate([kx_ref[:, ks]] + band(kp_ref, km_ref, kn_ref), axis=0)
            v_all = jnp.concatenate([vx_ref[:, ks]] + band(vp_ref, vm_ref, vn_ref), axis=0)
            o = _softmax_pv(_stack_heads(q_ref, kv, rows), k_all, v_all, bias_ref[...],
                            _sink_column(sink_ref, kv, BLOCK))
            for g in range(GQA_GROUP):
                h = kv * GQA_GROUP + g
                o_ref[rows, h * HEAD_DIM:(h + 1) * HEAD_DIM] = o[g * BLOCK:(g + 1) * BLOCK].astype(BF16)


def _window_bias(nb):
    qi = np.arange(GQA_GROUP * BLOCK)[:, None] % BLOCK
    sj = np.arange(3 * BLOCK)[None, :] - BLOCK
    near = np.abs(sj - qi) <= WINDOW
    out = np.zeros((3, GQA_GROUP * BLOCK, CTX_LEN + 3 * BLOCK), np.float32)
    for variant, blk in enumerate((0, 1, nb - 1)):
        key_pos = blk * BLOCK + sj
        ok = near & (key_pos >= 0) & (key_pos < nb * BLOCK)
        out[variant, :, CTX_LEN:] = np.where(ok, 0.0, NEG_INF)
    return jnp.asarray(out)


def _window_attention(p, pc, kx_blk, vx_blk, sink):
    b, seq, _ = p.shape
    nb = seq // BLOCK
    kvw = N_KV_HEADS * HEAD_DIM
    k_blk, v_blk = Q_END // kvw, K_END // kvw
    assert nb % Q_BLOCKS == 0 and Q_BLOCKS >= 2
    steps = nb // Q_BLOCKS
    prev = lambda bb, i: (bb, jnp.maximum(Q_BLOCKS * i - 1, 0))
    nxt = lambda bb, i: (bb, jnp.minimum(Q_BLOCKS * (i + 1), nb - 1))

    def edge(which, blk):
        return pl.BlockSpec((None, BLOCK, kvw), lambda bb, i: which(bb, i) + (blk,))

    def mid(blk):
        return pl.BlockSpec((None, Q_BLOCKS * BLOCK, kvw), lambda bb, i: (bb, i, blk))

    bias_shape = (None, GQA_GROUP * BLOCK, CTX_LEN + 3 * BLOCK)
    bias = _window_bias(nb)
    return pl.pallas_call(
        _window_attn_kernel,
        grid=(b, steps),
        in_specs=[
            pl.BlockSpec(memory_space=pltpu.SMEM),
            pl.BlockSpec(bias_shape, lambda bb, i: (jnp.where(i == 0, 0, 1), 0, 0)),
            pl.BlockSpec(bias_shape, lambda bb, i: (1, 0, 0)),
            pl.BlockSpec(bias_shape, lambda bb, i: (jnp.where(i == steps - 1, 2, 1), 0, 0)),
            pl.BlockSpec((None, Q_BLOCKS * BLOCK, ATTN_WIDTH), lambda bb, i: (bb, i, 0)),
            edge(prev, k_blk), mid(k_blk), edge(nxt, k_blk),
            edge(prev, v_blk), mid(v_blk), edge(nxt, v_blk),
            pl.BlockSpec((None, CTX_LEN, kvw), lambda bb, i: (bb, 0, kx_blk)),
            pl.BlockSpec((None, CTX_LEN, kvw), lambda bb, i: (bb, 0, vx_blk)),
        ],
        out_specs=pl.BlockSpec((None, Q_BLOCKS * BLOCK, ATTN_WIDTH), lambda bb, i: (bb, i, 0)),
        out_shape=jax.ShapeDtypeStruct((b, seq, ATTN_WIDTH), BF16),
        compiler_params=_cparams(("arbitrary", "arbitrary"), 32),
        name="window_attn",
    )(sink, bias, bias, bias, p, p, p, p, p, p, p, pc, pc)


def _ctx_attn_kernel(sink_ref, q_ref, k_ref, v_ref, o_ref):
    n = q_ref.shape[0]
    for kv in range(N_KV_HEADS):
        ks = slice(kv * HEAD_DIM, (kv + 1) * HEAD_DIM)
        o = _softmax_pv(_stack_heads(q_ref, kv), k_ref[:, ks], v_ref[:, ks], None, _sink_column(sink_ref, kv, n))
        for g in range(GQA_GROUP):
            h = kv * GQA_GROUP + g
            o_ref[:, h * HEAD_DIM:(h + 1) * HEAD_DIM] = o[g * n:(g + 1) * n].astype(BF16)


def _context_attention(pc, sink):
    b, n, _ = pc.shape
    kvw = N_KV_HEADS * HEAD_DIM
    return pl.pallas_call(
        _ctx_attn_kernel,
        grid=(b,),
        in_specs=[
            pl.BlockSpec(memory_space=pltpu.SMEM),
            pl.BlockSpec((None, n, ATTN_WIDTH), lambda bb: (bb, 0, 0)),
            pl.BlockSpec((None, n, kvw), lambda bb: (bb, 0, Q_END // kvw)),
            pl.BlockSpec((None, n, kvw), lambda bb: (bb, 0, K_END // kvw)),
        ],
        out_specs=pl.BlockSpec((None, n, ATTN_WIDTH), lambda bb: (bb, 0, 0)),
        out_shape=jax.ShapeDtypeStruct((b, n, ATTN_WIDTH), BF16),
        compiler_params=_cparams(("arbitrary",), 32),
        name="ctx_attn",
    )(sink, pc, pc, pc)


def _outproj_kernel(*refs, n_riders):
    a_ref, y_ref, x_ref, m_ref, nw_ref, wa_ref, wy_ref = refs[:7]
    o_ref, h_ref = refs[7 + n_riders], refs[8 + n_riders]
    _cast_riders(refs[7:7 + n_riders], refs[9 + n_riders:])
    gain = nw_ref[...] * (1.0 + m_ref[4:5, :])
    shift = m_ref[3:4, :]
    for r0 in range(0, x_ref.shape[0], SUB_ROWS):
        rows = slice(r0, r0 + SUB_ROWS)
        mix = (jnp.dot(a_ref[rows, :], wa_ref[...], preferred_element_type=F32)
               + jnp.dot(y_ref[rows, :], wy_ref[...], preferred_element_type=F32))
        x1 = x_ref[rows, :] + m_ref[2:3, :] * mix
        o_ref[rows, :] = x1
        h_ref[rows, :] = _rms_modulate(x1, gain, shift).astype(BF16)


def _out_projection(attn, hy, x, mod, mod_row, nw, w_bf, *, tm, riders=()):
    b, t, _ = x.shape
    assert tm % SUB_ROWS == 0
    n_i = t // tm
    row = (lambda bb: bb) if mod_row is None else (lambda bb: mod_row)
    tile = lambda width: pl.BlockSpec((None, tm, width), lambda bb, i: (bb, i, 0))
    half = lambda k: pl.BlockSpec((None, D_MODEL // 2, D_MODEL), lambda bb, i: (0, k, 0),
                                  pipeline_mode=pl.Buffered(1))
    plans = [_rider_plan(w, layer, b * n_i, lambda bb, i: bb * n_i + i) for w, layer in riders]
    out = pl.pallas_call(
        functools.partial(_outproj_kernel, n_riders=len(plans)),
        grid=(b, n_i),
        in_specs=[
            tile(ATTN_WIDTH), tile(HYENA_WIDTH), tile(D_MODEL),
            pl.BlockSpec((None, N_MOD, D_MODEL), lambda bb, i: (row(bb), 0, 0)),
            pl.BlockSpec((1, D_MODEL), lambda bb, i: (0, 0)),
            half(0), half(1),
        ] + [pl_[1] for pl_ in plans],
        out_specs=[tile(D_MODEL), tile(D_MODEL)] + [pl_[2] for pl_ in plans],
        out_shape=[jax.ShapeDtypeStruct(x.shape, F32), jax.ShapeDtypeStruct(x.shape, BF16)]
        + [pl_[3] for pl_ in plans],
        compiler_params=_cparams(("arbitrary", "arbitrary"), 48),
        name="out_proj",
    )(attn, hy, x, mod, nw, w_bf, w_bf, *[pl_[0] for pl_ in plans])
    return out[0], out[1], [pl_[4](a) for pl_, a in zip(plans, out[2:])]


MLP_OUT_CHUNK = 512


def _mlp_kernel(h_ref, x_ref, m_ref, w1_ref, w2_ref, o_ref, *, n_f):
    f = pl.program_id(2)

    @pl.when(f == 0)
    def _():
        o_ref[...] = jnp.zeros_like(o_ref)

    a = jnp.maximum(jnp.dot(h_ref[...], w1_ref[...], preferred_element_type=F32), 0.0)
    a = (a * a).astype(BF16)
    for n0 in range(0, D_MODEL, MLP_OUT_CHUNK):
        sl = slice(n0, n0 + MLP_OUT_CHUNK)
        o_ref[:, sl] += jnp.dot(a, w2_ref[:, sl], preferred_element_type=F32)

    @pl.when(f == n_f - 1)
    def _():
        o_ref[...] = x_ref[...] + m_ref[5:6, :] * o_ref[...]


def _mlp(h, x, mod, mod_row, w1_bf, w2_bf, *, tm, tf):
    b, t, _ = x.shape
    n_f = D_FF // tf
    row = (lambda bb: bb) if mod_row is None else (lambda bb: mod_row)
    tile = pl.BlockSpec((None, tm, D_MODEL), lambda bb, i, f: (bb, i, 0))
    return pl.pallas_call(
        functools.partial(_mlp_kernel, n_f=n_f),
        grid=(b, t // tm, n_f),
        in_specs=[
            tile, tile,
            pl.BlockSpec((None, N_MOD, D_MODEL), lambda bb, i, f: (row(bb), 0, 0)),
            pl.BlockSpec((None, D_MODEL, tf), lambda bb, i, f: (0, 0, f)),
            pl.BlockSpec((None, tf, D_MODEL), lambda bb, i, f: (0, f, 0)),
        ],
        out_specs=tile,
        out_shape=jax.ShapeDtypeStruct(x.shape, F32),
        compiler_params=_cparams(("arbitrary", "arbitrary", "arbitrary"), 56),
        name="mlp",
    )(h, x, mod, w1_bf, w2_bf)


def _tile_const(k, like):
    return jnp.full(like.shape, k, F32).astype(like.dtype)


def _p_add(p, q):
    if p is None:
        return q
    if q is None:
        return p
    (a, sa), (b, sb) = p, q
    if sa == sb:
        return (a + b, sa)
    return (a - b, 1) if sa > 0 else (b - a, 1)


def _p_scale(p, k):
    if p is None or k == 0.0:
        return None
    a, s = p
    if k < 0:
        s, k = -s, -k
    return (a, s) if k == 1.0 else (a * _tile_const(k, a), s)


def _c_add(x, y, sign=1):
    if y is None:
        return x
    if x is None:
        x = (None, None)
    return (_p_add(x[0], _p_scale(y[0], sign)), _p_add(x[1], _p_scale(y[1], sign)))


def _snap(v):
    for t in (0.0, 1.0, -1.0):
        if abs(v - t) < 1e-12:
            return t
    return v


def _c_mulc(x, c):
    if x is None:
        return None
    cr, ci = _snap(c.real), _snap(c.imag)
    re, im = x
    if cr != 0.0 and ci != 0.0 and abs(abs(cr) - abs(ci)) < 1e-12:
        k, sr, si = abs(cr), math.copysign(1.0, cr), math.copysign(1.0, ci)
        return (_p_scale(_p_add(_p_scale(re, sr), _p_scale(im, -si)), k),
                _p_scale(_p_add(_p_scale(re, si), _p_scale(im, sr)), k))
    return (_p_add(_p_scale(re, cr), _p_scale(im, -ci)), _p_add(_p_scale(re, ci), _p_scale(im, cr)))


def _is_zero(x):
    return x is None or (x[0] is None and x[1] is None)


def _slab_fft_dit(xs, sign, need):
    n = len(xs)
    if n == 1:
        return {0: xs[0]}
    if all(_is_zero(x) for x in xs):
        return {k: None for k in need}
    h = n // 2
    sub = {k % h for k in need}
    ev = _slab_fft_dit(xs[0::2], sign, sub)
    od = _slab_fft_dit(xs[1::2], sign, sub)
    out = {}
    for k in sub:
        t = _c_mulc(od[k], cmath.exp(sign * 2j * math.pi * k / n))
        if k in need:
            out[k] = _c_add(ev[k], t)
        if k + h in need:
            out[k + h] = _c_add(ev[k], t, -1)
    return out


def _slab_fft_dif(xs, sign, need):
    n = len(xs)
    if n == 1:
        return {0: xs[0]}
    h = n // 2
    need_e = {k // 2 for k in need if k % 2 == 0}
    need_o = {k // 2 for k in need if k % 2 == 1}
    out = {}
    if need_e:
        ev = _slab_fft_dif([_c_add(xs[j], xs[j + h]) for j in range(h)], sign, need_e)
        out.update({2 * k: v for k, v in ev.items()})
    if need_o:
        dif = [_c_mulc(_c_add(xs[j], xs[j + h], -1), cmath.exp(sign * 2j * math.pi * j / n)) for j in range(h)]
        od = _slab_fft_dif(dif, sign, need_o)
        out.update({2 * k + 1: v for k, v in od.items()})
    return out


def _part_value(p, like):
    if p is None:
        return jnp.zeros_like(like)
    return p[0] if p[1] > 0 else -p[0]


HY_CT = 128
SLAB = 256
N_HI = 2 * SEQ // SLAB
N_IN = SEQ // SLAB
N_SPEC = N_HI // 2 + 1
ROW_CHUNK = 8
HY_ROWS = 16
S_IM = SLAB
N_CT = HYENA_WIDTH // HY_CT
SPEC_GROUP = 4
FILTER_HIDDEN = 64
EMB_PAD = 128
HIGHEST = lax.Precision.HIGHEST


def _dft_constants():
    n = 2 * SEQ
    r = np.arange(SLAB)
    tall = np.zeros((N_SPEC, 2 * SLAB, SLAB))
    fwd = np.zeros((N_SPEC, 2 * SLAB, 2 * SLAB))
    inv = np.zeros((N_SPEC, 2 * SLAB, 2 * SLAB))
    for b in range(N_SPEC):
        m = np.exp(-2j * np.pi * (np.outer(r, r) / SLAB + (r * b)[None, :] / n))
        a = m.T / n
        tall[b] = np.concatenate([m.real, m.imag], axis=0)
        fwd[b] = np.block([[m.real, -m.imag], [m.imag, m.real]])
        inv[b] = np.block([[a.real, a.imag], [-a.imag, a.real]])
    return tuple(jnp.asarray(t, F32).astype(BF16) for t in (tall, fwd, inv))


def _ctx_dft_constants():
    n = 2 * CTX_LEN
    k = np.arange(n)
    f = np.exp(-2j * np.pi * np.outer(k, k) / n)
    a = np.conj(f)[:CTX_LEN, :] / n
    return (jnp.asarray(np.concatenate([f.real, f.imag], axis=0), F32).astype(BF16),
            jnp.asarray(np.concatenate([a.real, a.imag], axis=0), F32).astype(BF16))


def _filter_positions(l):
    t = np.linspace(0.0, 1.0, l)[:, None]
    bands = (FILTER_EMB - 1) // 2
    f = np.linspace(1e-4, bands - 1, bands)[None, :]
    w = 2.0 * math.pi * np.arange(l)[:, None] / l
    z = np.concatenate([t, np.cos(f * w), -np.sin(f * w)], axis=-1)
    z2 = np.concatenate([z, z[0:1], z[1:][::-1]], axis=0)
    return jnp.asarray(np.pad(z2, ((0, 0), (0, EMB_PAD - FILTER_EMB))), F32)


def _filter_feat_kernel(za_ref, zb_ref, w1_ref, b1_ref, w2_ref, b2_ref, w3_ref, b3_ref, fr_ref, o_ref):
    fr = fr_ref[...]
    dot = functools.partial(jnp.dot, precision=HIGHEST, preferred_element_type=F32)
    z = jnp.concatenate([za_ref[...], zb_ref[...]], axis=1)
    h = jnp.sin(fr * (dot(z, w1_ref[...]) + b1_ref[...]))
    h = jnp.sin(fr * (dot(h, w2_ref[...]) + b2_ref[...]))
    o_ref[...] = jnp.sin(fr * (dot(h, w3_ref[...]) + b3_ref[...]))


def _filter_features(emb, w1, b1, w2, b2, w3, b3, freq):
    l = emb.shape[0] // 2
    tr = min(l, 1024)
    nblk = l // tr
    diag2 = lambda w: jnp.kron(jnp.eye(2, dtype=F32), w)
    w1p = jnp.pad(w1, ((0, EMB_PAD - FILTER_EMB), (0, 0)))
    vec = lambda v: jnp.tile(v.reshape(1, FILTER_HIDDEN), (1, 2))
    full = lambda shape: pl.BlockSpec(shape, lambda i: (0, 0))
    wide = 2 * FILTER_HIDDEN
    return pl.pallas_call(
        _filter_feat_kernel,
        grid=(nblk,),
        in_specs=[pl.BlockSpec((tr, EMB_PAD), lambda i: (i, 0)),
                  pl.BlockSpec((tr, EMB_PAD), lambda i: (i + nblk, 0)),
                  full((2 * EMB_PAD, wide)), full((1, wide)),
                  full((wide, wide)), full((1, wide)),
                  full((wide, wide)), full((1, wide)),
                  full((1, wide))],
        out_specs=pl.BlockSpec((tr, wide), lambda i: (i, 0)),
        out_shape=jax.ShapeDtypeStruct((l, wide), F32),
        compiler_params=_cparams(("arbitrary",), 32),
        name="filter_feat",
    )(emb, emb, diag2(w1p), vec(b1), diag2(w2), vec(b2), diag2(w3), vec(b3), vec(freq))


def _filter_taps(raw, n, l, absdelta):
    m = jnp.where(n < l, n, 2 * l - n)
    t = m.astype(F32) / (l - 1)
    return jnp.where(n == l, 0.0, raw * jnp.exp(-t * absdelta))


def _cplx_from_pq(pq, conj):
    half = pq.shape[0] // 2
    pr, pi_ = pq[:half, :HY_CT], pq[:half, HY_CT:]
    qr, qi = pq[half:, :HY_CT], pq[half:, HY_CT:]
    if conj:
        return pr + qi, pi_ - qr
    return pr - qi, qr + pi_


def _slab_forward(src_ref, n_in, real_only, dst_ref, need, scale):
    def body(rc, carry):
        rows = pl.ds(pl.multiple_of(rc * ROW_CHUNK, ROW_CHUNK), ROW_CHUNK)
        xs = []
        for a in range(N_HI):
            if a >= n_in:
                xs.append(None)
                continue
            re = src_ref[a, rows, 0:HY_CT]
            if scale is not None:
                re = re * scale
            xs.append(((re, 1), None if real_only else (src_ref[a, rows, HY_CT:2 * HY_CT], 1)))
        like = xs[0][0][0]
        out = _slab_fft_dit(xs, -1, need)
        for b in sorted(need):
            c = out[b] or (None, None)
            dst_ref[b, rows, 0:HY_CT] = _part_value(c[0], like)
            dst_ref[b, rows, HY_CT:2 * HY_CT] = _part_value(c[1], like)
        return carry

    lax.fori_loop(0, SLAB // ROW_CHUNK, body, 0)


def _filter_spec_kernel(feat_ref, wf_ref, wb_ref, ad_ref, mf_ref, o_ref, ft_ref, s_ref):
    rate = ad_ref[...] * (-1.0 / (SEQ - 1))
    row = lax.broadcasted_iota(jnp.int32, (SLAB, HY_CT), 0)
    row_f = row.astype(F32)
    first_row = row == 0
    no_w = jnp.zeros((FILTER_HIDDEN, HY_CT), F32)
    w_fwd = jnp.concatenate([wf_ref[...], no_w], axis=0).astype(BF16)
    w_bwd = jnp.concatenate([no_w, wb_ref[...]], axis=0).astype(BF16)

    def taps(a, acc):
        fwd = a < N_IN
        src = jnp.where(fwd, a, a - N_IN)
        rows = feat_ref[pl.ds(pl.multiple_of(src * SLAB, SLAB), SLAB), :]
        raw = jnp.dot(rows.astype(BF16), jnp.where(fwd, w_fwd, w_bwd), preferred_element_type=F32)
        base = jnp.where(fwd, a * SLAB, 2 * SEQ - a * SLAB).astype(F32)
        step = jnp.where(fwd, 1.0, -1.0)
        v = raw * jnp.exp((base + step * row_f) * rate)
        v = jnp.where(jnp.logical_and(a == N_IN, first_row), 0.0, v)
        ft_ref[a] = v
        return acc + jnp.sum(jnp.abs(v), axis=0, keepdims=True)

    norm = lax.fori_loop(0, N_HI, taps, jnp.zeros((1, HY_CT), F32), unroll=2)
    _slab_forward(ft_ref, N_HI, True, s_ref, set(range(N_SPEC)), 1.0 / norm)

    def spec(b):
        pq = jnp.dot(mf_ref[b], s_ref[b].astype(BF16), preferred_element_type=F32)
        xr, xi = _cplx_from_pq(pq, False)
        rows = pl.ds(pl.multiple_of(b * SLAB, SLAB), SLAB)
        o_ref[rows, 0:HY_CT] = xr.astype(BF16)
        o_ref[rows, HY_CT:2 * HY_CT] = xi.astype(BF16)

    def spec_group(i, carry):
        for t in range(SPEC_GROUP):
            spec(1 + SPEC_GROUP * i + t)
        return carry

    spec(0)
    lax.fori_loop(0, (N_SPEC - 1) // SPEC_GROUP, spec_group, 0)


def _filter_spectrum(feat, w_fout, absdelta, mf):
    cols = 2 * N_CT
    return pl.pallas_call(
        _filter_spec_kernel,
        grid=(HYENA_ORDER, N_CT),
        in_specs=[
            pl.BlockSpec((SEQ, 2 * FILTER_HIDDEN), lambda o, c: (0, 0)),
            pl.BlockSpec((FILTER_HIDDEN, HY_CT), lambda o, c: (0, o * cols + c)),
            pl.BlockSpec((FILTER_HIDDEN, HY_CT), lambda o, c: (0, o * cols + N_CT + c)),
            pl.BlockSpec((1, HY_CT), lambda o, c: (0, c)),
            pl.BlockSpec((N_SPEC, 2 * SLAB, SLAB), lambda o, c: (0, 0, 0)),
        ],
        out_specs=pl.BlockSpec((None, None, N_SPEC * SLAB, 2 * HY_CT), lambda o, c: (o, c, 0, 0)),
        out_shape=jax.ShapeDtypeStruct((HYENA_ORDER, N_CT, N_SPEC * SLAB, 2 * HY_CT), BF16),
        scratch_shapes=[pltpu.VMEM((N_HI, SLAB, HY_CT), F32), pltpu.VMEM((N_SPEC, SLAB, 2 * HY_CT), F32)],
        compiler_params=_cparams(("arbitrary", "arbitrary"), 48),
        name="filter_spec",
    )(feat, w_fout, w_fout, absdelta, mf)


HALO = 8


def _conv_slab(u_ref, bi, a, w, b, st):
    r0 = a * SLAB
    x = u_ref[bi, pl.ds(pl.multiple_of(r0, SLAB), SLAB), :].astype(F32)
    before = u_ref[bi, pl.ds(pl.multiple_of(jnp.maximum(r0 - 16, 0), 16), 16), :].astype(F32)
    after = u_ref[bi, pl.ds(pl.multiple_of(jnp.minimum(r0 + SLAB, SEQ - 16), 16), 16), :].astype(F32)
    st[0:HALO, :] = jnp.where(a > 0, before[16 - HALO:16, :], 0.0)
    st[HALO:HALO + SLAB, :] = x
    st[HALO + SLAB:2 * HALO + SLAB, :] = jnp.where(a < N_IN - 1, after[0:HALO, :], 0.0)
    xm1 = st[HALO - 1:HALO - 1 + SLAB, :]
    xp1 = st[HALO + 1:HALO + 1 + SLAB, :]
    return xm1 * w[0:1, :] + x * w[1:2, :] + xp1 * w[2:3, :] + b


def _hyena_kernel(*refs):
    for o in range(HYENA_ORDER):
        _hyena_order(o, *refs)


def _hyena_order(o, uv_ref, u1_ref, u2_ref, cwv_ref, cw1_ref, cw2_ref, cbv_ref, cb1_ref, cb2_ref, skip_ref,
                 mf_ref, mi_ref, h_ref, o_ref, zc_ref, g_ref, s_ref, st_ref):
    if o == 0:
        def conv(a, carry):
            for bi in range(2):
                lanes = slice(bi * HY_CT, (bi + 1) * HY_CT)
                st = lambda k: st_ref.at[3 * bi + k]
                zc_ref[a, 0:SLAB, lanes] = _conv_slab(uv_ref, bi, a, cwv_ref[...], cbv_ref[...], st(0)).astype(BF16)
                g_ref[0, a, 0:SLAB, lanes] = _conv_slab(u1_ref, bi, a, cw1_ref[...], cb1_ref[...], st(1)).astype(BF16)
                g_ref[1, a, 0:SLAB, lanes] = _conv_slab(u2_ref, bi, a, cw2_ref[...], cb2_ref[...], st(2)).astype(BF16)
            return carry

        lax.fori_loop(0, N_IN, conv, 0)

    def im_sign(b):
        return -1 if b > N_HI // 2 else 1

    def forward_chunk(rc, carry):
        rows = pl.ds(pl.multiple_of(rc * HY_ROWS, HY_ROWS), HY_ROWS)
        rows_im = pl.ds(pl.multiple_of(S_IM + rc * HY_ROWS, HY_ROWS), HY_ROWS)
        xs = [((zc_ref[a, rows, 0:HY_CT], 1), (zc_ref[a, rows, HY_CT:2 * HY_CT], 1)) for a in range(N_IN)]
        like = xs[0][0][0]
        out = _slab_fft_dit(xs + [None] * (N_HI - N_IN), -1, set(range(N_HI)))
        for b in range(N_HI):
            re, im = out[b]
            s_ref[b, rows, :] = _part_value(re, like)
            s_ref[b, rows_im, :] = _part_value(None if im is None else (im[0], im[1] * im_sign(b)), like)
        return carry

    lax.fori_loop(0, SLAB // HY_ROWS, forward_chunk, 0)

    def through_filter(slabs, j):
        h = h_ref[o, j * SLAB:(j + 1) * SLAB, :].astype(F32)
        hr = jnp.concatenate([h[:, :HY_CT]] * len(slabs), axis=1)
        hi = jnp.concatenate([h[:, HY_CT:]] * len(slabs), axis=1)
        load = lambda b: jnp.concatenate([s_ref[b, 0:SLAB, :], s_ref[b, S_IM:S_IM + SLAB, :]], axis=0)
        d = load(slabs[0]) if len(slabs) == 1 else jnp.concatenate([load(b) for b in slabs], axis=1)
        x = jnp.dot(mf_ref[j], d, preferred_element_type=F32)
        xr, xi = x[:SLAB], x[SLAB:]
        y = jnp.concatenate([xr * hr - xi * hi, xr * hi + xi * hr], axis=0).astype(BF16)
        w = jnp.dot(mi_ref[j], y, preferred_element_type=F32).astype(BF16)
        return [(b, w[:, k * HY_CT:(k + 1) * HY_CT]) for k, b in enumerate(slabs)]

    for slabs, j in [([0], 0), ([N_HI // 2], N_HI // 2)] + [([j, N_HI - j], j) for j in range(1, N_HI // 2)]:
        for b, w in through_filter(slabs, j):
            s_ref[b, 0:SLAB, :] = w[:SLAB]
            s_ref[b, S_IM:S_IM + SLAB, :] = w[SLAB:]

    def inverse_chunk(rc, carry):
        rows = pl.ds(pl.multiple_of(rc * HY_ROWS, HY_ROWS), HY_ROWS)
        rows_im = pl.ds(pl.multiple_of(S_IM + rc * HY_ROWS, HY_ROWS), HY_ROWS)
        xs = [((s_ref[b, rows, :], 1), (s_ref[b, rows_im, :], im_sign(b))) for b in range(N_HI)]
        like = xs[0][0][0]
        out = _slab_fft_dif(xs, 1, set(range(N_IN)))
        for a in range(N_IN):
            s_ref[a, rows, :] = _part_value(out[a][0], like)
            s_ref[a, rows_im, :] = _part_value(out[a][1], like)
        return carry

    lax.fori_loop(0, SLAB // HY_ROWS, inverse_chunk, 0)

    sk = skip_ref[o:o + 1, :]
    skip2 = jnp.concatenate([sk, sk], axis=1)

    def gate(a, carry):
        y = jnp.concatenate([s_ref[a, 0:SLAB, :], s_ref[a, S_IM:S_IM + SLAB, :]], axis=1).astype(F32)
        z = g_ref[o, a, 0:SLAB, :].astype(F32) * (y + skip2 * zc_ref[a, 0:SLAB, :].astype(F32))
        zc_ref[a, 0:SLAB, :] = z.astype(BF16)
        return carry

    lax.fori_loop(0, N_IN, gate, 0)

    if o == HYENA_ORDER - 1:
        def emit(a, carry):
            rows = pl.ds(pl.multiple_of(a * SLAB, SLAB), SLAB)
            for bi in range(2):
                o_ref[bi, rows, :] = zc_ref[a, 0:SLAB, bi * HY_CT:(bi + 1) * HY_CT]
            return carry

        lax.fori_loop(0, N_IN, emit, 0)


def _hyena_conv(p, conv_w, conv_b, skip, mf, mi, hspec):
    b = p.shape[0]
    u0 = V_END // HY_CT
    cb = conv_b.reshape(1, -1)
    part = lambda k: pl.BlockSpec((2, SEQ, HY_CT), lambda c, q: (q, 0, u0 + k * N_CT + c))
    cw = lambda k: pl.BlockSpec((SHORT_CONV, HY_CT), lambda c, q: (0, k * N_CT + c))
    cbs = lambda k: pl.BlockSpec((1, HY_CT), lambda c, q: (0, k * N_CT + c))
    const = lambda: pl.BlockSpec((N_SPEC, 2 * SLAB, 2 * SLAB), lambda c, q: (0, 0, 0),
                                 pipeline_mode=pl.Buffered(1))
    return pl.pallas_call(
        _hyena_kernel,
        grid=(N_CT, b // 2),
        in_specs=[part(0), part(1), part(2), cw(0), cw(1), cw(2), cbs(0), cbs(1), cbs(2),
                  pl.BlockSpec((HYENA_ORDER, HY_CT), lambda c, q: (0, c)),
                  const(), const(),
                  pl.BlockSpec((HYENA_ORDER, None, N_SPEC * SLAB, 2 * HY_CT), lambda c, q: (0, c, 0, 0))],
        out_specs=pl.BlockSpec((2, SEQ, HY_CT), lambda c, q: (q, 0, c)),
        out_shape=jax.ShapeDtypeStruct((b, SEQ, HYENA_WIDTH), BF16),
        scratch_shapes=[pltpu.VMEM((N_IN, SLAB, 2 * HY_CT), BF16),
                        pltpu.VMEM((HYENA_ORDER, N_IN, SLAB, 2 * HY_CT), BF16),
                        pltpu.VMEM((N_HI, 2 * SLAB, HY_CT), BF16),
                        pltpu.VMEM((2 * (HYENA_ORDER + 1), SLAB + 2 * HALO, HY_CT), F32)],
        compiler_params=_cparams(("arbitrary", "arbitrary"), 56),
        name="hyena_conv",
    )(p, p, p, conv_w, conv_w, conv_w, cb, cb, cb, skip, mf, mi, hspec)


def _ctx_hyena_kernel(uv_ref, u1_ref, u2_ref, cwv_ref, cw1_ref, cw2_ref, cbv_ref, cb1_ref, cb2_ref, skip_ref,
                      feat_ref, wf0_ref, wb0_ref, wf1_ref, wb1_ref, ad_ref, ff_ref, ai_ref, o_ref):
    l = CTX_LEN
    nb = uv_ref.shape[0]
    n = lax.broadcasted_iota(jnp.int32, (2 * l, HY_CT), 0)
    spectra = []
    no_w = jnp.zeros((FILTER_HIDDEN, HY_CT), F32)
    for wf_ref, wb_ref in ((wf0_ref, wb0_ref), (wf1_ref, wb1_ref)):
        raw = jnp.concatenate(
            [jnp.dot(feat_ref[...], jnp.concatenate([wf_ref[...], no_w], axis=0), precision=HIGHEST,
                     preferred_element_type=F32),
             jnp.dot(feat_ref[...], jnp.concatenate([no_w, wb_ref[...]], axis=0), precision=HIGHEST,
                     preferred_element_type=F32)], axis=0)
        v = _filter_taps(raw, n, l, ad_ref[...])
        v = v / jnp.sum(jnp.abs(v), axis=0, keepdims=True)
        pq = jnp.dot(ff_ref[...], v.astype(BF16), preferred_element_type=F32)
        spectra.append((pq[:2 * l], pq[2 * l:]))

    row = lax.broadcasted_iota(jnp.int32, (l, HY_CT), 0)

    def conv(u_ref, bi, w_ref, b_ref):
        x = u_ref[bi].astype(F32)
        xm1 = jnp.where(row == 0, 0.0, pltpu.roll(x, 1, 0))
        xp1 = jnp.where(row == l - 1, 0.0, pltpu.roll(x, l - 1, 0))
        return xm1 * w_ref[0:1, :] + x * w_ref[1:2, :] + xp1 * w_ref[2:3, :] + b_ref[...]

    z = [conv(uv_ref, bi, cwv_ref, cbv_ref) for bi in range(nb)]
    gates = [[conv(u1_ref, bi, cw1_ref, cb1_ref) for bi in range(nb)],
             [conv(u2_ref, bi, cw2_ref, cb2_ref) for bi in range(nb)]]
    for o in range(HYENA_ORDER):
        hr, hi = spectra[o]
        pq = jnp.dot(ff_ref[:, 0:l], jnp.concatenate(z, axis=1).astype(BF16), preferred_element_type=F32)
        ys = []
        for q in range(nb // 2):
            xr, xi = _cplx_from_pq(pq[:, 2 * q * HY_CT:2 * (q + 1) * HY_CT], False)
            ys += [xr * hr - xi * hi, xr * hi + xi * hr]
        pq2 = jnp.dot(ai_ref[...], jnp.concatenate(ys, axis=1).astype(BF16), preferred_element_type=F32)
        y = []
        for q in range(nb // 2):
            y += list(_cplx_from_pq(pq2[:, 2 * q * HY_CT:2 * (q + 1) * HY_CT], False))
        z = [gates[o][bi] * (y[bi] + skip_ref[o:o + 1, :] * z[bi]) for bi in range(nb)]
    for bi in range(nb):
        o_ref[bi] = z[bi].astype(BF16)


def _ctx_hyena(pc, conv_w, conv_b, skip, feat, w_fout, absdelta, ff, ai):
    b, l, _ = pc.shape
    u0 = V_END // HY_CT
    cb = conv_b.reshape(1, -1)
    cols = 2 * N_CT
    part = lambda k: pl.BlockSpec((b, l, HY_CT), lambda c: (0, 0, u0 + k * N_CT + c))
    cw = lambda k: pl.BlockSpec((SHORT_CONV, HY_CT), lambda c: (0, k * N_CT + c))
    cbs = lambda k: pl.BlockSpec((1, HY_CT), lambda c: (0, k * N_CT + c))
    wcol = lambda off: pl.BlockSpec((FILTER_HIDDEN, HY_CT), lambda c: (0, off + c))
    return pl.pallas_call(
        _ctx_hyena_kernel,
        grid=(N_CT,),
        in_specs=[part(0), part(1), part(2), cw(0), cw(1), cw(2), cbs(0), cbs(1), cbs(2),
                  pl.BlockSpec((HYENA_ORDER, HY_CT), lambda c: (0, c)),
                  pl.BlockSpec((l, 2 * FILTER_HIDDEN), lambda c: (0, 0)),
                  wcol(0), wcol(N_CT), wcol(cols), wcol(cols + N_CT),
                  pl.BlockSpec((1, HY_CT), lambda c: (0, c)),
                  pl.BlockSpec((4 * l, 2 * l), lambda c: (0, 0)),
                  pl.BlockSpec((2 * l, 2 * l), lambda c: (0, 0))],
        out_specs=pl.BlockSpec((b, l, HY_CT), lambda c: (0, 0, c)),
        out_shape=jax.ShapeDtypeStruct((b, l, HYENA_WIDTH), BF16),
        compiler_params=_cparams(("arbitrary",), 32),
        name="ctx_hyena",
    )(pc, pc, pc, conv_w, conv_w, conv_w, cb, cb, cb, skip, feat, w_fout, w_fout, w_fout, w_fout, absdelta, ff, ai)


def _rope_tables(l):
    rows = l // GRID_W
    row = np.repeat(np.arange(rows), GRID_W).astype(np.float64)
    col = np.tile(np.arange(GRID_W), rows).astype(np.float64)
    n_freq = HEAD_DIM // 4
    inv = ROPE_BASE ** (-np.arange(n_freq) / n_freq)
    ang_r = row[:, None] * inv
    ang_c = col[:, None] * inv
    cos = np.concatenate([np.cos(ang_r), np.cos(ang_r), np.cos(ang_c), np.cos(ang_c)], axis=-1)
    sin = np.concatenate([-np.sin(ang_r), np.sin(ang_r), -np.sin(ang_c), np.sin(ang_c)], axis=-1)
    return jnp.asarray(cos, F32), jnp.asarray(sin, F32)


def kernel(x, c, ctx, c_ctx, w_ada, b_ada, norm_mix, norm_mlp, w_in, q_norm, k_norm, attn_sink, conv_w, conv_b, filt_w1, filt_b1, filt_w2, filt_b2, filt_w3, filt_b3, filt_freq, filt_w_out, hyena_skip, w_out, w_mlp_in, w_mlp_out):
    cos, sin = _rope_tables(SEQ)
    emb, emb_ctx = _filter_positions(SEQ), _filter_positions(CTX_LEN)
    m_tall, mf, mi = _dft_constants()
    ff_ctx, ai_ctx = _ctx_dft_constants()
    absdelta = jnp.abs(jnp.linspace(MIN_DECAY, MAX_DECAY, HYENA_WIDTH, dtype=F32)).reshape(1, HYENA_WIDTH)
    cvec = jnp.zeros((MOD_ROWS, D_MODEL), F32).at[:BATCH].set(c).at[CTX_ROW].set(c_ctx)
    mod_all = _modulation(cvec, w_ada, b_ada).reshape(DEPTH, MOD_ROWS, N_MOD, D_MODEL)
    w_in_bf = w_in[0:1].astype(BF16)
    ctx_rows = BATCH * CTX_LEN
    flat = lambda a: a.reshape(1, ctx_rows, a.shape[-1])
    per_batch = lambda a: a.reshape(BATCH, CTX_LEN, a.shape[-1])
    xc = flat(ctx)
    for i in range(DEPTH):
        update_ctx = i < DEPTH - 1
        mod = mod_all[i]
        nmix = norm_mix[i].reshape(1, D_MODEL)
        nmlp = norm_mlp[i].reshape(1, D_MODEL)
        qn = q_norm[i].reshape(1, HEAD_DIM)
        kn = k_norm[i].reshape(1, HEAD_DIM)
        filt_mlp = (filt_w1[i], filt_b1[i], filt_w2[i], filt_b2[i], filt_w3[i], filt_b3[i], filt_freq[i])

        p, (w_out_bf, w1_bf, w2_bf) = _in_projection(
            x, mod, None, nmix, w_in_bf, qn, kn, cos, sin, tm=512, n_off=0, n_tiles=IN_COLS // IN_TN,
            use_rope=True, riders=[(w_out, i), (w_mlp_in, i), (w_mlp_out, i)])
        if update_ctx:
            pc, _ = _in_projection(xc, mod, CTX_ROW, nmix, w_in_bf, qn, kn, cos, sin,
                                   tm=512, n_off=0, n_tiles=IN_COLS // IN_TN, use_rope=False)
            kx_blk, vx_blk = Q_END // KV_COLS * 2, Q_END // KV_COLS * 2 + 1
        else:
            pc, _ = _in_projection(xc, mod, CTX_ROW, nmix, w_in_bf, qn, kn, cos, sin,
                                   tm=512, n_off=KV_TILE, n_tiles=1, use_rope=False)
            kx_blk, vx_blk = 0, 1
        pc = per_batch(pc)

        attn = _window_attention(p, pc, kx_blk, vx_blk, attn_sink[i])
        hspec = _filter_spectrum(_filter_features(emb, *filt_mlp), filt_w_out[i], absdelta, m_tall)
        hy = _hyena_conv(p, conv_w[i], conv_b[i], hyena_skip[i], mf, mi, hspec)
        x, h2, next_w_in = _out_projection(attn, hy, x, mod, None, nmlp, w_out_bf, tm=512,
                                           riders=[(w_in, i + 1)] if update_ctx else [])
        x = _mlp(h2, x, mod, None, w1_bf, w2_bf, tm=512, tf=1024)

        if update_ctx:
            attn_c = _context_attention(pc, attn_sink[i])
            hy_c = _ctx_hyena(pc, conv_w[i], conv_b[i], hyena_skip[i], _filter_features(emb_ctx, *filt_mlp),
                              filt_w_out[i], absdelta, ff_ctx, ai_ctx)
            xc, h2c, _ = _out_projection(flat(attn_c), flat(hy_c), xc, mod, CTX_ROW, nmlp, w_out_bf, tm=512)
            xc = _mlp(h2c, xc, mod, CTX_ROW, w1_bf, w2_bf, tm=512, tf=1024)
            w_in_bf = next_w_in[0]
    return x
```

```python
import cmath
import functools
import math

import jax
import jax.numpy as jnp
import numpy as np
from jax import lax
from jax.experimental import pallas as pl
from jax.experimental.pallas import tpu as pltpu

D_MODEL = 2048
BATCH = 4
SEQ = 4096
DEPTH = 2
GRID_W = 64
CTX_LEN = 256
HEAD_DIM = 128
N_Q_HEADS = 8
N_KV_HEADS = 2
GQA_GROUP = N_Q_HEADS // N_KV_HEADS
ATTN_WIDTH = N_Q_HEADS * HEAD_DIM
HYENA_WIDTH = D_MODEL - ATTN_WIDTH
HYENA_ORDER = 2
SHORT_CONV = 3
FILTER_EMB = 33
DECAY_TARGET = 1e-2
FAST_DECAY_PCT = 0.3
SLOW_DECAY_PCT = 1.5
MAX_DECAY = math.log(DECAY_TARGET) / FAST_DECAY_PCT
MIN_DECAY = math.log(DECAY_TARGET) / SLOW_DECAY_PCT
WINDOW = 128
BLOCK = 128
ROPE_BASE = 10000.0
D_FF = 4 * D_MODEL
N_MOD = 6
EPS = 1e-6
NEG_INF = -1e30
Q_END = ATTN_WIDTH
K_END = Q_END + N_KV_HEADS * HEAD_DIM
V_END = K_END + N_KV_HEADS * HEAD_DIM
IN_COLS = V_END + (HYENA_ORDER + 1) * HYENA_WIDTH

MOD_ROWS = 8
CTX_ROW = BATCH
KV_COLS = 2 * N_KV_HEADS * HEAD_DIM

F32 = jnp.float32
BF16 = jnp.bfloat16
MIB = 1024 * 1024


def _cparams(sem, vmem_mib):
    return pltpu.CompilerParams(dimension_semantics=sem, vmem_limit_bytes=vmem_mib * MIB)


def _mod_kernel(c_ref, w_ref, b_ref, o_ref):
    c = c_ref[...]
    s = (c * jax.nn.sigmoid(c)).astype(BF16)
    o_ref[...] = jnp.dot(s, w_ref[...].astype(BF16), preferred_element_type=F32) + b_ref[...]


def _modulation(cvec, w_ada, b_ada):
    tn = 1024
    n_cols = N_MOD * D_MODEL
    return pl.pallas_call(
        _mod_kernel,
        grid=(DEPTH, n_cols // tn),
        in_specs=[
            pl.BlockSpec((MOD_ROWS, D_MODEL), lambda l, j: (0, 0)),
            pl.BlockSpec((None, D_MODEL, tn), lambda l, j: (l, 0, j)),
            pl.BlockSpec((None, 1, tn), lambda l, j: (l, 0, j)),
        ],
        out_specs=pl.BlockSpec((None, MOD_ROWS, tn), lambda l, j: (l, 0, j)),
        out_shape=jax.ShapeDtypeStruct((DEPTH, MOD_ROWS, n_cols), F32),
        compiler_params=_cparams(("arbitrary", "arbitrary"), 40),
        name="adaln_mod",
    )(cvec, w_ada, b_ada.reshape(DEPTH, 1, n_cols))


IN_TN = 512
Q_TILES = Q_END // IN_TN
KV_TILE = Q_TILES
HEADS_PER_TILE = IN_TN // HEAD_DIM


SUB_ROWS = 256


def _rms_modulate(xf, gain, shift):
    ms = jnp.mean(xf * xf, axis=-1, keepdims=True)
    return xf * lax.rsqrt(ms + EPS) * gain + shift


def _head_norm_rope(a, g, cos, sin):
    ms = jnp.mean(a * a, axis=-1, keepdims=True)
    y = a * lax.rsqrt(ms + EPS) * g
    if cos is None:
        return y
    lane = lax.broadcasted_iota(jnp.int32, y.shape, 1)
    partner = jnp.where((lane % 64) < 32, pltpu.roll(y, 96, 1), pltpu.roll(y, 32, 1))
    return y * cos + partner * sin


BF16_TILE_ROWS = 16


def _rider_plan(w, layer, steps, step_of):
    depth, r, c = w.shape
    assert r % (steps * BF16_TILE_ROWS) == 0
    blk = r // steps
    in_spec = pl.BlockSpec((blk, c), lambda *g: (layer * steps + step_of(*g), 0))
    out_spec = pl.BlockSpec((blk, c), lambda *g: (step_of(*g), 0))
    restore = lambda a: a.reshape(1, r, c)
    return w.reshape(depth * r, c), in_spec, out_spec, jax.ShapeDtypeStruct((r, c), BF16), restore


def _cast_riders(in_refs, out_refs):
    for src, dst in zip(in_refs, out_refs):
        dst[...] = src[...].astype(BF16)


def _inproj_kernel(*refs, n_off, n_tiles, use_rope, n_riders):
    x_ref, m_ref, nw_ref, w_ref, qn_ref, kn_ref, cos_ref, sin_ref = refs[:8]
    o_ref = refs[8 + n_riders]
    _cast_riders(refs[8:8 + n_riders], refs[9 + n_riders:])
    gain = nw_ref[...] * (1.0 + m_ref[1:2, :])
    shift = m_ref[0:1, :]
    for r0 in range(0, x_ref.shape[0], SUB_ROWS):
        rows = slice(r0, r0 + SUB_ROWS)
        h = _rms_modulate(x_ref[rows, :], gain, shift).astype(BF16)
        cos = cos_ref[rows, :] if use_rope else None
        sin = sin_ref[rows, :] if use_rope else None
        for n in range(n_tiles):
            cols = slice(n * IN_TN, (n + 1) * IN_TN)
            acc = jnp.dot(h, w_ref[:, cols], preferred_element_type=F32)
            t = n + n_off
            if t > KV_TILE:
                o_ref[rows, cols] = acc.astype(BF16)
                continue
            gains = [qn_ref[...]] * HEADS_PER_TILE if t < Q_TILES else [kn_ref[...]] * N_KV_HEADS + [None] * N_KV_HEADS
            for c, g in enumerate(gains):
                sl = slice(c * HEAD_DIM, (c + 1) * HEAD_DIM)
                osl = slice(n * IN_TN + c * HEAD_DIM, n * IN_TN + (c + 1) * HEAD_DIM)
                if g is None:
                    o_ref[rows, osl] = acc[:, sl].astype(BF16)
                else:
                    o_ref[rows, osl] = _head_norm_rope(acc[:, sl], g, cos, sin).astype(BF16)


def _in_projection(x, mod, mod_row, nw, w_bf, qn, kn, cos, sin, *, tm, n_off, n_tiles, use_rope, riders=()):
    b, t, _ = x.shape
    assert n_off % n_tiles == 0 and tm % SUB_ROWS == 0
    row = (lambda bb: bb) if mod_row is None else (lambda bb: mod_row)
    n_i = t // tm
    plans = [_rider_plan(w, layer, b * n_i, lambda bb, i: bb * n_i + i) for w, layer in riders]
    kern = functools.partial(_inproj_kernel, n_off=n_off, n_tiles=n_tiles, use_rope=use_rope, n_riders=len(plans))
    width = n_tiles * IN_TN
    out = pl.pallas_call(
        kern,
        grid=(b, n_i),
        in_specs=[
            pl.BlockSpec((None, tm, D_MODEL), lambda bb, i: (bb, i, 0)),
            pl.BlockSpec((None, N_MOD, D_MODEL), lambda bb, i: (row(bb), 0, 0)),
            pl.BlockSpec((1, D_MODEL), lambda bb, i: (0, 0)),
            pl.BlockSpec((None, D_MODEL, width), lambda bb, i: (0, 0, n_off // n_tiles),
                         pipeline_mode=pl.Buffered(1)),
            pl.BlockSpec((1, HEAD_DIM), lambda bb, i: (0, 0)),
            pl.BlockSpec((1, HEAD_DIM), lambda bb, i: (0, 0)),
            pl.BlockSpec((tm, HEAD_DIM), lambda bb, i: (i, 0)),
            pl.BlockSpec((tm, HEAD_DIM), lambda bb, i: (i, 0)),
        ] + [pl_[1] for pl_ in plans],
        out_specs=[pl.BlockSpec((None, tm, width), lambda bb, i: (bb, i, 0))] + [pl_[2] for pl_ in plans],
        out_shape=[jax.ShapeDtypeStruct((b, t, width), BF16)] + [pl_[3] for pl_ in plans],
        compiler_params=_cparams(("arbitrary", "arbitrary"), 56),
        name="in_proj",
    )(x, mod, nw, w_bf, qn, kn, cos, sin, *[pl_[0] for pl_ in plans])
    return out[0], [pl_[4](a) for pl_, a in zip(plans, out[1:])]


def _sink_column(sink_ref, kv, rows_per_head):
    rows = GQA_GROUP * rows_per_head
    head = lax.broadcasted_iota(jnp.int32, (rows, 1), 0) // rows_per_head
    col = jnp.full((rows, 1), sink_ref[kv * GQA_GROUP], F32)
    for g in range(1, GQA_GROUP):
        col = jnp.where(head == g, sink_ref[kv * GQA_GROUP + g], col)
    return col


def _stack_heads(q_ref, kv, rows=slice(None)):
    return jnp.concatenate(
        [q_ref[rows, (kv * GQA_GROUP + g) * HEAD_DIM:(kv * GQA_GROUP + g + 1) * HEAD_DIM]
         for g in range(GQA_GROUP)], axis=0)


LOG2E = math.log2(math.e)
QK_LOG2_SCALE = HEAD_DIM ** -0.5 * LOG2E


def _softmax_pv(q4, k_all, v_all, bias, sink_col):
    t = lax.dot_general(q4, k_all, (((1,), (1,)), ((), ())), preferred_element_type=F32) * QK_LOG2_SCALE
    if bias is not None:
        t = t + bias
    sink2 = sink_col * LOG2E
    m = jnp.maximum(jnp.max(t, axis=-1, keepdims=True), sink2)
    e = jnp.exp2(t - m).astype(BF16)
    v_ones = jnp.concatenate([v_all, jnp.ones_like(v_all)], axis=1)
    pv = jnp.dot(e, v_ones, preferred_element_type=F32)
    d = v_all.shape[1]
    return pv[:, :d] / (pv[:, d:] + jnp.exp2(sink2 - m))


Q_BLOCKS = 4


def _window_attn_kernel(sink_ref, bias_first_ref, bias_mid_ref, bias_last_ref, q_ref, kp_ref, km_ref, kn_ref,
                        vp_ref, vm_ref, vn_ref, kx_ref, vx_ref, o_ref):
    for blk in range(Q_BLOCKS):
        bias_ref = bias_first_ref if blk == 0 else bias_last_ref if blk == Q_BLOCKS - 1 else bias_mid_ref
        rows = slice(blk * BLOCK, (blk + 1) * BLOCK)
        for kv in range(N_KV_HEADS):
            ks = slice(kv * HEAD_DIM, (kv + 1) * HEAD_DIM)

            def band(prev_ref, mid_ref, next_ref):
                pieces = [prev_ref[:, ks]] + [mid_ref[j * BLOCK:(j + 1) * BLOCK, ks] for j in range(Q_BLOCKS)]
                return (pieces + [next_ref[:, ks]])[blk:blk + 3]

            k_all = jnp.concatenate([kx_ref[:, ks]] + band(kp_ref, km_ref, kn_ref), axis=0)
            v_all = jnp.concatenate([vx_ref[:, ks]] + band(vp_ref, vm_ref, vn_ref), axis=0)
            o = _softmax_pv(_stack_heads(q_ref, kv, rows), k_all, v_all, bias_ref[...],
                            _sink_column(sink_ref, kv, BLOCK))
            for g in range(GQA_GROUP):
                h = kv * GQA_GROUP + g
                o_ref[rows, h * HEAD_DIM:(h + 1) * HEAD_DIM] = o[g * BLOCK:(g + 1) * BLOCK].astype(BF16)


def _window_bias(nb):
    qi = np.arange(GQA_GROUP * BLOCK)[:, None] % BLOCK
    sj = np.arange(3 * BLOCK)[None, :] - BLOCK
    near = np.abs(sj - qi) <= WINDOW
    out = np.zeros((3, GQA_GROUP * BLOCK, CTX_LEN + 3 * BLOCK), np.float32)
    for variant, blk in enumerate((0, 1, nb - 1)):
        key_pos = blk * BLOCK + sj
        ok = near & (key_pos >= 0) & (key_pos < nb * BLOCK)
        out[variant, :, CTX_LEN:] = np.where(ok, 0.0, NEG_INF)
    return jnp.asarray(out)


def _window_attention(p, pc, kx_blk, vx_blk, sink):
    b, seq, _ = p.shape
    nb = seq // BLOCK
    kvw = N_KV_HEADS * HEAD_DIM
    k_blk, v_blk = Q_END // kvw, K_END // kvw
    assert nb % Q_BLOCKS == 0 and Q_BLOCKS >= 2
    steps = nb // Q_BLOCKS
    prev = lambda bb, i: (bb, jnp.maximum(Q_BLOCKS * i - 1, 0))
    nxt = lambda bb, i: (bb, jnp.minimum(Q_BLOCKS * (i + 1), nb - 1))

    def edge(which, blk):
        return pl.BlockSpec((None, BLOCK, kvw), lambda bb, i: which(bb, i) + (blk,))

    def mid(blk):
        return pl.BlockSpec((None, Q_BLOCKS * BLOCK, kvw), lambda bb, i: (bb, i, blk))

    bias_shape = (None, GQA_GROUP * BLOCK, CTX_LEN + 3 * BLOCK)
    bias = _window_bias(nb)
    return pl.pallas_call(
        _window_attn_kernel,
        grid=(b, steps),
        in_specs=[
            pl.BlockSpec(memory_space=pltpu.SMEM),
            pl.BlockSpec(bias_shape, lambda bb, i: (jnp.where(i == 0, 0, 1), 0, 0)),
            pl.BlockSpec(bias_shape, lambda bb, i: (1, 0, 0)),
            pl.BlockSpec(bias_shape, lambda bb, i: (jnp.where(i == steps - 1, 2, 1), 0, 0)),
            pl.BlockSpec((None, Q_BLOCKS * BLOCK, ATTN_WIDTH), lambda bb, i: (bb, i, 0)),
            edge(prev, k_blk), mid(k_blk), edge(nxt, k_blk),
            edge(prev, v_blk), mid(v_blk), edge(nxt, v_blk),
            pl.BlockSpec((None, CTX_LEN, kvw), lambda bb, i: (bb, 0, kx_blk)),
            pl.BlockSpec((None, CTX_LEN, kvw), lambda bb, i: (bb, 0, vx_blk)),
        ],
        out_specs=pl.BlockSpec((None, Q_BLOCKS * BLOCK, ATTN_WIDTH), lambda bb, i: (bb, i, 0)),
        out_shape=jax.ShapeDtypeStruct((b, seq, ATTN_WIDTH), BF16),
        compiler_params=_cparams(("arbitrary", "arbitrary"), 32),
        name="window_attn",
    )(sink, bias, bias, bias, p, p, p, p, p, p, p, pc, pc)


def _ctx_attn_kernel(sink_ref, q_ref, k_ref, v_ref, o_ref):
    n = q_ref.shape[0]
    for kv in range(N_KV_HEADS):
        ks = slice(kv * HEAD_DIM, (kv + 1) * HEAD_DIM)
        o = _softmax_pv(_stack_heads(q_ref, kv), k_ref[:, ks], v_ref[:, ks], None, _sink_column(sink_ref, kv, n))
        for g in range(GQA_GROUP):
            h = kv * GQA_GROUP + g
            o_ref[:, h * HEAD_DIM:(h + 1) * HEAD_DIM] = o[g * n:(g + 1) * n].astype(BF16)


def _context_attention(pc, sink):
    b, n, _ = pc.shape
    kvw = N_KV_HEADS * HEAD_DIM
    return pl.pallas_call(
        _ctx_attn_kernel,
        grid=(b,),
        in_specs=[
            pl.BlockSpec(memory_space=pltpu.SMEM),
            pl.BlockSpec((None, n, ATTN_WIDTH), lambda bb: (bb, 0, 0)),
            pl.BlockSpec((None, n, kvw), lambda bb: (bb, 0, Q_END // kvw)),
            pl.BlockSpec((None, n, kvw), lambda bb: (bb, 0, K_END // kvw)),
        ],
        out_specs=pl.BlockSpec((None, n, ATTN_WIDTH), lambda bb: (bb, 0, 0)),
        out_shape=jax.ShapeDtypeStruct((b, n, ATTN_WIDTH), BF16),
        compiler_params=_cparams(("arbitrary",), 32),
        name="ctx_attn",
    )(sink, pc, pc, pc)


def _outproj_kernel(*refs, n_riders):
    a_ref, y_ref, x_ref, m_ref, nw_ref, wa_ref, wy_ref = refs[:7]
    o_ref, h_ref = refs[7 + n_riders], refs[8 + n_riders]
    _cast_riders(refs[7:7 + n_riders], refs[9 + n_riders:])
    gain = nw_ref[...] * (1.0 + m_ref[4:5, :])
    shift = m_ref[3:4, :]
    for r0 in range(0, x_ref.shape[0], SUB_ROWS):
        rows = slice(r0, r0 + SUB_ROWS)
        mix = (jnp.dot(a_ref[rows, :], wa_ref[...], preferred_element_type=F32)
               + jnp.dot(y_ref[rows, :], wy_ref[...], preferred_element_type=F32))
        x1 = x_ref[rows, :] + m_ref[2:3, :] * mix
        o_ref[rows, :] = x1
        h_ref[rows, :] = _rms_modulate(x1, gain, shift).astype(BF16)


def _out_projection(attn, hy, x, mod, mod_row, nw, w_bf, *, tm, riders=()):
    b, t, _ = x.shape
    assert tm % SUB_ROWS == 0
    n_i = t // tm
    row = (lambda bb: bb) if mod_row is None else (lambda bb: mod_row)
    tile = lambda width: pl.BlockSpec((None, tm, width), lambda bb, i: (bb, i, 0))
    half = lambda k: pl.BlockSpec((None, D_MODEL // 2, D_MODEL), lambda bb, i: (0, k, 0),
                                  pipeline_mode=pl.Buffered(1))
    plans = [_rider_plan(w, layer, b * n_i, lambda bb, i: bb * n_i + i) for w, layer in riders]
    out = pl.pallas_call(
        functools.partial(_outproj_kernel, n_riders=len(plans)),
        grid=(b, n_i),
        in_specs=[
            tile(ATTN_WIDTH), tile(HYENA_WIDTH), tile(D_MODEL),
            pl.BlockSpec((None, N_MOD, D_MODEL), lambda bb, i: (row(bb), 0, 0)),
            pl.BlockSpec((1, D_MODEL), lambda bb, i: (0, 0)),
            half(0), half(1),
        ] + [pl_[1] for pl_ in plans],
        out_specs=[tile(D_MODEL), tile(D_MODEL)] + [pl_[2] for pl_ in plans],
        out_shape=[jax.ShapeDtypeStruct(x.shape, F32), jax.ShapeDtypeStruct(x.shape, BF16)]
        + [pl_[3] for pl_ in plans],
        compiler_params=_cparams(("arbitrary", "arbitrary"), 48),
        name="out_proj",
    )(attn, hy, x, mod, nw, w_bf, w_bf, *[pl_[0] for pl_ in plans])
    return out[0], out[1], [pl_[4](a) for pl_, a in zip(plans, out[2:])]


MLP_OUT_CHUNK = 512


def _mlp_kernel(h_ref, x_ref, m_ref, w1_ref, w2_ref, o_ref, *, n_f):
    f = pl.program_id(2)

    @pl.when(f == 0)
    def _():
        o_ref[...] = jnp.zeros_like(o_ref)

    a = jnp.maximum(jnp.dot(h_ref[...], w1_ref[...], preferred_element_type=F32), 0.0)
    a = (a * a).astype(BF16)
    for n0 in range(0, D_MODEL, MLP_OUT_CHUNK):
        sl = slice(n0, n0 + MLP_OUT_CHUNK)
        o_ref[:, sl] += jnp.dot(a, w2_ref[:, sl], preferred_element_type=F32)

    @pl.when(f == n_f - 1)
    def _():
        o_ref[...] = x_ref[...] + m_ref[5:6, :] * o_ref[...]


def _mlp(h, x, mod, mod_row, w1_bf, w2_bf, *, tm, tf):
    b, t, _ = x.shape
    n_f = D_FF // tf
    row = (lambda bb: bb) if mod_row is None else (lambda bb: mod_row)
    tile = pl.BlockSpec((None, tm, D_MODEL), lambda bb, i, f: (bb, i, 0))
    return pl.pallas_call(
        functools.partial(_mlp_kernel, n_f=n_f),
        grid=(b, t // tm, n_f),
        in_specs=[
            tile, tile,
            pl.BlockSpec((None, N_MOD, D_MODEL), lambda bb, i, f: (row(bb), 0, 0)),
            pl.BlockSpec((None, D_MODEL, tf), lambda bb, i, f: (0, 0, f)),
            pl.BlockSpec((None, tf, D_MODEL), lambda bb, i, f: (0, f, 0)),
        ],
        out_specs=tile,
        out_shape=jax.ShapeDtypeStruct(x.shape, F32),
        compiler_params=_cparams(("arbitrary", "arbitrary", "arbitrary"), 56),
        name="mlp",
    )(h, x, mod, w1_bf, w2_bf)


def _tile_const(k, like):
    return jnp.full(like.shape, k, F32).astype(like.dtype)


def _p_add(p, q):
    if p is None:
        return q
    if q is None:
        return p
    (a, sa), (b, sb) = p, q
    if sa == sb:
        return (a + b, sa)
    return (a - b, 1) if sa > 0 else (b - a, 1)


def _p_scale(p, k):
    if p is None or k == 0.0:
        return None
    a, s = p
    if k < 0:
        s, k = -s, -k
    return (a, s) if k == 1.0 else (a * _tile_const(k, a), s)


def _c_add(x, y, sign=1):
    if y is None:
        return x
    if x is None:
        x = (None, None)
    return (_p_add(x[0], _p_scale(y[0], sign)), _p_add(x[1], _p_scale(y[1], sign)))


def _snap(v):
    for t in (0.0, 1.0, -1.0):
        if abs(v - t) < 1e-12:
            return t
    return v


def _c_mulc(x, c):
    if x is None:
        return None
    cr, ci = _snap(c.real), _snap(c.imag)
    re, im = x
    if cr != 0.0 and ci != 0.0 and abs(abs(cr) - abs(ci)) < 1e-12:
        k, sr, si = abs(cr), math.copysign(1.0, cr), math.copysign(1.0, ci)
        return (_p_scale(_p_add(_p_scale(re, sr), _p_scale(im, -si)), k),
                _p_scale(_p_add(_p_scale(re, si), _p_scale(im, sr)), k))
    return (_p_add(_p_scale(re, cr), _p_scale(im, -ci)), _p_add(_p_scale(re, ci), _p_scale(im, cr)))


def _is_zero(x):
    return x is None or (x[0] is None and x[1] is None)


def _slab_fft_dit(xs, sign, need):
    n = len(xs)
    if n == 1:
        return {0: xs[0]}
    if all(_is_zero(x) for x in xs):
        return {k: None for k in need}
    h = n // 2
    sub = {k % h for k in need}
    ev = _slab_fft_dit(xs[0::2], sign, sub)
    od = _slab_fft_dit(xs[1::2], sign, sub)
    out = {}
    for k in sub:
        t = _c_mulc(od[k], cmath.exp(sign * 2j * math.pi * k / n))
        if k in need:
            out[k] = _c_add(ev[k], t)
        if k + h in need:
            out[k + h] = _c_add(ev[k], t, -1)
    return out


def _slab_fft_dif(xs, sign, need):
    n = len(xs)
    if n == 1:
        return {0: xs[0]}
    h = n // 2
    need_e = {k // 2 for k in need if k % 2 == 0}
    need_o = {k // 2 for k in need if k % 2 == 1}
    out = {}
    if need_e:
        ev = _slab_fft_dif([_c_add(xs[j], xs[j + h]) for j in range(h)], sign, need_e)
        out.update({2 * k: v for k, v in ev.items()})
    if need_o:
        dif = [_c_mulc(_c_add(xs[j], xs[j + h], -1), cmath.exp(sign * 2j * math.pi * j / n)) for j in range(h)]
        od = _slab_fft_dif(dif, sign, need_o)
        out.update({2 * k + 1: v for k, v in od.items()})
    return out


def _part_value(p, like):
    if p is None:
        return jnp.zeros_like(like)
    return p[0] if p[1] > 0 else -p[0]


HY_CT = 128
SLAB = 256
N_HI = 2 * SEQ // SLAB
N_IN = SEQ // SLAB
N_SPEC = N_HI // 2 + 1
ROW_CHUNK = 8
HY_ROWS = 16
S_IM = SLAB
N_CT = HYENA_WIDTH // HY_CT
SPEC_GROUP = 4
FILTER_HIDDEN = 64
EMB_PAD = 128
HIGHEST = lax.Precision.HIGHEST


def _dft_constants():
    n = 2 * SEQ
    r = np.arange(SLAB)
    tall = np.zeros((N_SPEC, 2 * SLAB, SLAB))
    fwd = np.zeros((N_SPEC, 2 * SLAB, 2 * SLAB))
    inv = np.zeros((N_SPEC, 2 * SLAB, 2 * SLAB))
    for b in range(N_SPEC):
        m = np.exp(-2j * np.pi * (np.outer(r, r) / SLAB + (r * b)[None, :] / n))
        a = m.T / n
        tall[b] = np.concatenate([m.real, m.imag], axis=0)
        fwd[b] = np.block([[m.real, -m.imag], [m.imag, m.real]])
        inv[b] = np.block([[a.real, a.imag], [-a.imag, a.real]])
    return tuple(jnp.asarray(t, F32).astype(BF16) for t in (tall, fwd, inv))


def _ctx_dft_constants():
    n = 2 * CTX_LEN
    k = np.arange(n)
    f = np.exp(-2j * np.pi * np.outer(k, k) / n)
    a = np.conj(f)[:CTX_LEN, :] / n
    return (jnp.asarray(np.concatenate([f.real, f.imag], axis=0), F32).astype(BF16),
            jnp.asarray(np.concatenate([a.real, a.imag], axis=0), F32).astype(BF16))


def _filter_positions(l):
    t = np.linspace(0.0, 1.0, l)[:, None]
    bands = (FILTER_EMB - 1) // 2
    f = np.linspace(1e-4, bands - 1, bands)[None, :]
    w = 2.0 * math.pi * np.arange(l)[:, None] / l
    z = np.concatenate([t, np.cos(f * w), -np.sin(f * w)], axis=-1)
    z2 = np.concatenate([z, z[0:1], z[1:][::-1]], axis=0)
    return jnp.asarray(np.pad(z2, ((0, 0), (0, EMB_PAD - FILTER_EMB))), F32)


def _filter_feat_kernel(za_ref, zb_ref, w1_ref, b1_ref, w2_ref, b2_ref, w3_ref, b3_ref, fr_ref, o_ref):
    fr = fr_ref[...]
    dot = functools.partial(jnp.dot, precision=HIGHEST, preferred_element_type=F32)
    z = jnp.concatenate([za_ref[...], zb_ref[...]], axis=1)
    h = jnp.sin(fr * (dot(z, w1_ref[...]) + b1_ref[...]))
    h = jnp.sin(fr * (dot(h, w2_ref[...]) + b2_ref[...]))
    o_ref[...] = jnp.sin(fr * (dot(h, w3_ref[...]) + b3_ref[...]))


def _filter_features(emb, w1, b1, w2, b2, w3, b3, freq):
    l = emb.shape[0] // 2
    tr = min(l, 1024)
    nblk = l // tr
    diag2 = lambda w: jnp.kron(jnp.eye(2, dtype=F32), w)
    w1p = jnp.pad(w1, ((0, EMB_PAD - FILTER_EMB), (0, 0)))
    vec = lambda v: jnp.tile(v.reshape(1, FILTER_HIDDEN), (1, 2))
    full = lambda shape: pl.BlockSpec(shape, lambda i: (0, 0))
    wide = 2 * FILTER_HIDDEN
    return pl.pallas_call(
        _filter_feat_kernel,
        grid=(nblk,),
        in_specs=[pl.BlockSpec((tr, EMB_PAD), lambda i: (i, 0)),
                  pl.BlockSpec((tr, EMB_PAD), lambda i: (i + nblk, 0)),
                  full((2 * EMB_PAD, wide)), full((1, wide)),
                  full((wide, wide)), full((1, wide)),
                  full((wide, wide)), full((1, wide)),
                  full((1, wide))],
        out_specs=pl.BlockSpec((tr, wide), lambda i: (i, 0)),
        out_shape=jax.ShapeDtypeStruct((l, wide), F32),
        compiler_params=_cparams(("arbitrary",), 32),
        name="filter_feat",
    )(emb, emb, diag2(w1p), vec(b1), diag2(w2), vec(b2), diag2(w3), vec(b3), vec(freq))


def _filter_taps(raw, n, l, absdelta):
    m = jnp.where(n < l, n, 2 * l - n)
    t = m.astype(F32) / (l - 1)
    return jnp.where(n == l, 0.0, raw * jnp.exp(-t * absdelta))


def _cplx_from_pq(pq, conj):
    half = pq.shape[0] // 2
    pr, pi_ = pq[:half, :HY_CT], pq[:half, HY_CT:]
    qr, qi = pq[half:, :HY_CT], pq[half:, HY_CT:]
    if conj:
        return pr + qi, pi_ - qr
    return pr - qi, qr + pi_


def _slab_forward(src_ref, n_in, real_only, dst_ref, need, scale):
    def body(rc, carry):
        rows = pl.ds(pl.multiple_of(rc * ROW_CHUNK, ROW_CHUNK), ROW_CHUNK)
        xs = []
        for a in range(N_HI):
            if a >= n_in:
                xs.append(None)
                continue
            re = src_ref[a, rows, 0:HY_CT]
            if scale is not None:
                re = re * scale
            xs.append(((re, 1), None if real_only else (src_ref[a, rows, HY_CT:2 * HY_CT], 1)))
        like = xs[0][0][0]
        out = _slab_fft_dit(xs, -1, need)
        for b in sorted(need):
            c = out[b] or (None, None)
            dst_ref[b, rows, 0:HY_CT] = _part_value(c[0], like)
            dst_ref[b, rows, HY_CT:2 * HY_CT] = _part_value(c[1], like)
        return carry

    lax.fori_loop(0, SLAB // ROW_CHUNK, body, 0)


def _filter_spec_kernel(feat_ref, wf_ref, wb_ref, ad_ref, mf_ref, o_ref, ft_ref, s_ref):
    rate = ad_ref[...] * (-1.0 / (SEQ - 1))
    row = lax.broadcasted_iota(jnp.int32, (SLAB, HY_CT), 0)
    row_f = row.astype(F32)
    first_row = row == 0
    no_w = jnp.zeros((FILTER_HIDDEN, HY_CT), F32)
    w_fwd = jnp.concatenate([wf_ref[...], no_w], axis=0).astype(BF16)
    w_bwd = jnp.concatenate([no_w, wb_ref[...]], axis=0).astype(BF16)

    def taps(a, acc):
        fwd = a < N_IN
        src = jnp.where(fwd, a, a - N_IN)
        rows = feat_ref[pl.ds(pl.multiple_of(src * SLAB, SLAB), SLAB), :]
        raw = jnp.dot(rows.astype(BF16), jnp.where(fwd, w_fwd, w_bwd), preferred_element_type=F32)
        base = jnp.where(fwd, a * SLAB, 2 * SEQ - a * SLAB).astype(F32)
        step = jnp.where(fwd, 1.0, -1.0)
        v = raw * jnp.exp((base + step * row_f) * rate)
        v = jnp.where(jnp.logical_and(a == N_IN, first_row), 0.0, v)
        ft_ref[a] = v
        return acc + jnp.sum(jnp.abs(v), axis=0, keepdims=True)

    norm = lax.fori_loop(0, N_HI, taps, jnp.zeros((1, HY_CT), F32), unroll=2)
    _slab_forward(ft_ref, N_HI, True, s_ref, set(range(N_SPEC)), 1.0 / norm)

    def spec(b):
        pq = jnp.dot(mf_ref[b], s_ref[b].astype(BF16), preferred_element_type=F32)
        xr, xi = _cplx_from_pq(pq, False)
        rows = pl.ds(pl.multiple_of(b * SLAB, SLAB), SLAB)
        o_ref[rows, 0:HY_CT] = xr.astype(BF16)
        o_ref[rows, HY_CT:2 * HY_CT] = xi.astype(BF16)

    def spec_group(i, carry):
        for t in range(SPEC_GROUP):
            spec(1 + SPEC_GROUP * i + t)
        return carry

    spec(0)
    lax.fori_loop(0, (N_SPEC - 1) // SPEC_GROUP, spec_group, 0)


def _filter_spectrum(feat, w_fout, absdelta, mf):
    cols = 2 * N_CT
    return pl.pallas_call(
        _filter_spec_kernel,
        grid=(HYENA_ORDER, N_CT),
        in_specs=[
            pl.BlockSpec((SEQ, 2 * FILTER_HIDDEN), lambda o, c: (0, 0)),
            pl.BlockSpec((FILTER_HIDDEN, HY_CT), lambda o, c: (0, o * cols + c)),
            pl.BlockSpec((FILTER_HIDDEN, HY_CT), lambda o, c: (0, o * cols + N_CT + c)),
            pl.BlockSpec((1, HY_CT), lambda o, c: (0, c)),
            pl.BlockSpec((N_SPEC, 2 * SLAB, SLAB), lambda o, c: (0, 0, 0)),
        ],
        out_specs=pl.BlockSpec((None, None, N_SPEC * SLAB, 2 * HY_CT), lambda o, c: (o, c, 0, 0)),
        out_shape=jax.ShapeDtypeStruct((HYENA_ORDER, N_CT, N_SPEC * SLAB, 2 * HY_CT), BF16),
        scratch_shapes=[pltpu.VMEM((N_HI, SLAB, HY_CT), F32), pltpu.VMEM((N_SPEC, SLAB, 2 * HY_CT), F32)],
        compiler_params=_cparams(("arbitrary", "arbitrary"), 48),
        name="filter_spec",
    )(feat, w_fout, w_fout, absdelta, mf)


HALO = 8


def _conv_slab(u_ref, bi, a, w, b, st):
    r0 = a * SLAB
    x = u_ref[bi, pl.ds(pl.multiple_of(r0, SLAB), SLAB), :].astype(F32)
    before = u_ref[bi, pl.ds(pl.multiple_of(jnp.maximum(r0 - 16, 0), 16), 16), :].astype(F32)
    after = u_ref[bi, pl.ds(pl.multiple_of(jnp.minimum(r0 + SLAB, SEQ - 16), 16), 16), :].astype(F32)
    st[0:HALO, :] = jnp.where(a > 0, before[16 - HALO:16, :], 0.0)
    st[HALO:HALO + SLAB, :] = x
    st[HALO + SLAB:2 * HALO + SLAB, :] = jnp.where(a < N_IN - 1, after[0:HALO, :], 0.0)
    xm1 = st[HALO - 1:HALO - 1 + SLAB, :]
    xp1 = st[HALO + 1:HALO + 1 + SLAB, :]
    return xm1 * w[0:1, :] + x * w[1:2, :] + xp1 * w[2:3, :] + b


def _hyena_kernel(uv_ref, u1_ref, u2_ref, cwv_ref, cw1_ref, cw2_ref, cbv_ref, cb1_ref, cb2_ref, skip_ref,
                  mf_ref, mi_ref, h_ref, o_ref, zc_ref, g_ref, s_ref, st_ref):
    o = pl.program_id(2)

    @pl.when(o == 0)
    def _():
        def conv(a, carry):
            for bi in range(2):
                lanes = slice(bi * HY_CT, (bi + 1) * HY_CT)
                st = lambda k: st_ref.at[3 * bi + k]
                zc_ref[a, 0:SLAB, lanes] = _conv_slab(uv_ref, bi, a, cwv_ref[...], cbv_ref[...], st(0)).astype(BF16)
                g_ref[0, a, 0:SLAB, lanes] = _conv_slab(u1_ref, bi, a, cw1_ref[...], cb1_ref[...], st(1)).astype(BF16)
                g_ref[1, a, 0:SLAB, lanes] = _conv_slab(u2_ref, bi, a, cw2_ref[...], cb2_ref[...], st(2)).astype(BF16)
            return carry

        lax.fori_loop(0, N_IN, conv, 0)

    def im_sign(b):
        return -1 if b > N_HI // 2 else 1

    def forward_chunk(rc, carry):
        rows = pl.ds(pl.multiple_of(rc * HY_ROWS, HY_ROWS), HY_ROWS)
        rows_im = pl.ds(pl.multiple_of(S_IM + rc * HY_ROWS, HY_ROWS), HY_ROWS)
        xs = [((zc_ref[a, rows, 0:HY_CT].astype(F32), 1), (zc_ref[a, rows, HY_CT:2 * HY_CT].astype(F32), 1))
              for a in range(N_IN)]
        like = xs[0][0][0]
        out = _slab_fft_dit(xs + [None] * (N_HI - N_IN), -1, set(range(N_HI)))
        for b in range(N_HI):
            re, im = out[b]
            s_ref[b, rows, :] = _part_value(re, like).astype(BF16)
            s_ref[b, rows_im, :] = _part_value(None if im is None else (im[0], im[1] * im_sign(b)),
                                               like).astype(BF16)
        return carry

    lax.fori_loop(0, SLAB // HY_ROWS, forward_chunk, 0)

    def through_filter(slabs, j):
        h = h_ref[j * SLAB:(j + 1) * SLAB, :].astype(F32)
        hr = jnp.concatenate([h[:, :HY_CT]] * len(slabs), axis=1)
        hi = jnp.concatenate([h[:, HY_CT:]] * len(slabs), axis=1)
        load = lambda b: jnp.concatenate([s_ref[b, 0:SLAB, :], s_ref[b, S_IM:S_IM + SLAB, :]], axis=0)
        d = load(slabs[0]) if len(slabs) == 1 else jnp.concatenate([load(b) for b in slabs], axis=1)
        x = jnp.dot(mf_ref[j], d, preferred_element_type=F32)
        xr, xi = x[:SLAB], x[SLAB:]
        y = jnp.concatenate([xr * hr - xi * hi, xr * hi + xi * hr], axis=0).astype(BF16)
        w = jnp.dot(mi_ref[j], y, preferred_element_type=F32).astype(BF16)
        return [(b, w[:, k * HY_CT:(k + 1) * HY_CT]) for k, b in enumerate(slabs)]

    for slabs, j in [([0], 0), ([N_HI // 2], N_HI // 2)] + [([j, N_HI - j], j) for j in range(1, N_HI // 2)]:
        for b, w in through_filter(slabs, j):
            s_ref[b, 0:SLAB, :] = w[:SLAB]
            s_ref[b, S_IM:S_IM + SLAB, :] = w[SLAB:]

    def inverse_chunk(rc, carry):
        rows = pl.ds(pl.multiple_of(rc * HY_ROWS, HY_ROWS), HY_ROWS)
        rows_im = pl.ds(pl.multiple_of(S_IM + rc * HY_ROWS, HY_ROWS), HY_ROWS)
        xs = [((s_ref[b, rows, :].astype(F32), 1), (s_ref[b, rows_im, :].astype(F32), im_sign(b)))
              for b in range(N_HI)]
        like = xs[0][0][0]
        out = _slab_fft_dif(xs, 1, set(range(N_IN)))
        for a in range(N_IN):
            s_ref[a, rows, :] = _part_value(out[a][0], like).astype(BF16)
            s_ref[a, rows_im, :] = _part_value(out[a][1], like).astype(BF16)
        return carry

    lax.fori_loop(0, SLAB // HY_ROWS, inverse_chunk, 0)

    sk = skip_ref[pl.ds(o, 1), :]
    skip2 = jnp.concatenate([sk, sk], axis=1)

    def gate(a, carry):
        y = jnp.concatenate([s_ref[a, 0:SLAB, :], s_ref[a, S_IM:S_IM + SLAB, :]], axis=1).astype(F32)
        z = g_ref[o, a, 0:SLAB, :].astype(F32) * (y + skip2 * zc_ref[a, 0:SLAB, :].astype(F32))
        zc_ref[a, 0:SLAB, :] = z.astype(BF16)
        return carry

    lax.fori_loop(0, N_IN, gate, 0)

    @pl.when(o == HYENA_ORDER - 1)
    def _():
        def emit(a, carry):
            rows = pl.ds(pl.multiple_of(a * SLAB, SLAB), SLAB)
            for bi in range(2):
                o_ref[bi, rows, :] = zc_ref[a, 0:SLAB, bi * HY_CT:(bi + 1) * HY_CT]
            return carry

        lax.fori_loop(0, N_IN, emit, 0)


def _hyena_conv(p, conv_w, conv_b, skip, mf, mi, hspec):
    b = p.shape[0]
    u0 = V_END // HY_CT
    cb = conv_b.reshape(1, -1)
    part = lambda k: pl.BlockSpec((2, SEQ, HY_CT), lambda c, q, o: (q, 0, u0 + k * N_CT + c))
    cw = lambda k: pl.BlockSpec((SHORT_CONV, HY_CT), lambda c, q, o: (0, k * N_CT + c))
    cbs = lambda k: pl.BlockSpec((1, HY_CT), lambda c, q, o: (0, k * N_CT + c))
    const = lambda: pl.BlockSpec((N_SPEC, 2 * SLAB, 2 * SLAB), lambda c, q, o: (0, 0, 0),
                                 pipeline_mode=pl.Buffered(1))
    return pl.pallas_call(
        _hyena_kernel,
        grid=(N_CT, b // 2, HYENA_ORDER),
        in_specs=[part(0), part(1), part(2), cw(0), cw(1), cw(2), cbs(0), cbs(1), cbs(2),
                  pl.BlockSpec((HYENA_ORDER, HY_CT), lambda c, q, o: (0, c)),
                  const(), const(),
                  pl.BlockSpec((None, None, N_SPEC * SLAB, 2 * HY_CT), lambda c, q, o: (o, c, 0, 0))],
        out_specs=pl.BlockSpec((2, SEQ, HY_CT), lambda c, q, o: (q, 0, c)),
        out_shape=jax.ShapeDtypeStruct((b, SEQ, HYENA_WIDTH), BF16),
        scratch_shapes=[pltpu.VMEM((N_IN, SLAB, 2 * HY_CT), BF16),
                        pltpu.VMEM((HYENA_ORDER, N_IN, SLAB, 2 * HY_CT), BF16),
                        pltpu.VMEM((N_HI, 2 * SLAB, HY_CT), BF16),
                        pltpu.VMEM((2 * (HYENA_ORDER + 1), SLAB + 2 * HALO, HY_CT), F32)],
        compiler_params=_cparams(("arbitrary", "arbitrary", "arbitrary"), 56),
        name="hyena_conv",
    )(p, p, p, conv_w, conv_w, conv_w, cb, cb, cb, skip, mf, mi, hspec)


def _ctx_hyena_kernel(uv_ref, u1_ref, u2_ref, cwv_ref, cw1_ref, cw2_ref, cbv_ref, cb1_ref, cb2_ref, skip_ref,
                      feat_ref, wf0_ref, wb0_ref, wf1_ref, wb1_ref, ad_ref, ff_ref, ai_ref, o_ref):
    l = CTX_LEN
    nb = uv_ref.shape[0]
    n = lax.broadcasted_iota(jnp.int32, (2 * l, HY_CT), 0)
    spectra = []
    no_w = jnp.zeros((FILTER_HIDDEN, HY_CT), F32)
    for wf_ref, wb_ref in ((wf0_ref, wb0_ref), (wf1_ref, wb1_ref)):
        raw = jnp.concatenate(
            [jnp.dot(feat_ref[...], jnp.concatenate([wf_ref[...], no_w], axis=0), precision=HIGHEST,
                     preferred_element_type=F32),
             jnp.dot(feat_ref[...], jnp.concatenate([no_w, wb_ref[...]], axis=0), precision=HIGHEST,
                     preferred_element_type=F32)], axis=0)
        v = _filter_taps(raw, n, l, ad_ref[...])
        v = v / jnp.sum(jnp.abs(v), axis=0, keepdims=True)
        pq = jnp.dot(ff_ref[...], v.astype(BF16), preferred_element_type=F32)
        spectra.append((pq[:2 * l], pq[2 * l:]))

    row = lax.broadcasted_iota(jnp.int32, (l, HY_CT), 0)

    def conv(u_ref, bi, w_ref, b_ref):
        x = u_ref[bi].astype(F32)
        xm1 = jnp.where(row == 0, 0.0, pltpu.roll(x, 1, 0))
        xp1 = jnp.where(row == l - 1, 0.0, pltpu.roll(x, l - 1, 0))
        return xm1 * w_ref[0:1, :] + x * w_ref[1:2, :] + xp1 * w_ref[2:3, :] + b_ref[...]

    z = [conv(uv_ref, bi, cwv_ref, cbv_ref) for bi in range(nb)]
    gates = [[conv(u1_ref, bi, cw1_ref, cb1_ref) for bi in range(nb)],
             [conv(u2_ref, bi, cw2_ref, cb2_ref) for bi in range(nb)]]
    for o in range(HYENA_ORDER):
        hr, hi = spectra[o]
        pq = jnp.dot(ff_ref[:, 0:l], jnp.concatenate(z, axis=1).astype(BF16), preferred_element_type=F32)
        ys = []
        for q in range(nb // 2):
            xr, xi = _cplx_from_pq(pq[:, 2 * q * HY_CT:2 * (q + 1) * HY_CT], False)
            ys += [xr * hr - xi * hi, xr * hi + xi * hr]
        pq2 = jnp.dot(ai_ref[...], jnp.concatenate(ys, axis=1).astype(BF16), preferred_element_type=F32)
        y = []
        for q in range(nb // 2):
            y += list(_cplx_from_pq(pq2[:, 2 * q * HY_CT:2 * (q + 1) * HY_CT], False))
        z = [gates[o][bi] * (y[bi] + skip_ref[o:o + 1, :] * z[bi]) for bi in range(nb)]
    for bi in range(nb):
        o_ref[bi] = z[bi].astype(BF16)


def _ctx_hyena(pc, conv_w, conv_b, skip, feat, w_fout, absdelta, ff, ai):
    b, l, _ = pc.shape
    u0 = V_END // HY_CT
    cb = conv_b.reshape(1, -1)
    cols = 2 * N_CT
    part = lambda k: pl.BlockSpec((b, l, HY_CT), lambda c: (0, 0, u0 + k * N_CT + c))
    cw = lambda k: pl.BlockSpec((SHORT_CONV, HY_CT), lambda c: (0, k * N_CT + c))
    cbs = lambda k: pl.BlockSpec((1, HY_CT), lambda c: (0, k * N_CT + c))
    wcol = lambda off: pl.BlockSpec((FILTER_HIDDEN, HY_CT), lambda c: (0, off + c))
    return pl.pallas_call(
        _ctx_hyena_kernel,
        grid=(N_CT,),
        in_specs=[part(0), part(1), part(2), cw(0), cw(1), cw(2), cbs(0), cbs(1), cbs(2),
                  pl.BlockSpec((HYENA_ORDER, HY_CT), lambda c: (0, c)),
                  pl.BlockSpec((l, 2 * FILTER_HIDDEN), lambda c: (0, 0)),
                  wcol(0), wcol(N_CT), wcol(cols), wcol(cols + N_CT),
                  pl.BlockSpec((1, HY_CT), lambda c: (0, c)),
                  pl.BlockSpec((4 * l, 2 * l), lambda c: (0, 0)),
                  pl.BlockSpec((2 * l, 2 * l), lambda c: (0, 0))],
        out_specs=pl.BlockSpec((b, l, HY_CT), lambda c: (0, 0, c)),
        out_shape=jax.ShapeDtypeStruct((b, l, HYENA_WIDTH), BF16),
        compiler_params=_cparams(("arbitrary",), 32),
        name="ctx_hyena",
    )(pc, pc, pc, conv_w, conv_w, conv_w, cb, cb, cb, skip, feat, w_fout, w_fout, w_fout, w_fout, absdelta, ff, ai)


def _rope_tables(l):
    rows = l // GRID_W
    row = np.repeat(np.arange(rows), GRID_W).astype(np.float64)
    col = np.tile(np.arange(GRID_W), rows).astype(np.float64)
    n_freq = HEAD_DIM // 4
    inv = ROPE_BASE ** (-np.arange(n_freq) / n_freq)
    ang_r = row[:, None] * inv
    ang_c = col[:, None] * inv
    cos = np.concatenate([np.cos(ang_r), np.cos(ang_r), np.cos(ang_c), np.cos(ang_c)], axis=-1)
    sin = np.concatenate([-np.sin(ang_r), np.sin(ang_r), -np.sin(ang_c), np.sin(ang_c)], axis=-1)
    return jnp.asarray(cos, F32), jnp.asarray(sin, F32)


def kernel(x, c, ctx, c_ctx, w_ada, b_ada, norm_mix, norm_mlp, w_in, q_norm, k_norm, attn_sink, conv_w, conv_b, filt_w1, filt_b1, filt_w2, filt_b2, filt_w3, filt_b3, filt_freq, filt_w_out, hyena_skip, w_out, w_mlp_in, w_mlp_out):
    cos, sin = _rope_tables(SEQ)
    emb, emb_ctx = _filter_positions(SEQ), _filter_positions(CTX_LEN)
    m_tall, mf, mi = _dft_constants()
    ff_ctx, ai_ctx = _ctx_dft_constants()
    absdelta = jnp.abs(jnp.linspace(MIN_DECAY, MAX_DECAY, HYENA_WIDTH, dtype=F32)).reshape(1, HYENA_WIDTH)
    cvec = jnp.zeros((MOD_ROWS, D_MODEL), F32).at[:BATCH].set(c).at[CTX_ROW].set(c_ctx)
    mod_all = _modulation(cvec, w_ada, b_ada).reshape(DEPTH, MOD_ROWS, N_MOD, D_MODEL)
    w_in_bf = w_in[0:1].astype(BF16)
    ctx_rows = BATCH * CTX_LEN
    flat = lambda a: a.reshape(1, ctx_rows, a.shape[-1])
    per_batch = lambda a: a.reshape(BATCH, CTX_LEN, a.shape[-1])
    xc = flat(ctx)
    for i in range(DEPTH):
        update_ctx = i < DEPTH - 1
        mod = mod_all[i]
        nmix = norm_mix[i].reshape(1, D_MODEL)
        nmlp = norm_mlp[i].reshape(1, D_MODEL)
        qn = q_norm[i].reshape(1, HEAD_DIM)
        kn = k_norm[i].reshape(1, HEAD_DIM)
        filt_mlp = (filt_w1[i], filt_b1[i], filt_w2[i], filt_b2[i], filt_w3[i], filt_b3[i], filt_freq[i])

        p, (w_out_bf, w1_bf, w2_bf) = _in_projection(
            x, mod, None, nmix, w_in_bf, qn, kn, cos, sin, tm=512, n_off=0, n_tiles=IN_COLS // IN_TN,
            use_rope=True, riders=[(w_out, i), (w_mlp_in, i), (w_mlp_out, i)])
        if update_ctx:
            pc, _ = _in_projection(xc, mod, CTX_ROW, nmix, w_in_bf, qn, kn, cos, sin,
                                   tm=512, n_off=0, n_tiles=IN_COLS // IN_TN, use_rope=False)
            kx_blk, vx_blk = Q_END // KV_COLS * 2, Q_END // KV_COLS * 2 + 1
        else:
            pc, _ = _in_projection(xc, mod, CTX_ROW, nmix, w_in_bf, qn, kn, cos, sin,
                                   tm=512, n_off=KV_TILE, n_tiles=1, use_rope=False)
            kx_blk, vx_blk = 0, 1
        pc = per_batch(pc)

        attn = _window_attention(p, pc, kx_blk, vx_blk, attn_sink[i])
        hspec = _filter_spectrum(_filter_features(emb, *filt_mlp), filt_w_out[i], absdelta, m_tall)
        hy = _hyena_conv(p, conv_w[i], conv_b[i], hyena_skip[i], mf, mi, hspec)
        x, h2, next_w_in = _out_projection(attn, hy, x, mod, None, nmlp, w_out_bf, tm=512,
                                           riders=[(w_in, i + 1)] if update_ctx else [])
        x = _mlp(h2, x, mod, None, w1_bf, w2_bf, tm=512, tf=1024)

        if update_ctx:
            attn_c = _context_attention(pc, attn_sink[i])
            hy_c = _ctx_hyena(pc, conv_w[i], conv_b[i], hyena_skip[i], _filter_features(emb_ctx, *filt_mlp),
                              filt_w_out[i], absdelta, ff_ctx, ai_ctx)
            xc, h2c, _ = _out_projection(flat(attn_c), flat(hy_c), xc, mod, CTX_ROW, nmlp, w_out_bf, tm=512)
            xc = _mlp(h2c, xc, mod, CTX_ROW, w1_bf, w2_bf, tm=512, tf=1024)
            w_in_bf = next_w_in[0]
    return x
```
